```python
import math
import jax, jax.numpy as jnp
from jax import lax
import numpy as np

D_MODEL = 1024
BATCH = 2
SEQ = 8192
DEPTH = 2

N_META = 16
N_A_LAYERS = DEPTH // 2
N_B_LAYERS = DEPTH - N_A_LAYERS
POOL_WINDOWS = (2, 4, 8, 16)
N_POOL_GROUPS = len(POOL_WINDOWS)
POOL_GROUP_DIM = D_MODEL // N_POOL_GROUPS
HEAD_DIM = 64
N_Q_HEADS = D_MODEL // HEAD_DIM
N_KV_HEADS = max(1, N_Q_HEADS // 8)
GROUP = N_Q_HEADS // N_KV_HEADS
ATTN_DIM = N_Q_HEADS * HEAD_DIM
KV_DIM = N_KV_HEADS * HEAD_DIM
WINDOW = 128
BLOCK = 128
N_BUCKETS = 32
MAX_DISTANCE = 128
D_FF = (((8 * D_MODEL + 2) // 3 + 255) // 256) * 256
EPS = 1e-6
PAD_FRONT = (-N_META) % BLOCK

kernel_name = "yoco_pool_swa_sink_hybrid"


def rms_norm(x, g):
    xf = x.astype(jnp.float32)
    y = xf * lax.rsqrt(jnp.mean(xf * xf, axis=-1, keepdims=True) + EPS)
    return (y * g.astype(jnp.float32)).astype(x.dtype)


def swiglu(h, w_gate_up, w_down):
    gu = h @ w_gate_up
    gate, up = jnp.split(gu, 2, axis=-1)
    return (jax.nn.silu(gate) * up) @ w_down


def pool_mixer(h, w_groups, scale):
    B, L, _ = h.shape
    hg = h.astype(jnp.float32).reshape(B, L, N_POOL_GROUPS, POOL_GROUP_DIM)
    cs = jnp.cumsum(hg, axis=1)
    pos = jnp.arange(1, L + 1, dtype=jnp.float32)
    outs = []
    for gi, w in enumerate(POOL_WINDOWS):
        c = cs[:, :, gi]
        lag = jnp.pad(c, ((0, 0), (w, 0), (0, 0)))[:, :L]
        cnt = jnp.minimum(pos, float(w))[None, :, None]
        outs.append((c - lag) / cnt - hg[:, :, gi])
    p = jnp.stack(outs, axis=2).astype(h.dtype)
    y = jnp.einsum('blgc,gcd->blgd', p, w_groups).reshape(B, L, D_MODEL)
    return y * scale


def t5_bucket(d):
    max_exact = N_BUCKETS // 2
    df = jnp.maximum(d, 1).astype(jnp.float32)
    large = max_exact + (jnp.log(df / max_exact) / math.log(MAX_DISTANCE / max_exact)
                         * (N_BUCKETS - max_exact)).astype(jnp.int32)
    large = jnp.minimum(large, N_BUCKETS - 1)
    return jnp.where(d < max_exact, d, large)


def band_bias_and_mask(rel_bias, n_blocks):
    q = jnp.arange(BLOCK)[:, None]
    s = jnp.arange(2 * BLOCK)[None, :]
    d = q + BLOCK - s
    in_window = (d >= 0) & (d < WINDOW)
    bias = rel_bias[t5_bucket(jnp.maximum(d, 0))]
    bias = jnp.transpose(bias, (2, 0, 1)).astype(jnp.float32)
    key_pos = jnp.arange(n_blocks)[:, None] * BLOCK + s - BLOCK
    valid = in_window[None] & (key_pos >= PAD_FRONT)[:, None, :]
    return bias, valid


def to_blocks_with_prev(t):
    B, L = t.shape[:2]
    nb = (L + PAD_FRONT) // BLOCK
    t = jnp.pad(t, ((0, 0), (PAD_FRONT + BLOCK, 0), (0, 0), (0, 0)))
    t = t.reshape(B, nb + 1, BLOCK, t.shape[2], t.shape[3])
    return jnp.concatenate([t[:, :-1], t[:, 1:]], axis=2)


def sliding_window_attention(h, k_blk, v_blk, w_q, b_q, w_o, b_o, sinks, bias, valid):
    B, L, _ = h.shape
    q = (h @ w_q + b_q).reshape(B, L, N_KV_HEADS, GROUP, HEAD_DIM)
    q = jnp.pad(q, ((0, 0), (PAD_FRONT, 0), (0, 0), (0, 0), (0, 0)))
    nb = q.shape[1] // BLOCK
    q = q.reshape(B, nb, BLOCK, N_KV_HEADS, GROUP, HEAD_DIM)
    logits = jnp.einsum('bnqhgd,bnshd->bhgnqs', q, k_blk,
                        preferred_element_type=jnp.float32) * (HEAD_DIM ** -0.5)
    logits = logits + bias.reshape(N_KV_HEADS, GROUP, 1, BLOCK, 2 * BLOCK)
    logits = jnp.where(valid, logits, -jnp.inf)
    sink = sinks.astype(jnp.float32).reshape(N_KV_HEADS, GROUP, 1, 1, 1)
    m = jnp.maximum(logits.max(axis=-1, keepdims=True), sink)
    p = jnp.exp(logits - m)
    denom = p.sum(axis=-1, keepdims=True) + jnp.exp(sink - m)
    probs = (p / denom).astype(v_blk.dtype)
    o = jnp.einsum('bhgnqs,bnshd->bnqhgd', probs, v_blk)
    o = o.reshape(B, nb * BLOCK, ATTN_DIM)[:, PAD_FRONT:]
    return o @ w_o + b_o


def setup_inputs(seed: int = 0) -> dict:
    key = jax.random.key(seed)
    ks = jax.random.split(key, 24)
    f32 = jnp.float32

    def nrm(k, shape, scale):
        return jax.random.normal(k, shape, f32) * scale

    def gain(k, shape):
        return jnp.ones(shape, f32) + 0.05 * jax.random.normal(k, shape, f32)

    return {
        "x": nrm(ks[0], (BATCH, SEQ, D_MODEL), 1.0),
        "meta_tokens": nrm(ks[1], (N_META, D_MODEL), 1.0),
        "norm_mix_pre": gain(ks[2], (DEPTH, D_MODEL)),
        "norm_mix_post": gain(ks[3], (DEPTH, D_MODEL)),
        "norm_ffn_pre": gain(ks[4], (DEPTH, D_MODEL)),
        "norm_ffn_post": gain(ks[5], (DEPTH, D_MODEL)),
        "pool_w": nrm(ks[6], (N_A_LAYERS, N_POOL_GROUPS, POOL_GROUP_DIM, POOL_GROUP_DIM), POOL_GROUP_DIM ** -0.5),
        "pool_scale": gain(ks[7], (N_A_LAYERS, D_MODEL)) + 0.05 * jax.random.normal(ks[8], (N_A_LAYERS, D_MODEL), f32),
        "kv_norm": gain(ks[9], (D_MODEL,)),
        "w_k": nrm(ks[10], (D_MODEL, KV_DIM), D_MODEL ** -0.5),
        "b_k": nrm(ks[11], (KV_DIM,), 0.02),
        "w_v": nrm(ks[12], (D_MODEL, KV_DIM), D_MODEL ** -0.5),
        "b_v": nrm(ks[13], (KV_DIM,), 0.02),
        "w_q": nrm(ks[14], (N_B_LAYERS, D_MODEL, ATTN_DIM), D_MODEL ** -0.5),
        "b_q": nrm(ks[15], (N_B_LAYERS, ATTN_DIM), 0.02),
        "w_o": nrm(ks[16], (N_B_LAYERS, ATTN_DIM, D_MODEL), ATTN_DIM ** -0.5),
        "b_o": nrm(ks[17], (N_B_LAYERS, D_MODEL), 0.02),
        "sinks": nrm(ks[18], (N_B_LAYERS, N_Q_HEADS), 0.5),
        "rel_bias": nrm(ks[19], (N_BUCKETS, N_Q_HEADS), 0.3),
        "w_gate_up": nrm(ks[20], (DEPTH, D_MODEL, 2 * D_FF), D_MODEL ** -0.5),
        "w_down": nrm(ks[21], (DEPTH, D_FF, D_MODEL), D_FF ** -0.5),
    }


def reference(x, meta_tokens, norm_mix_pre, norm_mix_post, norm_ffn_pre, norm_ffn_post,
              pool_w, pool_scale, kv_norm, w_k, b_k, w_v, b_v, w_q, b_q, w_o, b_o,
              sinks, rel_bias, w_gate_up, w_down):
    B = x.shape[0]
    meta = jnp.broadcast_to(meta_tokens[None].astype(x.dtype), (B, N_META, D_MODEL))
    hs = jnp.concatenate([meta, x], axis=1)
    L = hs.shape[1]
    n_blocks = (L + PAD_FRONT) // BLOCK
    bias, valid = band_bias_and_mask(rel_bias, n_blocks)
    k_blk = None
    v_blk = None
    for layer in range(DEPTH):
        h = rms_norm(hs, norm_mix_pre[layer])
        if layer < N_A_LAYERS:
            mix = pool_mixer(h, pool_w[layer], pool_scale[layer])
        else:
            j = layer - N_A_LAYERS
            mix = sliding_window_attention(h, k_blk, v_blk, w_q[j], b_q[j], w_o[j], b_o[j],
                                           sinks[j], bias, valid)
        hs = hs + rms_norm(mix, norm_mix_post[layer])
        h = rms_norm(hs, norm_ffn_pre[layer])
        hs = hs + rms_norm(swiglu(h, w_gate_up[layer], w_down[layer]), norm_ffn_post[layer])
        if layer == N_A_LAYERS - 1:
            kv_in = rms_norm(hs, kv_norm)
            k = (kv_in @ w_k + b_k).reshape(B, L, N_KV_HEADS, HEAD_DIM)
            v = (kv_in @ w_v + b_v).reshape(B, L, N_KV_HEADS, HEAD_DIM)
            k_blk = to_blocks_with_prev(k)
            v_blk = to_blocks_with_prev(v)
    return hs[:, N_META:]
```

```python
import functools
import math

import jax
import jax.numpy as jnp
from jax import lax
from jax.experimental import pallas as pl
from jax.experimental.pallas import tpu as pltpu

D_MODEL = 1024
N_META = 16
POOL_WINDOWS = (2, 4, 8, 16)
POOL_GROUP_DIM = D_MODEL // len(POOL_WINDOWS)
HEAD_DIM = 64
N_Q_HEADS = D_MODEL // HEAD_DIM
N_KV_HEADS = 2
HEADS_PER_KV = N_Q_HEADS // N_KV_HEADS
N_HEAD_PAIRS = N_Q_HEADS // 2
KV_DIM = N_KV_HEADS * HEAD_DIM
WINDOW = 128
BLOCK = 128
N_BUCKETS = 32
MAX_DISTANCE = 128
D_FF = 2816
EPS = 1e-6
PAD_FRONT = (-N_META) % BLOCK

HALO = 16
FFN_COL_CHUNKS = ((0, 1024), (1024, 1024), (2048, 768))
ROW_CHUNK = 32
POOL_ROW_CHUNK = 64
TILE_A = 512
TILE_B = 256
V7X_VMEM_LIMIT_BYTES = 56 * 1024 * 1024

G_MIX_PRE, G_MIX_POST, G_POOL_SCALE, G_FFN_PRE, G_FFN_POST, G_KV = range(6)


def _rms(x, g):
    ms = jnp.sum(x * x, axis=-1, keepdims=True) * (1.0 / D_MODEL)
    return x * lax.rsqrt(ms + EPS) * g


def _row_loop(n_rows, chunk, body):
    chunk = min(chunk, n_rows)
    n = n_rows // chunk
    if n == 1:
        body(pl.ds(0, chunk))
        return

    def step(i, carry):
        body(pl.ds(pl.multiple_of(i * chunk, chunk), chunk))
        return carry

    lax.fori_loop(0, n, step, 0)


def _shifted(rows, offset):
    start = rows.start + offset
    if not isinstance(start, int):
        start = pl.multiple_of(start, math.gcd(offset, rows.size))
    return pl.ds(start, rows.size)


def _swiglu(h_ref, wgu_ref, wd_ref, gbuf, ubuf, act, out_ref, tm):
    for c0, cw in FFN_COL_CHUNKS:
        gbuf[:, :cw] = jnp.dot(h_ref[...], wgu_ref[:, c0:c0 + cw], preferred_element_type=jnp.float32)
        ubuf[:, :cw] = jnp.dot(h_ref[...], wgu_ref[:, D_FF + c0:D_FF + c0 + cw],
                               preferred_element_type=jnp.float32)

        def silu_rows(rows, c0=c0, cw=cw):
            g = gbuf[rows, :cw]
            u = ubuf[rows, :cw]
            act[rows, c0:c0 + cw] = (g * (1.0 / (1.0 + jnp.exp(-g))) * u).astype(jnp.bfloat16)

        _row_loop(tm, ROW_CHUNK, silu_rows)
    out_ref[...] = jnp.dot(act[...], wd_ref[...], preferred_element_type=jnp.float32)


def _layer_a_kernel(x_ref, prev_ref, meta_ref, gains_ref, pool_w_ref, wgu_ref, wd_ref, wkv_ref, bkv_ref,
                    hs_out_ref, kv_out_ref,
                    hext, pbuf, ybuf, hs1, hb, gbuf, ubuf, act, *, tm, tiles_per_batch, is_meta):
    g_mix_pre = gains_ref[G_MIX_PRE:G_MIX_PRE + 1, :]
    g_mix_post = gains_ref[G_MIX_POST:G_MIX_POST + 1, :]
    pool_scale = gains_ref[G_POOL_SCALE:G_POOL_SCALE + 1, :]
    g_ffn_pre = gains_ref[G_FFN_PRE:G_FFN_PRE + 1, :]
    g_ffn_post = gains_ref[G_FFN_POST:G_FFN_POST + 1, :]
    g_kv = gains_ref[G_KV:G_KV + 1, :]

    if is_meta:
        hext[0:HALO, :] = jnp.zeros((HALO, D_MODEL), jnp.float32)
    else:
        first_of_batch = (pl.program_id(0) % tiles_per_batch) == 0
        prev = jnp.where(first_of_batch, meta_ref[...], prev_ref[...])
        hext[0:HALO, :] = _rms(prev, g_mix_pre)

    def norm_rows(rows):
        hext[_shifted(rows, HALO), :] = _rms(x_ref[rows, :], g_mix_pre)

    _row_loop(tm, ROW_CHUNK, norm_rows)

    def pool_rows(rows):
        for gi, w in enumerate(POOL_WINDOWS):
            cols = slice(gi * POOL_GROUP_DIM, (gi + 1) * POOL_GROUP_DIM)
            e0 = hext[pl.ds(rows.start, rows.size + HALO), cols]
            e = e0
            shift = 1
            while shift < w:
                e = e + pltpu.roll(e, shift, 0)
                shift *= 2
            win = e[HALO:, :]
            if is_meta:
                pos = lax.broadcasted_iota(jnp.int32, win.shape, 0) + 1
                cnt = jnp.minimum(pos, w).astype(jnp.float32)
                pooled = win / cnt - e0[HALO:, :]
            else:
                pooled = win * (1.0 / w) - e0[HALO:, :]
            pbuf[rows, cols] = pooled.astype(jnp.bfloat16)

    _row_loop(tm, POOL_ROW_CHUNK, pool_rows)

    for gi in range(len(POOL_WINDOWS)):
        cols = slice(gi * POOL_GROUP_DIM, (gi + 1) * POOL_GROUP_DIM)
        ybuf[:, cols] = jnp.dot(pbuf[:, cols], pool_w_ref[gi], preferred_element_type=jnp.float32)

    def post_mix_rows(rows):
        h1 = x_ref[rows, :] + _rms(ybuf[rows, :] * pool_scale, g_mix_post)
        hs1[rows, :] = h1
        hb[rows, :] = _rms(h1, g_ffn_pre).astype(jnp.bfloat16)

    _row_loop(tm, ROW_CHUNK, post_mix_rows)

    _swiglu(hb, wgu_ref, wd_ref, gbuf, ubuf, act, ybuf, tm)

    def post_ffn_rows(rows):
        h2 = hs1[rows, :] + _rms(ybuf[rows, :], g_ffn_post)
        hs_out_ref[rows, :] = h2
        hb[rows, :] = _rms(h2, g_kv).astype(jnp.bfloat16)

    _row_loop(tm, ROW_CHUNK, post_ffn_rows)

    kv = jnp.dot(hb[...], wkv_ref[...], preferred_element_type=jnp.float32) + bkv_ref[...]
    kv_out_ref[...] = kv.astype(jnp.bfloat16)


def _const_spec(shape):
    nd = len(shape)
    return pl.BlockSpec(shape, lambda *_: (0,) * nd, pipeline_mode=pl.Buffered(1))


def _layer_a_call(x2d, meta, gains, pool_w, wgu, wd, wkv, bkv, *, tm, tiles_per_batch, is_meta):
    n_rows = x2d.shape[0]
    n_tiles = n_rows // tm
    halo_blocks_per_tile = tm // HALO
    kern = functools.partial(_layer_a_kernel, tm=tm, tiles_per_batch=tiles_per_batch, is_meta=is_meta)
    return pl.pallas_call(
        kern,
        grid=(n_tiles,),
        in_specs=[
            pl.BlockSpec((tm, D_MODEL), lambda t: (t, 0)),
            pl.BlockSpec((HALO, D_MODEL), lambda t: (jnp.maximum(t * halo_blocks_per_tile - 1, 0), 0)),
            _const_spec((N_META, D_MODEL)),
            _const_spec(gains.shape),
            _const_spec(pool_w.shape),
            _const_spec(wgu.shape),
            _const_spec(wd.shape),
            _const_spec(wkv.shape),
            _const_spec(bkv.shape),
        ],
        out_specs=[
            pl.BlockSpec((tm, D_MODEL), lambda t: (t, 0)),
            pl.BlockSpec((tm, 2 * KV_DIM), lambda t: (t, 0)),
        ],
        out_shape=[
            jax.ShapeDtypeStruct((n_rows, D_MODEL), jnp.float32),
            jax.ShapeDtypeStruct((n_rows, 2 * KV_DIM), jnp.bfloat16),
        ],
        scratch_shapes=[
            pltpu.VMEM((tm + HALO, D_MODEL), jnp.float32),
            pltpu.VMEM((tm, D_MODEL), jnp.bfloat16),
            pltpu.VMEM((tm, D_MODEL), jnp.float32),
            pltpu.VMEM((tm, D_MODEL), jnp.float32),
            pltpu.VMEM((tm, D_MODEL), jnp.bfloat16),
            pltpu.VMEM((tm, FFN_COL_CHUNKS[0][1]), jnp.float32),
            pltpu.VMEM((tm, FFN_COL_CHUNKS[0][1]), jnp.float32),
            pltpu.VMEM((tm, D_FF), jnp.bfloat16),
        ],
        compiler_params=pltpu.CompilerParams(
            dimension_semantics=("arbitrary",), vmem_limit_bytes=V7X_VMEM_LIMIT_BYTES),
        name="layer_a_meta" if is_meta else "layer_a",
    )(x2d, x2d, meta, gains, pool_w, wgu, wd, wkv, bkv)


def _bias_table_kernel(bucket_ref, rel_bias_ref, out_ref):
    m = pl.program_id(0)
    bucket = bucket_ref[...]
    q = lax.broadcasted_iota(jnp.int32, (BLOCK, 2 * BLOCK), 0)
    s = lax.broadcasted_iota(jnp.int32, (BLOCK, 2 * BLOCK), 1)
    d = q + BLOCK - s
    in_window = (d >= 0) & (d < WINDOW)
    valid = (in_window & (s >= PAD_FRONT), in_window)
    for half in range(2):
        h = 2 * m + half
        acc = jnp.zeros((BLOCK, 2 * BLOCK), jnp.float32)
        for b in range(N_BUCKETS):
            acc = jnp.where(bucket == b, rel_bias_ref[b, h], acc)
        for jsel in range(2):
            out_ref[jsel, 0, :, half * 2 * BLOCK:(half + 1) * 2 * BLOCK] = jnp.where(valid[jsel], acc, -jnp.inf)


def _bias_table_call(bucket, rel_bias):
    return pl.pallas_call(
        _bias_table_kernel,
        grid=(N_HEAD_PAIRS,),
        in_specs=[
            pl.BlockSpec((BLOCK, 2 * BLOCK), lambda m: (0, 0)),
            pl.BlockSpec(memory_space=pltpu.SMEM),
        ],
        out_specs=pl.BlockSpec((2, 1, BLOCK, 4 * BLOCK), lambda m: (0, m, 0, 0)),
        out_shape=jax.ShapeDtypeStruct((2, N_HEAD_PAIRS, BLOCK, 4 * BLOCK), jnp.float32),
        compiler_params=pltpu.CompilerParams(dimension_semantics=("arbitrary",)),
        name="bias_table",
    )(bucket, rel_bias)


def _layer_b_kernel(hs_ref, kv_ref, metakv_ref, bias_ref, sinks_ref, gains_ref,
                    wq_ref, bq_ref, wo_ref, bo_ref, wgu_ref, wd_ref,
                    out_ref,
                    hb, qbuf, obuf, ybuf, hs3, gbuf, ubuf, act, *, tm):
    g_mix_pre = gains_ref[G_MIX_PRE:G_MIX_PRE + 1, :]
    g_mix_post = gains_ref[G_MIX_POST:G_MIX_POST + 1, :]
    g_ffn_pre = gains_ref[G_FFN_PRE:G_FFN_PRE + 1, :]
    g_ffn_post = gains_ref[G_FFN_POST:G_FFN_POST + 1, :]
    blocks_per_tile = tm // BLOCK
    t = pl.program_id(1)

    def norm_rows(rows):
        hb[rows, :] = _rms(hs_ref[rows, :], g_mix_pre).astype(jnp.bfloat16)

    _row_loop(tm, ROW_CHUNK, norm_rows)

    q = jnp.dot(hb[...], wq_ref[...], preferred_element_type=jnp.float32) + bq_ref[...]
    qbuf[...] = q.astype(jnp.bfloat16)

    lane = lax.broadcasted_iota(jnp.int32, (2 * BLOCK, 2 * HEAD_DIM), 1)
    low_half = lane < HEAD_DIM
    zero = jnp.zeros((2 * BLOCK, 2 * HEAD_DIM), jnp.bfloat16)

    def per_kv_head_operands(pair_cols):
        swapped = jnp.concatenate([pair_cols[:, HEAD_DIM:], pair_cols[:, :HEAD_DIM]], axis=1)
        g0 = jnp.concatenate([jnp.where(low_half, pair_cols, zero), jnp.where(low_half, zero, swapped)], axis=0)
        g1 = jnp.concatenate([jnp.where(low_half, swapped, zero), jnp.where(low_half, zero, pair_cols)], axis=0)
        return g0, g1

    for jb in range(blocks_per_tile):
        j = t * blocks_per_tile + jb
        own = kv_ref[pl.ds(pl.multiple_of(j * BLOCK, BLOCK), BLOCK), :]
        prev_start = pl.multiple_of(jnp.maximum(j - 1, 0) * BLOCK, BLOCK)
        prev = jnp.where(j == 0, metakv_ref[...], kv_ref[pl.ds(prev_start, BLOCK), :])
        kvb = jnp.concatenate([prev, own], axis=0)
        k_ops = per_kv_head_operands(kvb[:, :KV_DIM])
        v_ops = per_kv_head_operands(kvb[:, KV_DIM:])
        jsel = jnp.minimum(j, 1)
        rows = slice(jb * BLOCK, (jb + 1) * BLOCK)
        for m in range(N_HEAD_PAIRS):
            g = (2 * m) // HEADS_PER_KV
            cols = slice(m * 2 * HEAD_DIM, (m + 1) * 2 * HEAD_DIM)
            s = lax.dot_general(qbuf[rows, cols], k_ops[g], (((1,), (1,)), ((), ())),
                                preferred_element_type=jnp.float32)
            logits = s * (HEAD_DIM ** -0.5) + bias_ref[jsel, m]
            probs = []
            for half in range(2):
                lg = logits[:, half * 2 * BLOCK:(half + 1) * 2 * BLOCK]
                sink = sinks_ref[2 * m + half]
                mx = jnp.maximum(jnp.max(lg, axis=-1, keepdims=True), sink)
                p = jnp.exp(lg - mx)
                denom = jnp.sum(p, axis=-1, keepdims=True) + jnp.exp(sink - mx)
                probs.append((p * (1.0 / denom)).astype(jnp.bfloat16))
            o = jnp.dot(jnp.concatenate(probs, axis=1), v_ops[g], preferred_element_type=jnp.float32)
            obuf[rows, cols] = o.astype(jnp.bfloat16)

    ybuf[...] = jnp.dot(obuf[...], wo_ref[...], preferred_element_type=jnp.float32) + bo_ref[...]

    def post_mix_rows(rows):
        h3 = hs_ref[rows, :] + _rms(ybuf[rows, :], g_mix_post)
        hs3[rows, :] = h3
        hb[rows, :] = _rms(h3, g_ffn_pre).astype(jnp.bfloat16)

    _row_loop(tm, ROW_CHUNK, post_mix_rows)

    _swiglu(hb, wgu_ref, wd_ref, gbuf, ubuf, act, ybuf, tm)

    def post_ffn_rows(rows):
        out_ref[rows, :] = hs3[rows, :] + _rms(ybuf[rows, :], g_ffn_post)

    _row_loop(tm, ROW_CHUNK, post_ffn_rows)


def _layer_b_call(hs, kv, metakv, bias, sinks, gains, wq, bq, wo, bo, wgu, wd, *, tm):
    batch, seq, _ = hs.shape
    kern = functools.partial(_layer_b_kernel, tm=tm)
    return pl.pallas_call(
        kern,
        grid=(batch, seq // tm),
        in_specs=[
            pl.BlockSpec((None, tm, D_MODEL), lambda b, t: (b, t, 0)),
            pl.BlockSpec((None, seq, 2 * KV_DIM), lambda b, t: (b, 0, 0)),
            _const_spec(metakv.shape),
            _const_spec(bias.shape),
            pl.BlockSpec(memory_space=pltpu.SMEM),
            _const_spec(gains.shape),
            _const_spec(wq.shape),
            _const_spec(bq.shape),
            _const_spec(wo.shape),
            _const_spec(bo.shape),
            _const_spec(wgu.shape),
            _const_spec(wd.shape),
        ],
        out_specs=pl.BlockSpec((None, tm, D_MODEL), lambda b, t: (b, t, 0)),
        out_shape=jax.ShapeDtypeStruct((batch, seq, D_MODEL), jnp.float32),
        scratch_shapes=[
            pltpu.VMEM((tm, D_MODEL), jnp.bfloat16),
            pltpu.VMEM((tm, D_MODEL), jnp.bfloat16),
            pltpu.VMEM((tm, D_MODEL), jnp.bfloat16),
            pltpu.VMEM((tm, D_MODEL), jnp.float32),
            pltpu.VMEM((tm, D_MODEL), jnp.float32),
            pltpu.VMEM((tm, FFN_COL_CHUNKS[0][1]), jnp.float32),
            pltpu.VMEM((tm, FFN_COL_CHUNKS[0][1]), jnp.float32),
            pltpu.VMEM((tm, D_FF), jnp.bfloat16),
        ],
        compiler_params=pltpu.CompilerParams(
            dimension_semantics=("arbitrary", "arbitrary"), vmem_limit_bytes=V7X_VMEM_LIMIT_BYTES),
        name="layer_b",
    )(hs, kv, metakv, bias, sinks, gains, wq, bq, wo, bo, wgu, wd)


def _t5_bucket_table():
    q = jnp.arange(BLOCK)[:, None]
    s = jnp.arange(2 * BLOCK)[None, :]
    d = jnp.maximum(q + BLOCK - s, 0)
    max_exact = N_BUCKETS // 2
    df = jnp.maximum(d, 1).astype(jnp.float32)
    large = max_exact + (jnp.log(df / max_exact) / math.log(MAX_DISTANCE / max_exact)
                         * (N_BUCKETS - max_exact)).astype(jnp.int32)
    large = jnp.minimum(large, N_BUCKETS - 1)
    return jnp.where(d < max_exact, d, large).astype(jnp.int32)


def _gain_table(rows):
    table = jnp.zeros((8, D_MODEL), jnp.float32)
    for i, r in rows.items():
        table = table.at[i].set(r.astype(jnp.float32))
    return table


def kernel(x, meta_tokens, norm_mix_pre, norm_mix_post, norm_ffn_pre, norm_ffn_post, pool_w, pool_scale, kv_norm, w_k, b_k, w_v, b_v, w_q, b_q, w_o, b_o, sinks, rel_bias, w_gate_up, w_down):
    batch, seq, _ = x.shape
    bf16 = jnp.bfloat16

    gains_a = _gain_table({G_MIX_PRE: norm_mix_pre[0], G_MIX_POST: norm_mix_post[0], G_POOL_SCALE: pool_scale[0],
                           G_FFN_PRE: norm_ffn_pre[0], G_FFN_POST: norm_ffn_post[0], G_KV: kv_norm})
    gains_b = _gain_table({G_MIX_PRE: norm_mix_pre[1], G_MIX_POST: norm_mix_post[1],
                           G_FFN_PRE: norm_ffn_pre[1], G_FFN_POST: norm_ffn_post[1]})
    wkv = jnp.concatenate([w_k, w_v], axis=1).astype(bf16)
    bkv = jnp.concatenate([b_k, b_v])[None, :]
    layer_a_weights = (gains_a, pool_w[0].astype(bf16), w_gate_up[0].astype(bf16), w_down[0].astype(bf16), wkv, bkv)

    x2d = x.reshape(batch * seq, D_MODEL)
    hs2, kv_x = _layer_a_call(x2d, meta_tokens, *layer_a_weights,
                              tm=TILE_A, tiles_per_batch=seq // TILE_A, is_meta=False)
    _, kv_meta = _layer_a_call(meta_tokens, meta_tokens, *layer_a_weights,
                               tm=N_META, tiles_per_batch=1, is_meta=True)
    metakv = jnp.concatenate([jnp.zeros((PAD_FRONT, 2 * KV_DIM), bf16), kv_meta], axis=0)

    bias = _bias_table_call(_t5_bucket_table(), rel_bias)

    out = _layer_b_call(hs2.reshape(batch, seq, D_MODEL), kv_x.reshape(batch, seq, 2 * KV_DIM), metakv, bias,
                        sinks[0], gains_b, w_q[0].astype(bf16), b_q, w_o[0].astype(bf16), b_o,
                        w_gate_up[1].astype(bf16), w_down[1].astype(bf16), tm=TILE_B)
    return out
```

```python
import functools
import math

import jax
import jax.numpy as jnp
from jax import lax
from jax.experimental import pallas as pl
from jax.experimental.pallas import tpu as pltpu

D_MODEL = 1024
N_META = 16
POOL_WINDOWS = (2, 4, 8, 16)
POOL_GROUP_DIM = D_MODEL // len(POOL_WINDOWS)
HEAD_DIM = 64
N_Q_HEADS = D_MODEL // HEAD_DIM
N_KV_HEADS = 2
HEADS_PER_KV = N_Q_HEADS // N_KV_HEADS
N_HEAD_PAIRS = N_Q_HEADS // 2
KV_DIM = N_KV_HEADS * HEAD_DIM
WINDOW = 128
BLOCK = 128
N_BUCKETS = 32
MAX_DISTANCE = 128
D_FF = 2816
EPS = 1e-6
PAD_FRONT = (-N_META) % BLOCK

HALO = 16
FFN_COL_CHUNKS = ((0, 1024), (1024, 1024), (2048, 768))
ROW_CHUNK = 32
POOL_ROW_CHUNK = 64
TILE_A = 512
TILE_B = 256
V7X_VMEM_LIMIT_BYTES = 56 * 1024 * 1024

G_MIX_PRE, G_MIX_POST, G_POOL_SCALE, G_FFN_PRE, G_FFN_POST, G_KV = range(6)


def _rms(x, g):
    ms = jnp.sum(x * x, axis=-1, keepdims=True) * (1.0 / D_MODEL)
    return x * lax.rsqrt(ms + EPS) * g


def _row_loop(n_rows, chunk, body):
    chunk = min(chunk, n_rows)
    for i in range(n_rows // chunk):
        body(pl.ds(i * chunk, chunk))


def _shifted(rows, offset):
    start = rows.start + offset
    if not isinstance(start, int):
        start = pl.multiple_of(start, math.gcd(offset, rows.size))
    return pl.ds(start, rows.size)


def _swiglu(h_ref, wgu_ref, wd_ref, gbuf, ubuf, act, out_ref, tm):
    for c0, cw in FFN_COL_CHUNKS:
        gbuf[:, :cw] = jnp.dot(h_ref[...], wgu_ref[:, c0:c0 + cw], preferred_element_type=jnp.float32)
        ubuf[:, :cw] = jnp.dot(h_ref[...], wgu_ref[:, D_FF + c0:D_FF + c0 + cw],
                               preferred_element_type=jnp.float32)

        def silu_rows(rows, c0=c0, cw=cw):
            g = gbuf[rows, :cw]
            u = ubuf[rows, :cw]
            act[rows, c0:c0 + cw] = (g * (1.0 / (1.0 + jnp.exp(-g))) * u).astype(jnp.bfloat16)

        _row_loop(tm, ROW_CHUNK, silu_rows)
    out_ref[...] = jnp.dot(act[...], wd_ref[...], preferred_element_type=jnp.float32)


def _layer_a_kernel(x_ref, prev_ref, meta_ref, gains_ref, pool_w_ref, wgu_ref, wd_ref, wkv_ref, bkv_ref,
                    hs_out_ref, kv_out_ref,
                    hext, pbuf, ybuf, hs1, hb, gbuf, ubuf, act, *, tm, tiles_per_batch, is_meta):
    g_mix_pre = gains_ref[G_MIX_PRE:G_MIX_PRE + 1, :]
    g_mix_post = gains_ref[G_MIX_POST:G_MIX_POST + 1, :]
    pool_scale = gains_ref[G_POOL_SCALE:G_POOL_SCALE + 1, :]
    g_ffn_pre = gains_ref[G_FFN_PRE:G_FFN_PRE + 1, :]
    g_ffn_post = gains_ref[G_FFN_POST:G_FFN_POST + 1, :]
    g_kv = gains_ref[G_KV:G_KV + 1, :]

    if is_meta:
        hext[0:HALO, :] = jnp.zeros((HALO, D_MODEL), jnp.float32)
    else:
        first_of_batch = (pl.program_id(0) % tiles_per_batch) == 0
        prev = jnp.where(first_of_batch, meta_ref[...], prev_ref[...])
        hext[0:HALO, :] = _rms(prev, g_mix_pre)

    def norm_rows(rows):
        hext[_shifted(rows, HALO), :] = _rms(x_ref[rows, :], g_mix_pre)

    _row_loop(tm, ROW_CHUNK, norm_rows)

    def pool_rows(rows):
        for gi, w in enumerate(POOL_WINDOWS):
            cols = slice(gi * POOL_GROUP_DIM, (gi + 1) * POOL_GROUP_DIM)
            e0 = hext[pl.ds(rows.start, rows.size + HALO), cols]
            e = e0
            shift = 1
            while shift < w:
                e = e + pltpu.roll(e, shift, 0)
                shift *= 2
            win = e[HALO:, :]
            if is_meta:
                pos = lax.broadcasted_iota(jnp.int32, win.shape, 0) + 1
                cnt = jnp.minimum(pos, w).astype(jnp.float32)
                pooled = win / cnt - e0[HALO:, :]
            else:
                pooled = win * (1.0 / w) - e0[HALO:, :]
            pbuf[rows, cols] = pooled.astype(jnp.bfloat16)

    _row_loop(tm, POOL_ROW_CHUNK, pool_rows)

    for gi in range(len(POOL_WINDOWS)):
        cols = slice(gi * POOL_GROUP_DIM, (gi + 1) * POOL_GROUP_DIM)
        ybuf[:, cols] = jnp.dot(pbuf[:, cols], pool_w_ref[gi], preferred_element_type=jnp.float32)

    def post_mix_rows(rows):
        h1 = x_ref[rows, :] + _rms(ybuf[rows, :] * pool_scale, g_mix_post)
        hs1[rows, :] = h1
        hb[rows, :] = _rms(h1, g_ffn_pre).astype(jnp.bfloat16)

    _row_loop(tm, ROW_CHUNK, post_mix_rows)

    _swiglu(hb, wgu_ref, wd_ref, gbuf, ubuf, act, ybuf, tm)

    def post_ffn_rows(rows):
        h2 = hs1[rows, :] + _rms(ybuf[rows, :], g_ffn_post)
        hs_out_ref[rows, :] = h2
        hb[rows, :] = _rms(h2, g_kv).astype(jnp.bfloat16)

    _row_loop(tm, ROW_CHUNK, post_ffn_rows)

    kv = jnp.dot(hb[...], wkv_ref[...], preferred_element_type=jnp.float32) + bkv_ref[...]
    kv_out_ref[...] = kv.astype(jnp.bfloat16)


def _const_spec(shape):
    nd = len(shape)
    return pl.BlockSpec(shape, lambda *_: (0,) * nd, pipeline_mode=pl.Buffered(1))


def _layer_a_call(x2d, meta, gains, pool_w, wgu, wd, wkv, bkv, *, tm, tiles_per_batch, is_meta):
    n_rows = x2d.shape[0]
    n_tiles = n_rows // tm
    halo_blocks_per_tile = tm // HALO
    kern = functools.partial(_layer_a_kernel, tm=tm, tiles_per_batch=tiles_per_batch, is_meta=is_meta)
    return pl.pallas_call(
        kern,
        grid=(n_tiles,),
        in_specs=[
            pl.BlockSpec((tm, D_MODEL), lambda t: (t, 0)),
            pl.BlockSpec((HALO, D_MODEL), lambda t: (jnp.maximum(t * halo_blocks_per_tile - 1, 0), 0)),
            _const_spec((N_META, D_MODEL)),
            _const_spec(gains.shape),
            _const_spec(pool_w.shape),
            _const_spec(wgu.shape),
            _const_spec(wd.shape),
            _const_spec(wkv.shape),
            _const_spec(bkv.shape),
        ],
        out_specs=[
            pl.BlockSpec((tm, D_MODEL), lambda t: (t, 0)),
            pl.BlockSpec((tm, 2 * KV_DIM), lambda t: (t, 0)),
        ],
        out_shape=[
            jax.ShapeDtypeStruct((n_rows, D_MODEL), jnp.float32),
            jax.ShapeDtypeStruct((n_rows, 2 * KV_DIM), jnp.bfloat16),
        ],
        scratch_shapes=[
            pltpu.VMEM((tm + HALO, D_MODEL), jnp.float32),
            pltpu.VMEM((tm, D_MODEL), jnp.bfloat16),
            pltpu.VMEM((tm, D_MODEL), jnp.float32),
            pltpu.VMEM((tm, D_MODEL), jnp.float32),
            pltpu.VMEM((tm, D_MODEL), jnp.bfloat16),
            pltpu.VMEM((tm, FFN_COL_CHUNKS[0][1]), jnp.float32),
            pltpu.VMEM((tm, FFN_COL_CHUNKS[0][1]), jnp.float32),
            pltpu.VMEM((tm, D_FF), jnp.bfloat16),
        ],
        compiler_params=pltpu.CompilerParams(
            dimension_semantics=("arbitrary",), vmem_limit_bytes=V7X_VMEM_LIMIT_BYTES),
        name="layer_a_meta" if is_meta else "layer_a",
    )(x2d, x2d, meta, gains, pool_w, wgu, wd, wkv, bkv)


def _bias_table_kernel(bucket_ref, rel_bias_ref, out_ref):
    m = pl.program_id(0)
    bucket = bucket_ref[...]
    q = lax.broadcasted_iota(jnp.int32, (BLOCK, 2 * BLOCK), 0)
    s = lax.broadcasted_iota(jnp.int32, (BLOCK, 2 * BLOCK), 1)
    d = q + BLOCK - s
    in_window = (d >= 0) & (d < WINDOW)
    valid = (in_window & (s >= PAD_FRONT), in_window)
    for half in range(2):
        h = 2 * m + half
        acc = jnp.zeros((BLOCK, 2 * BLOCK), jnp.float32)
        for b in range(N_BUCKETS):
            acc = jnp.where(bucket == b, rel_bias_ref[b, h], acc)
        for jsel in range(2):
            out_ref[jsel, 0, :, half * 2 * BLOCK:(half + 1) * 2 * BLOCK] = jnp.where(valid[jsel], acc, -jnp.inf)


def _bias_table_call(bucket, rel_bias):
    return pl.pallas_call(
        _bias_table_kernel,
        grid=(N_HEAD_PAIRS,),
        in_specs=[
            pl.BlockSpec((BLOCK, 2 * BLOCK), lambda m: (0, 0)),
            pl.BlockSpec(memory_space=pltpu.SMEM),
        ],
        out_specs=pl.BlockSpec((2, 1, BLOCK, 4 * BLOCK), lambda m: (0, m, 0, 0)),
        out_shape=jax.ShapeDtypeStruct((2, N_HEAD_PAIRS, BLOCK, 4 * BLOCK), jnp.float32),
        compiler_params=pltpu.CompilerParams(dimension_semantics=("arbitrary",)),
        name="bias_table",
    )(bucket, rel_bias)


def _layer_b_kernel(hs_ref, kv_ref, metakv_ref, bias_ref, sinks_ref, gains_ref,
                    wq_ref, bq_ref, wo_ref, bo_ref, wgu_ref, wd_ref,
                    out_ref,
                    hb, qbuf, obuf, ybuf, hs3, gbuf, ubuf, act, *, tm):
    g_mix_pre = gains_ref[G_MIX_PRE:G_MIX_PRE + 1, :]
    g_mix_post = gains_ref[G_MIX_POST:G_MIX_POST + 1, :]
    g_ffn_pre = gains_ref[G_FFN_PRE:G_FFN_PRE + 1, :]
    g_ffn_post = gains_ref[G_FFN_POST:G_FFN_POST + 1, :]
    blocks_per_tile = tm // BLOCK
    t = pl.program_id(1)

    def norm_rows(rows):
        hb[rows, :] = _rms(hs_ref[rows, :], g_mix_pre).astype(jnp.bfloat16)

    _row_loop(tm, ROW_CHUNK, norm_rows)

    q = jnp.dot(hb[...], wq_ref[...], preferred_element_type=jnp.float32) + bq_ref[...]
    qbuf[...] = q.astype(jnp.bfloat16)

    lane = lax.broadcasted_iota(jnp.int32, (2 * BLOCK, 2 * HEAD_DIM), 1)
    low_half = lane < HEAD_DIM
    zero = jnp.zeros((2 * BLOCK, 2 * HEAD_DIM), jnp.bfloat16)

    def per_kv_head_operands(pair_cols):
        swapped = jnp.concatenate([pair_cols[:, HEAD_DIM:], pair_cols[:, :HEAD_DIM]], axis=1)
        g0 = jnp.concatenate([jnp.where(low_half, pair_cols, zero), jnp.where(low_half, zero, swapped)], axis=0)
        g1 = jnp.concatenate([jnp.where(low_half, swapped, zero), jnp.where(low_half, zero, pair_cols)], axis=0)
        return g0, g1

    for jb in range(blocks_per_tile):
        j = t * blocks_per_tile + jb
        own = kv_ref[pl.ds(pl.multiple_of(j * BLOCK, BLOCK), BLOCK), :]
        prev_start = pl.multiple_of(jnp.maximum(j - 1, 0) * BLOCK, BLOCK)
        prev = jnp.where(j == 0, metakv_ref[...], kv_ref[pl.ds(prev_start, BLOCK), :])
        kvb = jnp.concatenate([prev, own], axis=0)
        k_ops = per_kv_head_operands(kvb[:, :KV_DIM])
        v_ops = per_kv_head_operands(kvb[:, KV_DIM:])
        jsel = jnp.minimum(j, 1)
        rows = slice(jb * BLOCK, (jb + 1) * BLOCK)
        for m in range(N_HEAD_PAIRS):
            g = (2 * m) // HEADS_PER_KV
            cols = slice(m * 2 * HEAD_DIM, (m + 1) * 2 * HEAD_DIM)
            s = lax.dot_general(qbuf[rows, cols], k_ops[g], (((1,), (1,)), ((), ())),
                                preferred_element_type=jnp.float32)
            logits = s * (HEAD_DIM ** -0.5) + bias_ref[jsel, m]
            probs = []
            for half in range(2):
                lg = logits[:, half * 2 * BLOCK:(half + 1) * 2 * BLOCK]
                sink = sinks_ref[2 * m + half]
                mx = jnp.maximum(jnp.max(lg, axis=-1, keepdims=True), sink)
                p = jnp.exp(lg - mx)
                denom = jnp.sum(p, axis=-1, keepdims=True) + jnp.exp(sink - mx)
                probs.append((p * (1.0 / denom)).astype(jnp.bfloat16))
            o = jnp.dot(jnp.concatenate(probs, axis=1), v_ops[g], preferred_element_type=jnp.float32)
            obuf[rows, cols] = o.astype(jnp.bfloat16)

    ybuf[...] = jnp.dot(obuf[...], wo_ref[...], preferred_element_type=jnp.float32) + bo_ref[...]

    def post_mix_rows(rows):
        h3 = hs_ref[rows, :] + _rms(ybuf[rows, :], g_mix_post)
        hs3[rows, :] = h3
        hb[rows, :] = _rms(h3, g_ffn_pre).astype(jnp.bfloat16)

    _row_loop(tm, ROW_CHUNK, post_mix_rows)

    _swiglu(hb, wgu_ref, wd_ref, gbuf, ubuf, act, ybuf, tm)

    def post_ffn_rows(rows):
        out_ref[rows, :] = hs3[rows, :] + _rms(ybuf[rows, :], g_ffn_post)

    _row_loop(tm, ROW_CHUNK, post_ffn_rows)


def _layer_b_call(hs, kv, metakv, bias, sinks, gains, wq, bq, wo, bo, wgu, wd, *, tm):
    batch, seq, _ = hs.shape
    kern = functools.partial(_layer_b_kernel, tm=tm)
    return pl.pallas_call(
        kern,
        grid=(batch, seq // tm),
        in_specs=[
            pl.BlockSpec((None, tm, D_MODEL), lambda b, t: (b, t, 0)),
            pl.BlockSpec((None, seq, 2 * KV_DIM), lambda b, t: (b, 0, 0)),
            _const_spec(metakv.shape),
            _const_spec(bias.shape),
            pl.BlockSpec(memory_space=pltpu.SMEM),
            _const_spec(gains.shape),
            _const_spec(wq.shape),
            _const_spec(bq.shape),
            _const_spec(wo.shape),
            _const_spec(bo.shape),
            _const_spec(wgu.shape),
            _const_spec(wd.shape),
        ],
        out_specs=pl.BlockSpec((None, tm, D_MODEL), lambda b, t: (b, t, 0)),
        out_shape=jax.ShapeDtypeStruct((batch, seq, D_MODEL), jnp.float32),
        scratch_shapes=[
            pltpu.VMEM((tm, D_MODEL), jnp.bfloat16),
            pltpu.VMEM((tm, D_MODEL), jnp.bfloat16),
            pltpu.VMEM((tm, D_MODEL), jnp.bfloat16),
            pltpu.VMEM((tm, D_MODEL), jnp.float32),
            pltpu.VMEM((tm, D_MODEL), jnp.float32),
            pltpu.VMEM((tm, FFN_COL_CHUNKS[0][1]), jnp.float32),
            pltpu.VMEM((tm, FFN_COL_CHUNKS[0][1]), jnp.float32),
            pltpu.VMEM((tm, D_FF), jnp.bfloat16),
        ],
        compiler_params=pltpu.CompilerParams(
            dimension_semantics=("arbitrary", "arbitrary"), vmem_limit_bytes=V7X_VMEM_LIMIT_BYTES),
        name="layer_b",
    )(hs, kv, metakv, bias, sinks, gains, wq, bq, wo, bo, wgu, wd)


def _t5_bucket_table():
    q = jnp.arange(BLOCK)[:, None]
    s = jnp.arange(2 * BLOCK)[None, :]
    d = jnp.maximum(q + BLOCK - s, 0)
    max_exact = N_BUCKETS // 2
    df = jnp.maximum(d, 1).astype(jnp.float32)
    large = max_exact + (jnp.log(df / max_exact) / math.log(MAX_DISTANCE / max_exact)
                         * (N_BUCKETS - max_exact)).astype(jnp.int32)
    large = jnp.minimum(large, N_BUCKETS - 1)
    return jnp.where(d < max_exact, d, large).astype(jnp.int32)


def _gain_table(rows):
    table = jnp.zeros((8, D_MODEL), jnp.float32)
    for i, r in rows.items():
        table = table.at[i].set(r.astype(jnp.float32))
    return table


def kernel(x, meta_tokens, norm_mix_pre, norm_mix_post, norm_ffn_pre, norm_ffn_post, pool_w, pool_scale, kv_norm, w_k, b_k, w_v, b_v, w_q, b_q, w_o, b_o, sinks, rel_bias, w_gate_up, w_down):
    batch, seq, _ = x.shape
    bf16 = jnp.bfloat16

    gains_a = _gain_table({G_MIX_PRE: norm_mix_pre[0], G_MIX_POST: norm_mix_post[0], G_POOL_SCALE: pool_scale[0],
                           G_FFN_PRE: norm_ffn_pre[0], G_FFN_POST: norm_ffn_post[0], G_KV: kv_norm})
    gains_b = _gain_table({G_MIX_PRE: norm_mix_pre[1], G_MIX_POST: norm_mix_post[1],
                           G_FFN_PRE: norm_ffn_pre[1], G_FFN_POST: norm_ffn_post[1]})
    wkv = jnp.concatenate([w_k, w_v], axis=1).astype(bf16)
    bkv = jnp.concatenate([b_k, b_v])[None, :]
    layer_a_weights = (gains_a, pool_w[0].astype(bf16), w_gate_up[0].astype(bf16), w_down[0].astype(bf16), wkv, bkv)

    x2d = x.reshape(batch * seq, D_MODEL)
    hs2, kv_x = _layer_a_call(x2d, meta_tokens, *layer_a_weights,
                              tm=TILE_A, tiles_per_batch=seq // TILE_A, is_meta=False)
    _, kv_meta = _layer_a_call(meta_tokens, meta_tokens, *layer_a_weights,
                               tm=N_META, tiles_per_batch=1, is_meta=True)
    metakv = jnp.concatenate([jnp.zeros((PAD_FRONT, 2 * KV_DIM), bf16), kv_meta], axis=0)

    bias = _bias_table_call(_t5_bucket_table(), rel_bias)

    out = _layer_b_call(hs2.reshape(batch, seq, D_MODEL), kv_x.reshape(batch, seq, 2 * KV_DIM), metakv, bias,
                        sinks[0], gains_b, w_q[0].astype(bf16), b_q, w_o[0].astype(bf16), b_o,
                        w_gate_up[1].astype(bf16), w_down[1].astype(bf16), tm=TILE_B)
    return out
```

```python
import functools
import math

import jax
import jax.numpy as jnp
from jax import lax
from jax.experimental import pallas as pl
from jax.experimental.pallas import tpu as pltpu

D_MODEL = 1024
N_META = 16
POOL_WINDOWS = (2, 4, 8, 16)
POOL_GROUP_DIM = D_MODEL // len(POOL_WINDOWS)
HEAD_DIM = 64
N_Q_HEADS = D_MODEL // HEAD_DIM
N_KV_HEADS = 2
HEADS_PER_KV = N_Q_HEADS // N_KV_HEADS
PAIRS_PER_KV = HEADS_PER_KV // 2
N_HEAD_PAIRS = N_Q_HEADS // 2
KV_DIM = N_KV_HEADS * HEAD_DIM
WINDOW = 128
BLOCK = 128
N_BUCKETS = 32
MAX_DISTANCE = 128
D_FF = 2816
EPS = 1e-6
PAD_FRONT = (-N_META) % BLOCK

HALO = 16
FFN_COL_CHUNKS = ((0, 1024), (1024, 1024), (2048, 768))
ROW_CHUNK = 32
POOL_ROW_CHUNK = 64
ATT_ROWS = PAIRS_PER_KV * BLOCK
SOFTMAX_ROW_CHUNK = 16
TILE_A = 512
TILE_B = 256
V7X_VMEM_LIMIT_BYTES = 56 * 1024 * 1024

G_MIX_PRE, G_MIX_POST, G_POOL_SCALE, G_FFN_PRE, G_FFN_POST, G_KV = range(6)


def _rms(x, g):
    ms = jnp.sum(x * x, axis=-1, keepdims=True) * (1.0 / D_MODEL)
    return x * lax.rsqrt(ms + EPS) * g


def _row_loop(n_rows, chunk, body):
    chunk = min(chunk, n_rows)
    for i in range(n_rows // chunk):
        body(pl.ds(i * chunk, chunk))


def _shifted(rows, offset):
    start = rows.start + offset
    if not isinstance(start, int):
        start = pl.multiple_of(start, math.gcd(offset, rows.size))
    return pl.ds(start, rows.size)


def _swiglu(h_ref, wgu_ref, wd_ref, gbuf, ubuf, act, out_ref, tm):
    for c0, cw in FFN_COL_CHUNKS:
        gbuf[:, :cw] = jnp.dot(h_ref[...], wgu_ref[:, c0:c0 + cw], preferred_element_type=jnp.float32)
        ubuf[:, :cw] = jnp.dot(h_ref[...], wgu_ref[:, D_FF + c0:D_FF + c0 + cw],
                               preferred_element_type=jnp.float32)

        def silu_rows(rows, c0=c0, cw=cw):
            g = gbuf[rows, :cw]
            u = ubuf[rows, :cw]
            half_g = 0.5 * g
            act[rows, c0:c0 + cw] = ((half_g + half_g * jnp.tanh(half_g)) * u).astype(jnp.bfloat16)

        _row_loop(tm, ROW_CHUNK, silu_rows)
    out_ref[...] = jnp.dot(act[...], wd_ref[...], preferred_element_type=jnp.float32)


def _layer_a_kernel(x_ref, prev_ref, meta_ref, gains_ref, pool_w_ref, wgu_ref, wd_ref, wkv_ref, bkv_ref,
                    hs_out_ref, kv_out_ref,
                    hext, pbuf, ybuf, hs1, hb, gbuf, ubuf, act, *, tm, tiles_per_batch, is_meta):
    g_mix_pre = gains_ref[G_MIX_PRE:G_MIX_PRE + 1, :]
    g_mix_post = gains_ref[G_MIX_POST:G_MIX_POST + 1, :]
    pool_scale = gains_ref[G_POOL_SCALE:G_POOL_SCALE + 1, :]
    g_ffn_pre = gains_ref[G_FFN_PRE:G_FFN_PRE + 1, :]
    g_ffn_post = gains_ref[G_FFN_POST:G_FFN_POST + 1, :]
    g_kv = gains_ref[G_KV:G_KV + 1, :]

    if is_meta:
        hext[0:HALO, :] = jnp.zeros((HALO, D_MODEL), jnp.float32)
    else:
        first_of_batch = (pl.program_id(0) % tiles_per_batch) == 0
        prev = jnp.where(first_of_batch, meta_ref[...], prev_ref[...])
        hext[0:HALO, :] = _rms(prev, g_mix_pre)

    def norm_rows(rows):
        hext[_shifted(rows, HALO), :] = _rms(x_ref[rows, :], g_mix_pre)

    _row_loop(tm, ROW_CHUNK, norm_rows)

    def pool_rows(rows):
        for gi, w in enumerate(POOL_WINDOWS):
            cols = slice(gi * POOL_GROUP_DIM, (gi + 1) * POOL_GROUP_DIM)
            e0 = hext[pl.ds(rows.start, rows.size + HALO), cols]
            e = e0
            shift = 1
            while shift < w:
                e = e + pltpu.roll(e, shift, 0)
                shift *= 2
            win = e[HALO:, :]
            if is_meta:
                pos = lax.broadcasted_iota(jnp.int32, win.shape, 0) + 1
                cnt = jnp.minimum(pos, w).astype(jnp.float32)
                pooled = win / cnt - e0[HALO:, :]
            else:
                pooled = win * (1.0 / w) - e0[HALO:, :]
            pbuf[rows, cols] = pooled.astype(jnp.bfloat16)

    _row_loop(tm, POOL_ROW_CHUNK, pool_rows)

    for gi in range(len(POOL_WINDOWS)):
        cols = slice(gi * POOL_GROUP_DIM, (gi + 1) * POOL_GROUP_DIM)
        ybuf[:, cols] = jnp.dot(pbuf[:, cols], pool_w_ref[gi], preferred_element_type=jnp.float32)

    def post_mix_rows(rows):
        h1 = x_ref[rows, :] + _rms(ybuf[rows, :] * pool_scale, g_mix_post)
        hs1[rows, :] = h1
        hb[rows, :] = _rms(h1, g_ffn_pre).astype(jnp.bfloat16)

    _row_loop(tm, ROW_CHUNK, post_mix_rows)

    _swiglu(hb, wgu_ref, wd_ref, gbuf, ubuf, act, ybuf, tm)

    def post_ffn_rows(rows):
        h2 = hs1[rows, :] + _rms(ybuf[rows, :], g_ffn_post)
        hs_out_ref[rows, :] = h2
        hb[rows, :] = _rms(h2, g_kv).astype(jnp.bfloat16)

    _row_loop(tm, ROW_CHUNK, post_ffn_rows)

    kv = jnp.dot(hb[...], wkv_ref[...], preferred_element_type=jnp.float32) + bkv_ref[...]
    kv_out_ref[...] = kv.astype(jnp.bfloat16)


def _const_spec(shape):
    nd = len(shape)
    return pl.BlockSpec(shape, lambda *_: (0,) * nd, pipeline_mode=pl.Buffered(1))


def _layer_a_call(x2d, meta, gains, pool_w, wgu, wd, wkv, bkv, *, tm, tiles_per_batch, is_meta):
    n_rows = x2d.shape[0]
    n_tiles = n_rows // tm
    halo_blocks_per_tile = tm // HALO
    kern = functools.partial(_layer_a_kernel, tm=tm, tiles_per_batch=tiles_per_batch, is_meta=is_meta)
    return pl.pallas_call(
        kern,
        grid=(n_tiles,),
        in_specs=[
            pl.BlockSpec((tm, D_MODEL), lambda t: (t, 0)),
            pl.BlockSpec((HALO, D_MODEL), lambda t: (jnp.maximum(t * halo_blocks_per_tile - 1, 0), 0)),
            _const_spec((N_META, D_MODEL)),
            _const_spec(gains.shape),
            _const_spec(pool_w.shape),
            _const_spec(wgu.shape),
            _const_spec(wd.shape),
            _const_spec(wkv.shape),
            _const_spec(bkv.shape),
        ],
        out_specs=[
            pl.BlockSpec((tm, D_MODEL), lambda t: (t, 0)),
            pl.BlockSpec((tm, 2 * KV_DIM), lambda t: (t, 0)),
        ],
        out_shape=[
            jax.ShapeDtypeStruct((n_rows, D_MODEL), jnp.float32),
            jax.ShapeDtypeStruct((n_rows, 2 * KV_DIM), jnp.bfloat16),
        ],
        scratch_shapes=[
            pltpu.VMEM((tm + HALO, D_MODEL), jnp.float32),
            pltpu.VMEM((tm, D_MODEL), jnp.bfloat16),
            pltpu.VMEM((tm, D_MODEL), jnp.float32),
            pltpu.VMEM((tm, D_MODEL), jnp.float32),
            pltpu.VMEM((tm, D_MODEL), jnp.bfloat16),
            pltpu.VMEM((tm, FFN_COL_CHUNKS[0][1]), jnp.float32),
            pltpu.VMEM((tm, FFN_COL_CHUNKS[0][1]), jnp.float32),
            pltpu.VMEM((tm, D_FF), jnp.bfloat16),
        ],
        compiler_params=pltpu.CompilerParams(
            dimension_semantics=("arbitrary",), vmem_limit_bytes=V7X_VMEM_LIMIT_BYTES),
        name="layer_a_meta" if is_meta else "layer_a",
    )(x2d, x2d, meta, gains, pool_w, wgu, wd, wkv, bkv)


def _bias_table_kernel(bucket_ref, rel_bias_ref, out_ref):
    m = pl.program_id(0)
    bucket = bucket_ref[...]
    q = lax.broadcasted_iota(jnp.int32, (BLOCK, 2 * BLOCK), 0)
    s = lax.broadcasted_iota(jnp.int32, (BLOCK, 2 * BLOCK), 1)
    d = q + BLOCK - s
    in_window = (d >= 0) & (d < WINDOW)
    valid = (in_window & (s >= PAD_FRONT), in_window)
    for half in range(2):
        h = 2 * m + half
        acc = jnp.zeros((BLOCK, 2 * BLOCK), jnp.float32)
        for b in range(N_BUCKETS):
            acc = jnp.where(bucket == b, rel_bias_ref[b, h], acc)
        for jsel in range(2):
            out_ref[jsel, 0, :, half * 2 * BLOCK:(half + 1) * 2 * BLOCK] = jnp.where(valid[jsel], acc, -jnp.inf)


def _bias_table_call(bucket, rel_bias):
    return pl.pallas_call(
        _bias_table_kernel,
        grid=(N_HEAD_PAIRS,),
        in_specs=[
            pl.BlockSpec((BLOCK, 2 * BLOCK), lambda m: (0, 0)),
            pl.BlockSpec(memory_space=pltpu.SMEM),
        ],
        out_specs=pl.BlockSpec((2, 1, BLOCK, 4 * BLOCK), lambda m: (0, m, 0, 0)),
        out_shape=jax.ShapeDtypeStruct((2, N_HEAD_PAIRS, BLOCK, 4 * BLOCK), jnp.float32),
        compiler_params=pltpu.CompilerParams(dimension_semantics=("arbitrary",)),
        name="bias_table",
    )(bucket, rel_bias)


def _layer_b_kernel(hs_ref, kv_ref, metakv_ref, rowsum_cols_ref, bias_ref, sinks_ref, gains_ref,
                    wq_ref, bq_ref, wo_ref, bo_ref, wgu_ref, wd_ref,
                    out_ref,
                    hb, qbuf, obuf, ybuf, hs3, gbuf, ubuf, act, lbuf, mbuf, pbuf, *, tm):
    g_mix_pre = gains_ref[G_MIX_PRE:G_MIX_PRE + 1, :]
    g_mix_post = gains_ref[G_MIX_POST:G_MIX_POST + 1, :]
    g_ffn_pre = gains_ref[G_FFN_PRE:G_FFN_PRE + 1, :]
    g_ffn_post = gains_ref[G_FFN_POST:G_FFN_POST + 1, :]
    blocks_per_tile = tm // BLOCK
    t = pl.program_id(1)

    def norm_rows(rows):
        hb[rows, :] = _rms(hs_ref[rows, :], g_mix_pre).astype(jnp.bfloat16)

    _row_loop(tm, ROW_CHUNK, norm_rows)

    q = jnp.dot(hb[...], wq_ref[...], preferred_element_type=jnp.float32) + bq_ref[...]
    qbuf[...] = (q * (HEAD_DIM ** -0.5)).astype(jnp.bfloat16)

    lane = lax.broadcasted_iota(jnp.int32, (2 * BLOCK, 2 * HEAD_DIM), 1)
    low_half = lane < HEAD_DIM
    zero = jnp.zeros((2 * BLOCK, 2 * HEAD_DIM), jnp.bfloat16)

    low_half_q = lax.broadcasted_iota(jnp.int32, (BLOCK, 2 * HEAD_DIM), 1) < HEAD_DIM

    def per_kv_head_operands(pair_cols):
        swapped = jnp.concatenate([pair_cols[:, HEAD_DIM:], pair_cols[:, :HEAD_DIM]], axis=1)
        g0 = (jnp.where(low_half, pair_cols, zero), jnp.where(low_half, zero, swapped))
        g1 = (jnp.where(low_half, swapped, zero), jnp.where(low_half, zero, pair_cols))
        return g0, g1

    ones_top = rowsum_cols_ref[0:2 * BLOCK, :]
    ones_bottom = rowsum_cols_ref[2 * BLOCK:4 * BLOCK, :]

    for jb in range(blocks_per_tile):
        j = t * blocks_per_tile + jb
        own = kv_ref[pl.ds(pl.multiple_of(j * BLOCK, BLOCK), BLOCK), :]
        prev_start = pl.multiple_of(jnp.maximum(j - 1, 0) * BLOCK, BLOCK)
        prev = jnp.where(j == 0, metakv_ref[...], kv_ref[pl.ds(prev_start, BLOCK), :])
        kvb = jnp.concatenate([prev, own], axis=0)
        k_ops = per_kv_head_operands(kvb[:, :KV_DIM])
        v_ops = per_kv_head_operands(kvb[:, KV_DIM:])
        jsel = jnp.minimum(j, 1)
        qrows = slice(jb * BLOCK, (jb + 1) * BLOCK)
        for g in range(N_KV_HEADS):
            u = jb * N_KV_HEADS + g
            pairs = range(g * PAIRS_PER_KV, (g + 1) * PAIRS_PER_KV)
            q4 = jnp.concatenate([qbuf[qrows, m * 2 * HEAD_DIM:(m + 1) * 2 * HEAD_DIM] for m in pairs], axis=0)
            k_op = jnp.concatenate(k_ops[g], axis=0)
            v_op = jnp.concatenate([jnp.concatenate([v_ops[g][0], ones_top], axis=1),
                                    jnp.concatenate([v_ops[g][1], ones_bottom], axis=1)], axis=0)
            lbuf[u] = lax.dot_general(q4, k_op, (((1,), (1,)), ((), ())),
                                      preferred_element_type=jnp.float32) + bias_ref[jsel, g]
            for r0 in range(0, ATT_ROWS, SOFTMAX_ROW_CHUNK):
                rows = slice(r0, r0 + SOFTMAX_ROW_CHUNK)
                for half in range(2):
                    sink = sinks_ref[2 * (pairs[0] + r0 // BLOCK) + half]
                    lg = lbuf[u, rows, half * 2 * BLOCK:(half + 1) * 2 * BLOCK]
                    mx = jnp.maximum(jnp.max(lg, axis=-1, keepdims=True), sink)
                    mbuf[u, rows, half * BLOCK:(half + 1) * BLOCK] = jnp.broadcast_to(mx, (SOFTMAX_ROW_CHUNK, BLOCK))
            for r0 in range(0, ATT_ROWS, SOFTMAX_ROW_CHUNK):
                rows = slice(r0, r0 + SOFTMAX_ROW_CHUNK)
                for half in range(2):
                    mrep = mbuf[u, rows, half * BLOCK:(half + 1) * BLOCK]
                    lg = lbuf[u, rows, half * 2 * BLOCK:(half + 1) * 2 * BLOCK]
                    p = jnp.exp(lg - jnp.concatenate([mrep, mrep], axis=1))
                    pbuf[u, rows, half * 2 * BLOCK:(half + 1) * 2 * BLOCK] = p.astype(jnp.bfloat16)
            oa = jnp.dot(pbuf[u], v_op, preferred_element_type=jnp.float32)
            for i, m in enumerate(pairs):
                rows = slice(i * BLOCK, (i + 1) * BLOCK)
                sink_gap = jnp.where(low_half_q, sinks_ref[2 * m] - mbuf[u, rows, 0:BLOCK],
                                     sinks_ref[2 * m + 1] - mbuf[u, rows, BLOCK:2 * BLOCK])
                denom = oa[rows, 2 * HEAD_DIM:] + jnp.exp(sink_gap)
                o = oa[rows, :2 * HEAD_DIM] * (1.0 / denom)
                obuf[qrows, m * 2 * HEAD_DIM:(m + 1) * 2 * HEAD_DIM] = o.astype(jnp.bfloat16)

    ybuf[...] = jnp.dot(obuf[...], wo_ref[...], preferred_element_type=jnp.float32) + bo_ref[...]

    def post_mix_rows(rows):
        h3 = hs_ref[rows, :] + _rms(ybuf[rows, :], g_mix_post)
        hs3[rows, :] = h3
        hb[rows, :] = _rms(h3, g_ffn_pre).astype(jnp.bfloat16)

    _row_loop(tm, ROW_CHUNK, post_mix_rows)

    _swiglu(hb, wgu_ref, wd_ref, gbuf, ubuf, act, ybuf, tm)

    def post_ffn_rows(rows):
        out_ref[rows, :] = hs3[rows, :] + _rms(ybuf[rows, :], g_ffn_post)

    _row_loop(tm, ROW_CHUNK, post_ffn_rows)


def _layer_b_call(hs, kv, metakv, bias, sinks, gains, wq, bq, wo, bo, wgu, wd, *, tm):
    batch, seq, _ = hs.shape
    n_att_units = (tm // BLOCK) * N_KV_HEADS
    lane_head = jnp.arange(2 * HEAD_DIM)[None, :] // HEAD_DIM
    row_head = jnp.arange(4 * BLOCK)[:, None] // (2 * BLOCK)
    rowsum_cols = (lane_head == row_head).astype(jnp.bfloat16)
    kern = functools.partial(_layer_b_kernel, tm=tm)
    return pl.pallas_call(
        kern,
        grid=(batch, seq // tm),
        in_specs=[
            pl.BlockSpec((None, tm, D_MODEL), lambda b, t: (b, t, 0)),
            pl.BlockSpec((None, seq, 2 * KV_DIM), lambda b, t: (b, 0, 0)),
            _const_spec(metakv.shape),
            _const_spec(rowsum_cols.shape),
            _const_spec(bias.shape),
            pl.BlockSpec(memory_space=pltpu.SMEM),
            _const_spec(gains.shape),
            _const_spec(wq.shape),
            _const_spec(bq.shape),
            _const_spec(wo.shape),
            _const_spec(bo.shape),
            _const_spec(wgu.shape),
            _const_spec(wd.shape),
        ],
        out_specs=pl.BlockSpec((None, tm, D_MODEL), lambda b, t: (b, t, 0)),
        out_shape=jax.ShapeDtypeStruct((batch, seq, D_MODEL), jnp.float32),
        scratch_shapes=[
            pltpu.VMEM((tm, D_MODEL), jnp.bfloat16),
            pltpu.VMEM((tm, D_MODEL), jnp.bfloat16),
            pltpu.VMEM((tm, D_MODEL), jnp.bfloat16),
            pltpu.VMEM((tm, D_MODEL), jnp.float32),
            pltpu.VMEM((tm, D_MODEL), jnp.float32),
            pltpu.VMEM((tm, FFN_COL_CHUNKS[0][1]), jnp.float32),
            pltpu.VMEM((tm, FFN_COL_CHUNKS[0][1]), jnp.float32),
            pltpu.VMEM((tm, D_FF), jnp.bfloat16),
            pltpu.VMEM((n_att_units, ATT_ROWS, 4 * BLOCK), jnp.float32),
            pltpu.VMEM((n_att_units, ATT_ROWS, 2 * BLOCK), jnp.float32),
            pltpu.VMEM((n_att_units, ATT_ROWS, 4 * BLOCK), jnp.bfloat16),
        ],
        compiler_params=pltpu.CompilerParams(
            dimension_semantics=("arbitrary", "arbitrary"), vmem_limit_bytes=V7X_VMEM_LIMIT_BYTES),
        name="layer_b",
    )(hs, kv, metakv, rowsum_cols, bias, sinks, gains, wq, bq, wo, bo, wgu, wd)


def _t5_bucket_table():
    q = jnp.arange(BLOCK)[:, None]
    s = jnp.arange(2 * BLOCK)[None, :]
    d = jnp.maximum(q + BLOCK - s, 0)
    max_exact = N_BUCKETS // 2
    df = jnp.maximum(d, 1).astype(jnp.float32)
    large = max_exact + (jnp.log(df / max_exact) / math.log(MAX_DISTANCE / max_exact)
                         * (N_BUCKETS - max_exact)).astype(jnp.int32)
    large = jnp.minimum(large, N_BUCKETS - 1)
    return jnp.where(d < max_exact, d, large).astype(jnp.int32)


def _gain_table(rows):
    table = jnp.zeros((8, D_MODEL), jnp.float32)
    for i, r in rows.items():
        table = table.at[i].set(r.astype(jnp.float32))
    return table


def kernel(x, meta_tokens, norm_mix_pre, norm_mix_post, norm_ffn_pre, norm_ffn_post, pool_w, pool_scale, kv_norm, w_k, b_k, w_v, b_v, w_q, b_q, w_o, b_o, sinks, rel_bias, w_gate_up, w_down):
    batch, seq, _ = x.shape
    bf16 = jnp.bfloat16

    gains_a = _gain_table({G_MIX_PRE: norm_mix_pre[0], G_MIX_POST: norm_mix_post[0], G_POOL_SCALE: pool_scale[0],
                           G_FFN_PRE: norm_ffn_pre[0], G_FFN_POST: norm_ffn_post[0], G_KV: kv_norm})
    gains_b = _gain_table({G_MIX_PRE: norm_mix_pre[1], G_MIX_POST: norm_mix_post[1],
                           G_FFN_PRE: norm_ffn_pre[1], G_FFN_POST: norm_ffn_post[1]})
    wkv = jnp.concatenate([w_k, w_v], axis=1).astype(bf16)
    bkv = jnp.concatenate([b_k, b_v])[None, :]
    layer_a_weights = (gains_a, pool_w[0].astype(bf16), w_gate_up[0].astype(bf16), w_down[0].astype(bf16), wkv, bkv)

    x2d = x.reshape(batch * seq, D_MODEL)
    hs2, kv_x = _layer_a_call(x2d, meta_tokens, *layer_a_weights,
                              tm=TILE_A, tiles_per_batch=seq // TILE_A, is_meta=False)
    _, kv_meta = _layer_a_call(meta_tokens, meta_tokens, *layer_a_weights,
                               tm=N_META, tiles_per_batch=1, is_meta=True)
    metakv = jnp.concatenate([jnp.zeros((PAD_FRONT, 2 * KV_DIM), bf16), kv_meta], axis=0)

    bias = _bias_table_call(_t5_bucket_table(), rel_bias)
    bias = bias.reshape(2, N_KV_HEADS, ATT_ROWS, 4 * BLOCK)

    out = _layer_b_call(hs2.reshape(batch, seq, D_MODEL), kv_x.reshape(batch, seq, 2 * KV_DIM), metakv, bias,
                        sinks[0], gains_b, w_q[0].astype(bf16), b_q, w_o[0].astype(bf16), b_o,
                        w_gate_up[1].astype(bf16), w_down[1].astype(bf16), tm=TILE_B)
    return out
```

```python
import functools
import math

import numpy as np

import jax
import jax.numpy as jnp
from jax import lax
from jax.experimental import pallas as pl
from jax.experimental.pallas import tpu as pltpu

D_MODEL = 1024
N_META = 16
POOL_WINDOWS = (2, 4, 8, 16)
POOL_GROUP_DIM = D_MODEL // len(POOL_WINDOWS)
HEAD_DIM = 64
N_Q_HEADS = D_MODEL // HEAD_DIM
N_KV_HEADS = 2
HEADS_PER_KV = N_Q_HEADS // N_KV_HEADS
PAIRS_PER_KV = HEADS_PER_KV // 2
N_HEAD_PAIRS = N_Q_HEADS // 2
KV_DIM = N_KV_HEADS * HEAD_DIM
WINDOW = 128
BLOCK = 128
N_BUCKETS = 32
MAX_DISTANCE = 128
D_FF = 2816
EPS = 1e-6
PAD_FRONT = (-N_META) % BLOCK

LANES = 128
GATE_ROWS = 8
FFN_COLS = 256
SILU_ROW_CHUNK = 64
UNRIDDEN_SEGMENTS = 2
HALO = 16
ROW_CHUNK = 32
POOL_ROW_CHUNK = 64
ATT_ROWS = PAIRS_PER_KV * BLOCK
SOFTMAX_ROW_CHUNK = 16
TILE_A = 512
PIPELINE_DEPTH = 3
TILE_B = 256
V7X_VMEM_LIMIT_BYTES = 56 * 1024 * 1024

G_MIX_PRE, G_MIX_POST, G_POOL_SCALE, G_FFN_PRE, G_FFN_POST, G_KV = range(6)


def _rms(x, g):
    ms = jnp.sum(x * x, axis=-1, keepdims=True) * (1.0 / D_MODEL)
    return x * lax.rsqrt(ms + EPS) * g


def _gain(gains_ref, i):
    return gains_ref[i:i + 1, :]


def _row_loop(rows, chunk, body):
    chunk = min(chunk, rows.size)
    for i in range(rows.size // chunk):
        body(pl.ds(rows.start + i * chunk, chunk))


def _row_items(rows, chunk, body):
    chunk = min(chunk, rows.size)
    return [functools.partial(body, pl.ds(rows.start + i * chunk, chunk)) for i in range(rows.size // chunk)]


def _run_items(items, gate=None):
    for item in items:
        item(gate)


def _gate_of(value):
    bits = pltpu.bitcast(value[-GATE_ROWS:, -LANES:], jnp.uint32)
    return pltpu.bitcast((bits >> 16) >> 16, jnp.float32)


def _gated(x, gate):
    if gate is None:
        return x
    z = jnp.concatenate([gate] * (x.shape[0] // GATE_ROWS), axis=0)
    return jnp.concatenate([x[:, :LANES] + z, x[:, LANES:]], axis=1)


def _spread(items, n_bins):
    bins = [[] for _ in range(n_bins)]
    for k, item in enumerate(items):
        bins[k * n_bins // max(len(items), 1)].append(item)
    return bins


def _swiglu(h_ref, wgu_ref, wd_ref, gbuf, ubuf, act, out_ref, rows, side_items=()):
    segments = ([("gate_up", c) for c in range(0, D_FF, FFN_COLS)]
                + [("down", c) for c in range(0, D_MODEL, FFN_COLS)])
    side = _spread(list(side_items), len(segments) - UNRIDDEN_SEGMENTS) + [[]] * UNRIDDEN_SEGMENTS
    for (kind, c0), side_group in zip(segments, side):
        if kind == "gate_up":
            g = jnp.dot(h_ref[rows, :], wgu_ref[:, c0:c0 + FFN_COLS], preferred_element_type=jnp.float32)
            gbuf[rows, :FFN_COLS] = g
            ubuf[rows, :FFN_COLS] = jnp.dot(h_ref[rows, :], wgu_ref[:, D_FF + c0:D_FF + c0 + FFN_COLS],
                                            preferred_element_type=jnp.float32)
            gate_source = g

            def silu_rows(r, gate, c0=c0):
                half_g = 0.5 * gbuf[r, :FFN_COLS]
                silu = half_g + half_g * jnp.tanh(half_g)
                act[r, c0:c0 + FFN_COLS] = (silu * ubuf[r, :FFN_COLS]).astype(jnp.bfloat16)

            _run_items(_row_items(rows, SILU_ROW_CHUNK, silu_rows))
        else:
            y = jnp.dot(act[rows, :], wd_ref[:, c0:c0 + FFN_COLS], preferred_element_type=jnp.float32)
            out_ref[rows, c0:c0 + FFN_COLS] = y
            gate_source = y
        if side_group:
            _run_items(side_group, _gate_of(gate_source))


def _run_pipelined(step, n_tiles, carried, stages):
    if n_tiles == 1:
        mixer, ffn, epilogue = stages(0)
        _run_items(mixer)
        ffn(())
        _run_items(epilogue)
        return

    @pl.when(step == 0)
    def _():
        for ref in carried:
            ref[...] = jnp.zeros(ref.shape, ref.dtype)

    for parity in range(2):
        @pl.when(step % 2 == parity)
        def _(parity=parity):
            mixer, _, epilogue = stages(parity)
            _, ffn, _ = stages(1 - parity)
            ffn(epilogue + mixer)


def _layer_a_kernel(x_ref, prev_ref, meta_ref, gains_ref, pool_w_ref, wgu_ref, wd_ref, wkv_ref, bkv_ref,
                    hs_out_ref, kv_out_ref,
                    hext, pbuf, ymix, hs1, hb, yffn, gbuf, ubuf, act, kvin,
                    *, tm, tiles_per_batch, n_tiles, is_meta):
    g_mix_pre = _gain(gains_ref, G_MIX_PRE)
    step = pl.program_id(0)
    tile_rows = pl.ds(0, tm)

    def stages(slot):
        def halo_rows(gate):
            if is_meta:
                hext[0:HALO, :] = jnp.zeros((HALO, D_MODEL), jnp.float32)
            else:
                first_of_batch = (jnp.minimum(step, n_tiles - 1) % tiles_per_batch) == 0
                prev = jnp.where(first_of_batch, meta_ref[...], prev_ref[...])
                hext[0:HALO, :] = _rms(prev, g_mix_pre)

        def norm_rows(rows, gate):
            hext[pl.ds(rows.start + HALO, rows.size), :] = _rms(_gated(x_ref[rows, :], gate), g_mix_pre)

        def pool_rows(rows, gate):
            for gi, w in enumerate(POOL_WINDOWS):
                cols = slice(gi * POOL_GROUP_DIM, (gi + 1) * POOL_GROUP_DIM)
                e0 = _gated(hext[pl.ds(rows.start, rows.size + HALO), cols], gate)
                e = e0
                shift = 1
                while shift < w:
                    e = e + pltpu.roll(e, shift, 0)
                    shift *= 2
                win = e[HALO:, :]
                if is_meta:
                    pos = lax.broadcasted_iota(jnp.int32, win.shape, 0) + 1
                    cnt = jnp.minimum(pos, w).astype(jnp.float32)
                    pooled = win / cnt - e0[HALO:, :]
                else:
                    pooled = win * (1.0 / w) - e0[HALO:, :]
                pbuf[rows, cols] = pooled.astype(jnp.bfloat16)

        def group_matmul(gi, gate):
            cols = slice(gi * POOL_GROUP_DIM, (gi + 1) * POOL_GROUP_DIM)
            ymix[:, cols] = jnp.dot(pbuf[:, cols], pool_w_ref[gi], preferred_element_type=jnp.float32)

        def post_mix_rows(rows, gate):
            y = _gated(ymix[rows, :], gate)
            h1 = x_ref[rows, :] + _rms(y * _gain(gains_ref, G_POOL_SCALE), _gain(gains_ref, G_MIX_POST))
            hs1[slot, rows, :] = h1
            hb[slot, rows, :] = _rms(h1, _gain(gains_ref, G_FFN_PRE)).astype(jnp.bfloat16)

        mixer = ([halo_rows] + _row_items(tile_rows, ROW_CHUNK, norm_rows)
                 + _row_items(tile_rows, POOL_ROW_CHUNK, pool_rows)
                 + [functools.partial(group_matmul, gi) for gi in range(len(POOL_WINDOWS))]
                 + _row_items(tile_rows, ROW_CHUNK, post_mix_rows))

        def ffn(side_items):
            _swiglu(hb.at[slot], wgu_ref, wd_ref, gbuf, ubuf, act, yffn.at[slot], tile_rows, side_items)

        def post_ffn_rows(rows, gate):
            h2 = hs1[slot, rows, :] + _rms(_gated(yffn[slot, rows, :], gate), _gain(gains_ref, G_FFN_POST))
            hs_out_ref[rows, :] = h2
            kvin[rows, :] = _rms(h2, _gain(gains_ref, G_KV)).astype(jnp.bfloat16)

        def kv_projection(gate):
            kv = jnp.dot(kvin[...], wkv_ref[...], preferred_element_type=jnp.float32) + bkv_ref[...]
            kv_out_ref[...] = kv.astype(jnp.bfloat16)

        epilogue = _row_items(tile_rows, ROW_CHUNK, post_ffn_rows) + [kv_projection]
        return mixer, ffn, epilogue

    _run_pipelined(step, n_tiles, (hs1, hb, yffn), stages)


def _const_spec(shape):
    nd = len(shape)
    return pl.BlockSpec(shape, lambda *_: (0,) * nd, pipeline_mode=pl.Buffered(1))


def _layer_a_call(x2d, meta, gains, pool_w, wgu, wd, wkv, bkv, *, tm, tiles_per_batch, is_meta):
    n_rows = x2d.shape[0]
    n_tiles = n_rows // tm
    n_steps = n_tiles if n_tiles == 1 else n_tiles + PIPELINE_DEPTH - 1
    halo_blocks_per_tile = tm // HALO
    kern = functools.partial(_layer_a_kernel, tm=tm, tiles_per_batch=tiles_per_batch, n_tiles=n_tiles,
                             is_meta=is_meta)

    def mixer_tile(s):
        return jnp.minimum(s, n_tiles - 1)

    def epilogue_tile(s):
        return jnp.maximum(s - (n_steps - n_tiles), 0)

    return pl.pallas_call(
        kern,
        grid=(n_steps,),
        in_specs=[
            pl.BlockSpec((tm, D_MODEL), lambda s: (mixer_tile(s), 0)),
            pl.BlockSpec((HALO, D_MODEL), lambda s: (jnp.maximum(mixer_tile(s) * halo_blocks_per_tile - 1, 0), 0)),
            _const_spec((N_META, D_MODEL)),
            _const_spec(gains.shape),
            _const_spec(pool_w.shape),
            _const_spec(wgu.shape),
            _const_spec(wd.shape),
            _const_spec(wkv.shape),
            _const_spec(bkv.shape),
        ],
        out_specs=[
            pl.BlockSpec((tm, D_MODEL), lambda s: (epilogue_tile(s), 0)),
            pl.BlockSpec((tm, 2 * KV_DIM), lambda s: (epilogue_tile(s), 0)),
        ],
        out_shape=[
            jax.ShapeDtypeStruct((n_rows, D_MODEL), jnp.float32),
            jax.ShapeDtypeStruct((n_rows, 2 * KV_DIM), jnp.bfloat16),
        ],
        scratch_shapes=[
            pltpu.VMEM((tm + HALO, D_MODEL), jnp.float32),
            pltpu.VMEM((tm, D_MODEL), jnp.bfloat16),
            pltpu.VMEM((tm, D_MODEL), jnp.float32),
            pltpu.VMEM((2, tm, D_MODEL), jnp.float32),
            pltpu.VMEM((2, tm, D_MODEL), jnp.bfloat16),
            pltpu.VMEM((2, tm, D_MODEL), jnp.float32),
            pltpu.VMEM((tm, FFN_COLS), jnp.float32),
            pltpu.VMEM((tm, FFN_COLS), jnp.float32),
            pltpu.VMEM((tm, D_FF), jnp.bfloat16),
            pltpu.VMEM((tm, D_MODEL), jnp.bfloat16),
        ],
        compiler_params=pltpu.CompilerParams(
            dimension_semantics=("arbitrary",), vmem_limit_bytes=V7X_VMEM_LIMIT_BYTES),
        name="layer_a_meta" if is_meta else "layer_a",
    )(x2d, x2d, meta, gains, pool_w, wgu, wd, wkv, bkv)


def _bucket_distance_ranges():
    d = np.arange(WINDOW)
    max_exact = N_BUCKETS // 2
    df = np.maximum(d, 1).astype(np.float32)
    large = max_exact + (np.log(df / np.float32(max_exact)) / np.float32(math.log(MAX_DISTANCE / max_exact))
                         * np.float32(N_BUCKETS - max_exact)).astype(np.int32)
    bucket = np.where(d < max_exact, d, np.minimum(large, N_BUCKETS - 1))
    ranges = []
    for b in range(N_BUCKETS):
        members = d[bucket == b]
        if members.size:
            assert np.array_equal(members, np.arange(members[0], members[-1] + 1))
            ranges.append((int(members[0]), int(members[-1])))
        else:
            ranges.append(None)
    return ranges


def _bias_table_kernel(rel_bias_ref, out_ref):
    m = pl.program_id(0)
    q = lax.broadcasted_iota(jnp.int32, (BLOCK, 2 * BLOCK), 0)
    s = lax.broadcasted_iota(jnp.int32, (BLOCK, 2 * BLOCK), 1)
    d = q + BLOCK - s
    in_window = (d >= 0) & (d < WINDOW)
    valid = (in_window & (s >= PAD_FRONT), in_window)
    for half in range(2):
        h = 2 * m + half
        acc = jnp.zeros((BLOCK, 2 * BLOCK), jnp.float32)
        for b, distances in enumerate(_bucket_distance_ranges()):
            if distances is not None:
                lo, hi = distances
                acc = jnp.where((d >= lo) & (d <= hi), rel_bias_ref[b, h], acc)
        for jsel in range(2):
            out_ref[jsel, 0, :, half * 2 * BLOCK:(half + 1) * 2 * BLOCK] = jnp.where(valid[jsel], acc, -jnp.inf)


def _bias_table_call(rel_bias):
    return pl.pallas_call(
        _bias_table_kernel,
        grid=(N_HEAD_PAIRS,),
        in_specs=[
            pl.BlockSpec(memory_space=pltpu.SMEM),
        ],
        out_specs=pl.BlockSpec((2, 1, BLOCK, 4 * BLOCK), lambda m: (0, m, 0, 0)),
        out_shape=jax.ShapeDtypeStruct((2, N_HEAD_PAIRS, BLOCK, 4 * BLOCK), jnp.float32),
        compiler_params=pltpu.CompilerParams(dimension_semantics=("arbitrary",)),
        name="bias_table",
    )(rel_bias)


def _layer_b_kernel(hs_ref, kv_ref, metakv_ref, rowsum_cols_ref, bias_ref, sinks_ref, gains_ref,
                    wq_ref, bq_ref, wo_ref, bo_ref, wgu_ref, wd_ref,
                    out_ref,
                    hn, qbuf, obuf, yatt, hs3, hb, yffn, gbuf, ubuf, act, lbuf, mbuf, pbuf,
                    *, tm, tiles_per_batch, n_tiles):
    blocks_per_tile = tm // BLOCK
    step = pl.program_id(0)
    t = jnp.minimum(step, n_tiles - 1) % tiles_per_batch
    tile_rows = pl.ds(0, tm)

    lane = lax.broadcasted_iota(jnp.int32, (2 * BLOCK, 2 * HEAD_DIM), 1)
    low_half = lane < HEAD_DIM
    zero = jnp.zeros((2 * BLOCK, 2 * HEAD_DIM), jnp.bfloat16)
    low_half_q = lax.broadcasted_iota(jnp.int32, (BLOCK, 2 * HEAD_DIM), 1) < HEAD_DIM

    def per_kv_head_operands(pair_cols):
        swapped = jnp.concatenate([pair_cols[:, HEAD_DIM:], pair_cols[:, :HEAD_DIM]], axis=1)
        g0 = (jnp.where(low_half, pair_cols, zero), jnp.where(low_half, zero, swapped))
        g1 = (jnp.where(low_half, swapped, zero), jnp.where(low_half, zero, pair_cols))
        return g0, g1

    ones_top = rowsum_cols_ref[0:2 * BLOCK, :]
    ones_bottom = rowsum_cols_ref[2 * BLOCK:4 * BLOCK, :]

    def attention_block(jb, gate):
        j = t * blocks_per_tile + jb
        own = kv_ref[pl.ds(pl.multiple_of(j * BLOCK, BLOCK), BLOCK), :]
        prev_start = pl.multiple_of(jnp.maximum(j - 1, 0) * BLOCK, BLOCK)
        prev = jnp.where(j == 0, metakv_ref[...], kv_ref[pl.ds(prev_start, BLOCK), :])
        kvb = jnp.concatenate([prev, own], axis=0)
        k_ops = per_kv_head_operands(kvb[:, :KV_DIM])
        v_ops = per_kv_head_operands(kvb[:, KV_DIM:])
        jsel = jnp.minimum(j, 1)
        qrows = slice(jb * BLOCK, (jb + 1) * BLOCK)
        for g in range(N_KV_HEADS):
            u = jb * N_KV_HEADS + g
            pairs = range(g * PAIRS_PER_KV, (g + 1) * PAIRS_PER_KV)
            q4 = jnp.concatenate([qbuf[qrows, m * 2 * HEAD_DIM:(m + 1) * 2 * HEAD_DIM] for m in pairs], axis=0)
            k_op = jnp.concatenate(k_ops[g], axis=0)
            v_op = jnp.concatenate([jnp.concatenate([v_ops[g][0], ones_top], axis=1),
                                    jnp.concatenate([v_ops[g][1], ones_bottom], axis=1)], axis=0)
            lbuf[u] = _gated(lax.dot_general(q4, k_op, (((1,), (1,)), ((), ())),
                                             preferred_element_type=jnp.float32) + bias_ref[jsel, g], gate)
            for r0 in range(0, ATT_ROWS, SOFTMAX_ROW_CHUNK):
                rows = slice(r0, r0 + SOFTMAX_ROW_CHUNK)
                for half in range(2):
                    sink = sinks_ref[2 * (pairs[0] + r0 // BLOCK) + half]
                    lg = lbuf[u, rows, half * 2 * BLOCK:(half + 1) * 2 * BLOCK]
                    mx = jnp.maximum(jnp.max(lg, axis=-1, keepdims=True), sink)
                    mbuf[u, rows, half * BLOCK:(half + 1) * BLOCK] = jnp.broadcast_to(mx, (SOFTMAX_ROW_CHUNK, BLOCK))
            for r0 in range(0, ATT_ROWS, SOFTMAX_ROW_CHUNK):
                rows = slice(r0, r0 + SOFTMAX_ROW_CHUNK)
                for half in range(2):
                    mrep = mbuf[u, rows, half * BLOCK:(half + 1) * BLOCK]
                    lg = lbuf[u, rows, half * 2 * BLOCK:(half + 1) * 2 * BLOCK]
                    p = jnp.exp(lg - jnp.concatenate([mrep, mrep], axis=1))
                    pbuf[u, rows, half * 2 * BLOCK:(half + 1) * 2 * BLOCK] = p.astype(jnp.bfloat16)
            oa = jnp.dot(pbuf[u], v_op, preferred_element_type=jnp.float32)
            for i, m in enumerate(pairs):
                rows = slice(i * BLOCK, (i + 1) * BLOCK)
                sink_gap = jnp.where(low_half_q, sinks_ref[2 * m] - mbuf[u, rows, 0:BLOCK],
                                     sinks_ref[2 * m + 1] - mbuf[u, rows, BLOCK:2 * BLOCK])
                denom = oa[rows, 2 * HEAD_DIM:] + jnp.exp(sink_gap)
                o = oa[rows, :2 * HEAD_DIM] * (1.0 / denom)
                obuf[qrows, m * 2 * HEAD_DIM:(m + 1) * 2 * HEAD_DIM] = o.astype(jnp.bfloat16)

    def stages(slot):
        def norm_rows(rows, gate):
            h = _rms(_gated(hs_ref[rows, :], gate), _gain(gains_ref, G_MIX_PRE))
            hn[rows, :] = h.astype(jnp.bfloat16)

        def q_projection(gate):
            q = jnp.dot(hn[...], wq_ref[...], preferred_element_type=jnp.float32) + bq_ref[...]
            qbuf[...] = (q * (HEAD_DIM ** -0.5)).astype(jnp.bfloat16)

        def o_projection(gate):
            yatt[...] = jnp.dot(obuf[...], wo_ref[...], preferred_element_type=jnp.float32) + bo_ref[...]

        def post_mix_rows(rows, gate):
            h3 = hs_ref[rows, :] + _rms(_gated(yatt[rows, :], gate), _gain(gains_ref, G_MIX_POST))
            hs3[slot, rows, :] = h3
            hb[slot, rows, :] = _rms(h3, _gain(gains_ref, G_FFN_PRE)).astype(jnp.bfloat16)

        attention = (_row_items(tile_rows, ROW_CHUNK, norm_rows) + [q_projection]
                     + [functools.partial(attention_block, jb) for jb in range(blocks_per_tile)]
                     + [o_projection] + _row_items(tile_rows, ROW_CHUNK, post_mix_rows))

        def ffn(side_items):
            _swiglu(hb.at[slot], wgu_ref, wd_ref, gbuf, ubuf, act, yffn.at[slot], tile_rows, side_items)

        def post_ffn_rows(rows, gate):
            y = _gated(yffn[slot, rows, :], gate)
            out_ref[rows, :] = hs3[slot, rows, :] + _rms(y, _gain(gains_ref, G_FFN_POST))

        return attention, ffn, _row_items(tile_rows, ROW_CHUNK, post_ffn_rows)

    _run_pipelined(step, n_tiles, (hs3, hb, yffn), stages)


def _layer_b_call(hs, kv, metakv, bias, sinks, gains, wq, bq, wo, bo, wgu, wd, *, tm):
    batch, seq, _ = hs.shape
    tiles_per_batch = seq // tm
    n_tiles = batch * tiles_per_batch
    n_att_units = (tm // BLOCK) * N_KV_HEADS
    lane_head = jnp.arange(2 * HEAD_DIM)[None, :] // HEAD_DIM
    row_head = jnp.arange(4 * BLOCK)[:, None] // (2 * BLOCK)
    rowsum_cols = (lane_head == row_head).astype(jnp.bfloat16)
    kern = functools.partial(_layer_b_kernel, tm=tm, tiles_per_batch=tiles_per_batch, n_tiles=n_tiles)

    def att_tile(s):
        tile = jnp.minimum(s, n_tiles - 1)
        return tile // tiles_per_batch, tile % tiles_per_batch

    def epilogue_tile(s):
        tile = jnp.maximum(s - (PIPELINE_DEPTH - 1), 0)
        return tile // tiles_per_batch, tile % tiles_per_batch

    return pl.pallas_call(
        kern,
        grid=(n_tiles + PIPELINE_DEPTH - 1,),
        in_specs=[
            pl.BlockSpec((None, tm, D_MODEL), lambda s: (*att_tile(s), 0)),
            pl.BlockSpec((None, seq, 2 * KV_DIM), lambda s: (att_tile(s)[0], 0, 0), pipeline_mode=pl.Buffered(1)),
            _const_spec(metakv.shape),
            _const_spec(rowsum_cols.shape),
            _const_spec(bias.shape),
            pl.BlockSpec(memory_space=pltpu.SMEM),
            _const_spec(gains.shape),
            _const_spec(wq.shape),
            _const_spec(bq.shape),
            _const_spec(wo.shape),
            _const_spec(bo.shape),
            _const_spec(wgu.shape),
            _const_spec(wd.shape),
        ],
        out_specs=pl.BlockSpec((None, tm, D_MODEL), lambda s: (*epilogue_tile(s), 0)),
        out_shape=jax.ShapeDtypeStruct((batch, seq, D_MODEL), jnp.float32),
        scratch_shapes=[
            pltpu.VMEM((tm, D_MODEL), jnp.bfloat16),
            pltpu.VMEM((tm, D_MODEL), jnp.bfloat16),
            pltpu.VMEM((tm, D_MODEL), jnp.bfloat16),
            pltpu.VMEM((tm, D_MODEL), jnp.float32),
            pltpu.VMEM((2, tm, D_MODEL), jnp.float32),
            pltpu.VMEM((2, tm, D_MODEL), jnp.bfloat16),
            pltpu.VMEM((2, tm, D_MODEL), jnp.float32),
            pltpu.VMEM((tm, FFN_COLS), jnp.float32),
            pltpu.VMEM((tm, FFN_COLS), jnp.float32),
            pltpu.VMEM((tm, D_FF), jnp.bfloat16),
            pltpu.VMEM((n_att_units, ATT_ROWS, 4 * BLOCK), jnp.float32),
            pltpu.VMEM((n_att_units, ATT_ROWS, 2 * BLOCK), jnp.float32),
            pltpu.VMEM((n_att_units, ATT_ROWS, 4 * BLOCK), jnp.bfloat16),
        ],
        compiler_params=pltpu.CompilerParams(
            dimension_semantics=("arbitrary",), vmem_limit_bytes=V7X_VMEM_LIMIT_BYTES),
        name="layer_b",
    )(hs, kv, metakv, rowsum_cols, bias, sinks, gains, wq, bq, wo, bo, wgu, wd)


def _gain_table(rows):
    table = jnp.zeros((8, D_MODEL), jnp.float32)
    for i, r in rows.items():
        table = table.at[i].set(r.astype(jnp.float32))
    return table


def kernel(x, meta_tokens, norm_mix_pre, norm_mix_post, norm_ffn_pre, norm_ffn_post, pool_w, pool_scale, kv_norm, w_k, b_k, w_v, b_v, w_q, b_q, w_o, b_o, sinks, rel_bias, w_gate_up, w_down):
    batch, seq, _ = x.shape
    bf16 = jnp.bfloat16

    gains_a = _gain_table({G_MIX_PRE: norm_mix_pre[0], G_MIX_POST: norm_mix_post[0], G_POOL_SCALE: pool_scale[0],
                           G_FFN_PRE: norm_ffn_pre[0], G_FFN_POST: norm_ffn_post[0], G_KV: kv_norm})
    gains_b = _gain_table({G_MIX_PRE: norm_mix_pre[1], G_MIX_POST: norm_mix_post[1],
                           G_FFN_PRE: norm_ffn_pre[1], G_FFN_POST: norm_ffn_post[1]})
    wkv = jnp.concatenate([w_k, w_v], axis=1).astype(bf16)
    bkv = jnp.concatenate([b_k, b_v])[None, :]
    layer_a_weights = (gains_a, pool_w[0].astype(bf16), w_gate_up[0].astype(bf16), w_down[0].astype(bf16), wkv, bkv)

    x2d = x.reshape(batch * seq, D_MODEL)
    hs2, kv_x = _layer_a_call(x2d, meta_tokens, *layer_a_weights,
                              tm=TILE_A, tiles_per_batch=seq // TILE_A, is_meta=False)
    _, kv_meta = _layer_a_call(meta_tokens, meta_tokens, *layer_a_weights,
                               tm=N_META, tiles_per_batch=1, is_meta=True)
    metakv = jnp.concatenate([jnp.zeros((PAD_FRONT, 2 * KV_DIM), bf16), kv_meta], axis=0)

    bias = _bias_table_call(rel_bias)
    bias = bias.reshape(2, N_KV_HEADS, ATT_ROWS, 4 * BLOCK)

    out = _layer_b_call(hs2.reshape(batch, seq, D_MODEL), kv_x.reshape(batch, seq, 2 * KV_DIM), metakv, bias,
                        sinks[0], gains_b, w_q[0].astype(bf16), b_q, w_o[0].astype(bf16), b_o,
                        w_gate_up[1].astype(bf16), w_down[1].astype(bf16), tm=TILE_B)
    return out
```

```python
import functools
import math

import numpy as np

import jax
import jax.numpy as jnp
from jax import lax
from jax.experimental import pallas as pl
from jax.experimental.pallas import tpu as pltpu

D_MODEL = 1024
N_META = 16
POOL_WINDOWS = (2, 4, 8, 16)
POOL_GROUP_DIM = D_MODEL // len(POOL_WINDOWS)
HEAD_DIM = 64
N_Q_HEADS = D_MODEL // HEAD_DIM
N_KV_HEADS = 2
HEADS_PER_KV = N_Q_HEADS // N_KV_HEADS
PAIRS_PER_KV = HEADS_PER_KV // 2
N_HEAD_PAIRS = N_Q_HEADS // 2
KV_DIM = N_KV_HEADS * HEAD_DIM
WINDOW = 128
BLOCK = 128
N_BUCKETS = 32
MAX_DISTANCE = 128
D_FF = 2816
EPS = 1e-6
PAD_FRONT = (-N_META) % BLOCK

LANES = 128
GATE_ROWS = 8
FFN_COLS = 256
SILU_ROW_CHUNK = 64
UNRIDDEN_SEGMENTS = 2
HALO = 16
ROW_CHUNK = 32
POOL_ROW_CHUNK = 64
ATT_ROWS = PAIRS_PER_KV * BLOCK
SOFTMAX_ROW_CHUNK = 16
TILE_A = 512
PIPELINE_DEPTH = 3
TILE_B = 256
V7X_VMEM_LIMIT_BYTES = 56 * 1024 * 1024

G_MIX_PRE, G_MIX_POST, G_POOL_SCALE, G_FFN_PRE, G_FFN_POST, G_KV = range(6)


def _rms(x, g):
    ms = jnp.sum(x * x, axis=-1, keepdims=True) * (1.0 / D_MODEL)
    return x * lax.rsqrt(ms + EPS) * g


def _gain(gains_ref, i):
    return gains_ref[i:i + 1, :]


def _row_loop(rows, chunk, body):
    chunk = min(chunk, rows.size)
    for i in range(rows.size // chunk):
        body(pl.ds(rows.start + i * chunk, chunk))


def _row_items(rows, chunk, body):
    chunk = min(chunk, rows.size)
    return [functools.partial(body, pl.ds(rows.start + i * chunk, chunk)) for i in range(rows.size // chunk)]


def _run_items(items, gate=None):
    for item in items:
        item(gate)


def _gate_of(value):
    bits = pltpu.bitcast(value[-GATE_ROWS:, -LANES:], jnp.uint32)
    return pltpu.bitcast((bits >> 16) >> 16, jnp.float32)


def _gated(x, gate):
    if gate is None:
        return x
    z = jnp.concatenate([gate] * (x.shape[0] // GATE_ROWS), axis=0)
    return jnp.concatenate([x[:, :LANES] + z, x[:, LANES:]], axis=1)


def _spread(items, n_bins):
    bins = [[] for _ in range(n_bins)]
    for k, item in enumerate(items):
        bins[k * n_bins // max(len(items), 1)].append(item)
    return bins


def _swiglu(h_ref, wgu_ref, wd_ref, gbuf, ubuf, act, out_ref, rows, side_items=()):
    segments = ([("gate_up", c) for c in range(0, D_FF, FFN_COLS)]
                + [("down", c) for c in range(0, D_MODEL, FFN_COLS)])
    side = _spread(list(side_items), len(segments) - UNRIDDEN_SEGMENTS) + [[]] * UNRIDDEN_SEGMENTS
    for (kind, c0), side_group in zip(segments, side):
        if kind == "gate_up":
            g = jnp.dot(h_ref[rows, :], wgu_ref[:, c0:c0 + FFN_COLS], preferred_element_type=jnp.float32)
            gbuf[rows, :FFN_COLS] = g
            ubuf[rows, :FFN_COLS] = jnp.dot(h_ref[rows, :], wgu_ref[:, D_FF + c0:D_FF + c0 + FFN_COLS],
                                            preferred_element_type=jnp.float32)
            gate_source = g

            def silu_rows(r, gate, c0=c0):
                half_g = 0.5 * gbuf[r, :FFN_COLS]
                silu = half_g + half_g * jnp.tanh(half_g)
                act[r, c0:c0 + FFN_COLS] = (silu * ubuf[r, :FFN_COLS]).astype(jnp.bfloat16)

            _run_items(_row_items(rows, SILU_ROW_CHUNK, silu_rows))
        else:
            y = jnp.dot(act[rows, :], wd_ref[:, c0:c0 + FFN_COLS], preferred_element_type=jnp.float32)
            out_ref[rows, c0:c0 + FFN_COLS] = y
            gate_source = y
        if side_group:
            _run_items(side_group, _gate_of(gate_source))


def _run_pipelined(step, n_tiles, carried, stages):
    if n_tiles == 1:
        mixer, ffn, epilogue = stages(0)
        _run_items(mixer)
        ffn(())
        _run_items(epilogue)
        return

    last_step = n_tiles + 1

    @pl.when(step == 0)
    def _():
        for ref in carried:
            ref[...] = jnp.zeros(ref.shape, ref.dtype)
        _run_items(stages(0)[0])

    for parity in range(2):
        @pl.when((step % 2 == parity) & (step > 0) & (step < last_step))
        def _(parity=parity):
            mixer, _, epilogue = stages(parity)
            _, ffn, _ = stages(1 - parity)
            ffn(epilogue + mixer)

    @pl.when(step == last_step)
    def _():
        _run_items(stages(last_step % 2)[2])


def _layer_a_kernel(x_ref, prev_ref, meta_ref, gains_ref, pool_w_ref, wgu_ref, wd_ref, wkv_ref, bkv_ref,
                    hs_out_ref, kv_out_ref,
                    hext, pbuf, ymix, hs1, hb, yffn, gbuf, ubuf, act, kvin,
                    *, tm, tiles_per_batch, n_tiles, is_meta):
    g_mix_pre = _gain(gains_ref, G_MIX_PRE)
    step = pl.program_id(0)
    tile_rows = pl.ds(0, tm)

    def stages(slot):
        def halo_rows(gate):
            if is_meta:
                hext[0:HALO, :] = jnp.zeros((HALO, D_MODEL), jnp.float32)
            else:
                first_of_batch = (jnp.minimum(step, n_tiles - 1) % tiles_per_batch) == 0
                prev = jnp.where(first_of_batch, meta_ref[...], prev_ref[...])
                hext[0:HALO, :] = _rms(prev, g_mix_pre)

        def norm_rows(rows, gate):
            hext[pl.ds(rows.start + HALO, rows.size), :] = _rms(_gated(x_ref[rows, :], gate), g_mix_pre)

        def pool_rows(rows, gate):
            for gi, w in enumerate(POOL_WINDOWS):
                cols = slice(gi * POOL_GROUP_DIM, (gi + 1) * POOL_GROUP_DIM)
                e0 = _gated(hext[pl.ds(rows.start, rows.size + HALO), cols], gate)
                e = e0
                shift = 1
                while shift < w:
                    e = e + pltpu.roll(e, shift, 0)
                    shift *= 2
                win = e[HALO:, :]
                if is_meta:
                    pos = lax.broadcasted_iota(jnp.int32, win.shape, 0) + 1
                    cnt = jnp.minimum(pos, w).astype(jnp.float32)
                    pooled = win / cnt - e0[HALO:, :]
                else:
                    pooled = win * (1.0 / w) - e0[HALO:, :]
                pbuf[rows, cols] = pooled.astype(jnp.bfloat16)

        def group_matmul(gi, gate):
            cols = slice(gi * POOL_GROUP_DIM, (gi + 1) * POOL_GROUP_DIM)
            ymix[:, cols] = jnp.dot(pbuf[:, cols], pool_w_ref[gi], preferred_element_type=jnp.float32)

        def post_mix_rows(rows, gate):
            y = _gated(ymix[rows, :], gate)
            h1 = x_ref[rows, :] + _rms(y * _gain(gains_ref, G_POOL_SCALE), _gain(gains_ref, G_MIX_POST))
            hs1[slot, rows, :] = h1
            hb[slot, rows, :] = _rms(h1, _gain(gains_ref, G_FFN_PRE)).astype(jnp.bfloat16)

        mixer = ([halo_rows] + _row_items(tile_rows, ROW_CHUNK, norm_rows)
                 + _row_items(tile_rows, POOL_ROW_CHUNK, pool_rows)
                 + [functools.partial(group_matmul, gi) for gi in range(len(POOL_WINDOWS))]
                 + _row_items(tile_rows, ROW_CHUNK, post_mix_rows))

        def ffn(side_items):
            _swiglu(hb.at[slot], wgu_ref, wd_ref, gbuf, ubuf, act, yffn.at[slot], tile_rows, side_items)

        def post_ffn_rows(rows, gate):
            h2 = hs1[slot, rows, :] + _rms(_gated(yffn[slot, rows, :], gate), _gain(gains_ref, G_FFN_POST))
            hs_out_ref[rows, :] = h2
            kvin[rows, :] = _rms(h2, _gain(gains_ref, G_KV)).astype(jnp.bfloat16)

        def kv_projection(gate):
            kv = jnp.dot(kvin[...], wkv_ref[...], preferred_element_type=jnp.float32) + bkv_ref[...]
            kv_out_ref[...] = kv.astype(jnp.bfloat16)

        epilogue = _row_items(tile_rows, ROW_CHUNK, post_ffn_rows) + [kv_projection]
        return mixer, ffn, epilogue

    _run_pipelined(step, n_tiles, (hs1, hb, yffn), stages)


def _const_spec(shape):
    nd = len(shape)
    return pl.BlockSpec(shape, lambda *_: (0,) * nd, pipeline_mode=pl.Buffered(1))


def _layer_a_call(x2d, meta, gains, pool_w, wgu, wd, wkv, bkv, *, tm, tiles_per_batch, is_meta):
    n_rows = x2d.shape[0]
    n_tiles = n_rows // tm
    n_steps = n_tiles if n_tiles == 1 else n_tiles + PIPELINE_DEPTH - 1
    halo_blocks_per_tile = tm // HALO
    kern = functools.partial(_layer_a_kernel, tm=tm, tiles_per_batch=tiles_per_batch, n_tiles=n_tiles,
                             is_meta=is_meta)

    def mixer_tile(s):
        return jnp.minimum(s, n_tiles - 1)

    def epilogue_tile(s):
        return jnp.maximum(s - (n_steps - n_tiles), 0)

    return pl.pallas_call(
        kern,
        grid=(n_steps,),
        in_specs=[
            pl.BlockSpec((tm, D_MODEL), lambda s: (mixer_tile(s), 0)),
            pl.BlockSpec((HALO, D_MODEL), lambda s: (jnp.maximum(mixer_tile(s) * halo_blocks_per_tile - 1, 0), 0)),
            _const_spec((N_META, D_MODEL)),
            _const_spec(gains.shape),
            _const_spec(pool_w.shape),
            _const_spec(wgu.shape),
            _const_spec(wd.shape),
            _const_spec(wkv.shape),
            _const_spec(bkv.shape),
        ],
        out_specs=[
            pl.BlockSpec((tm, D_MODEL), lambda s: (epilogue_tile(s), 0)),
            pl.BlockSpec((tm, 2 * KV_DIM), lambda s: (epilogue_tile(s), 0)),
        ],
        out_shape=[
            jax.ShapeDtypeStruct((n_rows, D_MODEL), jnp.float32),
            jax.ShapeDtypeStruct((n_rows, 2 * KV_DIM), jnp.bfloat16),
        ],
        scratch_shapes=[
            pltpu.VMEM((tm + HALO, D_MODEL), jnp.float32),
            pltpu.VMEM((tm, D_MODEL), jnp.bfloat16),
            pltpu.VMEM((tm, D_MODEL), jnp.float32),
            pltpu.VMEM((2, tm, D_MODEL), jnp.float32),
            pltpu.VMEM((2, tm, D_MODEL), jnp.bfloat16),
            pltpu.VMEM((2, tm, D_MODEL), jnp.float32),
            pltpu.VMEM((tm, FFN_COLS), jnp.float32),
            pltpu.VMEM((tm, FFN_COLS), jnp.float32),
            pltpu.VMEM((tm, D_FF), jnp.bfloat16),
            pltpu.VMEM((tm, D_MODEL), jnp.bfloat16),
        ],
        compiler_params=pltpu.CompilerParams(
            dimension_semantics=("arbitrary",), vmem_limit_bytes=V7X_VMEM_LIMIT_BYTES),
        name="layer_a_meta" if is_meta else "layer_a",
    )(x2d, x2d, meta, gains, pool_w, wgu, wd, wkv, bkv)


def _bucket_distance_ranges():
    d = np.arange(WINDOW)
    max_exact = N_BUCKETS // 2
    df = np.maximum(d, 1).astype(np.float32)
    large = max_exact + (np.log(df / np.float32(max_exact)) / np.float32(math.log(MAX_DISTANCE / max_exact))
                         * np.float32(N_BUCKETS - max_exact)).astype(np.int32)
    bucket = np.where(d < max_exact, d, np.minimum(large, N_BUCKETS - 1))
    ranges = []
    for b in range(N_BUCKETS):
        members = d[bucket == b]
        if members.size:
            assert np.array_equal(members, np.arange(members[0], members[-1] + 1))
            ranges.append((int(members[0]), int(members[-1])))
        else:
            ranges.append(None)
    return ranges


def _bias_table_kernel(rel_bias_ref, out_ref):
    m = pl.program_id(0)
    q = lax.broadcasted_iota(jnp.int32, (BLOCK, 2 * BLOCK), 0)
    s = lax.broadcasted_iota(jnp.int32, (BLOCK, 2 * BLOCK), 1)
    d = q + BLOCK - s
    in_window = (d >= 0) & (d < WINDOW)
    valid = (in_window & (s >= PAD_FRONT), in_window)
    for half in range(2):
        h = 2 * m + half
        acc = jnp.zeros((BLOCK, 2 * BLOCK), jnp.float32)
        for b, distances in enumerate(_bucket_distance_ranges()):
            if distances is not None:
                lo, hi = distances
                acc = jnp.where((d >= lo) & (d <= hi), rel_bias_ref[b, h], acc)
        for jsel in range(2):
            out_ref[jsel, 0, :, half * 2 * BLOCK:(half + 1) * 2 * BLOCK] = jnp.where(valid[jsel], acc, -jnp.inf)


def _bias_table_call(rel_bias):
    return pl.pallas_call(
        _bias_table_kernel,
        grid=(N_HEAD_PAIRS,),
        in_specs=[
            pl.BlockSpec(memory_space=pltpu.SMEM),
        ],
        out_specs=pl.BlockSpec((2, 1, BLOCK, 4 * BLOCK), lambda m: (0, m, 0, 0)),
        out_shape=jax.ShapeDtypeStruct((2, N_HEAD_PAIRS, BLOCK, 4 * BLOCK), jnp.float32),
        compiler_params=pltpu.CompilerParams(dimension_semantics=("arbitrary",)),
        name="bias_table",
    )(rel_bias)


def _layer_b_kernel(hs_ref, kv_ref, metakv_ref, rowsum_cols_ref, bias_ref, sinks_ref, gains_ref,
                    wq_ref, bq_ref, wo_ref, bo_ref, wgu_ref, wd_ref,
                    out_ref,
                    hn, qbuf, obuf, yatt, hs3, hb, yffn, gbuf, ubuf, act, lbuf, mbuf, pbuf,
                    *, tm, tiles_per_batch, n_tiles):
    blocks_per_tile = tm // BLOCK
    step = pl.program_id(0)
    t = jnp.minimum(step, n_tiles - 1) % tiles_per_batch
    tile_rows = pl.ds(0, tm)

    lane = lax.broadcasted_iota(jnp.int32, (2 * BLOCK, 2 * HEAD_DIM), 1)
    low_half = lane < HEAD_DIM
    zero = jnp.zeros((2 * BLOCK, 2 * HEAD_DIM), jnp.bfloat16)
    low_half_q = lax.broadcasted_iota(jnp.int32, (BLOCK, 2 * HEAD_DIM), 1) < HEAD_DIM

    def per_kv_head_operands(pair_cols):
        swapped = jnp.concatenate([pair_cols[:, HEAD_DIM:], pair_cols[:, :HEAD_DIM]], axis=1)
        g0 = (jnp.where(low_half, pair_cols, zero), jnp.where(low_half, zero, swapped))
        g1 = (jnp.where(low_half, swapped, zero), jnp.where(low_half, zero, pair_cols))
        return g0, g1

    ones_top = rowsum_cols_ref[0:2 * BLOCK, :]
    ones_bottom = rowsum_cols_ref[2 * BLOCK:4 * BLOCK, :]

    def attention_block(jb, gate):
        j = t * blocks_per_tile + jb
        own = kv_ref[pl.ds(pl.multiple_of(j * BLOCK, BLOCK), BLOCK), :]
        prev_start = pl.multiple_of(jnp.maximum(j - 1, 0) * BLOCK, BLOCK)
        prev = jnp.where(j == 0, metakv_ref[...], kv_ref[pl.ds(prev_start, BLOCK), :])
        kvb = jnp.concatenate([prev, own], axis=0)
        k_ops = per_kv_head_operands(kvb[:, :KV_DIM])
        v_ops = per_kv_head_operands(kvb[:, KV_DIM:])
        jsel = jnp.minimum(j, 1)
        qrows = slice(jb * BLOCK, (jb + 1) * BLOCK)
        for g in range(N_KV_HEADS):
            u = jb * N_KV_HEADS + g
            pairs = range(g * PAIRS_PER_KV, (g + 1) * PAIRS_PER_KV)
            q4 = jnp.concatenate([qbuf[qrows, m * 2 * HEAD_DIM:(m + 1) * 2 * HEAD_DIM] for m in pairs], axis=0)
            k_op = jnp.concatenate(k_ops[g], axis=0)
            v_op = jnp.concatenate([jnp.concatenate([v_ops[g][0], ones_top], axis=1),
                                    jnp.concatenate([v_ops[g][1], ones_bottom], axis=1)], axis=0)
            lbuf[u] = _gated(lax.dot_general(q4, k_op, (((1,), (1,)), ((), ())),
                                             preferred_element_type=jnp.float32) + bias_ref[jsel, g], gate)
            for r0 in range(0, ATT_ROWS, SOFTMAX_ROW_CHUNK):
                rows = slice(r0, r0 + SOFTMAX_ROW_CHUNK)
                for half in range(2):
                    sink = sinks_ref[2 * (pairs[0] + r0 // BLOCK) + half]
                    lg = lbuf[u, rows, half * 2 * BLOCK:(half + 1) * 2 * BLOCK]
                    mx = jnp.maximum(jnp.max(lg, axis=-1, keepdims=True), sink)
                    mbuf[u, rows, half * BLOCK:(half + 1) * BLOCK] = jnp.broadcast_to(mx, (SOFTMAX_ROW_CHUNK, BLOCK))
            for r0 in range(0, ATT_ROWS, SOFTMAX_ROW_CHUNK):
                rows = slice(r0, r0 + SOFTMAX_ROW_CHUNK)
                for half in range(2):
                    mrep = mbuf[u, rows, half * BLOCK:(half + 1) * BLOCK]
                    lg = lbuf[u, rows, half * 2 * BLOCK:(half + 1) * 2 * BLOCK]
                    p = jnp.exp(lg - jnp.concatenate([mrep, mrep], axis=1))
                    pbuf[u, rows, half * 2 * BLOCK:(half + 1) * 2 * BLOCK] = p.astype(jnp.bfloat16)
            oa = jnp.dot(pbuf[u], v_op, preferred_element_type=jnp.float32)
            for i, m in enumerate(pairs):
                rows = slice(i * BLOCK, (i + 1) * BLOCK)
                sink_gap = jnp.where(low_half_q, sinks_ref[2 * m] - mbuf[u, rows, 0:BLOCK],
                                     sinks_ref[2 * m + 1] - mbuf[u, rows, BLOCK:2 * BLOCK])
                denom = oa[rows, 2 * HEAD_DIM:] + jnp.exp(sink_gap)
                o = oa[rows, :2 * HEAD_DIM] * (1.0 / denom)
                obuf[qrows, m * 2 * HEAD_DIM:(m + 1) * 2 * HEAD_DIM] = o.astype(jnp.bfloat16)

    def stages(slot):
        def norm_rows(rows, gate):
            h = _rms(_gated(hs_ref[rows, :], gate), _gain(gains_ref, G_MIX_PRE))
            hn[rows, :] = h.astype(jnp.bfloat16)

        def q_projection(gate):
            q = jnp.dot(hn[...], wq_ref[...], preferred_element_type=jnp.float32) + bq_ref[...]
            qbuf[...] = (q * (HEAD_DIM ** -0.5)).astype(jnp.bfloat16)

        def o_projection(gate):
            yatt[...] = jnp.dot(obuf[...], wo_ref[...], preferred_element_type=jnp.float32) + bo_ref[...]

        def post_mix_rows(rows, gate):
            h3 = hs_ref[rows, :] + _rms(_gated(yatt[rows, :], gate), _gain(gains_ref, G_MIX_POST))
            hs3[slot, rows, :] = h3
            hb[slot, rows, :] = _rms(h3, _gain(gains_ref, G_FFN_PRE)).astype(jnp.bfloat16)

        attention = (_row_items(tile_rows, ROW_CHUNK, norm_rows) + [q_projection]
                     + [functools.partial(attention_block, jb) for jb in range(blocks_per_tile)]
                     + [o_projection] + _row_items(tile_rows, ROW_CHUNK, post_mix_rows))

        def ffn(side_items):
            _swiglu(hb.at[slot], wgu_ref, wd_ref, gbuf, ubuf, act, yffn.at[slot], tile_rows, side_items)

        def post_ffn_rows(rows, gate):
            y = _gated(yffn[slot, rows, :], gate)
            out_ref[rows, :] = hs3[slot, rows, :] + _rms(y, _gain(gains_ref, G_FFN_POST))

        return attention, ffn, _row_items(tile_rows, ROW_CHUNK, post_ffn_rows)

    _run_pipelined(step, n_tiles, (hs3, hb, yffn), stages)


def _layer_b_call(hs, kv, metakv, bias, sinks, gains, wq, bq, wo, bo, wgu, wd, *, tm):
    batch, seq, _ = hs.shape
    tiles_per_batch = seq // tm
    n_tiles = batch * tiles_per_batch
    n_att_units = (tm // BLOCK) * N_KV_HEADS
    lane_head = jnp.arange(2 * HEAD_DIM)[None, :] // HEAD_DIM
    row_head = jnp.arange(4 * BLOCK)[:, None] // (2 * BLOCK)
    rowsum_cols = (lane_head == row_head).astype(jnp.bfloat16)
    kern = functools.partial(_layer_b_kernel, tm=tm, tiles_per_batch=tiles_per_batch, n_tiles=n_tiles)

    def att_tile(s):
        tile = jnp.minimum(s, n_tiles - 1)
        return tile // tiles_per_batch, tile % tiles_per_batch

    def epilogue_tile(s):
        tile = jnp.maximum(s - (PIPELINE_DEPTH - 1), 0)
        return tile // tiles_per_batch, tile % tiles_per_batch

    return pl.pallas_call(
        kern,
        grid=(n_tiles + PIPELINE_DEPTH - 1,),
        in_specs=[
            pl.BlockSpec((None, tm, D_MODEL), lambda s: (*att_tile(s), 0)),
            pl.BlockSpec((None, seq, 2 * KV_DIM), lambda s: (att_tile(s)[0], 0, 0), pipeline_mode=pl.Buffered(1)),
            _const_spec(metakv.shape),
            _const_spec(rowsum_cols.shape),
            _const_spec(bias.shape),
            pl.BlockSpec(memory_space=pltpu.SMEM),
            _const_spec(gains.shape),
            _const_spec(wq.shape),
            _const_spec(bq.shape),
            _const_spec(wo.shape),
            _const_spec(bo.shape),
            _const_spec(wgu.shape),
            _const_spec(wd.shape),
        ],
        out_specs=pl.BlockSpec((None, tm, D_MODEL), lambda s: (*epilogue_tile(s), 0)),
        out_shape=jax.ShapeDtypeStruct((batch, seq, D_MODEL), jnp.float32),
        scratch_shapes=[
            pltpu.VMEM((tm, D_MODEL), jnp.bfloat16),
            pltpu.VMEM((tm, D_MODEL), jnp.bfloat16),
            pltpu.VMEM((tm, D_MODEL), jnp.bfloat16),
            pltpu.VMEM((tm, D_MODEL), jnp.float32),
            pltpu.VMEM((2, tm, D_MODEL), jnp.float32),
            pltpu.VMEM((2, tm, D_MODEL), jnp.bfloat16),
            pltpu.VMEM((2, tm, D_MODEL), jnp.float32),
            pltpu.VMEM((tm, FFN_COLS), jnp.float32),
            pltpu.VMEM((tm, FFN_COLS), jnp.float32),
            pltpu.VMEM((tm, D_FF), jnp.bfloat16),
            pltpu.VMEM((n_att_units, ATT_ROWS, 4 * BLOCK), jnp.float32),
            pltpu.VMEM((n_att_units, ATT_ROWS, 2 * BLOCK), jnp.float32),
            pltpu.VMEM((n_att_units, ATT_ROWS, 4 * BLOCK), jnp.bfloat16),
        ],
        compiler_params=pltpu.CompilerParams(
            dimension_semantics=("arbitrary",), vmem_limit_bytes=V7X_VMEM_LIMIT_BYTES),
        name="layer_b",
    )(hs, kv, metakv, rowsum_cols, bias, sinks, gains, wq, bq, wo, bo, wgu, wd)


def _gain_table(rows):
    table = jnp.zeros((8, D_MODEL), jnp.float32)
    for i, r in rows.items():
        table = table.at[i].set(r.astype(jnp.float32))
    return table


def kernel(x, meta_tokens, norm_mix_pre, norm_mix_post, norm_ffn_pre, norm_ffn_post, pool_w, pool_scale, kv_norm, w_k, b_k, w_v, b_v, w_q, b_q, w_o, b_o, sinks, rel_bias, w_gate_up, w_down):
    batch, seq, _ = x.shape
    bf16 = jnp.bfloat16

    gains_a = _gain_table({G_MIX_PRE: norm_mix_pre[0], G_MIX_POST: norm_mix_post[0], G_POOL_SCALE: pool_scale[0],
                           G_FFN_PRE: norm_ffn_pre[0], G_FFN_POST: norm_ffn_post[0], G_KV: kv_norm})
    gains_b = _gain_table({G_MIX_PRE: norm_mix_pre[1], G_MIX_POST: norm_mix_post[1],
                           G_FFN_PRE: norm_ffn_pre[1], G_FFN_POST: norm_ffn_post[1]})
    wkv = jnp.concatenate([w_k, w_v], axis=1).astype(bf16)
    bkv = jnp.concatenate([b_k, b_v])[None, :]
    layer_a_weights = (gains_a, pool_w[0].astype(bf16), w_gate_up[0].astype(bf16), w_down[0].astype(bf16), wkv, bkv)

    x2d = x.reshape(batch * seq, D_MODEL)
    hs2, kv_x = _layer_a_call(x2d, meta_tokens, *layer_a_weights,
                              tm=TILE_A, tiles_per_batch=seq // TILE_A, is_meta=False)
    _, kv_meta = _layer_a_call(meta_tokens, meta_tokens, *layer_a_weights,
                               tm=N_META, tiles_per_batch=1, is_meta=True)
    metakv = jnp.concatenate([jnp.zeros((PAD_FRONT, 2 * KV_DIM), bf16), kv_meta], axis=0)

    bias = _bias_table_call(rel_bias)
    bias = bias.reshape(2, N_KV_HEADS, ATT_ROWS, 4 * BLOCK)

    out = _layer_b_call(hs2.reshape(batch, seq, D_MODEL), kv_x.reshape(batch, seq, 2 * KV_DIM), metakv, bias,
                        sinks[0], gains_b, w_q[0].astype(bf16), b_q, w_o[0].astype(bf16), b_o,
                        w_gate_up[1].astype(bf16), w_down[1].astype(bf16), tm=TILE_B)
    return out
```

```python
import functools
import math

import numpy as np

import jax
import jax.numpy as jnp
from jax import lax
from jax.experimental import pallas as pl
from jax.experimental.pallas import tpu as pltpu

D_MODEL = 1024
N_META = 16
POOL_WINDOWS = (2, 4, 8, 16)
POOL_GROUP_DIM = D_MODEL // len(POOL_WINDOWS)
HEAD_DIM = 64
N_Q_HEADS = D_MODEL // HEAD_DIM
N_KV_HEADS = 2
HEADS_PER_KV = N_Q_HEADS // N_KV_HEADS
PAIRS_PER_KV = HEADS_PER_KV // 2
N_HEAD_PAIRS = N_Q_HEADS // 2
KV_DIM = N_KV_HEADS * HEAD_DIM
WINDOW = 128
BLOCK = 128
N_BUCKETS = 32
MAX_DISTANCE = 128
D_FF = 2816
EPS = 1e-6
PAD_FRONT = (-N_META) % BLOCK

LANES = 128
GATE_ROWS = 8
FFN_COLS = 256
SILU_ROW_CHUNK = 64
UNRIDDEN_SEGMENTS = 2
CAST_BLOCK_BYTES = 6 * 1024 * 1024
HALO = 16
ROW_CHUNK = 32
POOL_ROW_CHUNK = 64
ATT_ROWS = PAIRS_PER_KV * BLOCK
SOFTMAX_ROW_CHUNK = 16
TILE_A = 512
PIPELINE_DEPTH = 3
TILE_B = 256
V7X_VMEM_LIMIT_BYTES = 56 * 1024 * 1024

G_MIX_PRE, G_MIX_POST, G_POOL_SCALE, G_FFN_PRE, G_FFN_POST, G_KV = range(6)


def _rms(x, g):
    ms = jnp.sum(x * x, axis=-1, keepdims=True) * (1.0 / D_MODEL)
    return x * lax.rsqrt(ms + EPS) * g


def _gain(gains_ref, i):
    return gains_ref[i:i + 1, :]


def _row_loop(rows, chunk, body):
    chunk = min(chunk, rows.size)
    for i in range(rows.size // chunk):
        body(pl.ds(rows.start + i * chunk, chunk))


def _row_items(rows, chunk, body):
    chunk = min(chunk, rows.size)
    return [functools.partial(body, pl.ds(rows.start + i * chunk, chunk)) for i in range(rows.size // chunk)]


def _run_items(items, gate=None):
    for item in items:
        item(gate)


def _gate_of(value):
    bits = pltpu.bitcast(value[-GATE_ROWS:, -LANES:], jnp.uint32)
    return pltpu.bitcast((bits >> 16) >> 16, jnp.float32)


def _gated(x, gate):
    if gate is None:
        return x
    z = jnp.concatenate([gate] * (x.shape[0] // GATE_ROWS), axis=0)
    return jnp.concatenate([x[:, :LANES] + z, x[:, LANES:]], axis=1)


def _spread(items, n_bins):
    bins = [[] for _ in range(n_bins)]
    for k, item in enumerate(items):
        bins[k * n_bins // max(len(items), 1)].append(item)
    return bins


def _swiglu(h_ref, wgu_ref, wd_ref, act, out_ref, rows, side_items=()):
    segments = ([("gate_up", c) for c in range(0, D_FF, FFN_COLS)]
                + [("down", c) for c in range(0, D_MODEL, FFN_COLS)])
    side = _spread(list(side_items), len(segments) - UNRIDDEN_SEGMENTS) + [[]] * UNRIDDEN_SEGMENTS
    for (kind, c0), side_group in zip(segments, side):
        if kind == "gate_up":
            gu = jnp.dot(h_ref[rows, :], wgu_ref[:, 2 * c0:2 * (c0 + FFN_COLS)], preferred_element_type=jnp.float32)
            for b in range(FFN_COLS // LANES):
                half_g = 0.5 * gu[:, 2 * b * LANES:(2 * b + 1) * LANES]
                up = gu[:, (2 * b + 1) * LANES:(2 * b + 2) * LANES]
                silu = half_g + half_g * jnp.tanh(half_g)
                act[rows, c0 + b * LANES:c0 + (b + 1) * LANES] = (silu * up).astype(jnp.bfloat16)
            gate_source = gu
        else:
            y = jnp.dot(act[rows, :], wd_ref[:, c0:c0 + FFN_COLS], preferred_element_type=jnp.float32)
            out_ref[rows, c0:c0 + FFN_COLS] = y
            gate_source = y
        if side_group:
            _run_items(side_group, _gate_of(gate_source))


def _run_pipelined(step, n_tiles, carried, stages):
    if n_tiles == 1:
        mixer, ffn, epilogue = stages(0)
        _run_items(mixer)
        ffn(())
        _run_items(epilogue)
        return

    last_step = n_tiles + 1

    @pl.when(step == 0)
    def _():
        for ref in carried:
            ref[...] = jnp.zeros(ref.shape, ref.dtype)
        _run_items(stages(0)[0])

    for parity in range(2):
        @pl.when((step % 2 == parity) & (step > 0) & (step < last_step))
        def _(parity=parity):
            mixer, _, epilogue = stages(parity)
            _, ffn, _ = stages(1 - parity)
            ffn(epilogue + mixer)

    @pl.when(step == last_step)
    def _():
        _run_items(stages(last_step % 2)[2])


def _layer_a_kernel(x_ref, prev_ref, meta_ref, gains_ref, pool_w_ref, wgu_ref, wd_ref, wkv_ref, bkv_ref,
                    hs_out_ref, kv_out_ref,
                    hext, pbuf, ymix, hs1, hb, yffn, act, kvin,
                    *, tm, tiles_per_batch, n_tiles, is_meta):
    g_mix_pre = _gain(gains_ref, G_MIX_PRE)
    step = pl.program_id(0)
    tile_rows = pl.ds(0, tm)

    def stages(slot):
        def halo_rows(gate):
            if is_meta:
                hext[0:HALO, :] = jnp.zeros((HALO, D_MODEL), jnp.float32)
            else:
                first_of_batch = (jnp.minimum(step, n_tiles - 1) % tiles_per_batch) == 0
                prev = jnp.where(first_of_batch, meta_ref[...], prev_ref[...])
                hext[0:HALO, :] = _rms(prev, g_mix_pre)

        def norm_rows(rows, gate):
            hext[pl.ds(rows.start + HALO, rows.size), :] = _rms(_gated(x_ref[rows, :], gate), g_mix_pre)

        def pool_rows(rows, gate):
            for gi, w in enumerate(POOL_WINDOWS):
                cols = slice(gi * POOL_GROUP_DIM, (gi + 1) * POOL_GROUP_DIM)
                e0 = _gated(hext[pl.ds(rows.start, rows.size + HALO), cols], gate)
                e = e0
                shift = 1
                while shift < w:
                    e = e + pltpu.roll(e, shift, 0)
                    shift *= 2
                win = e[HALO:, :]
                if is_meta:
                    pos = lax.broadcasted_iota(jnp.int32, win.shape, 0) + 1
                    cnt = jnp.minimum(pos, w).astype(jnp.float32)
                    pooled = win / cnt - e0[HALO:, :]
                else:
                    pooled = win * (1.0 / w) - e0[HALO:, :]
                pbuf[rows, cols] = pooled.astype(jnp.bfloat16)

        def group_matmul(gi, gate):
            cols = slice(gi * POOL_GROUP_DIM, (gi + 1) * POOL_GROUP_DIM)
            ymix[:, cols] = jnp.dot(pbuf[:, cols], pool_w_ref[gi], preferred_element_type=jnp.float32)

        def post_mix_rows(rows, gate):
            y = _gated(ymix[rows, :], gate)
            h1 = x_ref[rows, :] + _rms(y * _gain(gains_ref, G_POOL_SCALE), _gain(gains_ref, G_MIX_POST))
            hs1[slot, rows, :] = h1
            hb[slot, rows, :] = _rms(h1, _gain(gains_ref, G_FFN_PRE)).astype(jnp.bfloat16)

        mixer = ([halo_rows] + _row_items(tile_rows, ROW_CHUNK, norm_rows)
                 + _row_items(tile_rows, POOL_ROW_CHUNK, pool_rows)
                 + [functools.partial(group_matmul, gi) for gi in range(len(POOL_WINDOWS))]
                 + _row_items(tile_rows, ROW_CHUNK, post_mix_rows))

        def ffn(side_items):
            _swiglu(hb.at[slot], wgu_ref, wd_ref, act, yffn.at[slot], tile_rows, side_items)

        def post_ffn_rows(rows, gate):
            h2 = hs1[slot, rows, :] + _rms(_gated(yffn[slot, rows, :], gate), _gain(gains_ref, G_FFN_POST))
            hs_out_ref[rows, :] = h2
            kvin[rows, :] = _rms(h2, _gain(gains_ref, G_KV)).astype(jnp.bfloat16)

        def kv_projection(gate):
            kv = jnp.dot(kvin[...], wkv_ref[...], preferred_element_type=jnp.float32) + bkv_ref[...]
            kv_out_ref[...] = kv.astype(jnp.bfloat16)

        epilogue = _row_items(tile_rows, ROW_CHUNK, post_ffn_rows) + [kv_projection]
        return mixer, ffn, epilogue

    _run_pipelined(step, n_tiles, (hs1, hb, yffn), stages)


def _const_spec(shape):
    nd = len(shape)
    return pl.BlockSpec(shape, lambda *_: (0,) * nd, pipeline_mode=pl.Buffered(1))


def _layer_spec(stacked_shape, layer):
    _, rows, cols = stacked_shape
    return pl.BlockSpec((None, rows, cols), lambda *_: (layer, 0, 0), pipeline_mode=pl.Buffered(1))


def _layer_a_call(x2d, meta, gains, pool_w, wgu, wd, wkv, bkv, *, layer, tm, tiles_per_batch, is_meta):
    n_rows = x2d.shape[0]
    n_tiles = n_rows // tm
    n_steps = n_tiles if n_tiles == 1 else n_tiles + PIPELINE_DEPTH - 1
    halo_blocks_per_tile = tm // HALO
    kern = functools.partial(_layer_a_kernel, tm=tm, tiles_per_batch=tiles_per_batch, n_tiles=n_tiles,
                             is_meta=is_meta)

    def mixer_tile(s):
        return jnp.minimum(s, n_tiles - 1)

    def epilogue_tile(s):
        return jnp.maximum(s - (n_steps - n_tiles), 0)

    return pl.pallas_call(
        kern,
        grid=(n_steps,),
        in_specs=[
            pl.BlockSpec((tm, D_MODEL), lambda s: (mixer_tile(s), 0)),
            pl.BlockSpec((HALO, D_MODEL), lambda s: (jnp.maximum(mixer_tile(s) * halo_blocks_per_tile - 1, 0), 0)),
            _const_spec((N_META, D_MODEL)),
            _const_spec(gains.shape),
            _const_spec(pool_w.shape),
            _layer_spec(wgu.shape, layer),
            _layer_spec(wd.shape, layer),
            _const_spec(wkv.shape),
            _const_spec(bkv.shape),
        ],
        out_specs=[
            pl.BlockSpec((tm, D_MODEL), lambda s: (epilogue_tile(s), 0)),
            pl.BlockSpec((tm, 2 * KV_DIM), lambda s: (epilogue_tile(s), 0)),
        ],
        out_shape=[
            jax.ShapeDtypeStruct((n_rows, D_MODEL), jnp.float32),
            jax.ShapeDtypeStruct((n_rows, 2 * KV_DIM), jnp.bfloat16),
        ],
        scratch_shapes=[
            pltpu.VMEM((tm + HALO, D_MODEL), jnp.float32),
            pltpu.VMEM((tm, D_MODEL), jnp.bfloat16),
            pltpu.VMEM((tm, D_MODEL), jnp.float32),
            pltpu.VMEM((2, tm, D_MODEL), jnp.float32),
            pltpu.VMEM((2, tm, D_MODEL), jnp.bfloat16),
            pltpu.VMEM((2, tm, D_MODEL), jnp.float32),
            pltpu.VMEM((tm, D_FF), jnp.bfloat16),
            pltpu.VMEM((tm, D_MODEL), jnp.bfloat16),
        ],
        compiler_params=pltpu.CompilerParams(
            dimension_semantics=("arbitrary",), vmem_limit_bytes=V7X_VMEM_LIMIT_BYTES),
        name="layer_a_meta" if is_meta else "layer_a",
    )(x2d, x2d, meta, gains, pool_w, wgu, wd, wkv, bkv)


def _bucket_distance_ranges():
    d = np.arange(WINDOW)
    max_exact = N_BUCKETS // 2
    df = np.maximum(d, 1).astype(np.float32)
    large = max_exact + (np.log(df / np.float32(max_exact)) / np.float32(math.log(MAX_DISTANCE / max_exact))
                         * np.float32(N_BUCKETS - max_exact)).astype(np.int32)
    bucket = np.where(d < max_exact, d, np.minimum(large, N_BUCKETS - 1))
    ranges = []
    for b in range(N_BUCKETS):
        members = d[bucket == b]
        if members.size:
            assert np.array_equal(members, np.arange(members[0], members[-1] + 1))
            ranges.append((int(members[0]), int(members[-1])))
        else:
            ranges.append(None)
    return ranges


def _bias_table_kernel(rel_bias_ref, out_ref):
    m = pl.program_id(0)
    q = lax.broadcasted_iota(jnp.int32, (BLOCK, 2 * BLOCK), 0)
    s = lax.broadcasted_iota(jnp.int32, (BLOCK, 2 * BLOCK), 1)
    d = q + BLOCK - s
    in_window = (d >= 0) & (d < WINDOW)
    valid = (in_window & (s >= PAD_FRONT), in_window)
    for half in range(2):
        h = 2 * m + half
        acc = jnp.zeros((BLOCK, 2 * BLOCK), jnp.float32)
        for b, distances in enumerate(_bucket_distance_ranges()):
            if distances is not None:
                lo, hi = distances
                acc = jnp.where((d >= lo) & (d <= hi), rel_bias_ref[b, h], acc)
        for jsel in range(2):
            out_ref[jsel, 0, :, half * 2 * BLOCK:(half + 1) * 2 * BLOCK] = jnp.where(valid[jsel], acc, -jnp.inf)


def _bias_table_call(rel_bias):
    return pl.pallas_call(
        _bias_table_kernel,
        grid=(N_HEAD_PAIRS,),
        in_specs=[
            pl.BlockSpec(memory_space=pltpu.SMEM),
        ],
        out_specs=pl.BlockSpec((2, 1, BLOCK, 4 * BLOCK), lambda m: (0, m, 0, 0)),
        out_shape=jax.ShapeDtypeStruct((2, N_HEAD_PAIRS, BLOCK, 4 * BLOCK), jnp.float32),
        compiler_params=pltpu.CompilerParams(dimension_semantics=("arbitrary",)),
        name="bias_table",
    )(rel_bias)


def _layer_b_kernel(hs_ref, kv_ref, metakv_ref, rowsum_cols_ref, bias_ref, sinks_ref, gains_ref,
                    wq_ref, bq_ref, wo_ref, bo_ref, wgu_ref, wd_ref,
                    out_ref,
                    hn, qbuf, obuf, yatt, hs3, hb, yffn, act, lbuf, mbuf, pbuf,
                    *, tm, tiles_per_batch, n_tiles):
    blocks_per_tile = tm // BLOCK
    step = pl.program_id(0)
    t = jnp.minimum(step, n_tiles - 1) % tiles_per_batch
    tile_rows = pl.ds(0, tm)

    lane = lax.broadcasted_iota(jnp.int32, (2 * BLOCK, 2 * HEAD_DIM), 1)
    low_half = lane < HEAD_DIM
    zero = jnp.zeros((2 * BLOCK, 2 * HEAD_DIM), jnp.bfloat16)
    low_half_q = lax.broadcasted_iota(jnp.int32, (BLOCK, 2 * HEAD_DIM), 1) < HEAD_DIM

    def per_kv_head_operands(pair_cols):
        swapped = jnp.concatenate([pair_cols[:, HEAD_DIM:], pair_cols[:, :HEAD_DIM]], axis=1)
        g0 = (jnp.where(low_half, pair_cols, zero), jnp.where(low_half, zero, swapped))
        g1 = (jnp.where(low_half, swapped, zero), jnp.where(low_half, zero, pair_cols))
        return g0, g1

    ones_top = rowsum_cols_ref[0:2 * BLOCK, :]
    ones_bottom = rowsum_cols_ref[2 * BLOCK:4 * BLOCK, :]

    def attention_block(jb, gate):
        j = t * blocks_per_tile + jb
        own = kv_ref[pl.ds(pl.multiple_of(j * BLOCK, BLOCK), BLOCK), :]
        prev_start = pl.multiple_of(jnp.maximum(j - 1, 0) * BLOCK, BLOCK)
        prev = jnp.where(j == 0, metakv_ref[...], kv_ref[pl.ds(prev_start, BLOCK), :])
        kvb = jnp.concatenate([prev, own], axis=0)
        k_ops = per_kv_head_operands(kvb[:, :KV_DIM])
        v_ops = per_kv_head_operands(kvb[:, KV_DIM:])
        jsel = jnp.minimum(j, 1)
        qrows = slice(jb * BLOCK, (jb + 1) * BLOCK)
        for g in range(N_KV_HEADS):
            u = jb * N_KV_HEADS + g
            pairs = range(g * PAIRS_PER_KV, (g + 1) * PAIRS_PER_KV)
            q4 = jnp.concatenate([qbuf[qrows, m * 2 * HEAD_DIM:(m + 1) * 2 * HEAD_DIM] for m in pairs], axis=0)
            k_op = jnp.concatenate(k_ops[g], axis=0)
            v_op = jnp.concatenate([jnp.concatenate([v_ops[g][0], ones_top], axis=1),
                                    jnp.concatenate([v_ops[g][1], ones_bottom], axis=1)], axis=0)
            lbuf[u] = _gated(lax.dot_general(q4, k_op, (((1,), (1,)), ((), ())),
                                             preferred_element_type=jnp.float32) + bias_ref[jsel, g], gate)
            for r0 in range(0, ATT_ROWS, SOFTMAX_ROW_CHUNK):
                rows = slice(r0, r0 + SOFTMAX_ROW_CHUNK)
                for half in range(2):
                    sink = sinks_ref[2 * (pairs[0] + r0 // BLOCK) + half]
                    lg = lbuf[u, rows, half * 2 * BLOCK:(half + 1) * 2 * BLOCK]
                    mx = jnp.maximum(jnp.max(lg, axis=-1, keepdims=True), sink)
                    mbuf[u, rows, half * BLOCK:(half + 1) * BLOCK] = jnp.broadcast_to(mx, (SOFTMAX_ROW_CHUNK, BLOCK))
            for r0 in range(0, ATT_ROWS, SOFTMAX_ROW_CHUNK):
                rows = slice(r0, r0 + SOFTMAX_ROW_CHUNK)
                for half in range(2):
                    mrep = mbuf[u, rows, half * BLOCK:(half + 1) * BLOCK]
                    lg = lbuf[u, rows, half * 2 * BLOCK:(half + 1) * 2 * BLOCK]
                    p = jnp.exp(lg - jnp.concatenate([mrep, mrep], axis=1))
                    pbuf[u, rows, half * 2 * BLOCK:(half + 1) * 2 * BLOCK] = p.astype(jnp.bfloat16)
            oa = jnp.dot(pbuf[u], v_op, preferred_element_type=jnp.float32)
            for i, m in enumerate(pairs):
                rows = slice(i * BLOCK, (i + 1) * BLOCK)
                sink_gap = jnp.where(low_half_q, sinks_ref[2 * m] - mbuf[u, rows, 0:BLOCK],
                                     sinks_ref[2 * m + 1] - mbuf[u, rows, BLOCK:2 * BLOCK])
                denom = oa[rows, 2 * HEAD_DIM:] + jnp.exp(sink_gap)
                o = oa[rows, :2 * HEAD_DIM] * (1.0 / denom)
                obuf[qrows, m * 2 * HEAD_DIM:(m + 1) * 2 * HEAD_DIM] = o.astype(jnp.bfloat16)

    def stages(slot):
        def norm_rows(rows, gate):
            h = _rms(_gated(hs_ref[rows, :], gate), _gain(gains_ref, G_MIX_PRE))
            hn[rows, :] = h.astype(jnp.bfloat16)

        def q_projection(gate):
            q = jnp.dot(hn[...], wq_ref[...], preferred_element_type=jnp.float32) + bq_ref[...]
            qbuf[...] = (q * (HEAD_DIM ** -0.5)).astype(jnp.bfloat16)

        def o_projection(gate):
            yatt[...] = jnp.dot(obuf[...], wo_ref[...], preferred_element_type=jnp.float32) + bo_ref[...]

        def post_mix_rows(rows, gate):
            h3 = hs_ref[rows, :] + _rms(_gated(yatt[rows, :], gate), _gain(gains_ref, G_MIX_POST))
            hs3[slot, rows, :] = h3
            hb[slot, rows, :] = _rms(h3, _gain(gains_ref, G_FFN_PRE)).astype(jnp.bfloat16)

        attention = (_row_items(tile_rows, ROW_CHUNK, norm_rows) + [q_projection]
                     + [functools.partial(attention_block, jb) for jb in range(blocks_per_tile)]
                     + [o_projection] + _row_items(tile_rows, ROW_CHUNK, post_mix_rows))

        def ffn(side_items):
            _swiglu(hb.at[slot], wgu_ref, wd_ref, act, yffn.at[slot], tile_rows, side_items)

        def post_ffn_rows(rows, gate):
            y = _gated(yffn[slot, rows, :], gate)
            out_ref[rows, :] = hs3[slot, rows, :] + _rms(y, _gain(gains_ref, G_FFN_POST))

        return attention, ffn, _row_items(tile_rows, ROW_CHUNK, post_ffn_rows)

    _run_pipelined(step, n_tiles, (hs3, hb, yffn), stages)


def _layer_b_call(hs, kv, metakv, bias, sinks, gains, wq, bq, wo, bo, wgu, wd, *, layer, tm):
    batch, seq, _ = hs.shape
    tiles_per_batch = seq // tm
    n_tiles = batch * tiles_per_batch
    n_att_units = (tm // BLOCK) * N_KV_HEADS
    lane_head = jnp.arange(2 * HEAD_DIM)[None, :] // HEAD_DIM
    row_head = jnp.arange(4 * BLOCK)[:, None] // (2 * BLOCK)
    rowsum_cols = (lane_head == row_head).astype(jnp.bfloat16)
    kern = functools.partial(_layer_b_kernel, tm=tm, tiles_per_batch=tiles_per_batch, n_tiles=n_tiles)

    def att_tile(s):
        tile = jnp.minimum(s, n_tiles - 1)
        return tile // tiles_per_batch, tile % tiles_per_batch

    def epilogue_tile(s):
        tile = jnp.maximum(s - (PIPELINE_DEPTH - 1), 0)
        return tile // tiles_per_batch, tile % tiles_per_batch

    return pl.pallas_call(
        kern,
        grid=(n_tiles + PIPELINE_DEPTH - 1,),
        in_specs=[
            pl.BlockSpec((None, tm, D_MODEL), lambda s: (*att_tile(s), 0)),
            pl.BlockSpec((None, seq, 2 * KV_DIM), lambda s: (att_tile(s)[0], 0, 0), pipeline_mode=pl.Buffered(1)),
            _const_spec(metakv.shape),
            _const_spec(rowsum_cols.shape),
            _const_spec(bias.shape),
            pl.BlockSpec(memory_space=pltpu.SMEM),
            _const_spec(gains.shape),
            _const_spec(wq.shape),
            _const_spec(bq.shape),
            _const_spec(wo.shape),
            _const_spec(bo.shape),
            _layer_spec(wgu.shape, layer),
            _layer_spec(wd.shape, layer),
        ],
        out_specs=pl.BlockSpec((None, tm, D_MODEL), lambda s: (*epilogue_tile(s), 0)),
        out_shape=jax.ShapeDtypeStruct((batch, seq, D_MODEL), jnp.float32),
        scratch_shapes=[
            pltpu.VMEM((tm, D_MODEL), jnp.bfloat16),
            pltpu.VMEM((tm, D_MODEL), jnp.bfloat16),
            pltpu.VMEM((tm, D_MODEL), jnp.bfloat16),
            pltpu.VMEM((tm, D_MODEL), jnp.float32),
            pltpu.VMEM((2, tm, D_MODEL), jnp.float32),
            pltpu.VMEM((2, tm, D_MODEL), jnp.bfloat16),
            pltpu.VMEM((2, tm, D_MODEL), jnp.float32),
            pltpu.VMEM((tm, D_FF), jnp.bfloat16),
            pltpu.VMEM((n_att_units, ATT_ROWS, 4 * BLOCK), jnp.float32),
            pltpu.VMEM((n_att_units, ATT_ROWS, 2 * BLOCK), jnp.float32),
            pltpu.VMEM((n_att_units, ATT_ROWS, 4 * BLOCK), jnp.bfloat16),
        ],
        compiler_params=pltpu.CompilerParams(
            dimension_semantics=("arbitrary",), vmem_limit_bytes=V7X_VMEM_LIMIT_BYTES),
        name="layer_b",
    )(hs, kv, metakv, rowsum_cols, bias, sinks, gains, wq, bq, wo, bo, wgu, wd)


def _cast_gate_up_kernel(w_ref, out_ref):
    for b in range(D_FF // LANES):
        out_ref[:, 2 * b * LANES:(2 * b + 1) * LANES] = w_ref[:, b * LANES:(b + 1) * LANES].astype(jnp.bfloat16)
        out_ref[:, (2 * b + 1) * LANES:(2 * b + 2) * LANES] = (
            w_ref[:, D_FF + b * LANES:D_FF + (b + 1) * LANES].astype(jnp.bfloat16))


def _cast_kernel(w_ref, out_ref):
    out_ref[...] = w_ref[...].astype(jnp.bfloat16)


def _cast_weights(w, body, name):
    layers, rows, cols = w.shape
    w2d = w.reshape(layers * rows, cols)
    target = max(16, CAST_BLOCK_BYTES // (4 * cols) // 16 * 16)
    block_rows = next(r for r in range(target, 0, -16) if (layers * rows) % r == 0)
    spec = pl.BlockSpec((block_rows, cols), lambda i: (i, 0))
    out = pl.pallas_call(
        body,
        grid=(layers * rows // block_rows,),
        in_specs=[spec],
        out_specs=spec,
        out_shape=jax.ShapeDtypeStruct(w2d.shape, jnp.bfloat16),
        compiler_params=pltpu.CompilerParams(dimension_semantics=("arbitrary",)),
        name=name,
    )(w2d)
    return out.reshape(layers, rows, cols)


def _gain_table(rows):
    table = jnp.zeros((8, D_MODEL), jnp.float32)
    for i, r in rows.items():
        table = table.at[i].set(r.astype(jnp.float32))
    return table


def kernel(x, meta_tokens, norm_mix_pre, norm_mix_post, norm_ffn_pre, norm_ffn_post, pool_w, pool_scale, kv_norm, w_k, b_k, w_v, b_v, w_q, b_q, w_o, b_o, sinks, rel_bias, w_gate_up, w_down):
    batch, seq, _ = x.shape
    bf16 = jnp.bfloat16

    gains_a = _gain_table({G_MIX_PRE: norm_mix_pre[0], G_MIX_POST: norm_mix_post[0], G_POOL_SCALE: pool_scale[0],
                           G_FFN_PRE: norm_ffn_pre[0], G_FFN_POST: norm_ffn_post[0], G_KV: kv_norm})
    gains_b = _gain_table({G_MIX_PRE: norm_mix_pre[1], G_MIX_POST: norm_mix_post[1],
                           G_FFN_PRE: norm_ffn_pre[1], G_FFN_POST: norm_ffn_post[1]})
    wkv = jnp.concatenate([w_k, w_v], axis=1).astype(bf16)
    bkv = jnp.concatenate([b_k, b_v])[None, :]
    wgu = _cast_weights(w_gate_up, _cast_gate_up_kernel, "cast_gate_up")
    wd = _cast_weights(w_down, _cast_kernel, "cast_down")
    layer_a_weights = (gains_a, pool_w[0].astype(bf16), wgu, wd, wkv, bkv)

    x2d = x.reshape(batch * seq, D_MODEL)
    hs2, kv_x = _layer_a_call(x2d, meta_tokens, *layer_a_weights,
                              layer=0, tm=TILE_A, tiles_per_batch=seq // TILE_A, is_meta=False)
    _, kv_meta = _layer_a_call(meta_tokens, meta_tokens, *layer_a_weights,
                               layer=0, tm=N_META, tiles_per_batch=1, is_meta=True)
    metakv = jnp.concatenate([jnp.zeros((PAD_FRONT, 2 * KV_DIM), bf16), kv_meta], axis=0)

    bias = _bias_table_call(rel_bias)
    bias = bias.reshape(2, N_KV_HEADS, ATT_ROWS, 4 * BLOCK)

    out = _layer_b_call(hs2.reshape(batch, seq, D_MODEL), kv_x.reshape(batch, seq, 2 * KV_DIM), metakv, bias,
                        sinks[0], gains_b, w_q[0].astype(bf16), b_q, w_o[0].astype(bf16), b_o,
                        wgu, wd, layer=1, tm=TILE_B)
    return out
```

```python
import functools
import math

import numpy as np

import jax
import jax.numpy as jnp
from jax import lax
from jax.experimental import pallas as pl
from jax.experimental.pallas import tpu as pltpu

D_MODEL = 1024
N_META = 16
POOL_WINDOWS = (2, 4, 8, 16)
POOL_GROUP_DIM = D_MODEL // len(POOL_WINDOWS)
HEAD_DIM = 64
N_Q_HEADS = D_MODEL // HEAD_DIM
N_KV_HEADS = 2
HEADS_PER_KV = N_Q_HEADS // N_KV_HEADS
PAIRS_PER_KV = HEADS_PER_KV // 2
N_HEAD_PAIRS = N_Q_HEADS // 2
KV_DIM = N_KV_HEADS * HEAD_DIM
WINDOW = 128
BLOCK = 128
N_BUCKETS = 32
MAX_DISTANCE = 128
D_FF = 2816
EPS = 1e-6
PAD_FRONT = (-N_META) % BLOCK

LANES = 128
GATE_ROWS = 8
FFN_COLS = 256
SILU_ROW_CHUNK = 64
UNRIDDEN_SEGMENTS = 2
CAST_BLOCK_BYTES = 6 * 1024 * 1024
HALO = 16
ROW_CHUNK = 32
POOL_ROW_CHUNK = 64
ATT_ROWS = PAIRS_PER_KV * BLOCK
SOFTMAX_ROW_CHUNK = 16
TILE_A = 512
PIPELINE_DEPTH = 3
TILE_B = 256
V7X_VMEM_LIMIT_BYTES = 56 * 1024 * 1024

G_MIX_PRE, G_MIX_POST, G_POOL_SCALE, G_FFN_PRE, G_FFN_POST, G_KV = range(6)


def _rms(x, g):
    ms = jnp.sum(x * x, axis=-1, keepdims=True) * (1.0 / D_MODEL)
    return x * lax.rsqrt(ms + EPS) * g


def _gain(gains_ref, i):
    return gains_ref[i:i + 1, :]


def _row_loop(rows, chunk, body):
    chunk = min(chunk, rows.size)
    for i in range(rows.size // chunk):
        body(pl.ds(rows.start + i * chunk, chunk))


def _row_items(rows, chunk, body):
    chunk = min(chunk, rows.size)
    return [functools.partial(body, pl.ds(rows.start + i * chunk, chunk)) for i in range(rows.size // chunk)]


def _run_items(items, gate=None):
    for item in items:
        item(gate)


def _gate_of(value):
    bits = pltpu.bitcast(value[-GATE_ROWS:, -LANES:], jnp.uint32)
    return pltpu.bitcast((bits >> 16) >> 16, jnp.float32)


def _gated(x, gate):
    if gate is None:
        return x
    z = jnp.concatenate([gate] * (x.shape[0] // GATE_ROWS), axis=0)
    return jnp.concatenate([x[:, :LANES] + z, x[:, LANES:]], axis=1)


def _spread(items, n_bins):
    bins = [[] for _ in range(n_bins)]
    for k, item in enumerate(items):
        bins[k * n_bins // max(len(items), 1)].append(item)
    return bins


def _swiglu(h_ref, wgu_ref, wd_ref, gbuf, ubuf, act, out_ref, rows, side_items=()):
    segments = ([("gate_up", c) for c in range(0, D_FF, FFN_COLS)]
                + [("down", c) for c in range(0, D_MODEL, FFN_COLS)])
    side = _spread(list(side_items), len(segments) - UNRIDDEN_SEGMENTS) + [[]] * UNRIDDEN_SEGMENTS
    for (kind, c0), side_group in zip(segments, side):
        if kind == "gate_up":
            g = jnp.dot(h_ref[rows, :], wgu_ref[:, c0:c0 + FFN_COLS], preferred_element_type=jnp.float32)
            gbuf[rows, :] = g
            ubuf[rows, :] = jnp.dot(h_ref[rows, :], wgu_ref[:, D_FF + c0:D_FF + c0 + FFN_COLS],
                                    preferred_element_type=jnp.float32)
            gate_source = g

            def silu_rows(r, gate, c0=c0):
                half_g = 0.5 * gbuf[r, :]
                silu = half_g + half_g * jnp.tanh(half_g)
                act[r, c0:c0 + FFN_COLS] = (silu * ubuf[r, :]).astype(jnp.bfloat16)

            _run_items(_row_items(rows, SILU_ROW_CHUNK, silu_rows))
        else:
            y = jnp.dot(act[rows, :], wd_ref[:, c0:c0 + FFN_COLS], preferred_element_type=jnp.float32)
            out_ref[rows, c0:c0 + FFN_COLS] = y
            gate_source = y
        if side_group:
            _run_items(side_group, _gate_of(gate_source))


def _run_pipelined(step, n_tiles, carried, stages):
    if n_tiles == 1:
        mixer, ffn, epilogue = stages(0)
        _run_items(mixer)
        ffn(())
        _run_items(epilogue)
        return

    last_step = n_tiles + 1

    @pl.when(step == 0)
    def _():
        for ref in carried:
            ref[...] = jnp.zeros(ref.shape, ref.dtype)
        _run_items(stages(0)[0])

    for parity in range(2):
        @pl.when((step % 2 == parity) & (step > 0) & (step < last_step))
        def _(parity=parity):
            mixer, _, epilogue = stages(parity)
            _, ffn, _ = stages(1 - parity)
            ffn(epilogue + mixer)

    @pl.when(step == last_step)
    def _():
        _run_items(stages(last_step % 2)[2])


def _layer_a_kernel(x_ref, prev_ref, meta_ref, gains_ref, pool_w_ref, wgu_ref, wd_ref, wkv_ref, bkv_ref,
                    hs_out_ref, kv_out_ref,
                    hext, pbuf, ymix, hs1, hb, yffn, gbuf, ubuf, act, kvin,
                    *, tm, tiles_per_batch, n_tiles, is_meta):
    g_mix_pre = _gain(gains_ref, G_MIX_PRE)
    step = pl.program_id(0)
    tile_rows = pl.ds(0, tm)

    def stages(slot):
        def halo_rows(gate):
            if is_meta:
                hext[0:HALO, :] = jnp.zeros((HALO, D_MODEL), jnp.float32)
            else:
                first_of_batch = (jnp.minimum(step, n_tiles - 1) % tiles_per_batch) == 0
                prev = jnp.where(first_of_batch, meta_ref[...], prev_ref[...])
                hext[0:HALO, :] = _rms(prev, g_mix_pre)

        def norm_rows(rows, gate):
            hext[pl.ds(rows.start + HALO, rows.size), :] = _rms(_gated(x_ref[rows, :], gate), g_mix_pre)

        def pool_rows(rows, gate):
            for gi, w in enumerate(POOL_WINDOWS):
                cols = slice(gi * POOL_GROUP_DIM, (gi + 1) * POOL_GROUP_DIM)
                e0 = _gated(hext[pl.ds(rows.start, rows.size + HALO), cols], gate)
                e = e0
                shift = 1
                while shift < w:
                    e = e + pltpu.roll(e, shift, 0)
                    shift *= 2
                win = e[HALO:, :]
                if is_meta:
                    pos = lax.broadcasted_iota(jnp.int32, win.shape, 0) + 1
                    cnt = jnp.minimum(pos, w).astype(jnp.float32)
                    pooled = win / cnt - e0[HALO:, :]
                else:
                    pooled = win * (1.0 / w) - e0[HALO:, :]
                pbuf[rows, cols] = pooled.astype(jnp.bfloat16)

        def group_matmul(gi, gate):
            cols = slice(gi * POOL_GROUP_DIM, (gi + 1) * POOL_GROUP_DIM)
            ymix[:, cols] = jnp.dot(pbuf[:, cols], pool_w_ref[gi], preferred_element_type=jnp.float32)

        def post_mix_rows(rows, gate):
            y = _gated(ymix[rows, :], gate)
            h1 = x_ref[rows, :] + _rms(y * _gain(gains_ref, G_POOL_SCALE), _gain(gains_ref, G_MIX_POST))
            hs1[slot, rows, :] = h1
            hb[slot, rows, :] = _rms(h1, _gain(gains_ref, G_FFN_PRE)).astype(jnp.bfloat16)

        mixer = ([halo_rows] + _row_items(tile_rows, ROW_CHUNK, norm_rows)
                 + _row_items(tile_rows, POOL_ROW_CHUNK, pool_rows)
                 + [functools.partial(group_matmul, gi) for gi in range(len(POOL_WINDOWS))]
                 + _row_items(tile_rows, ROW_CHUNK, post_mix_rows))

        def ffn(side_items):
            _swiglu(hb.at[slot], wgu_ref, wd_ref, gbuf, ubuf, act, yffn.at[slot], tile_rows, side_items)

        def post_ffn_rows(rows, gate):
            h2 = hs1[slot, rows, :] + _rms(_gated(yffn[slot, rows, :], gate), _gain(gains_ref, G_FFN_POST))
            hs_out_ref[rows, :] = h2
            kvin[rows, :] = _rms(h2, _gain(gains_ref, G_KV)).astype(jnp.bfloat16)

        def kv_projection(gate):
            kv = jnp.dot(kvin[...], wkv_ref[...], preferred_element_type=jnp.float32) + bkv_ref[...]
            kv_out_ref[...] = kv.astype(jnp.bfloat16)

        epilogue = _row_items(tile_rows, ROW_CHUNK, post_ffn_rows) + [kv_projection]
        return mixer, ffn, epilogue

    _run_pipelined(step, n_tiles, (hs1, hb, yffn), stages)


def _const_spec(shape):
    nd = len(shape)
    return pl.BlockSpec(shape, lambda *_: (0,) * nd, pipeline_mode=pl.Buffered(1))


def _layer_spec(stacked_shape, layer):
    _, rows, cols = stacked_shape
    return pl.BlockSpec((None, rows, cols), lambda *_: (layer, 0, 0), pipeline_mode=pl.Buffered(1))


def _layer_a_call(x2d, meta, gains, pool_w, wgu, wd, wkv, bkv, *, layer, tm, tiles_per_batch, is_meta):
    n_rows = x2d.shape[0]
    n_tiles = n_rows // tm
    n_steps = n_tiles if n_tiles == 1 else n_tiles + PIPELINE_DEPTH - 1
    halo_blocks_per_tile = tm // HALO
    kern = functools.partial(_layer_a_kernel, tm=tm, tiles_per_batch=tiles_per_batch, n_tiles=n_tiles,
                             is_meta=is_meta)

    def mixer_tile(s):
        return jnp.minimum(s, n_tiles - 1)

    def epilogue_tile(s):
        return jnp.maximum(s - (n_steps - n_tiles), 0)

    return pl.pallas_call(
        kern,
        grid=(n_steps,),
        in_specs=[
            pl.BlockSpec((tm, D_MODEL), lambda s: (mixer_tile(s), 0)),
            pl.BlockSpec((HALO, D_MODEL), lambda s: (jnp.maximum(mixer_tile(s) * halo_blocks_per_tile - 1, 0), 0)),
            _const_spec((N_META, D_MODEL)),
            _const_spec(gains.shape),
            _const_spec(pool_w.shape),
            _layer_spec(wgu.shape, layer),
            _layer_spec(wd.shape, layer),
            _const_spec(wkv.shape),
            _const_spec(bkv.shape),
        ],
        out_specs=[
            pl.BlockSpec((tm, D_MODEL), lambda s: (epilogue_tile(s), 0)),
            pl.BlockSpec((tm, 2 * KV_DIM), lambda s: (epilogue_tile(s), 0)),
        ],
        out_shape=[
            jax.ShapeDtypeStruct((n_rows, D_MODEL), jnp.float32),
            jax.ShapeDtypeStruct((n_rows, 2 * KV_DIM), jnp.bfloat16),
        ],
        scratch_shapes=[
            pltpu.VMEM((tm + HALO, D_MODEL), jnp.float32),
            pltpu.VMEM((tm, D_MODEL), jnp.bfloat16),
            pltpu.VMEM((tm, D_MODEL), jnp.float32),
            pltpu.VMEM((2, tm, D_MODEL), jnp.float32),
            pltpu.VMEM((2, tm, D_MODEL), jnp.bfloat16),
            pltpu.VMEM((2, tm, D_MODEL), jnp.float32),
            pltpu.VMEM((tm, FFN_COLS), jnp.float32),
            pltpu.VMEM((tm, FFN_COLS), jnp.float32),
            pltpu.VMEM((tm, D_FF), jnp.bfloat16),
            pltpu.VMEM((tm, D_MODEL), jnp.bfloat16),
        ],
        compiler_params=pltpu.CompilerParams(
            dimension_semantics=("arbitrary",), vmem_limit_bytes=V7X_VMEM_LIMIT_BYTES),
        name="layer_a_meta" if is_meta else "layer_a",
    )(x2d, x2d, meta, gains, pool_w, wgu, wd, wkv, bkv)


def _bucket_distance_ranges():
    d = np.arange(WINDOW)
    max_exact = N_BUCKETS // 2
    df = np.maximum(d, 1).astype(np.float32)
    large = max_exact + (np.log(df / np.float32(max_exact)) / np.float32(math.log(MAX_DISTANCE / max_exact))
                         * np.float32(N_BUCKETS - max_exact)).astype(np.int32)
    bucket = np.where(d < max_exact, d, np.minimum(large, N_BUCKETS - 1))
    ranges = []
    for b in range(N_BUCKETS):
        members = d[bucket == b]
        if members.size:
            assert np.array_equal(members, np.arange(members[0], members[-1] + 1))
            ranges.append((int(members[0]), int(members[-1])))
        else:
            ranges.append(None)
    return ranges


def _bias_table_kernel(rel_bias_ref, out_ref):
    m = pl.program_id(0)
    q = lax.broadcasted_iota(jnp.int32, (BLOCK, 2 * BLOCK), 0)
    s = lax.broadcasted_iota(jnp.int32, (BLOCK, 2 * BLOCK), 1)
    d = q + BLOCK - s
    in_window = (d >= 0) & (d < WINDOW)
    valid = (in_window & (s >= PAD_FRONT), in_window)
    for half in range(2):
        h = 2 * m + half
        acc = jnp.zeros((BLOCK, 2 * BLOCK), jnp.float32)
        for b, distances in enumerate(_bucket_distance_ranges()):
            if distances is not None:
                lo, hi = distances
                acc = jnp.where((d >= lo) & (d <= hi), rel_bias_ref[b, h], acc)
        for jsel in range(2):
            out_ref[jsel, 0, :, half * 2 * BLOCK:(half + 1) * 2 * BLOCK] = jnp.where(valid[jsel], acc, -jnp.inf)


def _bias_table_call(rel_bias):
    return pl.pallas_call(
        _bias_table_kernel,
        grid=(N_HEAD_PAIRS,),
        in_specs=[
            pl.BlockSpec(memory_space=pltpu.SMEM),
        ],
        out_specs=pl.BlockSpec((2, 1, BLOCK, 4 * BLOCK), lambda m: (0, m, 0, 0)),
        out_shape=jax.ShapeDtypeStruct((2, N_HEAD_PAIRS, BLOCK, 4 * BLOCK), jnp.float32),
        compiler_params=pltpu.CompilerParams(dimension_semantics=("arbitrary",)),
        name="bias_table",
    )(rel_bias)


def _layer_b_kernel(hs_ref, kv_ref, metakv_ref, rowsum_cols_ref, bias_ref, sinks_ref, gains_ref,
                    wq_ref, bq_ref, wo_ref, bo_ref, wgu_ref, wd_ref,
                    out_ref,
                    hn, qbuf, obuf, yatt, hs3, hb, yffn, gbuf, ubuf, act, lbuf, mbuf, pbuf,
                    *, tm, tiles_per_batch, n_tiles):
    blocks_per_tile = tm // BLOCK
    step = pl.program_id(0)
    t = jnp.minimum(step, n_tiles - 1) % tiles_per_batch
    tile_rows = pl.ds(0, tm)

    lane = lax.broadcasted_iota(jnp.int32, (2 * BLOCK, 2 * HEAD_DIM), 1)
    low_half = lane < HEAD_DIM
    zero = jnp.zeros((2 * BLOCK, 2 * HEAD_DIM), jnp.bfloat16)
    low_half_q = lax.broadcasted_iota(jnp.int32, (BLOCK, 2 * HEAD_DIM), 1) < HEAD_DIM

    def per_kv_head_operands(pair_cols):
        swapped = jnp.concatenate([pair_cols[:, HEAD_DIM:], pair_cols[:, :HEAD_DIM]], axis=1)
        g0 = (jnp.where(low_half, pair_cols, zero), jnp.where(low_half, zero, swapped))
        g1 = (jnp.where(low_half, swapped, zero), jnp.where(low_half, zero, pair_cols))
        return g0, g1

    ones_top = rowsum_cols_ref[0:2 * BLOCK, :]
    ones_bottom = rowsum_cols_ref[2 * BLOCK:4 * BLOCK, :]

    def attention_block(jb, gate):
        j = t * blocks_per_tile + jb
        own = kv_ref[pl.ds(pl.multiple_of(j * BLOCK, BLOCK), BLOCK), :]
        prev_start = pl.multiple_of(jnp.maximum(j - 1, 0) * BLOCK, BLOCK)
        prev = jnp.where(j == 0, metakv_ref[...], kv_ref[pl.ds(prev_start, BLOCK), :])
        kvb = jnp.concatenate([prev, own], axis=0)
        k_ops = per_kv_head_operands(kvb[:, :KV_DIM])
        v_ops = per_kv_head_operands(kvb[:, KV_DIM:])
        jsel = jnp.minimum(j, 1)
        qrows = slice(jb * BLOCK, (jb + 1) * BLOCK)
        for g in range(N_KV_HEADS):
            u = jb * N_KV_HEADS + g
            pairs = range(g * PAIRS_PER_KV, (g + 1) * PAIRS_PER_KV)
            q4 = jnp.concatenate([qbuf[qrows, m * 2 * HEAD_DIM:(m + 1) * 2 * HEAD_DIM] for m in pairs], axis=0)
            k_op = jnp.concatenate(k_ops[g], axis=0)
            v_op = jnp.concatenate([jnp.concatenate([v_ops[g][0], ones_top], axis=1),
                                    jnp.concatenate([v_ops[g][1], ones_bottom], axis=1)], axis=0)
            lbuf[u] = _gated(lax.dot_general(q4, k_op, (((1,), (1,)), ((), ())),
                                             preferred_element_type=jnp.float32) + bias_ref[jsel, g], gate)
            for r0 in range(0, ATT_ROWS, SOFTMAX_ROW_CHUNK):
                rows = slice(r0, r0 + SOFTMAX_ROW_CHUNK)
                for half in range(2):
                    sink = sinks_ref[2 * (pairs[0] + r0 // BLOCK) + half]
                    lg = lbuf[u, rows, half * 2 * BLOCK:(half + 1) * 2 * BLOCK]
                    mx = jnp.maximum(jnp.max(lg, axis=-1, keepdims=True), sink)
                    mbuf[u, rows, half * BLOCK:(half + 1) * BLOCK] = jnp.broadcast_to(mx, (SOFTMAX_ROW_CHUNK, BLOCK))
            for r0 in range(0, ATT_ROWS, SOFTMAX_ROW_CHUNK):
                rows = slice(r0, r0 + SOFTMAX_ROW_CHUNK)
                for half in range(2):
                    mrep = mbuf[u, rows, half * BLOCK:(half + 1) * BLOCK]
                    lg = lbuf[u, rows, half * 2 * BLOCK:(half + 1) * 2 * BLOCK]
                    p = jnp.exp(lg - jnp.concatenate([mrep, mrep], axis=1))
                    pbuf[u, rows, half * 2 * BLOCK:(half + 1) * 2 * BLOCK] = p.astype(jnp.bfloat16)
            oa = jnp.dot(pbuf[u], v_op, preferred_element_type=jnp.float32)
            for i, m in enumerate(pairs):
                rows = slice(i * BLOCK, (i + 1) * BLOCK)
                sink_gap = jnp.where(low_half_q, sinks_ref[2 * m] - mbuf[u, rows, 0:BLOCK],
                                     sinks_ref[2 * m + 1] - mbuf[u, rows, BLOCK:2 * BLOCK])
                denom = oa[rows, 2 * HEAD_DIM:] + jnp.exp(sink_gap)
                o = oa[rows, :2 * HEAD_DIM] * (1.0 / denom)
                obuf[qrows, m * 2 * HEAD_DIM:(m + 1) * 2 * HEAD_DIM] = o.astype(jnp.bfloat16)

    def stages(slot):
        def norm_rows(rows, gate):
            h = _rms(_gated(hs_ref[rows, :], gate), _gain(gains_ref, G_MIX_PRE))
            hn[rows, :] = h.astype(jnp.bfloat16)

        def q_projection(gate):
            q = jnp.dot(hn[...], wq_ref[...], preferred_element_type=jnp.float32) + bq_ref[...]
            qbuf[...] = (q * (HEAD_DIM ** -0.5)).astype(jnp.bfloat16)

        def o_projection(gate):
            yatt[...] = jnp.dot(obuf[...], wo_ref[...], preferred_element_type=jnp.float32) + bo_ref[...]

        def post_mix_rows(rows, gate):
            h3 = hs_ref[rows, :] + _rms(_gated(yatt[rows, :], gate), _gain(gains_ref, G_MIX_POST))
            hs3[slot, rows, :] = h3
            hb[slot, rows, :] = _rms(h3, _gain(gains_ref, G_FFN_PRE)).astype(jnp.bfloat16)

        attention = (_row_items(tile_rows, ROW_CHUNK, norm_rows) + [q_projection]
                     + [functools.partial(attention_block, jb) for jb in range(blocks_per_tile)]
                     + [o_projection] + _row_items(tile_rows, ROW_CHUNK, post_mix_rows))

        def ffn(side_items):
            _swiglu(hb.at[slot], wgu_ref, wd_ref, gbuf, ubuf, act, yffn.at[slot], tile_rows, side_items)

        def post_ffn_rows(rows, gate):
            y = _gated(yffn[slot, rows, :], gate)
            out_ref[rows, :] = hs3[slot, rows, :] + _rms(y, _gain(gains_ref, G_FFN_POST))

        return attention, ffn, _row_items(tile_rows, ROW_CHUNK, post_ffn_rows)

    _run_pipelined(step, n_tiles, (hs3, hb, yffn), stages)


def _layer_b_call(hs, kv, metakv, bias, sinks, gains, wq, bq, wo, bo, wgu, wd, *, layer, tm):
    batch, seq, _ = hs.shape
    tiles_per_batch = seq // tm
    n_tiles = batch * tiles_per_batch
    n_att_units = (tm // BLOCK) * N_KV_HEADS
    lane_head = jnp.arange(2 * HEAD_DIM)[None, :] // HEAD_DIM
    row_head = jnp.arange(4 * BLOCK)[:, None] // (2 * BLOCK)
    rowsum_cols = (lane_head == row_head).astype(jnp.bfloat16)
    kern = functools.partial(_layer_b_kernel, tm=tm, tiles_per_batch=tiles_per_batch, n_tiles=n_tiles)

    def att_tile(s):
        tile = jnp.minimum(s, n_tiles - 1)
        return tile // tiles_per_batch, tile % tiles_per_batch

    def epilogue_tile(s):
        tile = jnp.maximum(s - (PIPELINE_DEPTH - 1), 0)
        return tile // tiles_per_batch, tile % tiles_per_batch

    return pl.pallas_call(
        kern,
        grid=(n_tiles + PIPELINE_DEPTH - 1,),
        in_specs=[
            pl.BlockSpec((None, tm, D_MODEL), lambda s: (*att_tile(s), 0)),
            pl.BlockSpec((None, seq, 2 * KV_DIM), lambda s: (att_tile(s)[0], 0, 0), pipeline_mode=pl.Buffered(1)),
            _const_spec(metakv.shape),
            _const_spec(rowsum_cols.shape),
            _const_spec(bias.shape),
            pl.BlockSpec(memory_space=pltpu.SMEM),
            _const_spec(gains.shape),
            _const_spec(wq.shape),
            _const_spec(bq.shape),
            _const_spec(wo.shape),
            _const_spec(bo.shape),
            _layer_spec(wgu.shape, layer),
            _layer_spec(wd.shape, layer),
        ],
        out_specs=pl.BlockSpec((None, tm, D_MODEL), lambda s: (*epilogue_tile(s), 0)),
        out_shape=jax.ShapeDtypeStruct((batch, seq, D_MODEL), jnp.float32),
        scratch_shapes=[
            pltpu.VMEM((tm, D_MODEL), jnp.bfloat16),
            pltpu.VMEM((tm, D_MODEL), jnp.bfloat16),
            pltpu.VMEM((tm, D_MODEL), jnp.bfloat16),
            pltpu.VMEM((tm, D_MODEL), jnp.float32),
            pltpu.VMEM((2, tm, D_MODEL), jnp.float32),
            pltpu.VMEM((2, tm, D_MODEL), jnp.bfloat16),
            pltpu.VMEM((2, tm, D_MODEL), jnp.float32),
            pltpu.VMEM((tm, FFN_COLS), jnp.float32),
            pltpu.VMEM((tm, FFN_COLS), jnp.float32),
            pltpu.VMEM((tm, D_FF), jnp.bfloat16),
            pltpu.VMEM((n_att_units, ATT_ROWS, 4 * BLOCK), jnp.float32),
            pltpu.VMEM((n_att_units, ATT_ROWS, 2 * BLOCK), jnp.float32),
            pltpu.VMEM((n_att_units, ATT_ROWS, 4 * BLOCK), jnp.bfloat16),
        ],
        compiler_params=pltpu.CompilerParams(
            dimension_semantics=("arbitrary",), vmem_limit_bytes=V7X_VMEM_LIMIT_BYTES),
        name="layer_b",
    )(hs, kv, metakv, rowsum_cols, bias, sinks, gains, wq, bq, wo, bo, wgu, wd)


def _cast_kernel(w_ref, out_ref):
    out_ref[...] = w_ref[...].astype(jnp.bfloat16)


def _cast_weights(w, body, name):
    layers, rows, cols = w.shape
    w2d = w.reshape(layers * rows, cols)
    target = max(16, CAST_BLOCK_BYTES // (4 * cols) // 16 * 16)
    block_rows = next(r for r in range(target, 0, -16) if (layers * rows) % r == 0)
    spec = pl.BlockSpec((block_rows, cols), lambda i: (i, 0))
    out = pl.pallas_call(
        body,
        grid=(layers * rows // block_rows,),
        in_specs=[spec],
        out_specs=spec,
        out_shape=jax.ShapeDtypeStruct(w2d.shape, jnp.bfloat16),
        compiler_params=pltpu.CompilerParams(dimension_semantics=("arbitrary",)),
        name=name,
    )(w2d)
    return out.reshape(layers, rows, cols)


def _gain_table(rows):
    table = jnp.zeros((8, D_MODEL), jnp.float32)
    for i, r in rows.items():
        table = table.at[i].set(r.astype(jnp.float32))
    return table


def kernel(x, meta_tokens, norm_mix_pre, norm_mix_post, norm_ffn_pre, norm_ffn_post, pool_w, pool_scale, kv_norm, w_k, b_k, w_v, b_v, w_q, b_q, w_o, b_o, sinks, rel_bias, w_gate_up, w_down):
    batch, seq, _ = x.shape
    bf16 = jnp.bfloat16

    gains_a = _gain_table({G_MIX_PRE: norm_mix_pre[0], G_MIX_POST: norm_mix_post[0], G_POOL_SCALE: pool_scale[0],
                           G_FFN_PRE: norm_ffn_pre[0], G_FFN_POST: norm_ffn_post[0], G_KV: kv_norm})
    gains_b = _gain_table({G_MIX_PRE: norm_mix_pre[1], G_MIX_POST: norm_mix_post[1],
                           G_FFN_PRE: norm_ffn_pre[1], G_FFN_POST: norm_ffn_post[1]})
    wkv = jnp.concatenate([w_k, w_v], axis=1).astype(bf16)
    bkv = jnp.concatenate([b_k, b_v])[None, :]
    wgu = _cast_weights(w_gate_up, _cast_kernel, "cast_gate_up")
    wd = _cast_weights(w_down, _cast_kernel, "cast_down")
    layer_a_weights = (gains_a, pool_w[0].astype(bf16), wgu, wd, wkv, bkv)

    x2d = x.reshape(batch * seq, D_MODEL)
    hs2, kv_x = _layer_a_call(x2d, meta_tokens, *layer_a_weights,
                              layer=0, tm=TILE_A, tiles_per_batch=seq // TILE_A, is_meta=False)
    _, kv_meta = _layer_a_call(meta_tokens, meta_tokens, *layer_a_weights,
                               layer=0, tm=N_META, tiles_per_batch=1, is_meta=True)
    metakv = jnp.concatenate([jnp.zeros((PAD_FRONT, 2 * KV_DIM), bf16), kv_meta], axis=0)

    bias = _bias_table_call(rel_bias)
    bias = bias.reshape(2, N_KV_HEADS, ATT_ROWS, 4 * BLOCK)

    out = _layer_b_call(hs2.reshape(batch, seq, D_MODEL), kv_x.reshape(batch, seq, 2 * KV_DIM), metakv, bias,
                        sinks[0], gains_b, w_q[0].astype(bf16), b_q, w_o[0].astype(bf16), b_o,
                        wgu, wd, layer=1, tm=TILE_B)
    return out
```

```python
import functools
import math

import numpy as np

import jax
import jax.numpy as jnp
from jax import lax
from jax.experimental import pallas as pl
from jax.experimental.pallas import tpu as pltpu

D_MODEL = 1024
N_META = 16
POOL_WINDOWS = (2, 4, 8, 16)
POOL_GROUP_DIM = D_MODEL // len(POOL_WINDOWS)
HEAD_DIM = 64
N_Q_HEADS = D_MODEL // HEAD_DIM
N_KV_HEADS = 2
HEADS_PER_KV = N_Q_HEADS // N_KV_HEADS
PAIRS_PER_KV = HEADS_PER_KV // 2
N_HEAD_PAIRS = N_Q_HEADS // 2
KV_DIM = N_KV_HEADS * HEAD_DIM
WINDOW = 128
BLOCK = 128
N_BUCKETS = 32
MAX_DISTANCE = 128
D_FF = 2816
EPS = 1e-6
PAD_FRONT = (-N_META) % BLOCK

LANES = 128
GATE_ROWS = 8
FFN_COLS = 256
SILU_ROW_CHUNK = 64
UNRIDDEN_SEGMENTS = 2
CAST_BLOCK_BYTES = 6 * 1024 * 1024
HALO = 16
ROW_CHUNK = 32
POOL_ROW_CHUNK = 64
ATT_ROWS = PAIRS_PER_KV * BLOCK
SOFTMAX_ROW_CHUNK = 16
TILE_A = 512
PIPELINE_DEPTH = 3
TILE_B = 256
V7X_VMEM_LIMIT_BYTES = 56 * 1024 * 1024

G_MIX_PRE, G_MIX_POST, G_POOL_SCALE, G_FFN_PRE, G_FFN_POST, G_KV = range(6)


def _rms(x, g):
    ms = jnp.sum(x * x, axis=-1, keepdims=True) * (1.0 / D_MODEL)
    return x * lax.rsqrt(ms + EPS) * g


def _gain(gains_ref, i):
    return gains_ref[i:i + 1, :]


def _row_items(rows, chunk, body):
    chunk = min(chunk, rows.size)
    return [functools.partial(body, pl.ds(rows.start + i * chunk, chunk)) for i in range(rows.size // chunk)]


def _run_items(items, gate=None):
    for item in items:
        item(gate)


def _gate_of(value):
    bits = pltpu.bitcast(value[-GATE_ROWS:, -LANES:], jnp.uint32)
    return pltpu.bitcast((bits >> 16) >> 16, jnp.float32)


def _gated(x, gate):
    if gate is None:
        return x
    z = jnp.concatenate([gate] * (x.shape[0] // GATE_ROWS), axis=0)
    return jnp.concatenate([x[:, :LANES] + z, x[:, LANES:]], axis=1)


def _spread(items, n_bins):
    bins = [[] for _ in range(n_bins)]
    for k, item in enumerate(items):
        bins[k * n_bins // max(len(items), 1)].append(item)
    return bins


def _swiglu(h_ref, wgu_ref, wd_ref, gbuf, ubuf, act, out_ref, rows, side_items=()):
    segments = ([("gate_up", c) for c in range(0, D_FF, FFN_COLS)]
                + [("down", c) for c in range(0, D_MODEL, FFN_COLS)])
    side = _spread(list(side_items), len(segments) - UNRIDDEN_SEGMENTS) + [[]] * UNRIDDEN_SEGMENTS
    for (kind, c0), side_group in zip(segments, side):
        if kind == "gate_up":
            g = jnp.dot(h_ref[rows, :], wgu_ref[:, c0:c0 + FFN_COLS], preferred_element_type=jnp.float32)
            gbuf[rows, :] = g
            ubuf[rows, :] = jnp.dot(h_ref[rows, :], wgu_ref[:, D_FF + c0:D_FF + c0 + FFN_COLS],
                                    preferred_element_type=jnp.float32)
            gate_source = g

            def silu_rows(r, gate, c0=c0):
                half_g = 0.5 * gbuf[r, :]
                silu = half_g + half_g * jnp.tanh(half_g)
                act[r, c0:c0 + FFN_COLS] = (silu * ubuf[r, :]).astype(jnp.bfloat16)

            _run_items(_row_items(rows, SILU_ROW_CHUNK, silu_rows))
        else:
            y = jnp.dot(act[rows, :], wd_ref[:, c0:c0 + FFN_COLS], preferred_element_type=jnp.float32)
            out_ref[rows, c0:c0 + FFN_COLS] = y
            gate_source = y
        if side_group:
            _run_items(side_group, _gate_of(gate_source))


def _run_pipelined(step, n_tiles, carried, stages):
    if n_tiles == 1:
        mixer, ffn, epilogue = stages(0)
        _run_items(mixer)
        ffn(())
        _run_items(epilogue)
        return

    last_step = n_tiles + 1

    @pl.when(step == 0)
    def _():
        for ref in carried:
            ref[...] = jnp.zeros(ref.shape, ref.dtype)
        _run_items(stages(0)[0])

    for parity in range(2):
        @pl.when((step % 2 == parity) & (step > 0) & (step < last_step))
        def _(parity=parity):
            mixer, _, epilogue = stages(parity)
            _, ffn, _ = stages(1 - parity)
            ffn(epilogue + mixer)

    @pl.when(step == last_step)
    def _():
        _run_items(stages(last_step % 2)[2])


def _layer_a_kernel(x_ref, prev_ref, meta_ref, gains_ref, pool_w_ref, wgu_ref, wd_ref, wkv_ref, bkv_ref,
                    hs_out_ref, kv_out_ref,
                    hext, pbuf, ymix, hs1, hb, yffn, gbuf, ubuf, act, kvin,
                    *, tm, tiles_per_batch, n_tiles, is_meta):
    g_mix_pre = _gain(gains_ref, G_MIX_PRE)
    step = pl.program_id(0)
    tile_rows = pl.ds(0, tm)

    def stages(slot):
        def halo_rows(gate):
            if is_meta:
                hext[0:HALO, :] = jnp.zeros((HALO, D_MODEL), jnp.float32)
            else:
                first_of_batch = (jnp.minimum(step, n_tiles - 1) % tiles_per_batch) == 0
                prev = jnp.where(first_of_batch, meta_ref[...], prev_ref[...])
                hext[0:HALO, :] = _rms(prev, g_mix_pre)

        def norm_rows(rows, gate):
            hext[pl.ds(rows.start + HALO, rows.size), :] = _rms(_gated(x_ref[rows, :], gate), g_mix_pre)

        def pool_rows(rows, gate):
            for gi, w in enumerate(POOL_WINDOWS):
                cols = slice(gi * POOL_GROUP_DIM, (gi + 1) * POOL_GROUP_DIM)
                e0 = _gated(hext[pl.ds(rows.start, rows.size + HALO), cols], gate)
                e = e0
                shift = 1
                while shift < w:
                    e = e + pltpu.roll(e, shift, 0)
                    shift *= 2
                win = e[HALO:, :]
                if is_meta:
                    pos = lax.broadcasted_iota(jnp.int32, win.shape, 0) + 1
                    cnt = jnp.minimum(pos, w).astype(jnp.float32)
                    pooled = win / cnt - e0[HALO:, :]
                else:
                    pooled = win * (1.0 / w) - e0[HALO:, :]
                pbuf[rows, cols] = pooled.astype(jnp.bfloat16)

        def group_matmul(gi, gate):
            cols = slice(gi * POOL_GROUP_DIM, (gi + 1) * POOL_GROUP_DIM)
            ymix[:, cols] = jnp.dot(pbuf[:, cols], pool_w_ref[gi], preferred_element_type=jnp.float32)

        def post_mix_rows(rows, gate):
            y = _gated(ymix[rows, :], gate)
            h1 = x_ref[rows, :] + _rms(y * _gain(gains_ref, G_POOL_SCALE), _gain(gains_ref, G_MIX_POST))
            hs1[slot, rows, :] = h1
            hb[slot, rows, :] = _rms(h1, _gain(gains_ref, G_FFN_PRE)).astype(jnp.bfloat16)

        mixer = ([halo_rows] + _row_items(tile_rows, ROW_CHUNK, norm_rows)
                 + _row_items(tile_rows, POOL_ROW_CHUNK, pool_rows)
                 + [functools.partial(group_matmul, gi) for gi in range(len(POOL_WINDOWS))]
                 + _row_items(tile_rows, ROW_CHUNK, post_mix_rows))

        def ffn(side_items):
            _swiglu(hb.at[slot], wgu_ref, wd_ref, gbuf, ubuf, act, yffn.at[slot], tile_rows, side_items)

        def post_ffn_rows(rows, gate):
            h2 = hs1[slot, rows, :] + _rms(_gated(yffn[slot, rows, :], gate), _gain(gains_ref, G_FFN_POST))
            hs_out_ref[rows, :] = h2
            kvin[rows, :] = _rms(h2, _gain(gains_ref, G_KV)).astype(jnp.bfloat16)

        def kv_projection(gate):
            kv = jnp.dot(kvin[...], wkv_ref[...], preferred_element_type=jnp.float32) + bkv_ref[...]
            kv_out_ref[...] = kv.astype(jnp.bfloat16)

        epilogue = _row_items(tile_rows, ROW_CHUNK, post_ffn_rows) + [kv_projection]
        return mixer, ffn, epilogue

    _run_pipelined(step, n_tiles, (hs1, hb, yffn), stages)


def _const_spec(shape):
    nd = len(shape)
    return pl.BlockSpec(shape, lambda *_: (0,) * nd, pipeline_mode=pl.Buffered(1))


def _layer_spec(stacked_shape, layer):
    _, rows, cols = stacked_shape
    return pl.BlockSpec((None, rows, cols), lambda *_: (layer, 0, 0), pipeline_mode=pl.Buffered(1))


def _layer_a_call(x2d, meta, gains, pool_w, wgu, wd, wkv, bkv, *, layer, tm, tiles_per_batch, is_meta):
    n_rows = x2d.shape[0]
    n_tiles = n_rows // tm
    n_steps = n_tiles if n_tiles == 1 else n_tiles + PIPELINE_DEPTH - 1
    halo_blocks_per_tile = tm // HALO
    kern = functools.partial(_layer_a_kernel, tm=tm, tiles_per_batch=tiles_per_batch, n_tiles=n_tiles,
                             is_meta=is_meta)

    def mixer_tile(s):
        return jnp.minimum(s, n_tiles - 1)

    def epilogue_tile(s):
        return jnp.maximum(s - (n_steps - n_tiles), 0)

    return pl.pallas_call(
        kern,
        grid=(n_steps,),
        in_specs=[
            pl.BlockSpec((tm, D_MODEL), lambda s: (mixer_tile(s), 0)),
            pl.BlockSpec((HALO, D_MODEL), lambda s: (jnp.maximum(mixer_tile(s) * halo_blocks_per_tile - 1, 0), 0)),
            _const_spec((N_META, D_MODEL)),
            _const_spec(gains.shape),
            _const_spec(pool_w.shape),
            _layer_spec(wgu.shape, layer),
            _layer_spec(wd.shape, layer),
            _const_spec(wkv.shape),
            _const_spec(bkv.shape),
        ],
        out_specs=[
            pl.BlockSpec((tm, D_MODEL), lambda s: (epilogue_tile(s), 0)),
            pl.BlockSpec((tm, 2 * KV_DIM), lambda s: (epilogue_tile(s), 0)),
        ],
        out_shape=[
            jax.ShapeDtypeStruct((n_rows, D_MODEL), jnp.float32),
            jax.ShapeDtypeStruct((n_rows, 2 * KV_DIM), jnp.bfloat16),
        ],
        scratch_shapes=[
            pltpu.VMEM((tm + HALO, D_MODEL), jnp.float32),
            pltpu.VMEM((tm, D_MODEL), jnp.bfloat16),
            pltpu.VMEM((tm, D_MODEL), jnp.float32),
            pltpu.VMEM((2, tm, D_MODEL), jnp.float32),
            pltpu.VMEM((2, tm, D_MODEL), jnp.bfloat16),
            pltpu.VMEM((2, tm, D_MODEL), jnp.float32),
            pltpu.VMEM((tm, FFN_COLS), jnp.float32),
            pltpu.VMEM((tm, FFN_COLS), jnp.float32),
            pltpu.VMEM((tm, D_FF), jnp.bfloat16),
            pltpu.VMEM((tm, D_MODEL), jnp.bfloat16),
        ],
        compiler_params=pltpu.CompilerParams(
            dimension_semantics=("arbitrary",), vmem_limit_bytes=V7X_VMEM_LIMIT_BYTES),
        name="layer_a_meta" if is_meta else "layer_a",
    )(x2d, x2d, meta, gains, pool_w, wgu, wd, wkv, bkv)


def _bucket_distance_ranges():
    d = np.arange(WINDOW)
    max_exact = N_BUCKETS // 2
    df = np.maximum(d, 1).astype(np.float32)
    large = max_exact + (np.log(df / np.float32(max_exact)) / np.float32(math.log(MAX_DISTANCE / max_exact))
                         * np.float32(N_BUCKETS - max_exact)).astype(np.int32)
    bucket = np.where(d < max_exact, d, np.minimum(large, N_BUCKETS - 1))
    ranges = []
    for b in range(N_BUCKETS):
        members = d[bucket == b]
        if members.size:
            assert np.array_equal(members, np.arange(members[0], members[-1] + 1))
            ranges.append((int(members[0]), int(members[-1])))
        else:
            ranges.append(None)
    return ranges


def _bias_table_kernel(rel_bias_ref, out_ref):
    m = pl.program_id(0)
    q = lax.broadcasted_iota(jnp.int32, (BLOCK, 2 * BLOCK), 0)
    s = lax.broadcasted_iota(jnp.int32, (BLOCK, 2 * BLOCK), 1)
    d = q + BLOCK - s
    in_window = (d >= 0) & (d < WINDOW)
    valid = (in_window & (s >= PAD_FRONT), in_window)
    for half in range(2):
        h = 2 * m + half
        acc = jnp.zeros((BLOCK, 2 * BLOCK), jnp.float32)
        for b, distances in enumerate(_bucket_distance_ranges()):
            if distances is not None:
                lo, hi = distances
                acc = jnp.where((d >= lo) & (d <= hi), rel_bias_ref[b, h], acc)
        for jsel in range(2):
            out_ref[jsel, 0, :, half * 2 * BLOCK:(half + 1) * 2 * BLOCK] = jnp.where(valid[jsel], acc, -jnp.inf)


def _bias_table_call(rel_bias):
    return pl.pallas_call(
        _bias_table_kernel,
        grid=(N_HEAD_PAIRS,),
        in_specs=[
            pl.BlockSpec(memory_space=pltpu.SMEM),
        ],
        out_specs=pl.BlockSpec((2, 1, BLOCK, 4 * BLOCK), lambda m: (0, m, 0, 0)),
        out_shape=jax.ShapeDtypeStruct((2, N_HEAD_PAIRS, BLOCK, 4 * BLOCK), jnp.float32),
        compiler_params=pltpu.CompilerParams(dimension_semantics=("arbitrary",)),
        name="bias_table",
    )(rel_bias)


def _layer_b_kernel(hs_ref, kv_ref, metakv_ref, rowsum_cols_ref, bias_ref, sinks_ref, gains_ref,
                    wq_ref, bq_ref, wo_ref, bo_ref, wgu_ref, wd_ref,
                    out_ref,
                    hn, qbuf, obuf, yatt, hs3, hb, yffn, gbuf, ubuf, act, lbuf, mbuf, pbuf,
                    *, tm, tiles_per_batch, n_tiles):
    blocks_per_tile = tm // BLOCK
    step = pl.program_id(0)
    t = jnp.minimum(step, n_tiles - 1) % tiles_per_batch
    tile_rows = pl.ds(0, tm)

    lane = lax.broadcasted_iota(jnp.int32, (2 * BLOCK, 2 * HEAD_DIM), 1)
    low_half = lane < HEAD_DIM
    zero = jnp.zeros((2 * BLOCK, 2 * HEAD_DIM), jnp.bfloat16)
    low_half_q = lax.broadcasted_iota(jnp.int32, (BLOCK, 2 * HEAD_DIM), 1) < HEAD_DIM

    def per_kv_head_operands(pair_cols):
        swapped = jnp.concatenate([pair_cols[:, HEAD_DIM:], pair_cols[:, :HEAD_DIM]], axis=1)
        g0 = (jnp.where(low_half, pair_cols, zero), jnp.where(low_half, zero, swapped))
        g1 = (jnp.where(low_half, swapped, zero), jnp.where(low_half, zero, pair_cols))
        return g0, g1

    ones_top = rowsum_cols_ref[0:2 * BLOCK, :]
    ones_bottom = rowsum_cols_ref[2 * BLOCK:4 * BLOCK, :]

    def attention_block(jb, gate):
        j = t * blocks_per_tile + jb
        own = kv_ref[pl.ds(pl.multiple_of(j * BLOCK, BLOCK), BLOCK), :]
        prev_start = pl.multiple_of(jnp.maximum(j - 1, 0) * BLOCK, BLOCK)
        prev = jnp.where(j == 0, metakv_ref[...], kv_ref[pl.ds(prev_start, BLOCK), :])
        kvb = jnp.concatenate([prev, own], axis=0)
        k_ops = per_kv_head_operands(kvb[:, :KV_DIM])
        v_ops = per_kv_head_operands(kvb[:, KV_DIM:])
        jsel = jnp.minimum(j, 1)
        qrows = slice(jb * BLOCK, (jb + 1) * BLOCK)
        for g in range(N_KV_HEADS):
            u = jb * N_KV_HEADS + g
            pairs = range(g * PAIRS_PER_KV, (g + 1) * PAIRS_PER_KV)
            q4 = jnp.concatenate([qbuf[qrows, m * 2 * HEAD_DIM:(m + 1) * 2 * HEAD_DIM] for m in pairs], axis=0)
            k_op = jnp.concatenate(k_ops[g], axis=0)
            v_op = jnp.concatenate([jnp.concatenate([v_ops[g][0], ones_top], axis=1),
                                    jnp.concatenate([v_ops[g][1], ones_bottom], axis=1)], axis=0)
            lbuf[u] = _gated(lax.dot_general(q4, k_op, (((1,), (1,)), ((), ())),
                                             preferred_element_type=jnp.float32) + bias_ref[jsel, g], gate)
            for r0 in range(0, ATT_ROWS, SOFTMAX_ROW_CHUNK):
                rows = slice(r0, r0 + SOFTMAX_ROW_CHUNK)
                for half in range(2):
                    sink = sinks_ref[2 * (pairs[0] + r0 // BLOCK) + half]
                    lg = lbuf[u, rows, half * 2 * BLOCK:(half + 1) * 2 * BLOCK]
                    mx = jnp.maximum(jnp.max(lg, axis=-1, keepdims=True), sink)
                    mbuf[u, rows, half * BLOCK:(half + 1) * BLOCK] = jnp.broadcast_to(mx, (SOFTMAX_ROW_CHUNK, BLOCK))
            for r0 in range(0, ATT_ROWS, SOFTMAX_ROW_CHUNK):
                rows = slice(r0, r0 + SOFTMAX_ROW_CHUNK)
                for half in range(2):
                    mrep = mbuf[u, rows, half * BLOCK:(half + 1) * BLOCK]
                    lg = lbuf[u, rows, half * 2 * BLOCK:(half + 1) * 2 * BLOCK]
                    p = jnp.exp(lg - jnp.concatenate([mrep, mrep], axis=1))
                    pbuf[u, rows, half * 2 * BLOCK:(half + 1) * 2 * BLOCK] = p.astype(jnp.bfloat16)
            oa = jnp.dot(pbuf[u], v_op, preferred_element_type=jnp.float32)
            for i, m in enumerate(pairs):
                rows = slice(i * BLOCK, (i + 1) * BLOCK)
                sink_gap = jnp.where(low_half_q, sinks_ref[2 * m] - mbuf[u, rows, 0:BLOCK],
                                     sinks_ref[2 * m + 1] - mbuf[u, rows, BLOCK:2 * BLOCK])
                denom = oa[rows, 2 * HEAD_DIM:] + jnp.exp(sink_gap)
                o = oa[rows, :2 * HEAD_DIM] * (1.0 / denom)
                obuf[qrows, m * 2 * HEAD_DIM:(m + 1) * 2 * HEAD_DIM] = o.astype(jnp.bfloat16)

    def stages(slot):
        def norm_rows(rows, gate):
            h = _rms(_gated(hs_ref[rows, :], gate), _gain(gains_ref, G_MIX_PRE))
            hn[rows, :] = h.astype(jnp.bfloat16)

        def q_projection(gate):
            q = jnp.dot(hn[...], wq_ref[...], preferred_element_type=jnp.float32) + bq_ref[...]
            qbuf[...] = (q * (HEAD_DIM ** -0.5)).astype(jnp.bfloat16)

        def o_projection(gate):
            yatt[...] = jnp.dot(obuf[...], wo_ref[...], preferred_element_type=jnp.float32) + bo_ref[...]

        def post_mix_rows(rows, gate):
            h3 = hs_ref[rows, :] + _rms(_gated(yatt[rows, :], gate), _gain(gains_ref, G_MIX_POST))
            hs3[slot, rows, :] = h3
            hb[slot, rows, :] = _rms(h3, _gain(gains_ref, G_FFN_PRE)).astype(jnp.bfloat16)

        attention = (_row_items(tile_rows, ROW_CHUNK, norm_rows) + [q_projection]
                     + [functools.partial(attention_block, jb) for jb in range(blocks_per_tile)]
                     + [o_projection] + _row_items(tile_rows, ROW_CHUNK, post_mix_rows))

        def ffn(side_items):
            _swiglu(hb.at[slot], wgu_ref, wd_ref, gbuf, ubuf, act, yffn.at[slot], tile_rows, side_items)

        def post_ffn_rows(rows, gate):
            y = _gated(yffn[slot, rows, :], gate)
            out_ref[rows, :] = hs3[slot, rows, :] + _rms(y, _gain(gains_ref, G_FFN_POST))

        return attention, ffn, _row_items(tile_rows, ROW_CHUNK, post_ffn_rows)

    _run_pipelined(step, n_tiles, (hs3, hb, yffn), stages)


def _layer_b_call(hs, kv, metakv, bias, sinks, gains, wq, bq, wo, bo, wgu, wd, *, layer, tm):
    batch, seq, _ = hs.shape
    tiles_per_batch = seq // tm
    n_tiles = batch * tiles_per_batch
    n_att_units = (tm // BLOCK) * N_KV_HEADS
    lane_head = jnp.arange(2 * HEAD_DIM)[None, :] // HEAD_DIM
    row_head = jnp.arange(4 * BLOCK)[:, None] // (2 * BLOCK)
    rowsum_cols = (lane_head == row_head).astype(jnp.bfloat16)
    kern = functools.partial(_layer_b_kernel, tm=tm, tiles_per_batch=tiles_per_batch, n_tiles=n_tiles)

    def att_tile(s):
        tile = jnp.minimum(s, n_tiles - 1)
        return tile // tiles_per_batch, tile % tiles_per_batch

    def epilogue_tile(s):
        tile = jnp.maximum(s - (PIPELINE_DEPTH - 1), 0)
        return tile // tiles_per_batch, tile % tiles_per_batch

    return pl.pallas_call(
        kern,
        grid=(n_tiles + PIPELINE_DEPTH - 1,),
        in_specs=[
            pl.BlockSpec((None, tm, D_MODEL), lambda s: (*att_tile(s), 0)),
            pl.BlockSpec((None, seq, 2 * KV_DIM), lambda s: (att_tile(s)[0], 0, 0), pipeline_mode=pl.Buffered(1)),
            _const_spec(metakv.shape),
            _const_spec(rowsum_cols.shape),
            _const_spec(bias.shape),
            pl.BlockSpec(memory_space=pltpu.SMEM),
            _const_spec(gains.shape),
            _const_spec(wq.shape),
            _const_spec(bq.shape),
            _const_spec(wo.shape),
            _const_spec(bo.shape),
            _layer_spec(wgu.shape, layer),
            _layer_spec(wd.shape, layer),
        ],
        out_specs=pl.BlockSpec((None, tm, D_MODEL), lambda s: (*epilogue_tile(s), 0)),
        out_shape=jax.ShapeDtypeStruct((batch, seq, D_MODEL), jnp.float32),
        scratch_shapes=[
            pltpu.VMEM((tm, D_MODEL), jnp.bfloat16),
            pltpu.VMEM((tm, D_MODEL), jnp.bfloat16),
            pltpu.VMEM((tm, D_MODEL), jnp.bfloat16),
            pltpu.VMEM((tm, D_MODEL), jnp.float32),
            pltpu.VMEM((2, tm, D_MODEL), jnp.float32),
            pltpu.VMEM((2, tm, D_MODEL), jnp.bfloat16),
            pltpu.VMEM((2, tm, D_MODEL), jnp.float32),
            pltpu.VMEM((tm, FFN_COLS), jnp.float32),
            pltpu.VMEM((tm, FFN_COLS), jnp.float32),
            pltpu.VMEM((tm, D_FF), jnp.bfloat16),
            pltpu.VMEM((n_att_units, ATT_ROWS, 4 * BLOCK), jnp.float32),
            pltpu.VMEM((n_att_units, ATT_ROWS, 2 * BLOCK), jnp.float32),
            pltpu.VMEM((n_att_units, ATT_ROWS, 4 * BLOCK), jnp.bfloat16),
        ],
        compiler_params=pltpu.CompilerParams(
            dimension_semantics=("arbitrary",), vmem_limit_bytes=V7X_VMEM_LIMIT_BYTES),
        name="layer_b",
    )(hs, kv, metakv, rowsum_cols, bias, sinks, gains, wq, bq, wo, bo, wgu, wd)


def _cast_kernel(w_ref, out_ref):
    out_ref[...] = w_ref[...].astype(jnp.bfloat16)


def _cast_weights(w, body, name):
    layers, rows, cols = w.shape
    w2d = w.reshape(layers * rows, cols)
    target = max(16, CAST_BLOCK_BYTES // (4 * cols) // 16 * 16)
    block_rows = next(r for r in range(target, 0, -16) if (layers * rows) % r == 0)
    spec = pl.BlockSpec((block_rows, cols), lambda i: (i, 0))
    out = pl.pallas_call(
        body,
        grid=(layers * rows // block_rows,),
        in_specs=[spec],
        out_specs=spec,
        out_shape=jax.ShapeDtypeStruct(w2d.shape, jnp.bfloat16),
        compiler_params=pltpu.CompilerParams(dimension_semantics=("arbitrary",)),
        name=name,
    )(w2d)
    return out.reshape(layers, rows, cols)


def _gain_table(rows):
    zero = jnp.zeros((D_MODEL,), jnp.float32)
    return jnp.stack([rows[i].astype(jnp.float32) if i in rows else zero for i in range(8)])


def kernel(x, meta_tokens, norm_mix_pre, norm_mix_post, norm_ffn_pre, norm_ffn_post, pool_w, pool_scale, kv_norm, w_k, b_k, w_v, b_v, w_q, b_q, w_o, b_o, sinks, rel_bias, w_gate_up, w_down):
    batch, seq, _ = x.shape
    bf16 = jnp.bfloat16

    gains_a = _gain_table({G_MIX_PRE: norm_mix_pre[0], G_MIX_POST: norm_mix_post[0], G_POOL_SCALE: pool_scale[0],
                           G_FFN_PRE: norm_ffn_pre[0], G_FFN_POST: norm_ffn_post[0], G_KV: kv_norm})
    gains_b = _gain_table({G_MIX_PRE: norm_mix_pre[1], G_MIX_POST: norm_mix_post[1],
                           G_FFN_PRE: norm_ffn_pre[1], G_FFN_POST: norm_ffn_post[1]})
    wkv = jnp.concatenate([w_k, w_v], axis=1).astype(bf16)
    bkv = jnp.concatenate([b_k, b_v])[None, :]
    wgu = _cast_weights(w_gate_up, _cast_kernel, "cast_gate_up")
    wd = _cast_weights(w_down, _cast_kernel, "cast_down")
    layer_a_weights = (gains_a, pool_w[0].astype(bf16), wgu, wd, wkv, bkv)

    x2d = x.reshape(batch * seq, D_MODEL)
    hs2, kv_x = _layer_a_call(x2d, meta_tokens, *layer_a_weights,
                              layer=0, tm=TILE_A, tiles_per_batch=seq // TILE_A, is_meta=False)
    _, kv_meta = _layer_a_call(meta_tokens, meta_tokens, *layer_a_weights,
                               layer=0, tm=N_META, tiles_per_batch=1, is_meta=True)
    metakv = jnp.concatenate([jnp.zeros((PAD_FRONT, 2 * KV_DIM), bf16), kv_meta], axis=0)

    bias = _bias_table_call(rel_bias)
    bias = bias.reshape(2, N_KV_HEADS, ATT_ROWS, 4 * BLOCK)

    out = _layer_b_call(hs2.reshape(batch, seq, D_MODEL), kv_x.reshape(batch, seq, 2 * KV_DIM), metakv, bias,
                        sinks[0], gains_b, w_q[0].astype(bf16), b_q, w_o[0].astype(bf16), b_o,
                        wgu, wd, layer=1, tm=TILE_B)
    return out
```

```python
import functools
import math

import numpy as np

import jax
import jax.numpy as jnp
from jax import lax
from jax.experimental import pallas as pl
from jax.experimental.pallas import tpu as pltpu

D_MODEL = 1024
N_META = 16
POOL_WINDOWS = (2, 4, 8, 16)
POOL_GROUP_DIM = D_MODEL // len(POOL_WINDOWS)
HEAD_DIM = 64
N_Q_HEADS = D_MODEL // HEAD_DIM
N_KV_HEADS = 2
HEADS_PER_KV = N_Q_HEADS // N_KV_HEADS
PAIRS_PER_KV = HEADS_PER_KV // 2
N_HEAD_PAIRS = N_Q_HEADS // 2
KV_DIM = N_KV_HEADS * HEAD_DIM
WINDOW = 128
BLOCK = 128
N_BUCKETS = 32
MAX_DISTANCE = 128
D_FF = 2816
EPS = 1e-6
PAD_FRONT = (-N_META) % BLOCK

LANES = 128
GATE_ROWS = 8
FFN_COLS = 256
SILU_ROW_CHUNK = 64
UNRIDDEN_SEGMENTS = 2
CAST_BLOCK_BYTES = 6 * 1024 * 1024
HALO = 16
ROW_CHUNK = 64
POOL_ROW_CHUNK = 64
ATT_ROWS = PAIRS_PER_KV * BLOCK
SOFTMAX_ROW_CHUNK = 32
TILE_A = 512
PIPELINE_DEPTH = 3
TILE_B = 256
V7X_VMEM_LIMIT_BYTES = 56 * 1024 * 1024

G_MIX_PRE, G_MIX_POST, G_POOL_SCALE, G_FFN_PRE, G_FFN_POST, G_KV = range(6)


def _rms(x, g):
    ms = jnp.sum(x * x, axis=-1, keepdims=True) * (1.0 / D_MODEL)
    return x * lax.rsqrt(ms + EPS) * g


def _gain(gains_ref, i):
    return gains_ref[i:i + 1, :]


def _row_items(rows, chunk, body):
    chunk = min(chunk, rows.size)
    return [functools.partial(body, pl.ds(rows.start + i * chunk, chunk)) for i in range(rows.size // chunk)]


def _run_items(items, gate=None):
    for item in items:
        item(gate)


def _gate_of(value):
    bits = pltpu.bitcast(value[-GATE_ROWS:, -LANES:], jnp.uint32)
    return pltpu.bitcast((bits >> 16) >> 16, jnp.float32)


def _gated(x, gate):
    if gate is None:
        return x
    z = jnp.concatenate([gate] * (x.shape[0] // GATE_ROWS), axis=0)
    return jnp.concatenate([x[:, :LANES] + z, x[:, LANES:]], axis=1)


def _spread(items, n_bins):
    bins = [[] for _ in range(n_bins)]
    for k, item in enumerate(items):
        bins[k * n_bins // max(len(items), 1)].append(item)
    return bins


def _swiglu(h_ref, wgu_ref, wd_ref, gbuf, ubuf, act, out_ref, rows, side_items=()):
    segments = ([("gate_up", c) for c in range(0, D_FF, FFN_COLS)]
                + [("down", c) for c in range(0, D_MODEL, FFN_COLS)])
    side = _spread(list(side_items), len(segments) - UNRIDDEN_SEGMENTS) + [[]] * UNRIDDEN_SEGMENTS
    for (kind, c0), side_group in zip(segments, side):
        if kind == "gate_up":
            g = jnp.dot(h_ref[rows, :], wgu_ref[:, c0:c0 + FFN_COLS], preferred_element_type=jnp.float32)
            gbuf[rows, :] = g
            ubuf[rows, :] = jnp.dot(h_ref[rows, :], wgu_ref[:, D_FF + c0:D_FF + c0 + FFN_COLS],
                                    preferred_element_type=jnp.float32)
            gate_source = g

            def silu_rows(r, gate, c0=c0):
                half_g = 0.5 * gbuf[r, :]
                silu = half_g + half_g * jnp.tanh(half_g)
                act[r, c0:c0 + FFN_COLS] = (silu * ubuf[r, :]).astype(jnp.bfloat16)

            _run_items(_row_items(rows, SILU_ROW_CHUNK, silu_rows))
        else:
            y = jnp.dot(act[rows, :], wd_ref[:, c0:c0 + FFN_COLS], preferred_element_type=jnp.float32)
            out_ref[rows, c0:c0 + FFN_COLS] = y
            gate_source = y
        if side_group:
            _run_items(side_group, _gate_of(gate_source))


def _run_pipelined(step, n_tiles, carried, stages):
    if n_tiles == 1:
        mixer, ffn, epilogue = stages(0)
        _run_items(mixer)
        ffn(())
        _run_items(epilogue)
        return

    last_step = n_tiles + 1

    @pl.when(step == 0)
    def _():
        for ref in carried:
            ref[...] = jnp.zeros(ref.shape, ref.dtype)
        _run_items(stages(0)[0])

    for parity in range(2):
        @pl.when((step % 2 == parity) & (step > 0) & (step < last_step))
        def _(parity=parity):
            mixer, _, epilogue = stages(parity)
            _, ffn, _ = stages(1 - parity)
            ffn(epilogue + mixer)

    @pl.when(step == last_step)
    def _():
        _run_items(stages(last_step % 2)[2])


def _layer_a_kernel(x_ref, prev_ref, meta_ref, gains_ref, pool_w_ref, wgu_ref, wd_ref, wkv_ref, bkv_ref,
                    hs_out_ref, kv_out_ref,
                    hext, pbuf, ymix, hs1, hb, yffn, gbuf, ubuf, act, kvin,
                    *, tm, tiles_per_batch, n_tiles, is_meta):
    g_mix_pre = _gain(gains_ref, G_MIX_PRE)
    step = pl.program_id(0)
    tile_rows = pl.ds(0, tm)

    def stages(slot):
        def halo_rows(gate):
            if is_meta:
                hext[0:HALO, :] = jnp.zeros((HALO, D_MODEL), jnp.float32)
            else:
                first_of_batch = (jnp.minimum(step, n_tiles - 1) % tiles_per_batch) == 0
                prev = jnp.where(first_of_batch, meta_ref[...], prev_ref[...])
                hext[0:HALO, :] = _rms(prev, g_mix_pre)

        def norm_rows(rows, gate):
            hext[pl.ds(rows.start + HALO, rows.size), :] = _rms(_gated(x_ref[rows, :], gate), g_mix_pre)

        def pool_rows(rows, gate):
            for gi, w in enumerate(POOL_WINDOWS):
                cols = slice(gi * POOL_GROUP_DIM, (gi + 1) * POOL_GROUP_DIM)
                e0 = _gated(hext[pl.ds(rows.start, rows.size + HALO), cols], gate)
                e = e0
                shift = 1
                while shift < w:
                    e = e + pltpu.roll(e, shift, 0)
                    shift *= 2
                win = e[HALO:, :]
                if is_meta:
                    pos = lax.broadcasted_iota(jnp.int32, win.shape, 0) + 1
                    cnt = jnp.minimum(pos, w).astype(jnp.float32)
                    pooled = win / cnt - e0[HALO:, :]
                else:
                    pooled = win * (1.0 / w) - e0[HALO:, :]
                pbuf[rows, cols] = pooled.astype(jnp.bfloat16)

        def group_matmul(gi, gate):
            cols = slice(gi * POOL_GROUP_DIM, (gi + 1) * POOL_GROUP_DIM)
            ymix[:, cols] = jnp.dot(pbuf[:, cols], pool_w_ref[gi], preferred_element_type=jnp.float32)

        def post_mix_rows(rows, gate):
            y = _gated(ymix[rows, :], gate)
            h1 = x_ref[rows, :] + _rms(y * _gain(gains_ref, G_POOL_SCALE), _gain(gains_ref, G_MIX_POST))
            hs1[slot, rows, :] = h1
            hb[slot, rows, :] = _rms(h1, _gain(gains_ref, G_FFN_PRE)).astype(jnp.bfloat16)

        mixer = ([halo_rows] + _row_items(tile_rows, ROW_CHUNK, norm_rows)
                 + _row_items(tile_rows, POOL_ROW_CHUNK, pool_rows)
                 + [functools.partial(group_matmul, gi) for gi in range(len(POOL_WINDOWS))]
                 + _row_items(tile_rows, ROW_CHUNK, post_mix_rows))

        def ffn(side_items):
            _swiglu(hb.at[slot], wgu_ref, wd_ref, gbuf, ubuf, act, yffn.at[slot], tile_rows, side_items)

        def post_ffn_rows(rows, gate):
            h2 = hs1[slot, rows, :] + _rms(_gated(yffn[slot, rows, :], gate), _gain(gains_ref, G_FFN_POST))
            hs_out_ref[rows, :] = h2
            kvin[rows, :] = _rms(h2, _gain(gains_ref, G_KV)).astype(jnp.bfloat16)

        def kv_projection(gate):
            kv = jnp.dot(kvin[...], wkv_ref[...], preferred_element_type=jnp.float32) + bkv_ref[...]
            kv_out_ref[...] = kv.astype(jnp.bfloat16)

        epilogue = _row_items(tile_rows, ROW_CHUNK, post_ffn_rows) + [kv_projection]
        return mixer, ffn, epilogue

    _run_pipelined(step, n_tiles, (hs1, hb, yffn), stages)


def _const_spec(shape):
    nd = len(shape)
    return pl.BlockSpec(shape, lambda *_: (0,) * nd, pipeline_mode=pl.Buffered(1))


def _layer_spec(stacked_shape, layer):
    _, rows, cols = stacked_shape
    return pl.BlockSpec((None, rows, cols), lambda *_: (layer, 0, 0), pipeline_mode=pl.Buffered(1))


def _layer_a_call(x2d, meta, gains, pool_w, wgu, wd, wkv, bkv, *, layer, tm, tiles_per_batch, is_meta):
    n_rows = x2d.shape[0]
    n_tiles = n_rows // tm
    n_steps = n_tiles if n_tiles == 1 else n_tiles + PIPELINE_DEPTH - 1
    halo_blocks_per_tile = tm // HALO
    kern = functools.partial(_layer_a_kernel, tm=tm, tiles_per_batch=tiles_per_batch, n_tiles=n_tiles,
                             is_meta=is_meta)

    def mixer_tile(s):
        return jnp.minimum(s, n_tiles - 1)

    def epilogue_tile(s):
        return jnp.maximum(s - (n_steps - n_tiles), 0)

    return pl.pallas_call(
        kern,
        grid=(n_steps,),
        in_specs=[
            pl.BlockSpec((tm, D_MODEL), lambda s: (mixer_tile(s), 0)),
            pl.BlockSpec((HALO, D_MODEL), lambda s: (jnp.maximum(mixer_tile(s) * halo_blocks_per_tile - 1, 0), 0)),
            _const_spec((N_META, D_MODEL)),
            _const_spec(gains.shape),
            _const_spec(pool_w.shape),
            _layer_spec(wgu.shape, layer),
            _layer_spec(wd.shape, layer),
            _const_spec(wkv.shape),
            _const_spec(bkv.shape),
        ],
        out_specs=[
            pl.BlockSpec((tm, D_MODEL), lambda s: (epilogue_tile(s), 0)),
            pl.BlockSpec((tm, 2 * KV_DIM), lambda s: (epilogue_tile(s), 0)),
        ],
        out_shape=[
            jax.ShapeDtypeStruct((n_rows, D_MODEL), jnp.float32),
            jax.ShapeDtypeStruct((n_rows, 2 * KV_DIM), jnp.bfloat16),
        ],
        scratch_shapes=[
            pltpu.VMEM((tm + HALO, D_MODEL), jnp.float32),
            pltpu.VMEM((tm, D_MODEL), jnp.bfloat16),
            pltpu.VMEM((tm, D_MODEL), jnp.float32),
            pltpu.VMEM((2, tm, D_MODEL), jnp.float32),
            pltpu.VMEM((2, tm, D_MODEL), jnp.bfloat16),
            pltpu.VMEM((2, tm, D_MODEL), jnp.float32),
            pltpu.VMEM((tm, FFN_COLS), jnp.float32),
            pltpu.VMEM((tm, FFN_COLS), jnp.float32),
            pltpu.VMEM((tm, D_FF), jnp.bfloat16),
            pltpu.VMEM((tm, D_MODEL), jnp.bfloat16),
        ],
        compiler_params=pltpu.CompilerParams(
            dimension_semantics=("arbitrary",), vmem_limit_bytes=V7X_VMEM_LIMIT_BYTES),
        name="layer_a_meta" if is_meta else "layer_a",
    )(x2d, x2d, meta, gains, pool_w, wgu, wd, wkv, bkv)


def _bucket_distance_ranges():
    d = np.arange(WINDOW)
    max_exact = N_BUCKETS // 2
    df = np.maximum(d, 1).astype(np.float32)
    large = max_exact + (np.log(df / np.float32(max_exact)) / np.float32(math.log(MAX_DISTANCE / max_exact))
                         * np.float32(N_BUCKETS - max_exact)).astype(np.int32)
    bucket = np.where(d < max_exact, d, np.minimum(large, N_BUCKETS - 1))
    ranges = []
    for b in range(N_BUCKETS):
        members = d[bucket == b]
        if members.size:
            assert np.array_equal(members, np.arange(members[0], members[-1] + 1))
            ranges.append((int(members[0]), int(members[-1])))
        else:
            ranges.append(None)
    return ranges


def _bias_table_kernel(rel_bias_ref, out_ref):
    m = pl.program_id(0)
    q = lax.broadcasted_iota(jnp.int32, (BLOCK, 2 * BLOCK), 0)
    s = lax.broadcasted_iota(jnp.int32, (BLOCK, 2 * BLOCK), 1)
    d = q + BLOCK - s
    in_window = (d >= 0) & (d < WINDOW)
    valid = (in_window & (s >= PAD_FRONT), in_window)
    for half in range(2):
        h = 2 * m + half
        acc = jnp.zeros((BLOCK, 2 * BLOCK), jnp.float32)
        for b, distances in enumerate(_bucket_distance_ranges()):
            if distances is not None:
                lo, hi = distances
                acc = jnp.where((d >= lo) & (d <= hi), rel_bias_ref[b, h], acc)
        for jsel in range(2):
            out_ref[jsel, 0, :, half * 2 * BLOCK:(half + 1) * 2 * BLOCK] = jnp.where(valid[jsel], acc, -jnp.inf)


def _bias_table_call(rel_bias):
    return pl.pallas_call(
        _bias_table_kernel,
        grid=(N_HEAD_PAIRS,),
        in_specs=[
            pl.BlockSpec(memory_space=pltpu.SMEM),
        ],
        out_specs=pl.BlockSpec((2, 1, BLOCK, 4 * BLOCK), lambda m: (0, m, 0, 0)),
        out_shape=jax.ShapeDtypeStruct((2, N_HEAD_PAIRS, BLOCK, 4 * BLOCK), jnp.float32),
        compiler_params=pltpu.CompilerParams(dimension_semantics=("arbitrary",)),
        name="bias_table",
    )(rel_bias)


def _layer_b_kernel(hs_ref, kv_ref, metakv_ref, rowsum_cols_ref, bias_ref, sinks_ref, gains_ref,
                    wq_ref, bq_ref, wo_ref, bo_ref, wgu_ref, wd_ref,
                    out_ref,
                    hn, qbuf, obuf, yatt, hs3, hb, yffn, gbuf, ubuf, act, lbuf, mbuf, pbuf,
                    *, tm, tiles_per_batch, n_tiles):
    blocks_per_tile = tm // BLOCK
    step = pl.program_id(0)
    t = jnp.minimum(step, n_tiles - 1) % tiles_per_batch
    tile_rows = pl.ds(0, tm)

    lane = lax.broadcasted_iota(jnp.int32, (2 * BLOCK, 2 * HEAD_DIM), 1)
    low_half = lane < HEAD_DIM
    zero = jnp.zeros((2 * BLOCK, 2 * HEAD_DIM), jnp.bfloat16)
    low_half_q = lax.broadcasted_iota(jnp.int32, (BLOCK, 2 * HEAD_DIM), 1) < HEAD_DIM

    def per_kv_head_operands(pair_cols):
        swapped = jnp.concatenate([pair_cols[:, HEAD_DIM:], pair_cols[:, :HEAD_DIM]], axis=1)
        g0 = (jnp.where(low_half, pair_cols, zero), jnp.where(low_half, zero, swapped))
        g1 = (jnp.where(low_half, swapped, zero), jnp.where(low_half, zero, pair_cols))
        return g0, g1

    ones_top = rowsum_cols_ref[0:2 * BLOCK, :]
    ones_bottom = rowsum_cols_ref[2 * BLOCK:4 * BLOCK, :]

    def attention_block(jb, gate):
        j = t * blocks_per_tile + jb
        own = kv_ref[pl.ds(pl.multiple_of(j * BLOCK, BLOCK), BLOCK), :]
        prev_start = pl.multiple_of(jnp.maximum(j - 1, 0) * BLOCK, BLOCK)
        prev = jnp.where(j == 0, metakv_ref[...], kv_ref[pl.ds(prev_start, BLOCK), :])
        kvb = jnp.concatenate([prev, own], axis=0)
        k_ops = per_kv_head_operands(kvb[:, :KV_DIM])
        v_ops = per_kv_head_operands(kvb[:, KV_DIM:])
        jsel = jnp.minimum(j, 1)
        qrows = slice(jb * BLOCK, (jb + 1) * BLOCK)
        for g in range(N_KV_HEADS):
            u = jb * N_KV_HEADS + g
            pairs = range(g * PAIRS_PER_KV, (g + 1) * PAIRS_PER_KV)
            q4 = jnp.concatenate([qbuf[qrows, m * 2 * HEAD_DIM:(m + 1) * 2 * HEAD_DIM] for m in pairs], axis=0)
            k_op = jnp.concatenate(k_ops[g], axis=0)
            v_op = jnp.concatenate([jnp.concatenate([v_ops[g][0], ones_top], axis=1),
                                    jnp.concatenate([v_ops[g][1], ones_bottom], axis=1)], axis=0)
            logits = _gated(lax.dot_general(q4, k_op, (((1,), (1,)), ((), ())),
                                            preferred_element_type=jnp.float32) + bias_ref[jsel, g], gate)
            lbuf[u] = logits
            for r0 in range(0, ATT_ROWS, SOFTMAX_ROW_CHUNK):
                rows = slice(r0, r0 + SOFTMAX_ROW_CHUNK)
                for half in range(2):
                    sink = sinks_ref[2 * (pairs[0] + r0 // BLOCK) + half]
                    lg = logits[rows, half * 2 * BLOCK:(half + 1) * 2 * BLOCK]
                    mx = jnp.maximum(jnp.max(lg, axis=-1, keepdims=True), sink)
                    mbuf[u, rows, half * BLOCK:(half + 1) * BLOCK] = jnp.broadcast_to(mx, (SOFTMAX_ROW_CHUNK, BLOCK))
            for r0 in range(0, ATT_ROWS, SOFTMAX_ROW_CHUNK):
                rows = slice(r0, r0 + SOFTMAX_ROW_CHUNK)
                for half in range(2):
                    mrep = mbuf[u, rows, half * BLOCK:(half + 1) * BLOCK]
                    lg = lbuf[u, rows, half * 2 * BLOCK:(half + 1) * 2 * BLOCK]
                    p = jnp.exp(lg - jnp.concatenate([mrep, mrep], axis=1))
                    pbuf[u, rows, half * 2 * BLOCK:(half + 1) * 2 * BLOCK] = p.astype(jnp.bfloat16)
            oa = jnp.dot(pbuf[u], v_op, preferred_element_type=jnp.float32)
            for i, m in enumerate(pairs):
                rows = slice(i * BLOCK, (i + 1) * BLOCK)
                sink_gap = jnp.where(low_half_q, sinks_ref[2 * m] - mbuf[u, rows, 0:BLOCK],
                                     sinks_ref[2 * m + 1] - mbuf[u, rows, BLOCK:2 * BLOCK])
                denom = oa[rows, 2 * HEAD_DIM:] + jnp.exp(sink_gap)
                o = oa[rows, :2 * HEAD_DIM] * (1.0 / denom)
                obuf[qrows, m * 2 * HEAD_DIM:(m + 1) * 2 * HEAD_DIM] = o.astype(jnp.bfloat16)

    def stages(slot):
        def norm_rows(rows, gate):
            h = _rms(_gated(hs_ref[rows, :], gate), _gain(gains_ref, G_MIX_PRE))
            hn[rows, :] = h.astype(jnp.bfloat16)

        def q_projection(gate):
            q = jnp.dot(hn[...], wq_ref[...], preferred_element_type=jnp.float32) + bq_ref[...]
            qbuf[...] = (q * (HEAD_DIM ** -0.5)).astype(jnp.bfloat16)

        def o_projection(gate):
            yatt[...] = jnp.dot(obuf[...], wo_ref[...], preferred_element_type=jnp.float32) + bo_ref[...]

        def post_mix_rows(rows, gate):
            h3 = hs_ref[rows, :] + _rms(_gated(yatt[rows, :], gate), _gain(gains_ref, G_MIX_POST))
            hs3[slot, rows, :] = h3
            hb[slot, rows, :] = _rms(h3, _gain(gains_ref, G_FFN_PRE)).astype(jnp.bfloat16)

        attention = (_row_items(tile_rows, ROW_CHUNK, norm_rows) + [q_projection]
                     + [functools.partial(attention_block, jb) for jb in range(blocks_per_tile)]
                     + [o_projection] + _row_items(tile_rows, ROW_CHUNK, post_mix_rows))

        def ffn(side_items):
            _swiglu(hb.at[slot], wgu_ref, wd_ref, gbuf, ubuf, act, yffn.at[slot], tile_rows, side_items)

        def post_ffn_rows(rows, gate):
            y = _gated(yffn[slot, rows, :], gate)
            out_ref[rows, :] = hs3[slot, rows, :] + _rms(y, _gain(gains_ref, G_FFN_POST))

        return attention, ffn, _row_items(tile_rows, ROW_CHUNK, post_ffn_rows)

    _run_pipelined(step, n_tiles, (hs3, hb, yffn), stages)


def _layer_b_call(hs, kv, metakv, bias, sinks, gains, wq, bq, wo, bo, wgu, wd, *, layer, tm):
    batch, seq, _ = hs.shape
    tiles_per_batch = seq // tm
    n_tiles = batch * tiles_per_batch
    n_att_units = (tm // BLOCK) * N_KV_HEADS
    lane_head = jnp.arange(2 * HEAD_DIM)[None, :] // HEAD_DIM
    row_head = jnp.arange(4 * BLOCK)[:, None] // (2 * BLOCK)
    rowsum_cols = (lane_head == row_head).astype(jnp.bfloat16)
    kern = functools.partial(_layer_b_kernel, tm=tm, tiles_per_batch=tiles_per_batch, n_tiles=n_tiles)

    def att_tile(s):
        tile = jnp.minimum(s, n_tiles - 1)
        return tile // tiles_per_batch, tile % tiles_per_batch

    def epilogue_tile(s):
        tile = jnp.maximum(s - (PIPELINE_DEPTH - 1), 0)
        return tile // tiles_per_batch, tile % tiles_per_batch

    return pl.pallas_call(
        kern,
        grid=(n_tiles + PIPELINE_DEPTH - 1,),
        in_specs=[
            pl.BlockSpec((None, tm, D_MODEL), lambda s: (*att_tile(s), 0)),
            pl.BlockSpec((None, seq, 2 * KV_DIM), lambda s: (att_tile(s)[0], 0, 0), pipeline_mode=pl.Buffered(1)),
            _const_spec(metakv.shape),
            _const_spec(rowsum_cols.shape),
            _const_spec(bias.shape),
            pl.BlockSpec(memory_space=pltpu.SMEM),
            _const_spec(gains.shape),
            _const_spec(wq.shape),
            _const_spec(bq.shape),
            _const_spec(wo.shape),
            _const_spec(bo.shape),
            _layer_spec(wgu.shape, layer),
            _layer_spec(wd.shape, layer),
        ],
        out_specs=pl.BlockSpec((None, tm, D_MODEL), lambda s: (*epilogue_tile(s), 0)),
        out_shape=jax.ShapeDtypeStruct((batch, seq, D_MODEL), jnp.float32),
        scratch_shapes=[
            pltpu.VMEM((tm, D_MODEL), jnp.bfloat16),
            pltpu.VMEM((tm, D_MODEL), jnp.bfloat16),
            pltpu.VMEM((tm, D_MODEL), jnp.bfloat16),
            pltpu.VMEM((tm, D_MODEL), jnp.float32),
            pltpu.VMEM((2, tm, D_MODEL), jnp.float32),
            pltpu.VMEM((2, tm, D_MODEL), jnp.bfloat16),
            pltpu.VMEM((2, tm, D_MODEL), jnp.float32),
            pltpu.VMEM((tm, FFN_COLS), jnp.float32),
            pltpu.VMEM((tm, FFN_COLS), jnp.float32),
            pltpu.VMEM((tm, D_FF), jnp.bfloat16),
            pltpu.VMEM((n_att_units, ATT_ROWS, 4 * BLOCK), jnp.float32),
            pltpu.VMEM((n_att_units, ATT_ROWS, 2 * BLOCK), jnp.float32),
            pltpu.VMEM((n_att_units, ATT_ROWS, 4 * BLOCK), jnp.bfloat16),
        ],
        compiler_params=pltpu.CompilerParams(
            dimension_semantics=("arbitrary",), vmem_limit_bytes=V7X_VMEM_LIMIT_BYTES),
        name="layer_b",
    )(hs, kv, metakv, rowsum_cols, bias, sinks, gains, wq, bq, wo, bo, wgu, wd)


def _cast_kernel(w_ref, out_ref):
    out_ref[...] = w_ref[...].astype(jnp.bfloat16)


def _cast_weights(w, body, name):
    layers, rows, cols = w.shape
    w2d = w.reshape(layers * rows, cols)
    target = max(16, CAST_BLOCK_BYTES // (4 * cols) // 16 * 16)
    block_rows = next(r for r in range(target, 0, -16) if (layers * rows) % r == 0)
    spec = pl.BlockSpec((block_rows, cols), lambda i: (i, 0))
    out = pl.pallas_call(
        body,
        grid=(layers * rows // block_rows,),
        in_specs=[spec],
        out_specs=spec,
        out_shape=jax.ShapeDtypeStruct(w2d.shape, jnp.bfloat16),
        compiler_params=pltpu.CompilerParams(dimension_semantics=("arbitrary",)),
        name=name,
    )(w2d)
    return out.reshape(layers, rows, cols)


def _gain_table(rows):
    zero = jnp.zeros((D_MODEL,), jnp.float32)
    return jnp.stack([rows[i].astype(jnp.float32) if i in rows else zero for i in range(8)])


def kernel(x, meta_tokens, norm_mix_pre, norm_mix_post, norm_ffn_pre, norm_ffn_post, pool_w, pool_scale, kv_norm, w_k, b_k, w_v, b_v, w_q, b_q, w_o, b_o, sinks, rel_bias, w_gate_up, w_down):
    batch, seq, _ = x.shape
    bf16 = jnp.bfloat16

    gains_a = _gain_table({G_MIX_PRE: norm_mix_pre[0], G_MIX_POST: norm_mix_post[0], G_POOL_SCALE: pool_scale[0],
                           G_FFN_PRE: norm_ffn_pre[0], G_FFN_POST: norm_ffn_post[0], G_KV: kv_norm})
    gains_b = _gain_table({G_MIX_PRE: norm_mix_pre[1], G_MIX_POST: norm_mix_post[1],
                           G_FFN_PRE: norm_ffn_pre[1], G_FFN_POST: norm_ffn_post[1]})
    wkv = jnp.concatenate([w_k, w_v], axis=1).astype(bf16)
    bkv = jnp.concatenate([b_k, b_v])[None, :]
    wgu = _cast_weights(w_gate_up, _cast_kernel, "cast_gate_up")
    wd = _cast_weights(w_down, _cast_kernel, "cast_down")
    layer_a_weights = (gains_a, pool_w[0].astype(bf16), wgu, wd, wkv, bkv)

    x2d = x.reshape(batch * seq, D_MODEL)
    hs2, kv_x = _layer_a_call(x2d, meta_tokens, *layer_a_weights,
                              layer=0, tm=TILE_A, tiles_per_batch=seq // TILE_A, is_meta=False)
    _, kv_meta = _layer_a_call(meta_tokens, meta_tokens, *layer_a_weights,
                               layer=0, tm=N_META, tiles_per_batch=1, is_meta=True)
    metakv = jnp.concatenate([jnp.zeros((PAD_FRONT, 2 * KV_DIM), bf16), kv_meta], axis=0)

    bias = _bias_table_call(rel_bias)
    bias = bias.reshape(2, N_KV_HEADS, ATT_ROWS, 4 * BLOCK)

    out = _layer_b_call(hs2.reshape(batch, seq, D_MODEL), kv_x.reshape(batch, seq, 2 * KV_DIM), metakv, bias,
                        sinks[0], gains_b, w_q[0].astype(bf16), b_q, w_o[0].astype(bf16), b_o,
                        wgu, wd, layer=1, tm=TILE_B)
    return out
```

```python
import functools
import math

import numpy as np

import jax
import jax.numpy as jnp
from jax import lax
from jax.experimental import pallas as pl
from jax.experimental.pallas import tpu as pltpu

D_MODEL = 1024
N_META = 16
POOL_WINDOWS = (2, 4, 8, 16)
POOL_GROUP_DIM = D_MODEL // len(POOL_WINDOWS)
HEAD_DIM = 64
N_Q_HEADS = D_MODEL // HEAD_DIM
N_KV_HEADS = 2
HEADS_PER_KV = N_Q_HEADS // N_KV_HEADS
PAIRS_PER_KV = HEADS_PER_KV // 2
N_HEAD_PAIRS = N_Q_HEADS // 2
KV_DIM = N_KV_HEADS * HEAD_DIM
WINDOW = 128
BLOCK = 128
N_BUCKETS = 32
MAX_DISTANCE = 128
D_FF = 2816
EPS = 1e-6
PAD_FRONT = (-N_META) % BLOCK

LANES = 128
GATE_ROWS = 8
FFN_COLS = 256
SILU_ROW_CHUNK = 64
UNRIDDEN_SEGMENTS = 2
CAST_BLOCK_BYTES = 6 * 1024 * 1024
HALO = 16
ROW_CHUNK = 16
POOL_ROW_CHUNK = 64
ATT_ROWS = PAIRS_PER_KV * BLOCK
SOFTMAX_ROW_CHUNK = 16
TILE_A = 512
PIPELINE_DEPTH = 3
TILE_B = 256
V7X_VMEM_LIMIT_BYTES = 56 * 1024 * 1024

G_MIX_PRE, G_MIX_POST, G_POOL_SCALE, G_FFN_PRE, G_FFN_POST, G_KV = range(6)


def _rms(x, g):
    ms = jnp.sum(x * x, axis=-1, keepdims=True) * (1.0 / D_MODEL)
    return x * lax.rsqrt(ms + EPS) * g


def _gain(gains_ref, i):
    return gains_ref[i:i + 1, :]


def _row_items(rows, chunk, body):
    chunk = min(chunk, rows.size)
    return [functools.partial(body, pl.ds(rows.start + i * chunk, chunk)) for i in range(rows.size // chunk)]


def _run_items(items, gate=None):
    for item in items:
        item(gate)


def _gate_of(value):
    bits = pltpu.bitcast(value[-GATE_ROWS:, -LANES:], jnp.uint32)
    return pltpu.bitcast((bits >> 16) >> 16, jnp.float32)


def _gated(x, gate):
    if gate is None:
        return x
    z = jnp.concatenate([gate] * (x.shape[0] // GATE_ROWS), axis=0)
    return jnp.concatenate([x[:, :LANES] + z, x[:, LANES:]], axis=1)


def _spread(items, n_bins):
    bins = [[] for _ in range(n_bins)]
    for k, item in enumerate(items):
        bins[k * n_bins // max(len(items), 1)].append(item)
    return bins


def _swiglu(h_ref, wgu_ref, wd_ref, gbuf, ubuf, act, out_ref, rows, side_items=()):
    segments = ([("gate_up", c) for c in range(0, D_FF, FFN_COLS)]
                + [("down", c) for c in range(0, D_MODEL, FFN_COLS)])
    side = _spread(list(side_items), len(segments) - UNRIDDEN_SEGMENTS) + [[]] * UNRIDDEN_SEGMENTS
    for (kind, c0), side_group in zip(segments, side):
        if kind == "gate_up":
            g = jnp.dot(h_ref[rows, :], wgu_ref[:, c0:c0 + FFN_COLS], preferred_element_type=jnp.float32)
            gbuf[rows, :] = g
            ubuf[rows, :] = jnp.dot(h_ref[rows, :], wgu_ref[:, D_FF + c0:D_FF + c0 + FFN_COLS],
                                    preferred_element_type=jnp.float32)
            gate_source = g

            def silu_rows(r, gate, c0=c0):
                half_g = 0.5 * gbuf[r, :]
                silu = half_g + half_g * jnp.tanh(half_g)
                act[r, c0:c0 + FFN_COLS] = (silu * ubuf[r, :]).astype(jnp.bfloat16)

            _run_items(_row_items(rows, SILU_ROW_CHUNK, silu_rows))
        else:
            y = jnp.dot(act[rows, :], wd_ref[:, c0:c0 + FFN_COLS], preferred_element_type=jnp.float32)
            out_ref[rows, c0:c0 + FFN_COLS] = y
            gate_source = y
        if side_group:
            _run_items(side_group, _gate_of(gate_source))


def _run_pipelined(step, n_tiles, carried, stages):
    if n_tiles == 1:
        mixer, ffn, epilogue = stages(0)
        _run_items(mixer)
        ffn(())
        _run_items(epilogue)
        return

    last_step = n_tiles + 1

    @pl.when(step == 0)
    def _():
        for ref in carried:
            ref[...] = jnp.zeros(ref.shape, ref.dtype)
        _run_items(stages(0)[0])

    for parity in range(2):
        @pl.when((step % 2 == parity) & (step > 0) & (step < last_step))
        def _(parity=parity):
            mixer, _, epilogue = stages(parity)
            _, ffn, _ = stages(1 - parity)
            ffn(epilogue + mixer)

    @pl.when(step == last_step)
    def _():
        _run_items(stages(last_step % 2)[2])


def _layer_a_kernel(x_ref, prev_ref, meta_ref, gains_ref, pool_w_ref, wgu_ref, wd_ref, wkv_ref, bkv_ref,
                    hs_out_ref, kv_out_ref,
                    hext, pbuf, ymix, hs1, hb, yffn, gbuf, ubuf, act, kvin,
                    *, tm, tiles_per_batch, n_tiles, is_meta):
    g_mix_pre = _gain(gains_ref, G_MIX_PRE)
    step = pl.program_id(0)
    tile_rows = pl.ds(0, tm)

    def stages(slot):
        def halo_rows(gate):
            if is_meta:
                hext[0:HALO, :] = jnp.zeros((HALO, D_MODEL), jnp.float32)
            else:
                first_of_batch = (jnp.minimum(step, n_tiles - 1) % tiles_per_batch) == 0
                prev = jnp.where(first_of_batch, meta_ref[...], prev_ref[...])
                hext[0:HALO, :] = _rms(prev, g_mix_pre)

        def norm_rows(rows, gate):
            hext[pl.ds(rows.start + HALO, rows.size), :] = _rms(_gated(x_ref[rows, :], gate), g_mix_pre)

        def pool_rows(rows, gate):
            for gi, w in enumerate(POOL_WINDOWS):
                cols = slice(gi * POOL_GROUP_DIM, (gi + 1) * POOL_GROUP_DIM)
                e0 = _gated(hext[pl.ds(rows.start, rows.size + HALO), cols], gate)
                e = e0
                shift = 1
                while shift < w:
                    e = e + pltpu.roll(e, shift, 0)
                    shift *= 2
                win = e[HALO:, :]
                if is_meta:
                    pos = lax.broadcasted_iota(jnp.int32, win.shape, 0) + 1
                    cnt = jnp.minimum(pos, w).astype(jnp.float32)
                    pooled = win / cnt - e0[HALO:, :]
                else:
                    pooled = win * (1.0 / w) - e0[HALO:, :]
                pbuf[rows, cols] = pooled.astype(jnp.bfloat16)

        def group_matmul(gi, gate):
            cols = slice(gi * POOL_GROUP_DIM, (gi + 1) * POOL_GROUP_DIM)
            ymix[:, cols] = jnp.dot(pbuf[:, cols], pool_w_ref[gi], preferred_element_type=jnp.float32)

        def post_mix_rows(rows, gate):
            y = _gated(ymix[rows, :], gate)
            h1 = x_ref[rows, :] + _rms(y * _gain(gains_ref, G_POOL_SCALE), _gain(gains_ref, G_MIX_POST))
            hs1[slot, rows, :] = h1
            hb[slot, rows, :] = _rms(h1, _gain(gains_ref, G_FFN_PRE)).astype(jnp.bfloat16)

        mixer = ([halo_rows] + _row_items(tile_rows, ROW_CHUNK, norm_rows)
                 + _row_items(tile_rows, POOL_ROW_CHUNK, pool_rows)
                 + [functools.partial(group_matmul, gi) for gi in range(len(POOL_WINDOWS))]
                 + _row_items(tile_rows, ROW_CHUNK, post_mix_rows))

        def ffn(side_items):
            _swiglu(hb.at[slot], wgu_ref, wd_ref, gbuf, ubuf, act, yffn.at[slot], tile_rows, side_items)

        def post_ffn_rows(rows, gate):
            h2 = hs1[slot, rows, :] + _rms(_gated(yffn[slot, rows, :], gate), _gain(gains_ref, G_FFN_POST))
            hs_out_ref[rows, :] = h2
            kvin[rows, :] = _rms(h2, _gain(gains_ref, G_KV)).astype(jnp.bfloat16)

        def kv_projection(gate):
            kv = jnp.dot(kvin[...], wkv_ref[...], preferred_element_type=jnp.float32) + bkv_ref[...]
            kv_out_ref[...] = kv.astype(jnp.bfloat16)

        epilogue = _row_items(tile_rows, ROW_CHUNK, post_ffn_rows) + [kv_projection]
        return mixer, ffn, epilogue

    _run_pipelined(step, n_tiles, (hs1, hb, yffn), stages)


def _const_spec(shape):
    nd = len(shape)
    return pl.BlockSpec(shape, lambda *_: (0,) * nd, pipeline_mode=pl.Buffered(1))


def _layer_spec(stacked_shape, layer):
    _, rows, cols = stacked_shape
    return pl.BlockSpec((None, rows, cols), lambda *_: (layer, 0, 0), pipeline_mode=pl.Buffered(1))


def _layer_a_call(x2d, meta, gains, pool_w, wgu, wd, wkv, bkv, *, layer, tm, tiles_per_batch, is_meta):
    n_rows = x2d.shape[0]
    n_tiles = n_rows // tm
    n_steps = n_tiles if n_tiles == 1 else n_tiles + PIPELINE_DEPTH - 1
    halo_blocks_per_tile = tm // HALO
    kern = functools.partial(_layer_a_kernel, tm=tm, tiles_per_batch=tiles_per_batch, n_tiles=n_tiles,
                             is_meta=is_meta)

    def mixer_tile(s):
        return jnp.minimum(s, n_tiles - 1)

    def epilogue_tile(s):
        return jnp.maximum(s - (n_steps - n_tiles), 0)

    return pl.pallas_call(
        kern,
        grid=(n_steps,),
        in_specs=[
            pl.BlockSpec((tm, D_MODEL), lambda s: (mixer_tile(s), 0)),
            pl.BlockSpec((HALO, D_MODEL), lambda s: (jnp.maximum(mixer_tile(s) * halo_blocks_per_tile - 1, 0), 0)),
            _const_spec((N_META, D_MODEL)),
            _const_spec(gains.shape),
            _const_spec(pool_w.shape),
            _layer_spec(wgu.shape, layer),
            _layer_spec(wd.shape, layer),
            _const_spec(wkv.shape),
            _const_spec(bkv.shape),
        ],
        out_specs=[
            pl.BlockSpec((tm, D_MODEL), lambda s: (epilogue_tile(s), 0)),
            pl.BlockSpec((tm, 2 * KV_DIM), lambda s: (epilogue_tile(s), 0)),
        ],
        out_shape=[
            jax.ShapeDtypeStruct((n_rows, D_MODEL), jnp.float32),
            jax.ShapeDtypeStruct((n_rows, 2 * KV_DIM), jnp.bfloat16),
        ],
        scratch_shapes=[
            pltpu.VMEM((tm + HALO, D_MODEL), jnp.float32),
            pltpu.VMEM((tm, D_MODEL), jnp.bfloat16),
            pltpu.VMEM((tm, D_MODEL), jnp.float32),
            pltpu.VMEM((2, tm, D_MODEL), jnp.float32),
            pltpu.VMEM((2, tm, D_MODEL), jnp.bfloat16),
            pltpu.VMEM((2, tm, D_MODEL), jnp.float32),
            pltpu.VMEM((tm, FFN_COLS), jnp.float32),
            pltpu.VMEM((tm, FFN_COLS), jnp.float32),
            pltpu.VMEM((tm, D_FF), jnp.bfloat16),
            pltpu.VMEM((tm, D_MODEL), jnp.bfloat16),
        ],
        compiler_params=pltpu.CompilerParams(
            dimension_semantics=("arbitrary",), vmem_limit_bytes=V7X_VMEM_LIMIT_BYTES),
        name="layer_a_meta" if is_meta else "layer_a",
    )(x2d, x2d, meta, gains, pool_w, wgu, wd, wkv, bkv)


def _bucket_distance_ranges():
    d = np.arange(WINDOW)
    max_exact = N_BUCKETS // 2
    df = np.maximum(d, 1).astype(np.float32)
    large = max_exact + (np.log(df / np.float32(max_exact)) / np.float32(math.log(MAX_DISTANCE / max_exact))
                         * np.float32(N_BUCKETS - max_exact)).astype(np.int32)
    bucket = np.where(d < max_exact, d, np.minimum(large, N_BUCKETS - 1))
    ranges = []
    for b in range(N_BUCKETS):
        members = d[bucket == b]
        if members.size:
            assert np.array_equal(members, np.arange(members[0], members[-1] + 1))
            ranges.append((int(members[0]), int(members[-1])))
        else:
            ranges.append(None)
    return ranges


def _bias_table_kernel(rel_bias_ref, out_ref):
    m = pl.program_id(0)
    q = lax.broadcasted_iota(jnp.int32, (BLOCK, 2 * BLOCK), 0)
    s = lax.broadcasted_iota(jnp.int32, (BLOCK, 2 * BLOCK), 1)
    d = q + BLOCK - s
    in_window = (d >= 0) & (d < WINDOW)
    valid = (in_window & (s >= PAD_FRONT), in_window)
    for half in range(2):
        h = 2 * m + half
        acc = jnp.zeros((BLOCK, 2 * BLOCK), jnp.float32)
        for b, distances in enumerate(_bucket_distance_ranges()):
            if distances is not None:
                lo, hi = distances
                acc = jnp.where((d >= lo) & (d <= hi), rel_bias_ref[b, h], acc)
        for jsel in range(2):
            out_ref[jsel, 0, :, half * 2 * BLOCK:(half + 1) * 2 * BLOCK] = jnp.where(valid[jsel], acc, -jnp.inf)


def _bias_table_call(rel_bias):
    return pl.pallas_call(
        _bias_table_kernel,
        grid=(N_HEAD_PAIRS,),
        in_specs=[
            pl.BlockSpec(memory_space=pltpu.SMEM),
        ],
        out_specs=pl.BlockSpec((2, 1, BLOCK, 4 * BLOCK), lambda m: (0, m, 0, 0)),
        out_shape=jax.ShapeDtypeStruct((2, N_HEAD_PAIRS, BLOCK, 4 * BLOCK), jnp.float32),
        compiler_params=pltpu.CompilerParams(dimension_semantics=("arbitrary",)),
        name="bias_table",
    )(rel_bias)


def _layer_b_kernel(hs_ref, kv_ref, metakv_ref, rowsum_cols_ref, bias_ref, sinks_ref, gains_ref,
                    wq_ref, bq_ref, wo_ref, bo_ref, wgu_ref, wd_ref,
                    out_ref,
                    hn, qbuf, obuf, yatt, hs3, hb, yffn, gbuf, ubuf, act, lbuf, mbuf, pbuf,
                    *, tm, tiles_per_batch, n_tiles):
    blocks_per_tile = tm // BLOCK
    step = pl.program_id(0)
    t = jnp.minimum(step, n_tiles - 1) % tiles_per_batch
    tile_rows = pl.ds(0, tm)

    lane = lax.broadcasted_iota(jnp.int32, (2 * BLOCK, 2 * HEAD_DIM), 1)
    low_half = lane < HEAD_DIM
    zero = jnp.zeros((2 * BLOCK, 2 * HEAD_DIM), jnp.bfloat16)
    low_half_q = lax.broadcasted_iota(jnp.int32, (BLOCK, 2 * HEAD_DIM), 1) < HEAD_DIM

    def per_kv_head_operands(pair_cols):
        swapped = jnp.concatenate([pair_cols[:, HEAD_DIM:], pair_cols[:, :HEAD_DIM]], axis=1)
        g0 = (jnp.where(low_half, pair_cols, zero), jnp.where(low_half, zero, swapped))
        g1 = (jnp.where(low_half, swapped, zero), jnp.where(low_half, zero, pair_cols))
        return g0, g1

    ones_top = rowsum_cols_ref[0:2 * BLOCK, :]
    ones_bottom = rowsum_cols_ref[2 * BLOCK:4 * BLOCK, :]

    def attention_block(jb, gate):
        j = t * blocks_per_tile + jb
        own = kv_ref[pl.ds(pl.multiple_of(j * BLOCK, BLOCK), BLOCK), :]
        prev_start = pl.multiple_of(jnp.maximum(j - 1, 0) * BLOCK, BLOCK)
        prev = jnp.where(j == 0, metakv_ref[...], kv_ref[pl.ds(prev_start, BLOCK), :])
        kvb = jnp.concatenate([prev, own], axis=0)
        k_ops = per_kv_head_operands(kvb[:, :KV_DIM])
        v_ops = per_kv_head_operands(kvb[:, KV_DIM:])
        jsel = jnp.minimum(j, 1)
        qrows = slice(jb * BLOCK, (jb + 1) * BLOCK)
        for g in range(N_KV_HEADS):
            u = jb * N_KV_HEADS + g
            pairs = range(g * PAIRS_PER_KV, (g + 1) * PAIRS_PER_KV)
            q4 = jnp.concatenate([qbuf[qrows, m * 2 * HEAD_DIM:(m + 1) * 2 * HEAD_DIM] for m in pairs], axis=0)
            k_op = jnp.concatenate(k_ops[g], axis=0)
            v_op = jnp.concatenate([jnp.concatenate([v_ops[g][0], ones_top], axis=1),
                                    jnp.concatenate([v_ops[g][1], ones_bottom], axis=1)], axis=0)
            logits = _gated(lax.dot_general(q4, k_op, (((1,), (1,)), ((), ())),
                                            preferred_element_type=jnp.float32) + bias_ref[jsel, g], gate)
            lbuf[u] = logits
            for r0 in range(0, ATT_ROWS, SOFTMAX_ROW_CHUNK):
                rows = slice(r0, r0 + SOFTMAX_ROW_CHUNK)
                for half in range(2):
                    sink = sinks_ref[2 * (pairs[0] + r0 // BLOCK) + half]
                    lg = logits[rows, half * 2 * BLOCK:(half + 1) * 2 * BLOCK]
                    mx = jnp.maximum(jnp.max(lg, axis=-1, keepdims=True), sink)
                    mbuf[u, rows, half * BLOCK:(half + 1) * BLOCK] = jnp.broadcast_to(mx, (SOFTMAX_ROW_CHUNK, BLOCK))
            for r0 in range(0, ATT_ROWS, SOFTMAX_ROW_CHUNK):
                rows = slice(r0, r0 + SOFTMAX_ROW_CHUNK)
                for half in range(2):
                    mrep = mbuf[u, rows, half * BLOCK:(half + 1) * BLOCK]
                    lg = lbuf[u, rows, half * 2 * BLOCK:(half + 1) * 2 * BLOCK]
                    p = jnp.exp(lg - jnp.concatenate([mrep, mrep], axis=1))
                    pbuf[u, rows, half * 2 * BLOCK:(half + 1) * 2 * BLOCK] = p.astype(jnp.bfloat16)
            oa = jnp.dot(pbuf[u], v_op, preferred_element_type=jnp.float32)
            for i, m in enumerate(pairs):
                rows = slice(i * BLOCK, (i + 1) * BLOCK)
                sink_gap = jnp.where(low_half_q, sinks_ref[2 * m] - mbuf[u, rows, 0:BLOCK],
                                     sinks_ref[2 * m + 1] - mbuf[u, rows, BLOCK:2 * BLOCK])
                denom = oa[rows, 2 * HEAD_DIM:] + jnp.exp(sink_gap)
                o = oa[rows, :2 * HEAD_DIM] * (1.0 / denom)
                obuf[qrows, m * 2 * HEAD_DIM:(m + 1) * 2 * HEAD_DIM] = o.astype(jnp.bfloat16)

    def stages(slot):
        def norm_rows(rows, gate):
            h = _rms(_gated(hs_ref[rows, :], gate), _gain(gains_ref, G_MIX_PRE))
            hn[rows, :] = h.astype(jnp.bfloat16)

        def q_projection(gate):
            q = jnp.dot(hn[...], wq_ref[...], preferred_element_type=jnp.float32) + bq_ref[...]
            qbuf[...] = (q * (HEAD_DIM ** -0.5)).astype(jnp.bfloat16)

        def o_projection(gate):
            yatt[...] = jnp.dot(obuf[...], wo_ref[...], preferred_element_type=jnp.float32) + bo_ref[...]

        def post_mix_rows(rows, gate):
            h3 = hs_ref[rows, :] + _rms(_gated(yatt[rows, :], gate), _gain(gains_ref, G_MIX_POST))
            hs3[slot, rows, :] = h3
            hb[slot, rows, :] = _rms(h3, _gain(gains_ref, G_FFN_PRE)).astype(jnp.bfloat16)

        attention = (_row_items(tile_rows, ROW_CHUNK, norm_rows) + [q_projection]
                     + [functools.partial(attention_block, jb) for jb in range(blocks_per_tile)]
                     + [o_projection] + _row_items(tile_rows, ROW_CHUNK, post_mix_rows))

        def ffn(side_items):
            _swiglu(hb.at[slot], wgu_ref, wd_ref, gbuf, ubuf, act, yffn.at[slot], tile_rows, side_items)

        def post_ffn_rows(rows, gate):
            y = _gated(yffn[slot, rows, :], gate)
            out_ref[rows, :] = hs3[slot, rows, :] + _rms(y, _gain(gains_ref, G_FFN_POST))

        return attention, ffn, _row_items(tile_rows, ROW_CHUNK, post_ffn_rows)

    _run_pipelined(step, n_tiles, (hs3, hb, yffn), stages)


def _layer_b_call(hs, kv, metakv, bias, sinks, gains, wq, bq, wo, bo, wgu, wd, *, layer, tm):
    batch, seq, _ = hs.shape
    tiles_per_batch = seq // tm
    n_tiles = batch * tiles_per_batch
    n_att_units = (tm // BLOCK) * N_KV_HEADS
    lane_head = jnp.arange(2 * HEAD_DIM)[None, :] // HEAD_DIM
    row_head = jnp.arange(4 * BLOCK)[:, None] // (2 * BLOCK)
    rowsum_cols = (lane_head == row_head).astype(jnp.bfloat16)
    kern = functools.partial(_layer_b_kernel, tm=tm, tiles_per_batch=tiles_per_batch, n_tiles=n_tiles)

    def att_tile(s):
        tile = jnp.minimum(s, n_tiles - 1)
        return tile // tiles_per_batch, tile % tiles_per_batch

    def epilogue_tile(s):
        tile = jnp.maximum(s - (PIPELINE_DEPTH - 1), 0)
        return tile // tiles_per_batch, tile % tiles_per_batch

    return pl.pallas_call(
        kern,
        grid=(n_tiles + PIPELINE_DEPTH - 1,),
        in_specs=[
            pl.BlockSpec((None, tm, D_MODEL), lambda s: (*att_tile(s), 0)),
            pl.BlockSpec((None, seq, 2 * KV_DIM), lambda s: (att_tile(s)[0], 0, 0), pipeline_mode=pl.Buffered(1)),
            _const_spec(metakv.shape),
            _const_spec(rowsum_cols.shape),
            _const_spec(bias.shape),
            pl.BlockSpec(memory_space=pltpu.SMEM),
            _const_spec(gains.shape),
            _const_spec(wq.shape),
            _const_spec(bq.shape),
            _const_spec(wo.shape),
            _const_spec(bo.shape),
            _layer_spec(wgu.shape, layer),
            _layer_spec(wd.shape, layer),
        ],
        out_specs=pl.BlockSpec((None, tm, D_MODEL), lambda s: (*epilogue_tile(s), 0)),
        out_shape=jax.ShapeDtypeStruct((batch, seq, D_MODEL), jnp.float32),
        scratch_shapes=[
            pltpu.VMEM((tm, D_MODEL), jnp.bfloat16),
            pltpu.VMEM((tm, D_MODEL), jnp.bfloat16),
            pltpu.VMEM((tm, D_MODEL), jnp.bfloat16),
            pltpu.VMEM((tm, D_MODEL), jnp.float32),
            pltpu.VMEM((2, tm, D_MODEL), jnp.float32),
            pltpu.VMEM((2, tm, D_MODEL), jnp.bfloat16),
            pltpu.VMEM((2, tm, D_MODEL), jnp.float32),
            pltpu.VMEM((tm, FFN_COLS), jnp.float32),
            pltpu.VMEM((tm, FFN_COLS), jnp.float32),
            pltpu.VMEM((tm, D_FF), jnp.bfloat16),
            pltpu.VMEM((n_att_units, ATT_ROWS, 4 * BLOCK), jnp.float32),
            pltpu.VMEM((n_att_units, ATT_ROWS, 2 * BLOCK), jnp.float32),
            pltpu.VMEM((n_att_units, ATT_ROWS, 4 * BLOCK), jnp.bfloat16),
        ],
        compiler_params=pltpu.CompilerParams(
            dimension_semantics=("arbitrary",), vmem_limit_bytes=V7X_VMEM_LIMIT_BYTES),
        name="layer_b",
    )(hs, kv, metakv, rowsum_cols, bias, sinks, gains, wq, bq, wo, bo, wgu, wd)


def _cast_kernel(w_ref, out_ref):
    out_ref[...] = w_ref[...].astype(jnp.bfloat16)


def _cast_weights(w, body, name):
    layers, rows, cols = w.shape
    w2d = w.reshape(layers * rows, cols)
    target = max(16, CAST_BLOCK_BYTES // (4 * cols) // 16 * 16)
    block_rows = next(r for r in range(target, 0, -16) if (layers * rows) % r == 0)
    spec = pl.BlockSpec((block_rows, cols), lambda i: (i, 0))
    out = pl.pallas_call(
        body,
        grid=(layers * rows // block_rows,),
        in_specs=[spec],
        out_specs=spec,
        out_shape=jax.ShapeDtypeStruct(w2d.shape, jnp.bfloat16),
        compiler_params=pltpu.CompilerParams(dimension_semantics=("arbitrary",)),
        name=name,
    )(w2d)
    return out.reshape(layers, rows, cols)


def _gain_table(rows):
    zero = jnp.zeros((D_MODEL,), jnp.float32)
    return jnp.stack([rows[i].astype(jnp.float32) if i in rows else zero for i in range(8)])


def kernel(x, meta_tokens, norm_mix_pre, norm_mix_post, norm_ffn_pre, norm_ffn_post, pool_w, pool_scale, kv_norm, w_k, b_k, w_v, b_v, w_q, b_q, w_o, b_o, sinks, rel_bias, w_gate_up, w_down):
    batch, seq, _ = x.shape
    bf16 = jnp.bfloat16

    gains_a = _gain_table({G_MIX_PRE: norm_mix_pre[0], G_MIX_POST: norm_mix_post[0], G_POOL_SCALE: pool_scale[0],
                           G_FFN_PRE: norm_ffn_pre[0], G_FFN_POST: norm_ffn_post[0], G_KV: kv_norm})
    gains_b = _gain_table({G_MIX_PRE: norm_mix_pre[1], G_MIX_POST: norm_mix_post[1],
                           G_FFN_PRE: norm_ffn_pre[1], G_FFN_POST: norm_ffn_post[1]})
    wkv = jnp.concatenate([w_k, w_v], axis=1).astype(bf16)
    bkv = jnp.concatenate([b_k, b_v])[None, :]
    wgu = _cast_weights(w_gate_up, _cast_kernel, "cast_gate_up")
    wd = _cast_weights(w_down, _cast_kernel, "cast_down")
    layer_a_weights = (gains_a, pool_w[0].astype(bf16), wgu, wd, wkv, bkv)

    x2d = x.reshape(batch * seq, D_MODEL)
    hs2, kv_x = _layer_a_call(x2d, meta_tokens, *layer_a_weights,
                              layer=0, tm=TILE_A, tiles_per_batch=seq // TILE_A, is_meta=False)
    _, kv_meta = _layer_a_call(meta_tokens, meta_tokens, *layer_a_weights,
                               layer=0, tm=N_META, tiles_per_batch=1, is_meta=True)
    metakv = jnp.concatenate([jnp.zeros((PAD_FRONT, 2 * KV_DIM), bf16), kv_meta], axis=0)

    bias = _bias_table_call(rel_bias)
    bias = bias.reshape(2, N_KV_HEADS, ATT_ROWS, 4 * BLOCK)

    out = _layer_b_call(hs2.reshape(batch, seq, D_MODEL), kv_x.reshape(batch, seq, 2 * KV_DIM), metakv, bias,
                        sinks[0], gains_b, w_q[0].astype(bf16), b_q, w_o[0].astype(bf16), b_o,
                        wgu, wd, layer=1, tm=TILE_B)
    return out
```

```python
import functools
import math

import numpy as np

import jax
import jax.numpy as jnp
from jax import lax
from jax.experimental import pallas as pl
from jax.experimental.pallas import tpu as pltpu

D_MODEL = 1024
N_META = 16
POOL_WINDOWS = (2, 4, 8, 16)
POOL_GROUP_DIM = D_MODEL // len(POOL_WINDOWS)
HEAD_DIM = 64
N_Q_HEADS = D_MODEL // HEAD_DIM
N_KV_HEADS = 2
HEADS_PER_KV = N_Q_HEADS // N_KV_HEADS
PAIRS_PER_KV = HEADS_PER_KV // 2
N_HEAD_PAIRS = N_Q_HEADS // 2
KV_DIM = N_KV_HEADS * HEAD_DIM
WINDOW = 128
BLOCK = 128
N_BUCKETS = 32
MAX_DISTANCE = 128
D_FF = 2816
EPS = 1e-6
PAD_FRONT = (-N_META) % BLOCK

LANES = 128
GATE_ROWS = 8
FFN_COLS = 256
SILU_ROW_CHUNK = 32
UNRIDDEN_SEGMENTS = 2
CAST_BLOCK_BYTES = 6 * 1024 * 1024
HALO = 16
ROW_CHUNK = 32
POOL_ROW_CHUNK = 64
ATT_ROWS = PAIRS_PER_KV * BLOCK
SOFTMAX_ROW_CHUNK = 16
TILE_A = 512
PIPELINE_DEPTH = 3
TILE_B = 256
V7X_VMEM_LIMIT_BYTES = 60 * 1024 * 1024

G_MIX_PRE, G_MIX_POST, G_POOL_SCALE, G_FFN_PRE, G_FFN_POST, G_KV = range(6)


def _rms(x, g):
    ms = jnp.sum(x * x, axis=-1, keepdims=True) * (1.0 / D_MODEL)
    return x * lax.rsqrt(ms + EPS) * g


def _gain(gains_ref, i):
    return gains_ref[i:i + 1, :]


def _row_items(rows, chunk, body):
    chunk = min(chunk, rows.size)
    return [functools.partial(body, pl.ds(rows.start + i * chunk, chunk)) for i in range(rows.size // chunk)]


def _run_items(items, gate=None):
    for item in items:
        item(gate)


def _gate_of(value):
    bits = pltpu.bitcast(value[-GATE_ROWS:, -LANES:], jnp.uint32)
    return pltpu.bitcast((bits >> 16) >> 16, jnp.float32)


def _gated(x, gate):
    if gate is None:
        return x
    z = jnp.concatenate([gate] * (x.shape[0] // GATE_ROWS), axis=0)
    return jnp.concatenate([x[:, :LANES] + z, x[:, LANES:]], axis=1)


def _spread(items, n_bins):
    bins = [[] for _ in range(n_bins)]
    for k, item in enumerate(items):
        bins[k * n_bins // max(len(items), 1)].append(item)
    return bins


def _swiglu(h_ref, wgu_ref, wd_ref, gbuf, ubuf, act, out_ref, rows, side_items=()):
    segments = ([("gate_up", c) for c in range(0, D_FF, FFN_COLS)]
                + [("down", c) for c in range(0, D_MODEL, FFN_COLS)])
    side = _spread(list(side_items), len(segments) - UNRIDDEN_SEGMENTS) + [[]] * UNRIDDEN_SEGMENTS
    for (kind, c0), side_group in zip(segments, side):
        if kind == "gate_up":
            g = jnp.dot(h_ref[rows, :], wgu_ref[:, c0:c0 + FFN_COLS], preferred_element_type=jnp.float32)
            gbuf[rows, :] = g
            ubuf[rows, :] = jnp.dot(h_ref[rows, :], wgu_ref[:, D_FF + c0:D_FF + c0 + FFN_COLS],
                                    preferred_element_type=jnp.float32)
            gate_source = g

            def silu_rows(r, gate, c0=c0):
                half_g = 0.5 * gbuf[r, :]
                silu = half_g + half_g * jnp.tanh(half_g)
                act[r, c0:c0 + FFN_COLS] = (silu * ubuf[r, :]).astype(jnp.bfloat16)

            _run_items(_row_items(rows, SILU_ROW_CHUNK, silu_rows))
        else:
            y = jnp.dot(act[rows, :], wd_ref[:, c0:c0 + FFN_COLS], preferred_element_type=jnp.float32)
            out_ref[rows, c0:c0 + FFN_COLS] = y
            gate_source = y
        if side_group:
            _run_items(side_group, _gate_of(gate_source))


def _run_pipelined(step, n_tiles, carried, stages):
    if n_tiles == 1:
        mixer, ffn, epilogue = stages(0)
        _run_items(mixer)
        ffn(())
        _run_items(epilogue)
        return

    last_step = n_tiles + 1

    @pl.when(step == 0)
    def _():
        for ref in carried:
            ref[...] = jnp.zeros(ref.shape, ref.dtype)
        _run_items(stages(0)[0])

    for parity in range(2):
        @pl.when((step % 2 == parity) & (step > 0) & (step < last_step))
        def _(parity=parity):
            mixer, _, epilogue = stages(parity)
            _, ffn, _ = stages(1 - parity)
            ffn(epilogue + mixer)

    @pl.when(step == last_step)
    def _():
        _run_items(stages(last_step % 2)[2])


def _run_pipelined_pairs(step, n_pairs, stages):
    even_mixer, even_ffn, even_epilogue = stages(0)
    odd_mixer, odd_ffn, odd_epilogue = stages(1)

    @pl.when(step == 0)
    def _():
        _run_items(even_mixer)
        even_ffn(odd_mixer)

    @pl.when((step > 0) & (step < n_pairs))
    def _():
        odd_ffn(even_epilogue + even_mixer)
        even_ffn(odd_epilogue + odd_mixer)

    @pl.when(step == n_pairs)
    def _():
        odd_ffn(even_epilogue)
        _run_items(odd_epilogue)


def _layer_a_kernel(x_ref, prev_ref, meta_ref, gains_ref, pool_w_ref, wgu_ref, wd_ref, wkv_ref, bkv_ref,
                    hs_out_ref, kv_out_ref,
                    hext, pbuf, ymix, hs1, hb, yffn, gbuf, ubuf, act, kvin,
                    *, tm, tiles_per_batch, n_tiles, is_meta):
    g_mix_pre = _gain(gains_ref, G_MIX_PRE)
    step = pl.program_id(0)
    tile_rows = pl.ds(0, tm)

    def stages(slot):
        def halo_rows(gate):
            if is_meta:
                hext[0:HALO, :] = jnp.zeros((HALO, D_MODEL), jnp.float32)
            else:
                first_of_batch = (jnp.minimum(step, n_tiles - 1) % tiles_per_batch) == 0
                prev = jnp.where(first_of_batch, meta_ref[...], prev_ref[...])
                hext[0:HALO, :] = _rms(prev, g_mix_pre)

        def norm_rows(rows, gate):
            hext[pl.ds(rows.start + HALO, rows.size), :] = _rms(_gated(x_ref[rows, :], gate), g_mix_pre)

        def pool_rows(rows, gate):
            for gi, w in enumerate(POOL_WINDOWS):
                cols = slice(gi * POOL_GROUP_DIM, (gi + 1) * POOL_GROUP_DIM)
                e0 = _gated(hext[pl.ds(rows.start, rows.size + HALO), cols], gate)
                e = e0
                shift = 1
                while shift < w:
                    e = e + pltpu.roll(e, shift, 0)
                    shift *= 2
                win = e[HALO:, :]
                if is_meta:
                    pos = lax.broadcasted_iota(jnp.int32, win.shape, 0) + 1
                    cnt = jnp.minimum(pos, w).astype(jnp.float32)
                    pooled = win / cnt - e0[HALO:, :]
                else:
                    pooled = win * (1.0 / w) - e0[HALO:, :]
                pbuf[rows, cols] = pooled.astype(jnp.bfloat16)

        def group_matmul(gi, gate):
            cols = slice(gi * POOL_GROUP_DIM, (gi + 1) * POOL_GROUP_DIM)
            ymix[:, cols] = jnp.dot(pbuf[:, cols], pool_w_ref[gi], preferred_element_type=jnp.float32)

        def post_mix_rows(rows, gate):
            y = _gated(ymix[rows, :], gate)
            h1 = x_ref[rows, :] + _rms(y * _gain(gains_ref, G_POOL_SCALE), _gain(gains_ref, G_MIX_POST))
            hs1[slot, rows, :] = h1
            hb[slot, rows, :] = _rms(h1, _gain(gains_ref, G_FFN_PRE)).astype(jnp.bfloat16)

        mixer = ([halo_rows] + _row_items(tile_rows, ROW_CHUNK, norm_rows)
                 + _row_items(tile_rows, POOL_ROW_CHUNK, pool_rows)
                 + [functools.partial(group_matmul, gi) for gi in range(len(POOL_WINDOWS))]
                 + _row_items(tile_rows, ROW_CHUNK, post_mix_rows))

        def ffn(side_items):
            _swiglu(hb.at[slot], wgu_ref, wd_ref, gbuf, ubuf, act, yffn.at[slot], tile_rows, side_items)

        def post_ffn_rows(rows, gate):
            h2 = hs1[slot, rows, :] + _rms(_gated(yffn[slot, rows, :], gate), _gain(gains_ref, G_FFN_POST))
            hs_out_ref[rows, :] = h2
            kvin[rows, :] = _rms(h2, _gain(gains_ref, G_KV)).astype(jnp.bfloat16)

        def kv_projection(gate):
            kv = jnp.dot(kvin[...], wkv_ref[...], preferred_element_type=jnp.float32) + bkv_ref[...]
            kv_out_ref[...] = kv.astype(jnp.bfloat16)

        epilogue = _row_items(tile_rows, ROW_CHUNK, post_ffn_rows) + [kv_projection]
        return mixer, ffn, epilogue

    _run_pipelined(step, n_tiles, (hs1, hb, yffn), stages)


def _const_spec(shape):
    nd = len(shape)
    return pl.BlockSpec(shape, lambda *_: (0,) * nd, pipeline_mode=pl.Buffered(1))


def _layer_spec(stacked_shape, layer):
    _, rows, cols = stacked_shape
    return pl.BlockSpec((None, rows, cols), lambda *_: (layer, 0, 0), pipeline_mode=pl.Buffered(1))


def _layer_a_call(x2d, meta, gains, pool_w, wgu, wd, wkv, bkv, *, layer, tm, tiles_per_batch, is_meta):
    n_rows = x2d.shape[0]
    n_tiles = n_rows // tm
    n_steps = n_tiles if n_tiles == 1 else n_tiles + PIPELINE_DEPTH - 1
    halo_blocks_per_tile = tm // HALO
    kern = functools.partial(_layer_a_kernel, tm=tm, tiles_per_batch=tiles_per_batch, n_tiles=n_tiles,
                             is_meta=is_meta)

    def mixer_tile(s):
        return jnp.minimum(s, n_tiles - 1)

    def epilogue_tile(s):
        return jnp.maximum(s - (n_steps - n_tiles), 0)

    return pl.pallas_call(
        kern,
        grid=(n_steps,),
        in_specs=[
            pl.BlockSpec((tm, D_MODEL), lambda s: (mixer_tile(s), 0)),
            pl.BlockSpec((HALO, D_MODEL), lambda s: (jnp.maximum(mixer_tile(s) * halo_blocks_per_tile - 1, 0), 0)),
            _const_spec((N_META, D_MODEL)),
            _const_spec(gains.shape),
            _const_spec(pool_w.shape),
            _layer_spec(wgu.shape, layer),
            _layer_spec(wd.shape, layer),
            _const_spec(wkv.shape),
            _const_spec(bkv.shape),
        ],
        out_specs=[
            pl.BlockSpec((tm, D_MODEL), lambda s: (epilogue_tile(s), 0)),
            pl.BlockSpec((tm, 2 * KV_DIM), lambda s: (epilogue_tile(s), 0)),
        ],
        out_shape=[
            jax.ShapeDtypeStruct((n_rows, D_MODEL), jnp.float32),
            jax.ShapeDtypeStruct((n_rows, 2 * KV_DIM), jnp.bfloat16),
        ],
        scratch_shapes=[
            pltpu.VMEM((tm + HALO, D_MODEL), jnp.float32),
            pltpu.VMEM((tm, D_MODEL), jnp.bfloat16),
            pltpu.VMEM((tm, D_MODEL), jnp.float32),
            pltpu.VMEM((2, tm, D_MODEL), jnp.float32),
            pltpu.VMEM((2, tm, D_MODEL), jnp.bfloat16),
            pltpu.VMEM((2, tm, D_MODEL), jnp.float32),
            pltpu.VMEM((tm, FFN_COLS), jnp.float32),
            pltpu.VMEM((tm, FFN_COLS), jnp.float32),
            pltpu.VMEM((tm, D_FF), jnp.bfloat16),
            pltpu.VMEM((tm, D_MODEL), jnp.bfloat16),
        ],
        compiler_params=pltpu.CompilerParams(
            dimension_semantics=("arbitrary",), vmem_limit_bytes=V7X_VMEM_LIMIT_BYTES),
        name="layer_a_meta" if is_meta else "layer_a",
    )(x2d, x2d, meta, gains, pool_w, wgu, wd, wkv, bkv)


def _bucket_distance_ranges():
    d = np.arange(WINDOW)
    max_exact = N_BUCKETS // 2
    df = np.maximum(d, 1).astype(np.float32)
    large = max_exact + (np.log(df / np.float32(max_exact)) / np.float32(math.log(MAX_DISTANCE / max_exact))
                         * np.float32(N_BUCKETS - max_exact)).astype(np.int32)
    bucket = np.where(d < max_exact, d, np.minimum(large, N_BUCKETS - 1))
    ranges = []
    for b in range(N_BUCKETS):
        members = d[bucket == b]
        if members.size:
            assert np.array_equal(members, np.arange(members[0], members[-1] + 1))
            ranges.append((int(members[0]), int(members[-1])))
        else:
            ranges.append(None)
    return ranges


def _bias_table_kernel(rel_bias_ref, out_ref):
    m = pl.program_id(0)
    q = lax.broadcasted_iota(jnp.int32, (BLOCK, 2 * BLOCK), 0)
    s = lax.broadcasted_iota(jnp.int32, (BLOCK, 2 * BLOCK), 1)
    d = q + BLOCK - s
    in_window = (d >= 0) & (d < WINDOW)
    valid = (in_window & (s >= PAD_FRONT), in_window)
    for half in range(2):
        h = 2 * m + half
        acc = jnp.zeros((BLOCK, 2 * BLOCK), jnp.float32)
        for b, distances in enumerate(_bucket_distance_ranges()):
            if distances is not None:
                lo, hi = distances
                acc = jnp.where((d >= lo) & (d <= hi), rel_bias_ref[b, h], acc)
        for jsel in range(2):
            out_ref[jsel, 0, :, half * 2 * BLOCK:(half + 1) * 2 * BLOCK] = jnp.where(valid[jsel], acc, -jnp.inf)


def _bias_table_call(rel_bias):
    return pl.pallas_call(
        _bias_table_kernel,
        grid=(N_HEAD_PAIRS,),
        in_specs=[
            pl.BlockSpec(memory_space=pltpu.SMEM),
        ],
        out_specs=pl.BlockSpec((2, 1, BLOCK, 4 * BLOCK), lambda m: (0, m, 0, 0)),
        out_shape=jax.ShapeDtypeStruct((2, N_HEAD_PAIRS, BLOCK, 4 * BLOCK), jnp.float32),
        compiler_params=pltpu.CompilerParams(dimension_semantics=("arbitrary",)),
        name="bias_table",
    )(rel_bias)


def _layer_b_kernel(hs_ref, kv_ref, metakv_ref, rowsum_cols_ref, bias_ref, sinks_ref, gains_ref,
                    wq_ref, bq_ref, wo_ref, bo_ref, wgu_ref, wd_ref,
                    out_ref,
                    hn, qbuf, obuf, yatt, hs3, hb, yffn, gbuf, ubuf, act, lbuf, mbuf, pbuf,
                    *, tm, tiles_per_batch, n_tiles):
    blocks_per_tile = tm // BLOCK
    step = pl.program_id(0)
    pair = jnp.minimum(step, n_tiles // 2 - 1)
    tile_rows = pl.ds(0, tm)

    lane = lax.broadcasted_iota(jnp.int32, (2 * BLOCK, 2 * HEAD_DIM), 1)
    low_half = lane < HEAD_DIM
    zero = jnp.zeros((2 * BLOCK, 2 * HEAD_DIM), jnp.bfloat16)
    low_half_q = lax.broadcasted_iota(jnp.int32, (BLOCK, 2 * HEAD_DIM), 1) < HEAD_DIM

    def per_kv_head_operands(pair_cols):
        swapped = jnp.concatenate([pair_cols[:, HEAD_DIM:], pair_cols[:, :HEAD_DIM]], axis=1)
        g0 = (jnp.where(low_half, pair_cols, zero), jnp.where(low_half, zero, swapped))
        g1 = (jnp.where(low_half, swapped, zero), jnp.where(low_half, zero, pair_cols))
        return g0, g1

    ones_top = rowsum_cols_ref[0:2 * BLOCK, :]
    ones_bottom = rowsum_cols_ref[2 * BLOCK:4 * BLOCK, :]

    def attention_block(slot, jb, gate):
        t = (2 * pair + slot) % tiles_per_batch
        j = t * blocks_per_tile + jb
        own = kv_ref[pl.ds(pl.multiple_of(j * BLOCK, BLOCK), BLOCK), :]
        prev_start = pl.multiple_of(jnp.maximum(j - 1, 0) * BLOCK, BLOCK)
        prev = jnp.where(j == 0, metakv_ref[...], kv_ref[pl.ds(prev_start, BLOCK), :])
        kvb = jnp.concatenate([prev, own], axis=0)
        k_ops = per_kv_head_operands(kvb[:, :KV_DIM])
        v_ops = per_kv_head_operands(kvb[:, KV_DIM:])
        jsel = jnp.minimum(j, 1)
        qrows = slice(jb * BLOCK, (jb + 1) * BLOCK)
        for g in range(N_KV_HEADS):
            u = jb * N_KV_HEADS + g
            pairs = range(g * PAIRS_PER_KV, (g + 1) * PAIRS_PER_KV)
            q4 = jnp.concatenate([qbuf[qrows, m * 2 * HEAD_DIM:(m + 1) * 2 * HEAD_DIM] for m in pairs], axis=0)
            k_op = jnp.concatenate(k_ops[g], axis=0)
            v_op = jnp.concatenate([jnp.concatenate([v_ops[g][0], ones_top], axis=1),
                                    jnp.concatenate([v_ops[g][1], ones_bottom], axis=1)], axis=0)
            logits = _gated(lax.dot_general(q4, k_op, (((1,), (1,)), ((), ())),
                                            preferred_element_type=jnp.float32) + bias_ref[jsel, g], gate)
            lbuf[u] = logits
            for r0 in range(0, ATT_ROWS, SOFTMAX_ROW_CHUNK):
                rows = slice(r0, r0 + SOFTMAX_ROW_CHUNK)
                for half in range(2):
                    sink = sinks_ref[2 * (pairs[0] + r0 // BLOCK) + half]
                    lg = logits[rows, half * 2 * BLOCK:(half + 1) * 2 * BLOCK]
                    mx = jnp.maximum(jnp.max(lg, axis=-1, keepdims=True), sink)
                    mbuf[u, rows, half * BLOCK:(half + 1) * BLOCK] = jnp.broadcast_to(mx, (SOFTMAX_ROW_CHUNK, BLOCK))
            for r0 in range(0, ATT_ROWS, SOFTMAX_ROW_CHUNK):
                rows = slice(r0, r0 + SOFTMAX_ROW_CHUNK)
                for half in range(2):
                    mrep = mbuf[u, rows, half * BLOCK:(half + 1) * BLOCK]
                    lg = lbuf[u, rows, half * 2 * BLOCK:(half + 1) * 2 * BLOCK]
                    p = jnp.exp(lg - jnp.concatenate([mrep, mrep], axis=1))
                    pbuf[u, rows, half * 2 * BLOCK:(half + 1) * 2 * BLOCK] = p.astype(jnp.bfloat16)
            oa = jnp.dot(pbuf[u], v_op, preferred_element_type=jnp.float32)
            for i, m in enumerate(pairs):
                rows = slice(i * BLOCK, (i + 1) * BLOCK)
                sink_gap = jnp.where(low_half_q, sinks_ref[2 * m] - mbuf[u, rows, 0:BLOCK],
                                     sinks_ref[2 * m + 1] - mbuf[u, rows, BLOCK:2 * BLOCK])
                denom = oa[rows, 2 * HEAD_DIM:] + jnp.exp(sink_gap)
                o = oa[rows, :2 * HEAD_DIM] * (1.0 / denom)
                obuf[qrows, m * 2 * HEAD_DIM:(m + 1) * 2 * HEAD_DIM] = o.astype(jnp.bfloat16)

    def stages(slot):
        def io(rows):
            return pl.ds(rows.start + slot * tm, rows.size)

        def norm_rows(rows, gate):
            h = _rms(_gated(hs_ref[io(rows), :], gate), _gain(gains_ref, G_MIX_PRE))
            hn[rows, :] = h.astype(jnp.bfloat16)

        def q_projection(gate):
            q = jnp.dot(hn[...], wq_ref[...], preferred_element_type=jnp.float32) + bq_ref[...]
            qbuf[...] = (q * (HEAD_DIM ** -0.5)).astype(jnp.bfloat16)

        def o_projection(gate):
            yatt[...] = jnp.dot(obuf[...], wo_ref[...], preferred_element_type=jnp.float32) + bo_ref[...]

        def post_mix_rows(rows, gate):
            h3 = hs_ref[io(rows), :] + _rms(_gated(yatt[rows, :], gate), _gain(gains_ref, G_MIX_POST))
            hs3[slot, rows, :] = h3
            hb[slot, rows, :] = _rms(h3, _gain(gains_ref, G_FFN_PRE)).astype(jnp.bfloat16)

        attention = (_row_items(tile_rows, ROW_CHUNK, norm_rows) + [q_projection]
                     + [functools.partial(attention_block, slot, jb) for jb in range(blocks_per_tile)]
                     + [o_projection] + _row_items(tile_rows, ROW_CHUNK, post_mix_rows))

        def ffn(side_items):
            _swiglu(hb.at[slot], wgu_ref, wd_ref, gbuf, ubuf, act, yffn.at[slot], tile_rows, side_items)

        def post_ffn_rows(rows, gate):
            y = _gated(yffn[slot, rows, :], gate)
            out_ref[io(rows), :] = hs3[slot, rows, :] + _rms(y, _gain(gains_ref, G_FFN_POST))

        return attention, ffn, _row_items(tile_rows, ROW_CHUNK, post_ffn_rows)

    _run_pipelined_pairs(step, n_tiles // 2, stages)


def _layer_b_call(hs, kv, metakv, bias, sinks, gains, wq, bq, wo, bo, wgu, wd, *, layer, tm):
    batch, seq, _ = hs.shape
    tiles_per_batch = seq // tm
    n_tiles = batch * tiles_per_batch
    n_att_units = (tm // BLOCK) * N_KV_HEADS
    lane_head = jnp.arange(2 * HEAD_DIM)[None, :] // HEAD_DIM
    row_head = jnp.arange(4 * BLOCK)[:, None] // (2 * BLOCK)
    rowsum_cols = (lane_head == row_head).astype(jnp.bfloat16)
    kern = functools.partial(_layer_b_kernel, tm=tm, tiles_per_batch=tiles_per_batch, n_tiles=n_tiles)

    n_pairs = n_tiles // 2
    pairs_per_batch = tiles_per_batch // 2

    def att_pair(s):
        pair = jnp.minimum(s, n_pairs - 1)
        return pair // pairs_per_batch, pair % pairs_per_batch

    def epilogue_pair(s):
        pair = jnp.maximum(s - 1, 0)
        return pair // pairs_per_batch, pair % pairs_per_batch

    return pl.pallas_call(
        kern,
        grid=(n_pairs + 1,),
        in_specs=[
            pl.BlockSpec((None, 2 * tm, D_MODEL), lambda s: (*att_pair(s), 0)),
            pl.BlockSpec((None, seq, 2 * KV_DIM), lambda s: (att_pair(s)[0], 0, 0), pipeline_mode=pl.Buffered(1)),
            _const_spec(metakv.shape),
            _const_spec(rowsum_cols.shape),
            _const_spec(bias.shape),
            pl.BlockSpec(memory_space=pltpu.SMEM),
            _const_spec(gains.shape),
            _const_spec(wq.shape),
            _const_spec(bq.shape),
            _const_spec(wo.shape),
            _const_spec(bo.shape),
            _layer_spec(wgu.shape, layer),
            _layer_spec(wd.shape, layer),
        ],
        out_specs=pl.BlockSpec((None, 2 * tm, D_MODEL), lambda s: (*epilogue_pair(s), 0)),
        out_shape=jax.ShapeDtypeStruct((batch, seq, D_MODEL), jnp.float32),
        scratch_shapes=[
            pltpu.VMEM((tm, D_MODEL), jnp.bfloat16),
            pltpu.VMEM((tm, D_MODEL), jnp.bfloat16),
            pltpu.VMEM((tm, D_MODEL), jnp.bfloat16),
            pltpu.VMEM((tm, D_MODEL), jnp.float32),
            pltpu.VMEM((2, tm, D_MODEL), jnp.float32),
            pltpu.VMEM((2, tm, D_MODEL), jnp.bfloat16),
            pltpu.VMEM((2, tm, D_MODEL), jnp.float32),
            pltpu.VMEM((tm, FFN_COLS), jnp.float32),
            pltpu.VMEM((tm, FFN_COLS), jnp.float32),
            pltpu.VMEM((tm, D_FF), jnp.bfloat16),
            pltpu.VMEM((n_att_units, ATT_ROWS, 4 * BLOCK), jnp.float32),
            pltpu.VMEM((n_att_units, ATT_ROWS, 2 * BLOCK), jnp.float32),
            pltpu.VMEM((n_att_units, ATT_ROWS, 4 * BLOCK), jnp.bfloat16),
        ],
        compiler_params=pltpu.CompilerParams(
            dimension_semantics=("arbitrary",), vmem_limit_bytes=V7X_VMEM_LIMIT_BYTES),
        name="layer_b",
    )(hs, kv, metakv, rowsum_cols, bias, sinks, gains, wq, bq, wo, bo, wgu, wd)


def _cast_kernel(w_ref, out_ref):
    out_ref[...] = w_ref[...].astype(jnp.bfloat16)


def _cast_weights(w, body, name):
    layers, rows, cols = w.shape
    w2d = w.reshape(layers * rows, cols)
    target = max(16, CAST_BLOCK_BYTES // (4 * cols) // 16 * 16)
    block_rows = next(r for r in range(target, 0, -16) if (layers * rows) % r == 0)
    spec = pl.BlockSpec((block_rows, cols), lambda i: (i, 0))
    out = pl.pallas_call(
        body,
        grid=(layers * rows // block_rows,),
        in_specs=[spec],
        out_specs=spec,
        out_shape=jax.ShapeDtypeStruct(w2d.shape, jnp.bfloat16),
        compiler_params=pltpu.CompilerParams(dimension_semantics=("arbitrary",)),
        name=name,
    )(w2d)
    return out.reshape(layers, rows, cols)


def _gain_table(rows):
    zero = jnp.zeros((D_MODEL,), jnp.float32)
    return jnp.stack([rows[i].astype(jnp.float32) if i in rows else zero for i in range(8)])


def kernel(x, meta_tokens, norm_mix_pre, norm_mix_post, norm_ffn_pre, norm_ffn_post, pool_w, pool_scale, kv_norm, w_k, b_k, w_v, b_v, w_q, b_q, w_o, b_o, sinks, rel_bias, w_gate_up, w_down):
    batch, seq, _ = x.shape
    bf16 = jnp.bfloat16

    gains_a = _gain_table({G_MIX_PRE: norm_mix_pre[0], G_MIX_POST: norm_mix_post[0], G_POOL_SCALE: pool_scale[0],
                           G_FFN_PRE: norm_ffn_pre[0], G_FFN_POST: norm_ffn_post[0], G_KV: kv_norm})
    gains_b = _gain_table({G_MIX_PRE: norm_mix_pre[1], G_MIX_POST: norm_mix_post[1],
                           G_FFN_PRE: norm_ffn_pre[1], G_FFN_POST: norm_ffn_post[1]})
    wkv = jnp.concatenate([w_k, w_v], axis=1).astype(bf16)
    bkv = jnp.concatenate([b_k, b_v])[None, :]
    wgu = _cast_weights(w_gate_up, _cast_kernel, "cast_gate_up")
    wd = _cast_weights(w_down, _cast_kernel, "cast_down")
    layer_a_weights = (gains_a, pool_w[0].astype(bf16), wgu, wd, wkv, bkv)

    x2d = x.reshape(batch * seq, D_MODEL)
    hs2, kv_x = _layer_a_call(x2d, meta_tokens, *layer_a_weights,
                              layer=0, tm=TILE_A, tiles_per_batch=seq // TILE_A, is_meta=False)
    _, kv_meta = _layer_a_call(meta_tokens, meta_tokens, *layer_a_weights,
                               layer=0, tm=N_META, tiles_per_batch=1, is_meta=True)
    metakv = jnp.concatenate([jnp.zeros((PAD_FRONT, 2 * KV_DIM), bf16), kv_meta], axis=0)

    bias = _bias_table_call(rel_bias)
    bias = bias.reshape(2, N_KV_HEADS, ATT_ROWS, 4 * BLOCK)

    out = _layer_b_call(hs2.reshape(batch, seq, D_MODEL), kv_x.reshape(batch, seq, 2 * KV_DIM), metakv, bias,
                        sinks[0], gains_b, w_q[0].astype(bf16), b_q, w_o[0].astype(bf16), b_o,
                        wgu, wd, layer=1, tm=TILE_B)
    return out
```

```python
import functools
import math

import numpy as np

import jax
import jax.numpy as jnp
from jax import lax
from jax.experimental import pallas as pl
from jax.experimental.pallas import tpu as pltpu

D_MODEL = 1024
N_META = 16
POOL_WINDOWS = (2, 4, 8, 16)
POOL_GROUP_DIM = D_MODEL // len(POOL_WINDOWS)
HEAD_DIM = 64
N_Q_HEADS = D_MODEL // HEAD_DIM
N_KV_HEADS = 2
HEADS_PER_KV = N_Q_HEADS // N_KV_HEADS
PAIRS_PER_KV = HEADS_PER_KV // 2
N_HEAD_PAIRS = N_Q_HEADS // 2
KV_DIM = N_KV_HEADS * HEAD_DIM
WINDOW = 128
BLOCK = 128
N_BUCKETS = 32
MAX_DISTANCE = 128
D_FF = 2816
EPS = 1e-6
PAD_FRONT = (-N_META) % BLOCK

LANES = 128
GATE_ROWS = 8
FFN_COLS = 256
SILU_ROW_CHUNK = 32
UNRIDDEN_SEGMENTS = 2
CAST_BLOCK_BYTES = 6 * 1024 * 1024
HALO = 16
ROW_CHUNK = 32
POOL_ROW_CHUNK = 64
ATT_ROWS = PAIRS_PER_KV * BLOCK
SOFTMAX_ROW_CHUNK = 16
TILE_A = 512
PIPELINE_DEPTH = 3
TILE_B = 256
V7X_VMEM_LIMIT_BYTES = 63 * 1024 * 1024

G_MIX_PRE, G_MIX_POST, G_POOL_SCALE, G_FFN_PRE, G_FFN_POST, G_KV = range(6)


def _rms(x, g):
    ms = jnp.sum(x * x, axis=-1, keepdims=True) * (1.0 / D_MODEL)
    return x * lax.rsqrt(ms + EPS) * g


def _gain(gains_ref, i):
    return gains_ref[i:i + 1, :]


def _row_items(rows, chunk, body):
    chunk = min(chunk, rows.size)
    return [functools.partial(body, pl.ds(rows.start + i * chunk, chunk)) for i in range(rows.size // chunk)]


def _run_items(items, gate=None):
    for item in items:
        item(gate)


def _gate_of(value):
    bits = pltpu.bitcast(value[-GATE_ROWS:, -LANES:], jnp.uint32)
    return pltpu.bitcast((bits >> 16) >> 16, jnp.float32)


def _gated(x, gate):
    if gate is None:
        return x
    z = jnp.concatenate([gate] * (x.shape[0] // GATE_ROWS), axis=0)
    return jnp.concatenate([x[:, :LANES] + z, x[:, LANES:]], axis=1)


def _spread(items, n_bins):
    bins = [[] for _ in range(n_bins)]
    for k, item in enumerate(items):
        bins[k * n_bins // max(len(items), 1)].append(item)
    return bins


def _swiglu(h_ref, wgu_ref, wd_ref, gbuf, ubuf, act, out_ref, rows, side_items=()):
    segments = ([("gate_up", c) for c in range(0, D_FF, FFN_COLS)]
                + [("down", c) for c in range(0, D_MODEL, FFN_COLS)])
    side = _spread(list(side_items), len(segments) - UNRIDDEN_SEGMENTS) + [[]] * UNRIDDEN_SEGMENTS
    for (kind, c0), side_group in zip(segments, side):
        if kind == "gate_up":
            g = jnp.dot(h_ref[rows, :], wgu_ref[:, c0:c0 + FFN_COLS], preferred_element_type=jnp.float32)
            gbuf[rows, :] = g
            ubuf[rows, :] = jnp.dot(h_ref[rows, :], wgu_ref[:, D_FF + c0:D_FF + c0 + FFN_COLS],
                                    preferred_element_type=jnp.float32)
            gate_source = g

            def silu_rows(r, gate, c0=c0):
                half_g = 0.5 * gbuf[r, :]
                silu = half_g + half_g * jnp.tanh(half_g)
                act[r, c0:c0 + FFN_COLS] = (silu * ubuf[r, :]).astype(jnp.bfloat16)

            _run_items(_row_items(rows, SILU_ROW_CHUNK, silu_rows))
        else:
            y = jnp.dot(act[rows, :], wd_ref[:, c0:c0 + FFN_COLS], preferred_element_type=jnp.float32)
            out_ref[rows, c0:c0 + FFN_COLS] = y
            gate_source = y
        if side_group:
            _run_items(side_group, _gate_of(gate_source))


def _run_pipelined(step, n_tiles, carried, stages):
    if n_tiles == 1:
        mixer, ffn, epilogue = stages(0)
        _run_items(mixer)
        ffn(())
        _run_items(epilogue)
        return

    last_step = n_tiles + 1

    @pl.when(step == 0)
    def _():
        for ref in carried:
            ref[...] = jnp.zeros(ref.shape, ref.dtype)
        _run_items(stages(0)[0])

    for parity in range(2):
        @pl.when((step % 2 == parity) & (step > 0) & (step < last_step))
        def _(parity=parity):
            mixer, _, epilogue = stages(parity)
            _, ffn, _ = stages(1 - parity)
            ffn(epilogue + mixer)

    @pl.when(step == last_step)
    def _():
        _run_items(stages(last_step % 2)[2])


def _run_pipelined_pairs(step, n_pairs, stages):
    even_mixer, even_ffn, even_epilogue = stages(0)
    odd_mixer, odd_ffn, odd_epilogue = stages(1)

    @pl.when(step == 0)
    def _():
        _run_items(even_mixer)
        even_ffn(odd_mixer)

    @pl.when((step > 0) & (step < n_pairs))
    def _():
        odd_ffn(even_epilogue + even_mixer)
        even_ffn(odd_epilogue + odd_mixer)

    @pl.when(step == n_pairs)
    def _():
        odd_ffn(even_epilogue)
        _run_items(odd_epilogue)


def _layer_a_kernel(x_ref, prev_ref, meta_ref, gains_ref, pool_w_ref, wgu_ref, wd_ref, wkv_ref, bkv_ref,
                    hs_out_ref, kv_out_ref,
                    hext, pbuf, ymix, hs1, hb, yffn, gbuf, ubuf, act, kvin,
                    *, tm, tiles_per_batch, n_tiles, is_meta):
    g_mix_pre = _gain(gains_ref, G_MIX_PRE)
    step = pl.program_id(0)
    tile_rows = pl.ds(0, tm)

    def stages(slot):
        def halo_rows(gate):
            if is_meta:
                hext[0:HALO, :] = jnp.zeros((HALO, D_MODEL), jnp.float32)
            else:
                first_of_batch = (jnp.minimum(step, n_tiles - 1) % tiles_per_batch) == 0
                prev = jnp.where(first_of_batch, meta_ref[...], prev_ref[...])
                hext[0:HALO, :] = _rms(prev, g_mix_pre)

        def norm_rows(rows, gate):
            hext[pl.ds(rows.start + HALO, rows.size), :] = _rms(_gated(x_ref[rows, :], gate), g_mix_pre)

        def pool_rows(rows, gate):
            for gi, w in enumerate(POOL_WINDOWS):
                cols = slice(gi * POOL_GROUP_DIM, (gi + 1) * POOL_GROUP_DIM)
                e0 = _gated(hext[pl.ds(rows.start, rows.size + HALO), cols], gate)
                e = e0
                shift = 1
                while shift < w:
                    e = e + pltpu.roll(e, shift, 0)
                    shift *= 2
                win = e[HALO:, :]
                if is_meta:
                    pos = lax.broadcasted_iota(jnp.int32, win.shape, 0) + 1
                    cnt = jnp.minimum(pos, w).astype(jnp.float32)
                    pooled = win / cnt - e0[HALO:, :]
                else:
                    pooled = win * (1.0 / w) - e0[HALO:, :]
                pbuf[rows, cols] = pooled.astype(jnp.bfloat16)

        def group_matmul(gi, gate):
            cols = slice(gi * POOL_GROUP_DIM, (gi + 1) * POOL_GROUP_DIM)
            ymix[:, cols] = jnp.dot(pbuf[:, cols], pool_w_ref[gi], preferred_element_type=jnp.float32)

        def post_mix_rows(rows, gate):
            y = _gated(ymix[rows, :], gate)
            h1 = x_ref[rows, :] + _rms(y * _gain(gains_ref, G_POOL_SCALE), _gain(gains_ref, G_MIX_POST))
            hs1[slot, rows, :] = h1
            hb[slot, rows, :] = _rms(h1, _gain(gains_ref, G_FFN_PRE)).astype(jnp.bfloat16)

        mixer = ([halo_rows] + _row_items(tile_rows, ROW_CHUNK, norm_rows)
                 + _row_items(tile_rows, POOL_ROW_CHUNK, pool_rows)
                 + [functools.partial(group_matmul, gi) for gi in range(len(POOL_WINDOWS))]
                 + _row_items(tile_rows, ROW_CHUNK, post_mix_rows))

        def ffn(side_items):
            _swiglu(hb.at[slot], wgu_ref, wd_ref, gbuf, ubuf, act, yffn.at[slot], tile_rows, side_items)

        def post_ffn_rows(rows, gate):
            h2 = hs1[slot, rows, :] + _rms(_gated(yffn[slot, rows, :], gate), _gain(gains_ref, G_FFN_POST))
            hs_out_ref[rows, :] = h2
            kvin[rows, :] = _rms(h2, _gain(gains_ref, G_KV)).astype(jnp.bfloat16)

        def kv_projection(gate):
            kv = jnp.dot(kvin[...], wkv_ref[...], preferred_element_type=jnp.float32) + bkv_ref[...]
            kv_out_ref[...] = kv.astype(jnp.bfloat16)

        epilogue = _row_items(tile_rows, ROW_CHUNK, post_ffn_rows) + [kv_projection]
        return mixer, ffn, epilogue

    _run_pipelined(step, n_tiles, (hs1, hb, yffn), stages)


def _const_spec(shape):
    nd = len(shape)
    return pl.BlockSpec(shape, lambda *_: (0,) * nd, pipeline_mode=pl.Buffered(1))


def _layer_spec(stacked_shape, layer):
    _, rows, cols = stacked_shape
    return pl.BlockSpec((None, rows, cols), lambda *_: (layer, 0, 0), pipeline_mode=pl.Buffered(1))


def _layer_a_call(x2d, meta, gains, pool_w, wgu, wd, wkv, bkv, *, layer, tm, tiles_per_batch, is_meta):
    n_rows = x2d.shape[0]
    n_tiles = n_rows // tm
    n_steps = n_tiles if n_tiles == 1 else n_tiles + PIPELINE_DEPTH - 1
    halo_blocks_per_tile = tm // HALO
    kern = functools.partial(_layer_a_kernel, tm=tm, tiles_per_batch=tiles_per_batch, n_tiles=n_tiles,
                             is_meta=is_meta)

    def mixer_tile(s):
        return jnp.minimum(s, n_tiles - 1)

    def epilogue_tile(s):
        return jnp.maximum(s - (n_steps - n_tiles), 0)

    return pl.pallas_call(
        kern,
        grid=(n_steps,),
        in_specs=[
            pl.BlockSpec((tm, D_MODEL), lambda s: (mixer_tile(s), 0)),
            pl.BlockSpec((HALO, D_MODEL), lambda s: (jnp.maximum(mixer_tile(s) * halo_blocks_per_tile - 1, 0), 0)),
            _const_spec((N_META, D_MODEL)),
            _const_spec(gains.shape),
            _const_spec(pool_w.shape),
            _layer_spec(wgu.shape, layer),
            _layer_spec(wd.shape, layer),
            _const_spec(wkv.shape),
            _const_spec(bkv.shape),
        ],
        out_specs=[
            pl.BlockSpec((tm, D_MODEL), lambda s: (epilogue_tile(s), 0)),
            pl.BlockSpec((tm, 2 * KV_DIM), lambda s: (epilogue_tile(s), 0)),
        ],
        out_shape=[
            jax.ShapeDtypeStruct((n_rows, D_MODEL), jnp.float32),
            jax.ShapeDtypeStruct((n_rows, 2 * KV_DIM), jnp.bfloat16),
        ],
        scratch_shapes=[
            pltpu.VMEM((tm + HALO, D_MODEL), jnp.float32),
            pltpu.VMEM((tm, D_MODEL), jnp.bfloat16),
            pltpu.VMEM((tm, D_MODEL), jnp.float32),
            pltpu.VMEM((2, tm, D_MODEL), jnp.float32),
            pltpu.VMEM((2, tm, D_MODEL), jnp.bfloat16),
            pltpu.VMEM((2, tm, D_MODEL), jnp.float32),
            pltpu.VMEM((tm, FFN_COLS), jnp.float32),
            pltpu.VMEM((tm, FFN_COLS), jnp.float32),
            pltpu.VMEM((tm, D_FF), jnp.bfloat16),
            pltpu.VMEM((tm, D_MODEL), jnp.bfloat16),
        ],
        compiler_params=pltpu.CompilerParams(
            dimension_semantics=("arbitrary",), vmem_limit_bytes=V7X_VMEM_LIMIT_BYTES),
        name="layer_a_meta" if is_meta else "layer_a",
    )(x2d, x2d, meta, gains, pool_w, wgu, wd, wkv, bkv)


def _bucket_distance_ranges():
    d = np.arange(WINDOW)
    max_exact = N_BUCKETS // 2
    df = np.maximum(d, 1).astype(np.float32)
    large = max_exact + (np.log(df / np.float32(max_exact)) / np.float32(math.log(MAX_DISTANCE / max_exact))
                         * np.float32(N_BUCKETS - max_exact)).astype(np.int32)
    bucket = np.where(d < max_exact, d, np.minimum(large, N_BUCKETS - 1))
    ranges = []
    for b in range(N_BUCKETS):
        members = d[bucket == b]
        if members.size:
            assert np.array_equal(members, np.arange(members[0], members[-1] + 1))
            ranges.append((int(members[0]), int(members[-1])))
        else:
            ranges.append(None)
    return ranges


def _bias_table_kernel(rel_bias_ref, out_ref):
    m = pl.program_id(0)
    q = lax.broadcasted_iota(jnp.int32, (BLOCK, 2 * BLOCK), 0)
    s = lax.broadcasted_iota(jnp.int32, (BLOCK, 2 * BLOCK), 1)
    d = q + BLOCK - s
    in_window = (d >= 0) & (d < WINDOW)
    valid = (in_window & (s >= PAD_FRONT), in_window)
    for half in range(2):
        h = 2 * m + half
        acc = jnp.zeros((BLOCK, 2 * BLOCK), jnp.float32)
        for b, distances in enumerate(_bucket_distance_ranges()):
            if distances is not None:
                lo, hi = distances
                acc = jnp.where((d >= lo) & (d <= hi), rel_bias_ref[b, h], acc)
        for jsel in range(2):
            out_ref[jsel, 0, :, half * 2 * BLOCK:(half + 1) * 2 * BLOCK] = jnp.where(valid[jsel], acc, -jnp.inf)


def _bias_table_call(rel_bias):
    return pl.pallas_call(
        _bias_table_kernel,
        grid=(N_HEAD_PAIRS,),
        in_specs=[
            pl.BlockSpec(memory_space=pltpu.SMEM),
        ],
        out_specs=pl.BlockSpec((2, 1, BLOCK, 4 * BLOCK), lambda m: (0, m, 0, 0)),
        out_shape=jax.ShapeDtypeStruct((2, N_HEAD_PAIRS, BLOCK, 4 * BLOCK), jnp.float32),
        compiler_params=pltpu.CompilerParams(dimension_semantics=("arbitrary",)),
        name="bias_table",
    )(rel_bias)


def _layer_b_kernel(hs_ref, kv_ref, metakv_ref, rowsum_cols_ref, bias_ref, sinks_ref, gains_ref,
                    wq_ref, bq_ref, wo_ref, bo_ref, wgu_ref, wd_ref,
                    out_ref,
                    hn, qbuf, obuf, yatt, hs3, hb, yffn, gbuf, ubuf, act, lbuf, mbuf, pbuf,
                    *, tm, tiles_per_batch, n_tiles):
    blocks_per_tile = tm // BLOCK
    step = pl.program_id(0)
    pair = jnp.minimum(step, n_tiles // 2 - 1)
    tile_rows = pl.ds(0, tm)

    lane = lax.broadcasted_iota(jnp.int32, (2 * BLOCK, 2 * HEAD_DIM), 1)
    low_half = lane < HEAD_DIM
    zero = jnp.zeros((2 * BLOCK, 2 * HEAD_DIM), jnp.bfloat16)
    low_half_q = lax.broadcasted_iota(jnp.int32, (BLOCK, 2 * HEAD_DIM), 1) < HEAD_DIM

    def per_kv_head_operands(pair_cols):
        swapped = jnp.concatenate([pair_cols[:, HEAD_DIM:], pair_cols[:, :HEAD_DIM]], axis=1)
        g0 = (jnp.where(low_half, pair_cols, zero), jnp.where(low_half, zero, swapped))
        g1 = (jnp.where(low_half, swapped, zero), jnp.where(low_half, zero, pair_cols))
        return g0, g1

    ones_top = rowsum_cols_ref[0:2 * BLOCK, :]
    ones_bottom = rowsum_cols_ref[2 * BLOCK:4 * BLOCK, :]

    def attention_block(slot, jb, gate):
        t = (2 * pair + slot) % tiles_per_batch
        j = t * blocks_per_tile + jb
        own = kv_ref[pl.ds(pl.multiple_of(j * BLOCK, BLOCK), BLOCK), :]
        prev_start = pl.multiple_of(jnp.maximum(j - 1, 0) * BLOCK, BLOCK)
        prev = jnp.where(j == 0, metakv_ref[...], kv_ref[pl.ds(prev_start, BLOCK), :])
        kvb = jnp.concatenate([prev, own], axis=0)
        k_ops = per_kv_head_operands(kvb[:, :KV_DIM])
        v_ops = per_kv_head_operands(kvb[:, KV_DIM:])
        jsel = jnp.minimum(j, 1)
        qrows = slice(jb * BLOCK, (jb + 1) * BLOCK)
        for g in range(N_KV_HEADS):
            u = jb * N_KV_HEADS + g
            pairs = range(g * PAIRS_PER_KV, (g + 1) * PAIRS_PER_KV)
            q4 = jnp.concatenate([qbuf[qrows, m * 2 * HEAD_DIM:(m + 1) * 2 * HEAD_DIM] for m in pairs], axis=0)
            k_op = jnp.concatenate(k_ops[g], axis=0)
            v_op = jnp.concatenate([jnp.concatenate([v_ops[g][0], ones_top], axis=1),
                                    jnp.concatenate([v_ops[g][1], ones_bottom], axis=1)], axis=0)
            logits = _gated(lax.dot_general(q4, k_op, (((1,), (1,)), ((), ())),
                                            preferred_element_type=jnp.float32) + bias_ref[jsel, g], gate)
            lbuf[u] = logits
            for r0 in range(0, ATT_ROWS, SOFTMAX_ROW_CHUNK):
                rows = slice(r0, r0 + SOFTMAX_ROW_CHUNK)
                for half in range(2):
                    sink = sinks_ref[2 * (pairs[0] + r0 // BLOCK) + half]
                    lg = logits[rows, half * 2 * BLOCK:(half + 1) * 2 * BLOCK]
                    mx = jnp.maximum(jnp.max(lg, axis=-1, keepdims=True), sink)
                    mbuf[u, rows, half * BLOCK:(half + 1) * BLOCK] = jnp.broadcast_to(mx, (SOFTMAX_ROW_CHUNK, BLOCK))
            for r0 in range(0, ATT_ROWS, SOFTMAX_ROW_CHUNK):
                rows = slice(r0, r0 + SOFTMAX_ROW_CHUNK)
                for half in range(2):
                    mrep = mbuf[u, rows, half * BLOCK:(half + 1) * BLOCK]
                    lg = lbuf[u, rows, half * 2 * BLOCK:(half + 1) * 2 * BLOCK]
                    p = jnp.exp(lg - jnp.concatenate([mrep, mrep], axis=1))
                    pbuf[u, rows, half * 2 * BLOCK:(half + 1) * 2 * BLOCK] = p.astype(jnp.bfloat16)
            oa = jnp.dot(pbuf[u], v_op, preferred_element_type=jnp.float32)
            for i, m in enumerate(pairs):
                rows = slice(i * BLOCK, (i + 1) * BLOCK)
                sink_gap = jnp.where(low_half_q, sinks_ref[2 * m] - mbuf[u, rows, 0:BLOCK],
                                     sinks_ref[2 * m + 1] - mbuf[u, rows, BLOCK:2 * BLOCK])
                denom = oa[rows, 2 * HEAD_DIM:] + jnp.exp(sink_gap)
                o = oa[rows, :2 * HEAD_DIM] * (1.0 / denom)
                obuf[qrows, m * 2 * HEAD_DIM:(m + 1) * 2 * HEAD_DIM] = o.astype(jnp.bfloat16)

    def stages(slot):
        def io(rows):
            return pl.ds(rows.start + slot * tm, rows.size)

        def norm_rows(rows, gate):
            h = _rms(_gated(hs_ref[io(rows), :], gate), _gain(gains_ref, G_MIX_PRE))
            hn[rows, :] = h.astype(jnp.bfloat16)

        def q_projection(gate):
            q = jnp.dot(hn[...], wq_ref[...], preferred_element_type=jnp.float32) + bq_ref[...]
            qbuf[...] = (q * (HEAD_DIM ** -0.5)).astype(jnp.bfloat16)

        def o_projection(gate):
            yatt[...] = jnp.dot(obuf[...], wo_ref[...], preferred_element_type=jnp.float32) + bo_ref[...]

        def post_mix_rows(rows, gate):
            h3 = hs_ref[io(rows), :] + _rms(_gated(yatt[rows, :], gate), _gain(gains_ref, G_MIX_POST))
            hs3[slot, rows, :] = h3
            hb[slot, rows, :] = _rms(h3, _gain(gains_ref, G_FFN_PRE)).astype(jnp.bfloat16)

        attention = (_row_items(tile_rows, ROW_CHUNK, norm_rows) + [q_projection]
                     + [functools.partial(attention_block, slot, jb) for jb in range(blocks_per_tile)]
                     + [o_projection] + _row_items(tile_rows, ROW_CHUNK, post_mix_rows))

        def ffn(side_items):
            _swiglu(hb.at[slot], wgu_ref, wd_ref, gbuf, ubuf, act, yffn.at[slot], tile_rows, side_items)

        def post_ffn_rows(rows, gate):
            y = _gated(yffn[slot, rows, :], gate)
            out_ref[io(rows), :] = hs3[slot, rows, :] + _rms(y, _gain(gains_ref, G_FFN_POST))

        return attention, ffn, _row_items(tile_rows, ROW_CHUNK, post_ffn_rows)

    _run_pipelined_pairs(step, n_tiles // 2, stages)


def _layer_b_call(hs, kv, metakv, bias, sinks, gains, wq, bq, wo, bo, wgu, wd, *, layer, tm):
    batch, seq, _ = hs.shape
    tiles_per_batch = seq // tm
    n_tiles = batch * tiles_per_batch
    n_att_units = (tm // BLOCK) * N_KV_HEADS
    lane_head = jnp.arange(2 * HEAD_DIM)[None, :] // HEAD_DIM
    row_head = jnp.arange(4 * BLOCK)[:, None] // (2 * BLOCK)
    rowsum_cols = (lane_head == row_head).astype(jnp.bfloat16)
    kern = functools.partial(_layer_b_kernel, tm=tm, tiles_per_batch=tiles_per_batch, n_tiles=n_tiles)

    n_pairs = n_tiles // 2
    pairs_per_batch = tiles_per_batch // 2

    def att_pair(s):
        pair = jnp.minimum(s, n_pairs - 1)
        return pair // pairs_per_batch, pair % pairs_per_batch

    def epilogue_pair(s):
        pair = jnp.maximum(s - 1, 0)
        return pair // pairs_per_batch, pair % pairs_per_batch

    return pl.pallas_call(
        kern,
        grid=(n_pairs + 1,),
        in_specs=[
            pl.BlockSpec((None, 2 * tm, D_MODEL), lambda s: (*att_pair(s), 0)),
            pl.BlockSpec((None, seq, 2 * KV_DIM), lambda s: (att_pair(s)[0], 0, 0), pipeline_mode=pl.Buffered(1)),
            _const_spec(metakv.shape),
            _const_spec(rowsum_cols.shape),
            _const_spec(bias.shape),
            pl.BlockSpec(memory_space=pltpu.SMEM),
            _const_spec(gains.shape),
            _const_spec(wq.shape),
            _const_spec(bq.shape),
            _const_spec(wo.shape),
            _const_spec(bo.shape),
            _layer_spec(wgu.shape, layer),
            _layer_spec(wd.shape, layer),
        ],
        out_specs=pl.BlockSpec((None, 2 * tm, D_MODEL), lambda s: (*epilogue_pair(s), 0)),
        out_shape=jax.ShapeDtypeStruct((batch, seq, D_MODEL), jnp.float32),
        scratch_shapes=[
            pltpu.VMEM((tm, D_MODEL), jnp.bfloat16),
            pltpu.VMEM((tm, D_MODEL), jnp.bfloat16),
            pltpu.VMEM((tm, D_MODEL), jnp.bfloat16),
            pltpu.VMEM((tm, D_MODEL), jnp.float32),
            pltpu.VMEM((2, tm, D_MODEL), jnp.float32),
            pltpu.VMEM((2, tm, D_MODEL), jnp.bfloat16),
            pltpu.VMEM((2, tm, D_MODEL), jnp.float32),
            pltpu.VMEM((tm, FFN_COLS), jnp.float32),
            pltpu.VMEM((tm, FFN_COLS), jnp.float32),
            pltpu.VMEM((tm, D_FF), jnp.bfloat16),
            pltpu.VMEM((n_att_units, ATT_ROWS, 4 * BLOCK), jnp.float32),
            pltpu.VMEM((n_att_units, ATT_ROWS, 2 * BLOCK), jnp.float32),
            pltpu.VMEM((n_att_units, ATT_ROWS, 4 * BLOCK), jnp.bfloat16),
        ],
        compiler_params=pltpu.CompilerParams(
            dimension_semantics=("arbitrary",), vmem_limit_bytes=V7X_VMEM_LIMIT_BYTES),
        name="layer_b",
    )(hs, kv, metakv, rowsum_cols, bias, sinks, gains, wq, bq, wo, bo, wgu, wd)


def _cast_kernel(w_ref, out_ref):
    out_ref[...] = w_ref[...].astype(jnp.bfloat16)


def _cast_weights(w, body, name):
    layers, rows, cols = w.shape
    w2d = w.reshape(layers * rows, cols)
    target = max(16, CAST_BLOCK_BYTES // (4 * cols) // 16 * 16)
    block_rows = next(r for r in range(target, 0, -16) if (layers * rows) % r == 0)
    spec = pl.BlockSpec((block_rows, cols), lambda i: (i, 0))
    out = pl.pallas_call(
        body,
        grid=(layers * rows // block_rows,),
        in_specs=[spec],
        out_specs=spec,
        out_shape=jax.ShapeDtypeStruct(w2d.shape, jnp.bfloat16),
        compiler_params=pltpu.CompilerParams(dimension_semantics=("arbitrary",)),
        name=name,
    )(w2d)
    return out.reshape(layers, rows, cols)


def _gain_table(rows):
    zero = jnp.zeros((D_MODEL,), jnp.float32)
    return jnp.stack([rows[i].astype(jnp.float32) if i in rows else zero for i in range(8)])


def kernel(x, meta_tokens, norm_mix_pre, norm_mix_post, norm_ffn_pre, norm_ffn_post, pool_w, pool_scale, kv_norm, w_k, b_k, w_v, b_v, w_q, b_q, w_o, b_o, sinks, rel_bias, w_gate_up, w_down):
    batch, seq, _ = x.shape
    bf16 = jnp.bfloat16

    gains_a = _gain_table({G_MIX_PRE: norm_mix_pre[0], G_MIX_POST: norm_mix_post[0], G_POOL_SCALE: pool_scale[0],
                           G_FFN_PRE: norm_ffn_pre[0], G_FFN_POST: norm_ffn_post[0], G_KV: kv_norm})
    gains_b = _gain_table({G_MIX_PRE: norm_mix_pre[1], G_MIX_POST: norm_mix_post[1],
                           G_FFN_PRE: norm_ffn_pre[1], G_FFN_POST: norm_ffn_post[1]})
    wkv = jnp.concatenate([w_k, w_v], axis=1).astype(bf16)
    bkv = jnp.concatenate([b_k, b_v])[None, :]
    wgu = _cast_weights(w_gate_up, _cast_kernel, "cast_gate_up")
    wd = _cast_weights(w_down, _cast_kernel, "cast_down")
    layer_a_weights = (gains_a, pool_w[0].astype(bf16), wgu, wd, wkv, bkv)

    x2d = x.reshape(batch * seq, D_MODEL)
    hs2, kv_x = _layer_a_call(x2d, meta_tokens, *layer_a_weights,
                              layer=0, tm=TILE_A, tiles_per_batch=seq // TILE_A, is_meta=False)
    _, kv_meta = _layer_a_call(meta_tokens, meta_tokens, *layer_a_weights,
                               layer=0, tm=N_META, tiles_per_batch=1, is_meta=True)
    metakv = jnp.concatenate([jnp.zeros((PAD_FRONT, 2 * KV_DIM), bf16), kv_meta], axis=0)

    bias = _bias_table_call(rel_bias)
    bias = bias.reshape(2, N_KV_HEADS, ATT_ROWS, 4 * BLOCK)

    out = _layer_b_call(hs2.reshape(batch, seq, D_MODEL), kv_x.reshape(batch, seq, 2 * KV_DIM), metakv, bias,
                        sinks[0], gains_b, w_q[0].astype(bf16), b_q, w_o[0].astype(bf16), b_o,
                        wgu, wd, layer=1, tm=TILE_B)
    return out
```

```python
import functools
import math

import numpy as np

import jax
import jax.numpy as jnp
from jax import lax
from jax.experimental import pallas as pl
from jax.experimental.pallas import tpu as pltpu

D_MODEL = 1024
N_META = 16
POOL_WINDOWS = (2, 4, 8, 16)
POOL_GROUP_DIM = D_MODEL // len(POOL_WINDOWS)
HEAD_DIM = 64
N_Q_HEADS = D_MODEL // HEAD_DIM
N_KV_HEADS = 2
HEADS_PER_KV = N_Q_HEADS // N_KV_HEADS
PAIRS_PER_KV = HEADS_PER_KV // 2
N_HEAD_PAIRS = N_Q_HEADS // 2
KV_DIM = N_KV_HEADS * HEAD_DIM
WINDOW = 128
BLOCK = 128
N_BUCKETS = 32
MAX_DISTANCE = 128
D_FF = 2816
EPS = 1e-6
PAD_FRONT = (-N_META) % BLOCK

LANES = 128
GATE_ROWS = 8
FFN_COLS = 256
SILU_ROW_CHUNK = 32
UNRIDDEN_SEGMENTS = 2
CAST_BLOCK_BYTES = 6 * 1024 * 1024
HALO = 16
ROW_CHUNK = 32
POOL_ROW_CHUNK = 64
ATT_ROWS = PAIRS_PER_KV * BLOCK
SOFTMAX_ROW_CHUNK = 16
TILE_A = 512
PIPELINE_DEPTH = 3
TILE_B = 256
V7X_VMEM_LIMIT_BYTES = 63 * 1024 * 1024

G_MIX_PRE, G_MIX_POST, G_POOL_SCALE, G_FFN_PRE, G_FFN_POST, G_KV = range(6)


def _rms(x, g):
    ms = jnp.sum(x * x, axis=-1, keepdims=True) * (1.0 / D_MODEL)
    return x * lax.rsqrt(ms + EPS) * g


def _gain(gains_ref, i):
    return gains_ref[i:i + 1, :]


def _row_items(rows, chunk, body):
    chunk = min(chunk, rows.size)
    return [functools.partial(body, pl.ds(rows.start + i * chunk, chunk)) for i in range(rows.size // chunk)]


def _run_items(items, gate=None):
    for item in items:
        item(gate)


def _gate_of(value):
    bits = pltpu.bitcast(value[-GATE_ROWS:, -LANES:], jnp.uint32)
    return pltpu.bitcast((bits >> 16) >> 16, jnp.float32)


def _gated(x, gate):
    if gate is None:
        return x
    z = jnp.concatenate([gate] * (x.shape[0] // GATE_ROWS), axis=0)
    return jnp.concatenate([x[:, :LANES] + z, x[:, LANES:]], axis=1)


def _spread(items, n_bins):
    bins = [[] for _ in range(n_bins)]
    for k, item in enumerate(items):
        bins[k * n_bins // max(len(items), 1)].append(item)
    return bins


def _swiglu(h_ref, wgu_ref, wd_ref, gbuf, ubuf, act, out_ref, rows, side_items=()):
    segments = ([("gate_up", c) for c in range(0, D_FF, FFN_COLS)]
                + [("down", c) for c in range(0, D_MODEL, FFN_COLS)])
    side = _spread(list(side_items), len(segments) - UNRIDDEN_SEGMENTS) + [[]] * UNRIDDEN_SEGMENTS
    for (kind, c0), side_group in zip(segments, side):
        if kind == "gate_up":
            g = jnp.dot(h_ref[rows, :], wgu_ref[:, c0:c0 + FFN_COLS], preferred_element_type=jnp.float32)
            gbuf[rows, :] = g
            ubuf[rows, :] = jnp.dot(h_ref[rows, :], wgu_ref[:, D_FF + c0:D_FF + c0 + FFN_COLS],
                                    preferred_element_type=jnp.float32)
            gate_source = g

            def silu_rows(r, gate, c0=c0):
                half_g = 0.5 * gbuf[r, :]
                silu = half_g + half_g * jnp.tanh(half_g)
                act[r, c0:c0 + FFN_COLS] = (silu * ubuf[r, :]).astype(jnp.bfloat16)

            _run_items(_row_items(rows, SILU_ROW_CHUNK, silu_rows))
        else:
            y = jnp.dot(act[rows, :], wd_ref[:, c0:c0 + FFN_COLS], preferred_element_type=jnp.float32)
            out_ref[rows, c0:c0 + FFN_COLS] = y
            gate_source = y
        if side_group:
            _run_items(side_group, _gate_of(gate_source))


def _run_pipelined(step, n_tiles, carried, stages):
    if n_tiles == 1:
        mixer, ffn, epilogue = stages(0)
        _run_items(mixer)
        ffn(())
        _run_items(epilogue)
        return

    last_step = n_tiles + 1

    @pl.when(step == 0)
    def _():
        for ref in carried:
            ref[...] = jnp.zeros(ref.shape, ref.dtype)
        _run_items(stages(0)[0])

    for parity in range(2):
        @pl.when((step % 2 == parity) & (step > 0) & (step < last_step))
        def _(parity=parity):
            mixer, _, epilogue = stages(parity)
            _, ffn, _ = stages(1 - parity)
            ffn(epilogue + mixer)

    @pl.when(step == last_step)
    def _():
        _run_items(stages(last_step % 2)[2])


def _layer_a_kernel(x_ref, prev_ref, meta_ref, gains_ref, pool_w_ref, wgu_ref, wd_ref, wkv_ref, bkv_ref,
                    hs_out_ref, kv_out_ref,
                    hext, pbuf, ymix, hs1, hb, yffn, gbuf, ubuf, act, kvin,
                    *, tm, tiles_per_batch, n_tiles, is_meta):
    g_mix_pre = _gain(gains_ref, G_MIX_PRE)
    step = pl.program_id(0)
    tile_rows = pl.ds(0, tm)

    def stages(slot):
        def halo_rows(gate):
            if is_meta:
                hext[0:HALO, :] = jnp.zeros((HALO, D_MODEL), jnp.float32)
            else:
                first_of_batch = (jnp.minimum(step, n_tiles - 1) % tiles_per_batch) == 0
                prev = jnp.where(first_of_batch, meta_ref[...], prev_ref[...])
                hext[0:HALO, :] = _rms(prev, g_mix_pre)

        def norm_rows(rows, gate):
            hext[pl.ds(rows.start + HALO, rows.size), :] = _rms(_gated(x_ref[rows, :], gate), g_mix_pre)

        def pool_rows(rows, gate):
            for gi, w in enumerate(POOL_WINDOWS):
                cols = slice(gi * POOL_GROUP_DIM, (gi + 1) * POOL_GROUP_DIM)
                e0 = _gated(hext[pl.ds(rows.start, rows.size + HALO), cols], gate)
                e = e0
                shift = 1
                while shift < w:
                    e = e + pltpu.roll(e, shift, 0)
                    shift *= 2
                win = e[HALO:, :]
                if is_meta:
                    pos = lax.broadcasted_iota(jnp.int32, win.shape, 0) + 1
                    cnt = jnp.minimum(pos, w).astype(jnp.float32)
                    pooled = win / cnt - e0[HALO:, :]
                else:
                    pooled = win * (1.0 / w) - e0[HALO:, :]
                pbuf[rows, cols] = pooled.astype(jnp.bfloat16)

        def group_matmul(gi, gate):
            cols = slice(gi * POOL_GROUP_DIM, (gi + 1) * POOL_GROUP_DIM)
            ymix[:, cols] = jnp.dot(pbuf[:, cols], pool_w_ref[gi], preferred_element_type=jnp.float32)

        def post_mix_rows(rows, gate):
            y = _gated(ymix[rows, :], gate)
            h1 = x_ref[rows, :] + _rms(y * _gain(gains_ref, G_POOL_SCALE), _gain(gains_ref, G_MIX_POST))
            hs1[slot, rows, :] = h1
            hb[slot, rows, :] = _rms(h1, _gain(gains_ref, G_FFN_PRE)).astype(jnp.bfloat16)

        mixer = ([halo_rows] + _row_items(tile_rows, ROW_CHUNK, norm_rows)
                 + _row_items(tile_rows, POOL_ROW_CHUNK, pool_rows)
                 + [functools.partial(group_matmul, gi) for gi in range(len(POOL_WINDOWS))]
                 + _row_items(tile_rows, ROW_CHUNK, post_mix_rows))

        def ffn(side_items):
            _swiglu(hb.at[slot], wgu_ref, wd_ref, gbuf, ubuf, act, yffn.at[slot], tile_rows, side_items)

        def post_ffn_rows(rows, gate):
            h2 = hs1[slot, rows, :] + _rms(_gated(yffn[slot, rows, :], gate), _gain(gains_ref, G_FFN_POST))
            hs_out_ref[rows, :] = h2
            kvin[rows, :] = _rms(h2, _gain(gains_ref, G_KV)).astype(jnp.bfloat16)

        def kv_projection(gate):
            kv = jnp.dot(kvin[...], wkv_ref[...], preferred_element_type=jnp.float32) + bkv_ref[...]
            kv_out_ref[...] = kv.astype(jnp.bfloat16)

        epilogue = _row_items(tile_rows, ROW_CHUNK, post_ffn_rows) + [kv_projection]
        return mixer, ffn, epilogue

    _run_pipelined(step, n_tiles, (hs1, hb, yffn), stages)


def _const_spec(shape):
    nd = len(shape)
    return pl.BlockSpec(shape, lambda *_: (0,) * nd, pipeline_mode=pl.Buffered(1))


def _layer_spec(stacked_shape, layer):
    _, rows, cols = stacked_shape
    return pl.BlockSpec((None, rows, cols), lambda *_: (layer, 0, 0), pipeline_mode=pl.Buffered(1))


def _layer_a_call(x2d, meta, gains, pool_w, wgu, wd, wkv, bkv, *, layer, tm, tiles_per_batch, is_meta):
    n_rows = x2d.shape[0]
    n_tiles = n_rows // tm
    n_steps = n_tiles if n_tiles == 1 else n_tiles + PIPELINE_DEPTH - 1
    halo_blocks_per_tile = tm // HALO
    kern = functools.partial(_layer_a_kernel, tm=tm, tiles_per_batch=tiles_per_batch, n_tiles=n_tiles,
                             is_meta=is_meta)

    def mixer_tile(s):
        return jnp.minimum(s, n_tiles - 1)

    def epilogue_tile(s):
        return jnp.maximum(s - (n_steps - n_tiles), 0)

    return pl.pallas_call(
        kern,
        grid=(n_steps,),
        in_specs=[
            pl.BlockSpec((tm, D_MODEL), lambda s: (mixer_tile(s), 0)),
            pl.BlockSpec((HALO, D_MODEL), lambda s: (jnp.maximum(mixer_tile(s) * halo_blocks_per_tile - 1, 0), 0)),
            _const_spec((N_META, D_MODEL)),
            _const_spec(gains.shape),
            _const_spec(pool_w.shape),
            _layer_spec(wgu.shape, layer),
            _layer_spec(wd.shape, layer),
            _const_spec(wkv.shape),
            _const_spec(bkv.shape),
        ],
        out_specs=[
            pl.BlockSpec((tm, D_MODEL), lambda s: (epilogue_tile(s), 0)),
            pl.BlockSpec((tm, 2 * KV_DIM), lambda s: (epilogue_tile(s), 0)),
        ],
        out_shape=[
            jax.ShapeDtypeStruct((n_rows, D_MODEL), jnp.float32),
            jax.ShapeDtypeStruct((n_rows, 2 * KV_DIM), jnp.bfloat16),
        ],
        scratch_shapes=[
            pltpu.VMEM((tm + HALO, D_MODEL), jnp.float32),
            pltpu.VMEM((tm, D_MODEL), jnp.bfloat16),
            pltpu.VMEM((tm, D_MODEL), jnp.float32),
            pltpu.VMEM((2, tm, D_MODEL), jnp.float32),
            pltpu.VMEM((2, tm, D_MODEL), jnp.bfloat16),
            pltpu.VMEM((2, tm, D_MODEL), jnp.float32),
            pltpu.VMEM((tm, FFN_COLS), jnp.float32),
            pltpu.VMEM((tm, FFN_COLS), jnp.float32),
            pltpu.VMEM((tm, D_FF), jnp.bfloat16),
            pltpu.VMEM((tm, D_MODEL), jnp.bfloat16),
        ],
        compiler_params=pltpu.CompilerParams(
            dimension_semantics=("arbitrary",), vmem_limit_bytes=V7X_VMEM_LIMIT_BYTES),
        name="layer_a_meta" if is_meta else "layer_a",
    )(x2d, x2d, meta, gains, pool_w, wgu, wd, wkv, bkv)


def _bucket_distance_ranges():
    d = np.arange(WINDOW)
    max_exact = N_BUCKETS // 2
    df = np.maximum(d, 1).astype(np.float32)
    large = max_exact + (np.log(df / np.float32(max_exact)) / np.float32(math.log(MAX_DISTANCE / max_exact))
                         * np.float32(N_BUCKETS - max_exact)).astype(np.int32)
    bucket = np.where(d < max_exact, d, np.minimum(large, N_BUCKETS - 1))
    ranges = []
    for b in range(N_BUCKETS):
        members = d[bucket == b]
        if members.size:
            assert np.array_equal(members, np.arange(members[0], members[-1] + 1))
            ranges.append((int(members[0]), int(members[-1])))
        else:
            ranges.append(None)
    return ranges


def _bias_table_kernel(rel_bias_ref, out_ref):
    m = pl.program_id(0)
    q = lax.broadcasted_iota(jnp.int32, (BLOCK, 2 * BLOCK), 0)
    s = lax.broadcasted_iota(jnp.int32, (BLOCK, 2 * BLOCK), 1)
    d = q + BLOCK - s
    in_window = (d >= 0) & (d < WINDOW)
    valid = (in_window & (s >= PAD_FRONT), in_window)
    for half in range(2):
        h = 2 * m + half
        acc = jnp.zeros((BLOCK, 2 * BLOCK), jnp.float32)
        for b, distances in enumerate(_bucket_distance_ranges()):
            if distances is not None:
                lo, hi = distances
                acc = jnp.where((d >= lo) & (d <= hi), rel_bias_ref[b, h], acc)
        for jsel in range(2):
            out_ref[jsel, 0, :, half * 2 * BLOCK:(half + 1) * 2 * BLOCK] = jnp.where(valid[jsel], acc, -jnp.inf)


def _bias_table_call(rel_bias):
    return pl.pallas_call(
        _bias_table_kernel,
        grid=(N_HEAD_PAIRS,),
        in_specs=[
            pl.BlockSpec(memory_space=pltpu.SMEM),
        ],
        out_specs=pl.BlockSpec((2, 1, BLOCK, 4 * BLOCK), lambda m: (0, m, 0, 0)),
        out_shape=jax.ShapeDtypeStruct((2, N_HEAD_PAIRS, BLOCK, 4 * BLOCK), jnp.float32),
        compiler_params=pltpu.CompilerParams(dimension_semantics=("arbitrary",)),
        name="bias_table",
    )(rel_bias)


def _layer_b_kernel(hs_ref, kv_ref, metakv_ref, rowsum_cols_ref, bias_ref, sinks_ref, gains_ref,
                    wq_ref, bq_ref, wo_ref, bo_ref, wgu_ref, wd_ref,
                    out_ref,
                    hn, qbuf, obuf, yatt, hs3, hb, yffn, gbuf, ubuf, act, lbuf, mbuf, pbuf,
                    *, tm, tiles_per_batch, n_tiles):
    blocks_per_tile = tm // BLOCK
    step = pl.program_id(0)
    t = jnp.minimum(step, n_tiles - 1) % tiles_per_batch
    tile_rows = pl.ds(0, tm)

    lane = lax.broadcasted_iota(jnp.int32, (2 * BLOCK, 2 * HEAD_DIM), 1)
    low_half = lane < HEAD_DIM
    zero = jnp.zeros((2 * BLOCK, 2 * HEAD_DIM), jnp.bfloat16)
    low_half_q = lax.broadcasted_iota(jnp.int32, (BLOCK, 2 * HEAD_DIM), 1) < HEAD_DIM

    def per_kv_head_operands(pair_cols):
        swapped = jnp.concatenate([pair_cols[:, HEAD_DIM:], pair_cols[:, :HEAD_DIM]], axis=1)
        g0 = (jnp.where(low_half, pair_cols, zero), jnp.where(low_half, zero, swapped))
        g1 = (jnp.where(low_half, swapped, zero), jnp.where(low_half, zero, pair_cols))
        return g0, g1

    ones_top = rowsum_cols_ref[0:2 * BLOCK, :]
    ones_bottom = rowsum_cols_ref[2 * BLOCK:4 * BLOCK, :]

    def attention_block(jb, gate):
        j = t * blocks_per_tile + jb
        own = kv_ref[pl.ds(pl.multiple_of(j * BLOCK, BLOCK), BLOCK), :]
        prev_start = pl.multiple_of(jnp.maximum(j - 1, 0) * BLOCK, BLOCK)
        prev = jnp.where(j == 0, metakv_ref[...], kv_ref[pl.ds(prev_start, BLOCK), :])
        kvb = jnp.concatenate([prev, own], axis=0)
        k_ops = per_kv_head_operands(kvb[:, :KV_DIM])
        v_ops = per_kv_head_operands(kvb[:, KV_DIM:])
        jsel = jnp.minimum(j, 1)
        qrows = slice(jb * BLOCK, (jb + 1) * BLOCK)
        for g in range(N_KV_HEADS):
            u = jb * N_KV_HEADS + g
            pairs = range(g * PAIRS_PER_KV, (g + 1) * PAIRS_PER_KV)
            q4 = jnp.concatenate([qbuf[qrows, m * 2 * HEAD_DIM:(m + 1) * 2 * HEAD_DIM] for m in pairs], axis=0)
            k_op = jnp.concatenate(k_ops[g], axis=0)
            v_op = jnp.concatenate([jnp.concatenate([v_ops[g][0], ones_top], axis=1),
                                    jnp.concatenate([v_ops[g][1], ones_bottom], axis=1)], axis=0)
            logits = _gated(lax.dot_general(q4, k_op, (((1,), (1,)), ((), ())),
                                            preferred_element_type=jnp.float32) + bias_ref[jsel, g], gate)
            lbuf[u] = logits
            for r0 in range(0, ATT_ROWS, SOFTMAX_ROW_CHUNK):
                rows = slice(r0, r0 + SOFTMAX_ROW_CHUNK)
                for half in range(2):
                    sink = sinks_ref[2 * (pairs[0] + r0 // BLOCK) + half]
                    lg = logits[rows, half * 2 * BLOCK:(half + 1) * 2 * BLOCK]
                    mx = jnp.maximum(jnp.max(lg, axis=-1, keepdims=True), sink)
                    mbuf[u, rows, half * BLOCK:(half + 1) * BLOCK] = jnp.broadcast_to(mx, (SOFTMAX_ROW_CHUNK, BLOCK))
            for r0 in range(0, ATT_ROWS, SOFTMAX_ROW_CHUNK):
                rows = slice(r0, r0 + SOFTMAX_ROW_CHUNK)
                for half in range(2):
                    mrep = mbuf[u, rows, half * BLOCK:(half + 1) * BLOCK]
                    lg = lbuf[u, rows, half * 2 * BLOCK:(half + 1) * 2 * BLOCK]
                    p = jnp.exp(lg - jnp.concatenate([mrep, mrep], axis=1))
                    pbuf[u, rows, half * 2 * BLOCK:(half + 1) * 2 * BLOCK] = p.astype(jnp.bfloat16)
            oa = jnp.dot(pbuf[u], v_op, preferred_element_type=jnp.float32)
            for i, m in enumerate(pairs):
                rows = slice(i * BLOCK, (i + 1) * BLOCK)
                sink_gap = jnp.where(low_half_q, sinks_ref[2 * m] - mbuf[u, rows, 0:BLOCK],
                                     sinks_ref[2 * m + 1] - mbuf[u, rows, BLOCK:2 * BLOCK])
                denom = oa[rows, 2 * HEAD_DIM:] + jnp.exp(sink_gap)
                o = oa[rows, :2 * HEAD_DIM] * (1.0 / denom)
                obuf[qrows, m * 2 * HEAD_DIM:(m + 1) * 2 * HEAD_DIM] = o.astype(jnp.bfloat16)

    def stages(slot):
        def norm_rows(rows, gate):
            h = _rms(_gated(hs_ref[rows, :], gate), _gain(gains_ref, G_MIX_PRE))
            hn[rows, :] = h.astype(jnp.bfloat16)

        def q_projection(gate):
            q = jnp.dot(hn[...], wq_ref[...], preferred_element_type=jnp.float32) + bq_ref[...]
            qbuf[...] = (q * (HEAD_DIM ** -0.5)).astype(jnp.bfloat16)

        def o_projection(gate):
            yatt[...] = jnp.dot(obuf[...], wo_ref[...], preferred_element_type=jnp.float32) + bo_ref[...]

        def post_mix_rows(rows, gate):
            h3 = hs_ref[rows, :] + _rms(_gated(yatt[rows, :], gate), _gain(gains_ref, G_MIX_POST))
            hs3[slot, rows, :] = h3
            hb[slot, rows, :] = _rms(h3, _gain(gains_ref, G_FFN_PRE)).astype(jnp.bfloat16)

        attention = (_row_items(tile_rows, ROW_CHUNK, norm_rows) + [q_projection]
                     + [functools.partial(attention_block, jb) for jb in range(blocks_per_tile)]
                     + [o_projection] + _row_items(tile_rows, ROW_CHUNK, post_mix_rows))

        def ffn(side_items):
            _swiglu(hb.at[slot], wgu_ref, wd_ref, gbuf, ubuf, act, yffn.at[slot], tile_rows, side_items)

        def post_ffn_rows(rows, gate):
            y = _gated(yffn[slot, rows, :], gate)
            out_ref[rows, :] = hs3[slot, rows, :] + _rms(y, _gain(gains_ref, G_FFN_POST))

        return attention, ffn, _row_items(tile_rows, ROW_CHUNK, post_ffn_rows)

    _run_pipelined(step, n_tiles, (hs3, hb, yffn), stages)


def _layer_b_call(hs, kv, metakv, bias, sinks, gains, wq, bq, wo, bo, wgu, wd, *, layer, tm):
    batch, seq, _ = hs.shape
    tiles_per_batch = seq // tm
    n_tiles = batch * tiles_per_batch
    n_att_units = (tm // BLOCK) * N_KV_HEADS
    lane_head = jnp.arange(2 * HEAD_DIM)[None, :] // HEAD_DIM
    row_head = jnp.arange(4 * BLOCK)[:, None] // (2 * BLOCK)
    rowsum_cols = (lane_head == row_head).astype(jnp.bfloat16)
    kern = functools.partial(_layer_b_kernel, tm=tm, tiles_per_batch=tiles_per_batch, n_tiles=n_tiles)

    def att_tile(s):
        tile = jnp.minimum(s, n_tiles - 1)
        return tile // tiles_per_batch, tile % tiles_per_batch

    def epilogue_tile(s):
        tile = jnp.maximum(s - (PIPELINE_DEPTH - 1), 0)
        return tile // tiles_per_batch, tile % tiles_per_batch

    return pl.pallas_call(
        kern,
        grid=(n_tiles + PIPELINE_DEPTH - 1,),
        in_specs=[
            pl.BlockSpec((None, tm, D_MODEL), lambda s: (*att_tile(s), 0)),
            pl.BlockSpec((None, seq, 2 * KV_DIM), lambda s: (att_tile(s)[0], 0, 0), pipeline_mode=pl.Buffered(1)),
            _const_spec(metakv.shape),
            _const_spec(rowsum_cols.shape),
            _const_spec(bias.shape),
            pl.BlockSpec(memory_space=pltpu.SMEM),
            _const_spec(gains.shape),
            _const_spec(wq.shape),
            _const_spec(bq.shape),
            _const_spec(wo.shape),
            _const_spec(bo.shape),
            _layer_spec(wgu.shape, layer),
            _layer_spec(wd.shape, layer),
        ],
        out_specs=pl.BlockSpec((None, tm, D_MODEL), lambda s: (*epilogue_tile(s), 0)),
        out_shape=jax.ShapeDtypeStruct((batch, seq, D_MODEL), jnp.float32),
        scratch_shapes=[
            pltpu.VMEM((tm, D_MODEL), jnp.bfloat16),
            pltpu.VMEM((tm, D_MODEL), jnp.bfloat16),
            pltpu.VMEM((tm, D_MODEL), jnp.bfloat16),
            pltpu.VMEM((tm, D_MODEL), jnp.float32),
            pltpu.VMEM((2, tm, D_MODEL), jnp.float32),
            pltpu.VMEM((2, tm, D_MODEL), jnp.bfloat16),
            pltpu.VMEM((2, tm, D_MODEL), jnp.float32),
            pltpu.VMEM((tm, FFN_COLS), jnp.float32),
            pltpu.VMEM((tm, FFN_COLS), jnp.float32),
            pltpu.VMEM((tm, D_FF), jnp.bfloat16),
            pltpu.VMEM((n_att_units, ATT_ROWS, 4 * BLOCK), jnp.float32),
            pltpu.VMEM((n_att_units, ATT_ROWS, 2 * BLOCK), jnp.float32),
            pltpu.VMEM((n_att_units, ATT_ROWS, 4 * BLOCK), jnp.bfloat16),
        ],
        compiler_params=pltpu.CompilerParams(
            dimension_semantics=("arbitrary",), vmem_limit_bytes=V7X_VMEM_LIMIT_BYTES),
        name="layer_b",
    )(hs, kv, metakv, rowsum_cols, bias, sinks, gains, wq, bq, wo, bo, wgu, wd)


def _cast_kernel(w_ref, out_ref):
    out_ref[...] = w_ref[...].astype(jnp.bfloat16)


def _cast_weights(w, body, name):
    layers, rows, cols = w.shape
    w2d = w.reshape(layers * rows, cols)
    target = max(16, CAST_BLOCK_BYTES // (4 * cols) // 16 * 16)
    block_rows = next(r for r in range(target, 0, -16) if (layers * rows) % r == 0)
    spec = pl.BlockSpec((block_rows, cols), lambda i: (i, 0))
    out = pl.pallas_call(
        body,
        grid=(layers * rows // block_rows,),
        in_specs=[spec],
        out_specs=spec,
        out_shape=jax.ShapeDtypeStruct(w2d.shape, jnp.bfloat16),
        compiler_params=pltpu.CompilerParams(dimension_semantics=("arbitrary",)),
        name=name,
    )(w2d)
    return out.reshape(layers, rows, cols)


def _gain_table(rows):
    zero = jnp.zeros((D_MODEL,), jnp.float32)
    return jnp.stack([rows[i].astype(jnp.float32) if i in rows else zero for i in range(8)])


def kernel(x, meta_tokens, norm_mix_pre, norm_mix_post, norm_ffn_pre, norm_ffn_post, pool_w, pool_scale, kv_norm, w_k, b_k, w_v, b_v, w_q, b_q, w_o, b_o, sinks, rel_bias, w_gate_up, w_down):
    batch, seq, _ = x.shape
    bf16 = jnp.bfloat16

    gains_a = _gain_table({G_MIX_PRE: norm_mix_pre[0], G_MIX_POST: norm_mix_post[0], G_POOL_SCALE: pool_scale[0],
                           G_FFN_PRE: norm_ffn_pre[0], G_FFN_POST: norm_ffn_post[0], G_KV: kv_norm})
    gains_b = _gain_table({G_MIX_PRE: norm_mix_pre[1], G_MIX_POST: norm_mix_post[1],
                           G_FFN_PRE: norm_ffn_pre[1], G_FFN_POST: norm_ffn_post[1]})
    wkv = jnp.concatenate([w_k, w_v], axis=1).astype(bf16)
    bkv = jnp.concatenate([b_k, b_v])[None, :]
    wgu = _cast_weights(w_gate_up, _cast_kernel, "cast_gate_up")
    wd = _cast_weights(w_down, _cast_kernel, "cast_down")
    layer_a_weights = (gains_a, pool_w[0].astype(bf16), wgu, wd, wkv, bkv)

    x2d = x.reshape(batch * seq, D_MODEL)
    hs2, kv_x = _layer_a_call(x2d, meta_tokens, *layer_a_weights,
                              layer=0, tm=TILE_A, tiles_per_batch=seq // TILE_A, is_meta=False)
    _, kv_meta = _layer_a_call(meta_tokens, meta_tokens, *layer_a_weights,
                               layer=0, tm=N_META, tiles_per_batch=1, is_meta=True)
    metakv = jnp.concatenate([jnp.zeros((PAD_FRONT, 2 * KV_DIM), bf16), kv_meta], axis=0)

    bias = _bias_table_call(rel_bias)
    bias = bias.reshape(2, N_KV_HEADS, ATT_ROWS, 4 * BLOCK)

    out = _layer_b_call(hs2.reshape(batch, seq, D_MODEL), kv_x.reshape(batch, seq, 2 * KV_DIM), metakv, bias,
                        sinks[0], gains_b, w_q[0].astype(bf16), b_q, w_o[0].astype(bf16), b_o,
                        wgu, wd, layer=1, tm=TILE_B)
    return out
```

```python
import functools
import math

import numpy as np

import jax
import jax.numpy as jnp
from jax import lax
from jax.experimental import pallas as pl
from jax.experimental.pallas import tpu as pltpu

D_MODEL = 1024
N_META = 16
POOL_WINDOWS = (2, 4, 8, 16)
POOL_GROUP_DIM = D_MODEL // len(POOL_WINDOWS)
HEAD_DIM = 64
N_Q_HEADS = D_MODEL // HEAD_DIM
N_KV_HEADS = 2
HEADS_PER_KV = N_Q_HEADS // N_KV_HEADS
PAIRS_PER_KV = HEADS_PER_KV // 2
N_HEAD_PAIRS = N_Q_HEADS // 2
KV_DIM = N_KV_HEADS * HEAD_DIM
WINDOW = 128
BLOCK = 128
N_BUCKETS = 32
MAX_DISTANCE = 128
D_FF = 2816
EPS = 1e-6
PAD_FRONT = (-N_META) % BLOCK

LANES = 128
GATE_ROWS = 8
FFN_COLS = 256
SILU_ROW_CHUNK = 32
UNRIDDEN_SEGMENTS = 2
CAST_BLOCK_BYTES = 6 * 1024 * 1024
HALO = 16
ROW_CHUNK = 32
POOL_ROW_CHUNK = 64
ATT_ROWS = PAIRS_PER_KV * BLOCK
SOFTMAX_ROW_CHUNK = 16
TILE_A = 512
TILE_B = 256
V7X_VMEM_LIMIT_BYTES = 63 * 1024 * 1024

G_MIX_PRE, G_MIX_POST, G_POOL_SCALE, G_FFN_PRE, G_FFN_POST, G_KV = range(6)


def _rms(x, g):
    ms = jnp.sum(x * x, axis=-1, keepdims=True) * (1.0 / D_MODEL)
    return x * lax.rsqrt(ms + EPS) * g


def _gain(gains_ref, i):
    return gains_ref[i:i + 1, :]


def _row_items(rows, chunk, body):
    chunk = min(chunk, rows.size)
    return [functools.partial(body, pl.ds(rows.start + i * chunk, chunk)) for i in range(rows.size // chunk)]


def _run_items(items, gate=None):
    for item in items:
        item(gate)


def _gate_of(value):
    bits = pltpu.bitcast(value[-GATE_ROWS:, -LANES:], jnp.uint32)
    return pltpu.bitcast((bits >> 16) >> 16, jnp.float32)


def _gated(x, gate):
    if gate is None:
        return x
    z = jnp.concatenate([gate] * (x.shape[0] // GATE_ROWS), axis=0)
    return jnp.concatenate([x[:, :LANES] + z, x[:, LANES:]], axis=1)


def _spread(items, n_bins):
    bins = [[] for _ in range(n_bins)]
    for k, item in enumerate(items):
        bins[k * n_bins // max(len(items), 1)].append(item)
    return bins


def _swiglu(h_ref, wgu_ref, wd_ref, gbuf, ubuf, act, out_ref, rows, side_items=()):
    segments = ([("gate_up", c) for c in range(0, D_FF, FFN_COLS)]
                + [("down", c) for c in range(0, D_MODEL, FFN_COLS)])
    side = _spread(list(side_items), len(segments) - UNRIDDEN_SEGMENTS) + [[]] * UNRIDDEN_SEGMENTS
    for (kind, c0), side_group in zip(segments, side):
        if kind == "gate_up":
            g = jnp.dot(h_ref[rows, :], wgu_ref[:, c0:c0 + FFN_COLS], preferred_element_type=jnp.float32)
            gbuf[rows, :] = g
            ubuf[rows, :] = jnp.dot(h_ref[rows, :], wgu_ref[:, D_FF + c0:D_FF + c0 + FFN_COLS],
                                    preferred_element_type=jnp.float32)
            gate_source = g

            def silu_rows(r, gate, c0=c0):
                half_g = 0.5 * gbuf[r, :]
                silu = half_g + half_g * jnp.tanh(half_g)
                act[r, c0:c0 + FFN_COLS] = (silu * ubuf[r, :]).astype(jnp.bfloat16)

            _run_items(_row_items(rows, SILU_ROW_CHUNK, silu_rows))
        else:
            y = jnp.dot(act[rows, :], wd_ref[:, c0:c0 + FFN_COLS], preferred_element_type=jnp.float32)
            out_ref[rows, c0:c0 + FFN_COLS] = y
            gate_source = y
        if side_group:
            _run_items(side_group, _gate_of(gate_source))


def _run_single_tile(stages):
    mixer, ffn, epilogue = stages(0)
    _run_items(mixer)
    ffn(())
    _run_items(epilogue)


def _run_pipelined_pairs(step, n_pairs, stages):
    even_mixer, even_ffn, even_epilogue = stages(0)
    odd_mixer, odd_ffn, odd_epilogue = stages(1)

    @pl.when(step == 0)
    def _():
        _run_items(even_mixer)
        even_ffn(odd_mixer)

    @pl.when((step > 0) & (step < n_pairs))
    def _():
        odd_ffn(even_epilogue + even_mixer)
        even_ffn(odd_epilogue + odd_mixer)

    @pl.when(step == n_pairs)
    def _():
        odd_ffn(even_epilogue)
        _run_items(odd_epilogue)


def _layer_a_kernel(x_ref, prev_ref, meta_ref, gains_ref, pool_w_ref, wgu_ref, wd_ref, wkv_ref, bkv_ref,
                    hs_out_ref, kv_out_ref,
                    hext, pbuf, ymix, hs1, hb, yffn, gbuf, ubuf, act, kvin,
                    *, tm, tiles_per_batch, n_tiles, is_meta):
    g_mix_pre = _gain(gains_ref, G_MIX_PRE)
    step = pl.program_id(0)
    tile_rows = pl.ds(0, tm)

    def stages(slot):
        def io(rows):
            return pl.ds(rows.start + slot * tm, rows.size)

        def halo_rows(gate):
            if is_meta:
                hext[0:HALO, :] = jnp.zeros((HALO, D_MODEL), jnp.float32)
            elif slot == 1:
                hext[0:HALO, :] = _rms(x_ref[tm - HALO:tm, :], g_mix_pre)
            else:
                pair = jnp.minimum(step, n_tiles // 2 - 1)
                first_of_batch = ((2 * pair) % tiles_per_batch) == 0
                prev = jnp.where(first_of_batch, meta_ref[...], prev_ref[...])
                hext[0:HALO, :] = _rms(prev, g_mix_pre)

        def norm_rows(rows, gate):
            hext[pl.ds(rows.start + HALO, rows.size), :] = _rms(_gated(x_ref[io(rows), :], gate), g_mix_pre)

        def pool_rows(rows, gate):
            for gi, w in enumerate(POOL_WINDOWS):
                cols = slice(gi * POOL_GROUP_DIM, (gi + 1) * POOL_GROUP_DIM)
                e0 = _gated(hext[pl.ds(rows.start, rows.size + HALO), cols], gate)
                e = e0
                shift = 1
                while shift < w:
                    e = e + pltpu.roll(e, shift, 0)
                    shift *= 2
                win = e[HALO:, :]
                if is_meta:
                    pos = lax.broadcasted_iota(jnp.int32, win.shape, 0) + 1
                    cnt = jnp.minimum(pos, w).astype(jnp.float32)
                    pooled = win / cnt - e0[HALO:, :]
                else:
                    pooled = win * (1.0 / w) - e0[HALO:, :]
                pbuf[rows, cols] = pooled.astype(jnp.bfloat16)

        def group_matmul(gi, gate):
            cols = slice(gi * POOL_GROUP_DIM, (gi + 1) * POOL_GROUP_DIM)
            ymix[:, cols] = jnp.dot(pbuf[:, cols], pool_w_ref[gi], preferred_element_type=jnp.float32)

        def post_mix_rows(rows, gate):
            y = _gated(ymix[rows, :], gate)
            h1 = x_ref[io(rows), :] + _rms(y * _gain(gains_ref, G_POOL_SCALE), _gain(gains_ref, G_MIX_POST))
            hs1[slot, rows, :] = h1
            hb[slot, rows, :] = _rms(h1, _gain(gains_ref, G_FFN_PRE)).astype(jnp.bfloat16)

        mixer = ([halo_rows] + _row_items(tile_rows, ROW_CHUNK, norm_rows)
                 + _row_items(tile_rows, POOL_ROW_CHUNK, pool_rows)
                 + [functools.partial(group_matmul, gi) for gi in range(len(POOL_WINDOWS))]
                 + _row_items(tile_rows, ROW_CHUNK, post_mix_rows))

        def ffn(side_items):
            _swiglu(hb.at[slot], wgu_ref, wd_ref, gbuf, ubuf, act, yffn.at[slot], tile_rows, side_items)

        def post_ffn_rows(rows, gate):
            h2 = hs1[slot, rows, :] + _rms(_gated(yffn[slot, rows, :], gate), _gain(gains_ref, G_FFN_POST))
            hs_out_ref[io(rows), :] = h2
            kvin[rows, :] = _rms(h2, _gain(gains_ref, G_KV)).astype(jnp.bfloat16)

        def kv_projection(gate):
            kv = jnp.dot(kvin[...], wkv_ref[...], preferred_element_type=jnp.float32) + bkv_ref[...]
            kv_out_ref[io(tile_rows), :] = kv.astype(jnp.bfloat16)

        epilogue = _row_items(tile_rows, ROW_CHUNK, post_ffn_rows) + [kv_projection]
        return mixer, ffn, epilogue

    if n_tiles == 1:
        _run_single_tile(stages)
    else:
        _run_pipelined_pairs(step, n_tiles // 2, stages)


def _const_spec(shape):
    nd = len(shape)
    return pl.BlockSpec(shape, lambda *_: (0,) * nd, pipeline_mode=pl.Buffered(1))


def _layer_spec(stacked_shape, layer):
    _, rows, cols = stacked_shape
    return pl.BlockSpec((None, rows, cols), lambda *_: (layer, 0, 0), pipeline_mode=pl.Buffered(1))


def _layer_a_call(x2d, meta, gains, pool_w, wgu, wd, wkv, bkv, *, layer, tm, tiles_per_batch, is_meta):
    n_rows = x2d.shape[0]
    n_tiles = n_rows // tm
    tiles_per_step = 1 if n_tiles == 1 else 2
    block_rows = tiles_per_step * tm
    n_blocks = n_tiles // tiles_per_step
    n_steps = n_blocks if n_tiles == 1 else n_blocks + 1
    halo_blocks_per_block = block_rows // HALO
    kern = functools.partial(_layer_a_kernel, tm=tm, tiles_per_batch=tiles_per_batch, n_tiles=n_tiles,
                             is_meta=is_meta)

    def mixer_block(s):
        return jnp.minimum(s, n_blocks - 1)

    def epilogue_block(s):
        return jnp.maximum(s - (n_steps - n_blocks), 0)

    return pl.pallas_call(
        kern,
        grid=(n_steps,),
        in_specs=[
            pl.BlockSpec((block_rows, D_MODEL), lambda s: (mixer_block(s), 0)),
            pl.BlockSpec((HALO, D_MODEL), lambda s: (jnp.maximum(mixer_block(s) * halo_blocks_per_block - 1, 0), 0)),
            _const_spec((N_META, D_MODEL)),
            _const_spec(gains.shape),
            _const_spec(pool_w.shape),
            _layer_spec(wgu.shape, layer),
            _layer_spec(wd.shape, layer),
            _const_spec(wkv.shape),
            _const_spec(bkv.shape),
        ],
        out_specs=[
            pl.BlockSpec((block_rows, D_MODEL), lambda s: (epilogue_block(s), 0)),
            pl.BlockSpec((block_rows, 2 * KV_DIM), lambda s: (epilogue_block(s), 0)),
        ],
        out_shape=[
            jax.ShapeDtypeStruct((n_rows, D_MODEL), jnp.float32),
            jax.ShapeDtypeStruct((n_rows, 2 * KV_DIM), jnp.bfloat16),
        ],
        scratch_shapes=[
            pltpu.VMEM((tm + HALO, D_MODEL), jnp.float32),
            pltpu.VMEM((tm, D_MODEL), jnp.bfloat16),
            pltpu.VMEM((tm, D_MODEL), jnp.float32),
            pltpu.VMEM((2, tm, D_MODEL), jnp.float32),
            pltpu.VMEM((2, tm, D_MODEL), jnp.bfloat16),
            pltpu.VMEM((2, tm, D_MODEL), jnp.float32),
            pltpu.VMEM((tm, FFN_COLS), jnp.float32),
            pltpu.VMEM((tm, FFN_COLS), jnp.float32),
            pltpu.VMEM((tm, D_FF), jnp.bfloat16),
            pltpu.VMEM((tm, D_MODEL), jnp.bfloat16),
        ],
        compiler_params=pltpu.CompilerParams(
            dimension_semantics=("arbitrary",), vmem_limit_bytes=V7X_VMEM_LIMIT_BYTES),
        name="layer_a_meta" if is_meta else "layer_a",
    )(x2d, x2d, meta, gains, pool_w, wgu, wd, wkv, bkv)


def _bucket_distance_ranges():
    d = np.arange(WINDOW)
    max_exact = N_BUCKETS // 2
    df = np.maximum(d, 1).astype(np.float32)
    large = max_exact + (np.log(df / np.float32(max_exact)) / np.float32(math.log(MAX_DISTANCE / max_exact))
                         * np.float32(N_BUCKETS - max_exact)).astype(np.int32)
    bucket = np.where(d < max_exact, d, np.minimum(large, N_BUCKETS - 1))
    ranges = []
    for b in range(N_BUCKETS):
        members = d[bucket == b]
        if members.size:
            assert np.array_equal(members, np.arange(members[0], members[-1] + 1))
            ranges.append((int(members[0]), int(members[-1])))
        else:
            ranges.append(None)
    return ranges


def _bias_table_kernel(rel_bias_ref, out_ref):
    m = pl.program_id(0)
    q = lax.broadcasted_iota(jnp.int32, (BLOCK, 2 * BLOCK), 0)
    s = lax.broadcasted_iota(jnp.int32, (BLOCK, 2 * BLOCK), 1)
    d = q + BLOCK - s
    in_window = (d >= 0) & (d < WINDOW)
    valid = (in_window & (s >= PAD_FRONT), in_window)
    for half in range(2):
        h = 2 * m + half
        acc = jnp.zeros((BLOCK, 2 * BLOCK), jnp.float32)
        for b, distances in enumerate(_bucket_distance_ranges()):
            if distances is not None:
                lo, hi = distances
                acc = jnp.where((d >= lo) & (d <= hi), rel_bias_ref[b, h], acc)
        for jsel in range(2):
            out_ref[jsel, 0, :, half * 2 * BLOCK:(half + 1) * 2 * BLOCK] = jnp.where(valid[jsel], acc, -jnp.inf)


def _bias_table_call(rel_bias):
    return pl.pallas_call(
        _bias_table_kernel,
        grid=(N_HEAD_PAIRS,),
        in_specs=[
            pl.BlockSpec(memory_space=pltpu.SMEM),
        ],
        out_specs=pl.BlockSpec((2, 1, BLOCK, 4 * BLOCK), lambda m: (0, m, 0, 0)),
        out_shape=jax.ShapeDtypeStruct((2, N_HEAD_PAIRS, BLOCK, 4 * BLOCK), jnp.float32),
        compiler_params=pltpu.CompilerParams(dimension_semantics=("arbitrary",)),
        name="bias_table",
    )(rel_bias)


def _layer_b_kernel(hs_ref, kv_ref, metakv_ref, rowsum_cols_ref, bias_ref, sinks_ref, gains_ref,
                    wq_ref, bq_ref, wo_ref, bo_ref, wgu_ref, wd_ref,
                    out_ref,
                    hn, qbuf, obuf, yatt, hs3, hb, yffn, gbuf, ubuf, act, lbuf, mbuf, pbuf,
                    *, tm, tiles_per_batch, n_tiles):
    blocks_per_tile = tm // BLOCK
    step = pl.program_id(0)
    pair = jnp.minimum(step, n_tiles // 2 - 1)
    tile_rows = pl.ds(0, tm)

    lane = lax.broadcasted_iota(jnp.int32, (2 * BLOCK, 2 * HEAD_DIM), 1)
    low_half = lane < HEAD_DIM
    zero = jnp.zeros((2 * BLOCK, 2 * HEAD_DIM), jnp.bfloat16)
    low_half_q = lax.broadcasted_iota(jnp.int32, (BLOCK, 2 * HEAD_DIM), 1) < HEAD_DIM

    def per_kv_head_operands(pair_cols):
        swapped = jnp.concatenate([pair_cols[:, HEAD_DIM:], pair_cols[:, :HEAD_DIM]], axis=1)
        g0 = (jnp.where(low_half, pair_cols, zero), jnp.where(low_half, zero, swapped))
        g1 = (jnp.where(low_half, swapped, zero), jnp.where(low_half, zero, pair_cols))
        return g0, g1

    ones_top = rowsum_cols_ref[0:2 * BLOCK, :]
    ones_bottom = rowsum_cols_ref[2 * BLOCK:4 * BLOCK, :]

    def attention_block(slot, jb, gate):
        t = (2 * pair + slot) % tiles_per_batch
        j = t * blocks_per_tile + jb
        own = kv_ref[pl.ds(pl.multiple_of(j * BLOCK, BLOCK), BLOCK), :]
        prev_start = pl.multiple_of(jnp.maximum(j - 1, 0) * BLOCK, BLOCK)
        prev = jnp.where(j == 0, metakv_ref[...], kv_ref[pl.ds(prev_start, BLOCK), :])
        kvb = jnp.concatenate([prev, own], axis=0)
        k_ops = per_kv_head_operands(kvb[:, :KV_DIM])
        v_ops = per_kv_head_operands(kvb[:, KV_DIM:])
        jsel = jnp.minimum(j, 1)
        qrows = slice(jb * BLOCK, (jb + 1) * BLOCK)
        for g in range(N_KV_HEADS):
            u = jb * N_KV_HEADS + g
            pairs = range(g * PAIRS_PER_KV, (g + 1) * PAIRS_PER_KV)
            q4 = jnp.concatenate([qbuf[qrows, m * 2 * HEAD_DIM:(m + 1) * 2 * HEAD_DIM] for m in pairs], axis=0)
            k_op = jnp.concatenate(k_ops[g], axis=0)
            v_op = jnp.concatenate([jnp.concatenate([v_ops[g][0], ones_top], axis=1),
                                    jnp.concatenate([v_ops[g][1], ones_bottom], axis=1)], axis=0)
            logits = _gated(lax.dot_general(q4, k_op, (((1,), (1,)), ((), ())),
                                            preferred_element_type=jnp.float32) + bias_ref[jsel, g], gate)
            lbuf[u] = logits
            for r0 in range(0, ATT_ROWS, SOFTMAX_ROW_CHUNK):
                rows = slice(r0, r0 + SOFTMAX_ROW_CHUNK)
                for half in range(2):
                    sink = sinks_ref[2 * (pairs[0] + r0 // BLOCK) + half]
                    lg = logits[rows, half * 2 * BLOCK:(half + 1) * 2 * BLOCK]
                    mx = jnp.maximum(jnp.max(lg, axis=-1, keepdims=True), sink)
                    mbuf[u, rows, half * BLOCK:(half + 1) * BLOCK] = jnp.broadcast_to(mx, (SOFTMAX_ROW_CHUNK, BLOCK))
            for r0 in range(0, ATT_ROWS, SOFTMAX_ROW_CHUNK):
                rows = slice(r0, r0 + SOFTMAX_ROW_CHUNK)
                for half in range(2):
                    mrep = mbuf[u, rows, half * BLOCK:(half + 1) * BLOCK]
                    lg = lbuf[u, rows, half * 2 * BLOCK:(half + 1) * 2 * BLOCK]
                    p = jnp.exp(lg - jnp.concatenate([mrep, mrep], axis=1))
                    pbuf[u, rows, half * 2 * BLOCK:(half + 1) * 2 * BLOCK] = p.astype(jnp.bfloat16)
            oa = jnp.dot(pbuf[u], v_op, preferred_element_type=jnp.float32)
            for i, m in enumerate(pairs):
                rows = slice(i * BLOCK, (i + 1) * BLOCK)
                sink_gap = jnp.where(low_half_q, sinks_ref[2 * m] - mbuf[u, rows, 0:BLOCK],
                                     sinks_ref[2 * m + 1] - mbuf[u, rows, BLOCK:2 * BLOCK])
                denom = oa[rows, 2 * HEAD_DIM:] + jnp.exp(sink_gap)
                o = oa[rows, :2 * HEAD_DIM] * (1.0 / denom)
                obuf[qrows, m * 2 * HEAD_DIM:(m + 1) * 2 * HEAD_DIM] = o.astype(jnp.bfloat16)

    def stages(slot):
        def io(rows):
            return pl.ds(rows.start + slot * tm, rows.size)

        def norm_rows(rows, gate):
            h = _rms(_gated(hs_ref[io(rows), :], gate), _gain(gains_ref, G_MIX_PRE))
            hn[rows, :] = h.astype(jnp.bfloat16)

        def q_projection(gate):
            q = jnp.dot(hn[...], wq_ref[...], preferred_element_type=jnp.float32) + bq_ref[...]
            qbuf[...] = (q * (HEAD_DIM ** -0.5)).astype(jnp.bfloat16)

        def o_projection(gate):
            yatt[...] = jnp.dot(obuf[...], wo_ref[...], preferred_element_type=jnp.float32) + bo_ref[...]

        def post_mix_rows(rows, gate):
            h3 = hs_ref[io(rows), :] + _rms(_gated(yatt[rows, :], gate), _gain(gains_ref, G_MIX_POST))
            hs3[slot, rows, :] = h3
            hb[slot, rows, :] = _rms(h3, _gain(gains_ref, G_FFN_PRE)).astype(jnp.bfloat16)

        attention = (_row_items(tile_rows, ROW_CHUNK, norm_rows) + [q_projection]
                     + [functools.partial(attention_block, slot, jb) for jb in range(blocks_per_tile)]
                     + [o_projection] + _row_items(tile_rows, ROW_CHUNK, post_mix_rows))

        def ffn(side_items):
            _swiglu(hb.at[slot], wgu_ref, wd_ref, gbuf, ubuf, act, yffn.at[slot], tile_rows, side_items)

        def post_ffn_rows(rows, gate):
            y = _gated(yffn[slot, rows, :], gate)
            out_ref[io(rows), :] = hs3[slot, rows, :] + _rms(y, _gain(gains_ref, G_FFN_POST))

        return attention, ffn, _row_items(tile_rows, ROW_CHUNK, post_ffn_rows)

    _run_pipelined_pairs(step, n_tiles // 2, stages)


def _layer_b_call(hs, kv, metakv, bias, sinks, gains, wq, bq, wo, bo, wgu, wd, *, layer, tm):
    batch, seq, _ = hs.shape
    tiles_per_batch = seq // tm
    n_tiles = batch * tiles_per_batch
    n_att_units = (tm // BLOCK) * N_KV_HEADS
    lane_head = jnp.arange(2 * HEAD_DIM)[None, :] // HEAD_DIM
    row_head = jnp.arange(4 * BLOCK)[:, None] // (2 * BLOCK)
    rowsum_cols = (lane_head == row_head).astype(jnp.bfloat16)
    kern = functools.partial(_layer_b_kernel, tm=tm, tiles_per_batch=tiles_per_batch, n_tiles=n_tiles)

    n_pairs = n_tiles // 2
    pairs_per_batch = tiles_per_batch // 2

    def att_pair(s):
        pair = jnp.minimum(s, n_pairs - 1)
        return pair // pairs_per_batch, pair % pairs_per_batch

    def epilogue_pair(s):
        pair = jnp.maximum(s - 1, 0)
        return pair // pairs_per_batch, pair % pairs_per_batch

    return pl.pallas_call(
        kern,
        grid=(n_pairs + 1,),
        in_specs=[
            pl.BlockSpec((None, 2 * tm, D_MODEL), lambda s: (*att_pair(s), 0)),
            pl.BlockSpec((None, seq, 2 * KV_DIM), lambda s: (att_pair(s)[0], 0, 0), pipeline_mode=pl.Buffered(1)),
            _const_spec(metakv.shape),
            _const_spec(rowsum_cols.shape),
            _const_spec(bias.shape),
            pl.BlockSpec(memory_space=pltpu.SMEM),
            _const_spec(gains.shape),
            _const_spec(wq.shape),
            _const_spec(bq.shape),
            _const_spec(wo.shape),
            _const_spec(bo.shape),
            _layer_spec(wgu.shape, layer),
            _layer_spec(wd.shape, layer),
        ],
        out_specs=pl.BlockSpec((None, 2 * tm, D_MODEL), lambda s: (*epilogue_pair(s), 0)),
        out_shape=jax.ShapeDtypeStruct((batch, seq, D_MODEL), jnp.float32),
        scratch_shapes=[
            pltpu.VMEM((tm, D_MODEL), jnp.bfloat16),
            pltpu.VMEM((tm, D_MODEL), jnp.bfloat16),
            pltpu.VMEM((tm, D_MODEL), jnp.bfloat16),
            pltpu.VMEM((tm, D_MODEL), jnp.float32),
            pltpu.VMEM((2, tm, D_MODEL), jnp.float32),
            pltpu.VMEM((2, tm, D_MODEL), jnp.bfloat16),
            pltpu.VMEM((2, tm, D_MODEL), jnp.float32),
            pltpu.VMEM((tm, FFN_COLS), jnp.float32),
            pltpu.VMEM((tm, FFN_COLS), jnp.float32),
            pltpu.VMEM((tm, D_FF), jnp.bfloat16),
            pltpu.VMEM((n_att_units, ATT_ROWS, 4 * BLOCK), jnp.float32),
            pltpu.VMEM((n_att_units, ATT_ROWS, 2 * BLOCK), jnp.float32),
            pltpu.VMEM((n_att_units, ATT_ROWS, 4 * BLOCK), jnp.bfloat16),
        ],
        compiler_params=pltpu.CompilerParams(
            dimension_semantics=("arbitrary",), vmem_limit_bytes=V7X_VMEM_LIMIT_BYTES),
        name="layer_b",
    )(hs, kv, metakv, rowsum_cols, bias, sinks, gains, wq, bq, wo, bo, wgu, wd)


def _cast_kernel(w_ref, out_ref):
    out_ref[...] = w_ref[...].astype(jnp.bfloat16)


def _cast_weights(w, body, name):
    layers, rows, cols = w.shape
    w2d = w.reshape(layers * rows, cols)
    target = max(16, CAST_BLOCK_BYTES // (4 * cols) // 16 * 16)
    block_rows = next(r for r in range(target, 0, -16) if (layers * rows) % r == 0)
    spec = pl.BlockSpec((block_rows, cols), lambda i: (i, 0))
    out = pl.pallas_call(
        body,
        grid=(layers * rows // block_rows,),
        in_specs=[spec],
        out_specs=spec,
        out_shape=jax.ShapeDtypeStruct(w2d.shape, jnp.bfloat16),
        compiler_params=pltpu.CompilerParams(dimension_semantics=("arbitrary",)),
        name=name,
    )(w2d)
    return out.reshape(layers, rows, cols)


def _gain_table(rows):
    zero = jnp.zeros((D_MODEL,), jnp.float32)
    return jnp.stack([rows[i].astype(jnp.float32) if i in rows else zero for i in range(8)])


def kernel(x, meta_tokens, norm_mix_pre, norm_mix_post, norm_ffn_pre, norm_ffn_post, pool_w, pool_scale, kv_norm, w_k, b_k, w_v, b_v, w_q, b_q, w_o, b_o, sinks, rel_bias, w_gate_up, w_down):
    batch, seq, _ = x.shape
    bf16 = jnp.bfloat16

    gains_a = _gain_table({G_MIX_PRE: norm_mix_pre[0], G_MIX_POST: norm_mix_post[0], G_POOL_SCALE: pool_scale[0],
                           G_FFN_PRE: norm_ffn_pre[0], G_FFN_POST: norm_ffn_post[0], G_KV: kv_norm})
    gains_b = _gain_table({G_MIX_PRE: norm_mix_pre[1], G_MIX_POST: norm_mix_post[1],
                           G_FFN_PRE: norm_ffn_pre[1], G_FFN_POST: norm_ffn_post[1]})
    wkv = jnp.concatenate([w_k, w_v], axis=1).astype(bf16)
    bkv = jnp.concatenate([b_k, b_v])[None, :]
    wgu = _cast_weights(w_gate_up, _cast_kernel, "cast_gate_up")
    wd = _cast_weights(w_down, _cast_kernel, "cast_down")
    layer_a_weights = (gains_a, pool_w[0].astype(bf16), wgu, wd, wkv, bkv)

    x2d = x.reshape(batch * seq, D_MODEL)
    hs2, kv_x = _layer_a_call(x2d, meta_tokens, *layer_a_weights,
                              layer=0, tm=TILE_A, tiles_per_batch=seq // TILE_A, is_meta=False)
    _, kv_meta = _layer_a_call(meta_tokens, meta_tokens, *layer_a_weights,
                               layer=0, tm=N_META, tiles_per_batch=1, is_meta=True)
    metakv = jnp.concatenate([jnp.zeros((PAD_FRONT, 2 * KV_DIM), bf16), kv_meta], axis=0)

    bias = _bias_table_call(rel_bias)
    bias = bias.reshape(2, N_KV_HEADS, ATT_ROWS, 4 * BLOCK)

    out = _layer_b_call(hs2.reshape(batch, seq, D_MODEL), kv_x.reshape(batch, seq, 2 * KV_DIM), metakv, bias,
                        sinks[0], gains_b, w_q[0].astype(bf16), b_q, w_o[0].astype(bf16), b_o,
                        wgu, wd, layer=1, tm=TILE_B)
    return out
```

```python
import functools
import math

import numpy as np

import jax
import jax.numpy as jnp
from jax import lax
from jax.experimental import pallas as pl
from jax.experimental.pallas import tpu as pltpu

D_MODEL = 1024
N_META = 16
POOL_WINDOWS = (2, 4, 8, 16)
POOL_GROUP_DIM = D_MODEL // len(POOL_WINDOWS)
HEAD_DIM = 64
N_Q_HEADS = D_MODEL // HEAD_DIM
N_KV_HEADS = 2
HEADS_PER_KV = N_Q_HEADS // N_KV_HEADS
PAIRS_PER_KV = HEADS_PER_KV // 2
N_HEAD_PAIRS = N_Q_HEADS // 2
KV_DIM = N_KV_HEADS * HEAD_DIM
WINDOW = 128
BLOCK = 128
N_BUCKETS = 32
MAX_DISTANCE = 128
D_FF = 2816
EPS = 1e-6
PAD_FRONT = (-N_META) % BLOCK

LANES = 128
GATE_ROWS = 8
FFN_COLS = 256
SILU_ROW_CHUNK = 32
UNRIDDEN_SEGMENTS = 4
CAST_BLOCK_BYTES = 6 * 1024 * 1024
HALO = 16
ROW_CHUNK = 32
POOL_ROW_CHUNK = 64
ATT_ROWS = PAIRS_PER_KV * BLOCK
SOFTMAX_ROW_CHUNK = 16
TILE_A = 512
PIPELINE_DEPTH = 3
TILE_B = 256
V7X_VMEM_LIMIT_BYTES = 63 * 1024 * 1024

G_MIX_PRE, G_MIX_POST, G_POOL_SCALE, G_FFN_PRE, G_FFN_POST, G_KV = range(6)


def _rms(x, g):
    ms = jnp.sum(x * x, axis=-1, keepdims=True) * (1.0 / D_MODEL)
    return x * lax.rsqrt(ms + EPS) * g


def _gain(gains_ref, i):
    return gains_ref[i:i + 1, :]


def _row_items(rows, chunk, body):
    chunk = min(chunk, rows.size)
    return [functools.partial(body, pl.ds(rows.start + i * chunk, chunk)) for i in range(rows.size // chunk)]


def _run_items(items, gate=None):
    for item in items:
        item(gate)


def _gate_of(value):
    bits = pltpu.bitcast(value[-GATE_ROWS:, -LANES:], jnp.uint32)
    return pltpu.bitcast((bits >> 16) >> 16, jnp.float32)


def _gated(x, gate):
    if gate is None:
        return x
    z = jnp.concatenate([gate] * (x.shape[0] // GATE_ROWS), axis=0)
    return jnp.concatenate([x[:, :LANES] + z, x[:, LANES:]], axis=1)


def _spread(items, n_bins):
    bins = [[] for _ in range(n_bins)]
    for k, item in enumerate(items):
        bins[k * n_bins // max(len(items), 1)].append(item)
    return bins


def _swiglu(h_ref, wgu_ref, wd_ref, gbuf, ubuf, act, out_ref, rows, side_items=()):
    segments = ([("gate_up", c) for c in range(0, D_FF, FFN_COLS)]
                + [("down", c) for c in range(0, D_MODEL, FFN_COLS)])
    side = _spread(list(side_items), len(segments) - UNRIDDEN_SEGMENTS) + [[]] * UNRIDDEN_SEGMENTS
    for (kind, c0), side_group in zip(segments, side):
        if kind == "gate_up":
            g = jnp.dot(h_ref[rows, :], wgu_ref[:, c0:c0 + FFN_COLS], preferred_element_type=jnp.float32)
            gbuf[rows, :] = g
            ubuf[rows, :] = jnp.dot(h_ref[rows, :], wgu_ref[:, D_FF + c0:D_FF + c0 + FFN_COLS],
                                    preferred_element_type=jnp.float32)
            gate_source = g

            def silu_rows(r, gate, c0=c0):
                half_g = 0.5 * gbuf[r, :]
                silu = half_g + half_g * jnp.tanh(half_g)
                act[r, c0:c0 + FFN_COLS] = (silu * ubuf[r, :]).astype(jnp.bfloat16)

            _run_items(_row_items(rows, SILU_ROW_CHUNK, silu_rows))
        else:
            y = jnp.dot(act[rows, :], wd_ref[:, c0:c0 + FFN_COLS], preferred_element_type=jnp.float32)
            out_ref[rows, c0:c0 + FFN_COLS] = y
            gate_source = y
        if side_group:
            _run_items(side_group, _gate_of(gate_source))


def _run_pipelined(step, n_tiles, carried, stages):
    if n_tiles == 1:
        mixer, ffn, epilogue = stages(0)
        _run_items(mixer)
        ffn(())
        _run_items(epilogue)
        return

    last_step = n_tiles + 1

    @pl.when(step == 0)
    def _():
        for ref in carried:
            ref[...] = jnp.zeros(ref.shape, ref.dtype)
        _run_items(stages(0)[0])

    for parity in range(2):
        @pl.when((step % 2 == parity) & (step > 0) & (step < last_step))
        def _(parity=parity):
            mixer, _, epilogue = stages(parity)
            _, ffn, _ = stages(1 - parity)
            ffn(epilogue + mixer)

    @pl.when(step == last_step)
    def _():
        _run_items(stages(last_step % 2)[2])


def _run_pipelined_pairs(step, n_pairs, stages):
    even_mixer, even_ffn, even_epilogue = stages(0)
    odd_mixer, odd_ffn, odd_epilogue = stages(1)

    @pl.when(step == 0)
    def _():
        _run_items(even_mixer)
        even_ffn(odd_mixer)

    @pl.when((step > 0) & (step < n_pairs))
    def _():
        odd_ffn(even_epilogue + even_mixer)
        even_ffn(odd_epilogue + odd_mixer)

    @pl.when(step == n_pairs)
    def _():
        odd_ffn(even_epilogue)
        _run_items(odd_epilogue)


def _layer_a_kernel(x_ref, prev_ref, meta_ref, gains_ref, pool_w_ref, wgu_ref, wd_ref, wkv_ref, bkv_ref,
                    hs_out_ref, kv_out_ref,
                    hext, pbuf, ymix, hs1, hb, yffn, gbuf, ubuf, act, kvin,
                    *, tm, tiles_per_batch, n_tiles, is_meta):
    g_mix_pre = _gain(gains_ref, G_MIX_PRE)
    step = pl.program_id(0)
    tile_rows = pl.ds(0, tm)

    def stages(slot):
        def halo_rows(gate):
            if is_meta:
                hext[0:HALO, :] = jnp.zeros((HALO, D_MODEL), jnp.float32)
            else:
                first_of_batch = (jnp.minimum(step, n_tiles - 1) % tiles_per_batch) == 0
                prev = jnp.where(first_of_batch, meta_ref[...], prev_ref[...])
                hext[0:HALO, :] = _rms(prev, g_mix_pre)

        def norm_rows(rows, gate):
            hext[pl.ds(rows.start + HALO, rows.size), :] = _rms(_gated(x_ref[rows, :], gate), g_mix_pre)

        def pool_rows(rows, gate):
            for gi, w in enumerate(POOL_WINDOWS):
                cols = slice(gi * POOL_GROUP_DIM, (gi + 1) * POOL_GROUP_DIM)
                e0 = _gated(hext[pl.ds(rows.start, rows.size + HALO), cols], gate)
                e = e0
                shift = 1
                while shift < w:
                    e = e + pltpu.roll(e, shift, 0)
                    shift *= 2
                win = e[HALO:, :]
                if is_meta:
                    pos = lax.broadcasted_iota(jnp.int32, win.shape, 0) + 1
                    cnt = jnp.minimum(pos, w).astype(jnp.float32)
                    pooled = win / cnt - e0[HALO:, :]
                else:
                    pooled = win * (1.0 / w) - e0[HALO:, :]
                pbuf[rows, cols] = pooled.astype(jnp.bfloat16)

        def group_matmul(gi, gate):
            cols = slice(gi * POOL_GROUP_DIM, (gi + 1) * POOL_GROUP_DIM)
            ymix[:, cols] = jnp.dot(pbuf[:, cols], pool_w_ref[gi], preferred_element_type=jnp.float32)

        def post_mix_rows(rows, gate):
            y = _gated(ymix[rows, :], gate)
            h1 = x_ref[rows, :] + _rms(y * _gain(gains_ref, G_POOL_SCALE), _gain(gains_ref, G_MIX_POST))
            hs1[slot, rows, :] = h1
            hb[slot, rows, :] = _rms(h1, _gain(gains_ref, G_FFN_PRE)).astype(jnp.bfloat16)

        mixer = ([halo_rows] + _row_items(tile_rows, ROW_CHUNK, norm_rows)
                 + _row_items(tile_rows, POOL_ROW_CHUNK, pool_rows)
                 + [functools.partial(group_matmul, gi) for gi in range(len(POOL_WINDOWS))]
                 + _row_items(tile_rows, ROW_CHUNK, post_mix_rows))

        def ffn(side_items):
            _swiglu(hb.at[slot], wgu_ref, wd_ref, gbuf, ubuf, act, yffn.at[slot], tile_rows, side_items)

        def post_ffn_rows(rows, gate):
            h2 = hs1[slot, rows, :] + _rms(_gated(yffn[slot, rows, :], gate), _gain(gains_ref, G_FFN_POST))
            hs_out_ref[rows, :] = h2
            kvin[rows, :] = _rms(h2, _gain(gains_ref, G_KV)).astype(jnp.bfloat16)

        def kv_projection(gate):
            kv = jnp.dot(kvin[...], wkv_ref[...], preferred_element_type=jnp.float32) + bkv_ref[...]
            kv_out_ref[...] = kv.astype(jnp.bfloat16)

        epilogue = _row_items(tile_rows, ROW_CHUNK, post_ffn_rows) + [kv_projection]
        return mixer, ffn, epilogue

    _run_pipelined(step, n_tiles, (hs1, hb, yffn), stages)


def _const_spec(shape):
    nd = len(shape)
    return pl.BlockSpec(shape, lambda *_: (0,) * nd, pipeline_mode=pl.Buffered(1))


def _layer_spec(stacked_shape, layer):
    _, rows, cols = stacked_shape
    return pl.BlockSpec((None, rows, cols), lambda *_: (layer, 0, 0), pipeline_mode=pl.Buffered(1))


def _layer_a_call(x2d, meta, gains, pool_w, wgu, wd, wkv, bkv, *, layer, tm, tiles_per_batch, is_meta):
    n_rows = x2d.shape[0]
    n_tiles = n_rows // tm
    n_steps = n_tiles if n_tiles == 1 else n_tiles + PIPELINE_DEPTH - 1
    halo_blocks_per_tile = tm // HALO
    kern = functools.partial(_layer_a_kernel, tm=tm, tiles_per_batch=tiles_per_batch, n_tiles=n_tiles,
                             is_meta=is_meta)

    def mixer_tile(s):
        return jnp.minimum(s, n_tiles - 1)

    def epilogue_tile(s):
        return jnp.maximum(s - (n_steps - n_tiles), 0)

    return pl.pallas_call(
        kern,
        grid=(n_steps,),
        in_specs=[
            pl.BlockSpec((tm, D_MODEL), lambda s: (mixer_tile(s), 0)),
            pl.BlockSpec((HALO, D_MODEL), lambda s: (jnp.maximum(mixer_tile(s) * halo_blocks_per_tile - 1, 0), 0)),
            _const_spec((N_META, D_MODEL)),
            _const_spec(gains.shape),
            _const_spec(pool_w.shape),
            _layer_spec(wgu.shape, layer),
            _layer_spec(wd.shape, layer),
            _const_spec(wkv.shape),
            _const_spec(bkv.shape),
        ],
        out_specs=[
            pl.BlockSpec((tm, D_MODEL), lambda s: (epilogue_tile(s), 0)),
            pl.BlockSpec((tm, 2 * KV_DIM), lambda s: (epilogue_tile(s), 0)),
        ],
        out_shape=[
            jax.ShapeDtypeStruct((n_rows, D_MODEL), jnp.float32),
            jax.ShapeDtypeStruct((n_rows, 2 * KV_DIM), jnp.bfloat16),
        ],
        scratch_shapes=[
            pltpu.VMEM((tm + HALO, D_MODEL), jnp.float32),
            pltpu.VMEM((tm, D_MODEL), jnp.bfloat16),
            pltpu.VMEM((tm, D_MODEL), jnp.float32),
            pltpu.VMEM((2, tm, D_MODEL), jnp.float32),
            pltpu.VMEM((2, tm, D_MODEL), jnp.bfloat16),
            pltpu.VMEM((2, tm, D_MODEL), jnp.float32),
            pltpu.VMEM((tm, FFN_COLS), jnp.float32),
            pltpu.VMEM((tm, FFN_COLS), jnp.float32),
            pltpu.VMEM((tm, D_FF), jnp.bfloat16),
            pltpu.VMEM((tm, D_MODEL), jnp.bfloat16),
        ],
        compiler_params=pltpu.CompilerParams(
            dimension_semantics=("arbitrary",), vmem_limit_bytes=V7X_VMEM_LIMIT_BYTES),
        name="layer_a_meta" if is_meta else "layer_a",
    )(x2d, x2d, meta, gains, pool_w, wgu, wd, wkv, bkv)


def _bucket_distance_ranges():
    d = np.arange(WINDOW)
    max_exact = N_BUCKETS // 2
    df = np.maximum(d, 1).astype(np.float32)
    large = max_exact + (np.log(df / np.float32(max_exact)) / np.float32(math.log(MAX_DISTANCE / max_exact))
                         * np.float32(N_BUCKETS - max_exact)).astype(np.int32)
    bucket = np.where(d < max_exact, d, np.minimum(large, N_BUCKETS - 1))
    ranges = []
    for b in range(N_BUCKETS):
        members = d[bucket == b]
        if members.size:
            assert np.array_equal(members, np.arange(members[0], members[-1] + 1))
            ranges.append((int(members[0]), int(members[-1])))
        else:
            ranges.append(None)
    return ranges


def _bias_table_kernel(rel_bias_ref, out_ref):
    m = pl.program_id(0)
    q = lax.broadcasted_iota(jnp.int32, (BLOCK, 2 * BLOCK), 0)
    s = lax.broadcasted_iota(jnp.int32, (BLOCK, 2 * BLOCK), 1)
    d = q + BLOCK - s
    in_window = (d >= 0) & (d < WINDOW)
    valid = (in_window & (s >= PAD_FRONT), in_window)
    for half in range(2):
        h = 2 * m + half
        acc = jnp.zeros((BLOCK, 2 * BLOCK), jnp.float32)
        for b, distances in enumerate(_bucket_distance_ranges()):
            if distances is not None:
                lo, hi = distances
                acc = jnp.where((d >= lo) & (d <= hi), rel_bias_ref[b, h], acc)
        for jsel in range(2):
            out_ref[jsel, 0, :, half * 2 * BLOCK:(half + 1) * 2 * BLOCK] = jnp.where(valid[jsel], acc, -jnp.inf)


def _bias_table_call(rel_bias):
    return pl.pallas_call(
        _bias_table_kernel,
        grid=(N_HEAD_PAIRS,),
        in_specs=[
            pl.BlockSpec(memory_space=pltpu.SMEM),
        ],
        out_specs=pl.BlockSpec((2, 1, BLOCK, 4 * BLOCK), lambda m: (0, m, 0, 0)),
        out_shape=jax.ShapeDtypeStruct((2, N_HEAD_PAIRS, BLOCK, 4 * BLOCK), jnp.float32),
        compiler_params=pltpu.CompilerParams(dimension_semantics=("arbitrary",)),
        name="bias_table",
    )(rel_bias)


def _layer_b_kernel(hs_ref, kv_ref, metakv_ref, rowsum_cols_ref, bias_ref, sinks_ref, gains_ref,
                    wq_ref, bq_ref, wo_ref, bo_ref, wgu_ref, wd_ref,
                    out_ref,
                    hn, qbuf, obuf, yatt, hs3, hb, yffn, gbuf, ubuf, act, lbuf, mbuf, pbuf,
                    *, tm, tiles_per_batch, n_tiles):
    blocks_per_tile = tm // BLOCK
    step = pl.program_id(0)
    pair = jnp.minimum(step, n_tiles // 2 - 1)
    tile_rows = pl.ds(0, tm)

    lane = lax.broadcasted_iota(jnp.int32, (2 * BLOCK, 2 * HEAD_DIM), 1)
    low_half = lane < HEAD_DIM
    zero = jnp.zeros((2 * BLOCK, 2 * HEAD_DIM), jnp.bfloat16)
    low_half_q = lax.broadcasted_iota(jnp.int32, (BLOCK, 2 * HEAD_DIM), 1) < HEAD_DIM

    def per_kv_head_operands(pair_cols):
        swapped = jnp.concatenate([pair_cols[:, HEAD_DIM:], pair_cols[:, :HEAD_DIM]], axis=1)
        g0 = (jnp.where(low_half, pair_cols, zero), jnp.where(low_half, zero, swapped))
        g1 = (jnp.where(low_half, swapped, zero), jnp.where(low_half, zero, pair_cols))
        return g0, g1

    ones_top = rowsum_cols_ref[0:2 * BLOCK, :]
    ones_bottom = rowsum_cols_ref[2 * BLOCK:4 * BLOCK, :]

    def attention_block(slot, jb, gate):
        t = (2 * pair + slot) % tiles_per_batch
        j = t * blocks_per_tile + jb
        own = kv_ref[pl.ds(pl.multiple_of(j * BLOCK, BLOCK), BLOCK), :]
        prev_start = pl.multiple_of(jnp.maximum(j - 1, 0) * BLOCK, BLOCK)
        prev = jnp.where(j == 0, metakv_ref[...], kv_ref[pl.ds(prev_start, BLOCK), :])
        kvb = jnp.concatenate([prev, own], axis=0)
        k_ops = per_kv_head_operands(kvb[:, :KV_DIM])
        v_ops = per_kv_head_operands(kvb[:, KV_DIM:])
        jsel = jnp.minimum(j, 1)
        qrows = slice(jb * BLOCK, (jb + 1) * BLOCK)
        for g in range(N_KV_HEADS):
            u = jb * N_KV_HEADS + g
            pairs = range(g * PAIRS_PER_KV, (g + 1) * PAIRS_PER_KV)
            q4 = jnp.concatenate([qbuf[qrows, m * 2 * HEAD_DIM:(m + 1) * 2 * HEAD_DIM] for m in pairs], axis=0)
            k_op = jnp.concatenate(k_ops[g], axis=0)
            v_op = jnp.concatenate([jnp.concatenate([v_ops[g][0], ones_top], axis=1),
                                    jnp.concatenate([v_ops[g][1], ones_bottom], axis=1)], axis=0)
            logits = _gated(lax.dot_general(q4, k_op, (((1,), (1,)), ((), ())),
                                            preferred_element_type=jnp.float32) + bias_ref[jsel, g], gate)
            lbuf[u] = logits
            for r0 in range(0, ATT_ROWS, SOFTMAX_ROW_CHUNK):
                rows = slice(r0, r0 + SOFTMAX_ROW_CHUNK)
                for half in range(2):
                    sink = sinks_ref[2 * (pairs[0] + r0 // BLOCK) + half]
                    lg = logits[rows, half * 2 * BLOCK:(half + 1) * 2 * BLOCK]
                    mx = jnp.maximum(jnp.max(lg, axis=-1, keepdims=True), sink)
                    mbuf[u, rows, half * BLOCK:(half + 1) * BLOCK] = jnp.broadcast_to(mx, (SOFTMAX_ROW_CHUNK, BLOCK))
            for r0 in range(0, ATT_ROWS, SOFTMAX_ROW_CHUNK):
                rows = slice(r0, r0 + SOFTMAX_ROW_CHUNK)
                for half in range(2):
                    mrep = mbuf[u, rows, half * BLOCK:(half + 1) * BLOCK]
                    lg = lbuf[u, rows, half * 2 * BLOCK:(half + 1) * 2 * BLOCK]
                    p = jnp.exp(lg - jnp.concatenate([mrep, mrep], axis=1))
                    pbuf[u, rows, half * 2 * BLOCK:(half + 1) * 2 * BLOCK] = p.astype(jnp.bfloat16)
            oa = jnp.dot(pbuf[u], v_op, preferred_element_type=jnp.float32)
            for i, m in enumerate(pairs):
                rows = slice(i * BLOCK, (i + 1) * BLOCK)
                sink_gap = jnp.where(low_half_q, sinks_ref[2 * m] - mbuf[u, rows, 0:BLOCK],
                                     sinks_ref[2 * m + 1] - mbuf[u, rows, BLOCK:2 * BLOCK])
                denom = oa[rows, 2 * HEAD_DIM:] + jnp.exp(sink_gap)
                o = oa[rows, :2 * HEAD_DIM] * (1.0 / denom)
                obuf[qrows, m * 2 * HEAD_DIM:(m + 1) * 2 * HEAD_DIM] = o.astype(jnp.bfloat16)

    def stages(slot):
        def io(rows):
            return pl.ds(rows.start + slot * tm, rows.size)

        def norm_rows(rows, gate):
            h = _rms(_gated(hs_ref[io(rows), :], gate), _gain(gains_ref, G_MIX_PRE))
            hn[rows, :] = h.astype(jnp.bfloat16)

        def q_projection(gate):
            q = jnp.dot(hn[...], wq_ref[...], preferred_element_type=jnp.float32) + bq_ref[...]
            qbuf[...] = (q * (HEAD_DIM ** -0.5)).astype(jnp.bfloat16)

        def o_projection(gate):
            yatt[...] = jnp.dot(obuf[...], wo_ref[...], preferred_element_type=jnp.float32) + bo_ref[...]

        def post_mix_rows(rows, gate):
            h3 = hs_ref[io(rows), :] + _rms(_gated(yatt[rows, :], gate), _gain(gains_ref, G_MIX_POST))
            hs3[slot, rows, :] = h3
            hb[slot, rows, :] = _rms(h3, _gain(gains_ref, G_FFN_PRE)).astype(jnp.bfloat16)

        attention = (_row_items(tile_rows, ROW_CHUNK, norm_rows) + [q_projection]
                     + [functools.partial(attention_block, slot, jb) for jb in range(blocks_per_tile)]
                     + [o_projection] + _row_items(tile_rows, ROW_CHUNK, post_mix_rows))

        def ffn(side_items):
            _swiglu(hb.at[slot], wgu_ref, wd_ref, gbuf, ubuf, act, yffn.at[slot], tile_rows, side_items)

        def post_ffn_rows(rows, gate):
            y = _gated(yffn[slot, rows, :], gate)
            out_ref[io(rows), :] = hs3[slot, rows, :] + _rms(y, _gain(gains_ref, G_FFN_POST))

        return attention, ffn, _row_items(tile_rows, ROW_CHUNK, post_ffn_rows)

    _run_pipelined_pairs(step, n_tiles // 2, stages)


def _layer_b_call(hs, kv, metakv, bias, sinks, gains, wq, bq, wo, bo, wgu, wd, *, layer, tm):
    batch, seq, _ = hs.shape
    tiles_per_batch = seq // tm
    n_tiles = batch * tiles_per_batch
    n_att_units = (tm // BLOCK) * N_KV_HEADS
    lane_head = jnp.arange(2 * HEAD_DIM)[None, :] // HEAD_DIM
    row_head = jnp.arange(4 * BLOCK)[:, None] // (2 * BLOCK)
    rowsum_cols = (lane_head == row_head).astype(jnp.bfloat16)
    kern = functools.partial(_layer_b_kernel, tm=tm, tiles_per_batch=tiles_per_batch, n_tiles=n_tiles)

    n_pairs = n_tiles // 2
    pairs_per_batch = tiles_per_batch // 2

    def att_pair(s):
        pair = jnp.minimum(s, n_pairs - 1)
        return pair // pairs_per_batch, pair % pairs_per_batch

    def epilogue_pair(s):
        pair = jnp.maximum(s - 1, 0)
        return pair // pairs_per_batch, pair % pairs_per_batch

    return pl.pallas_call(
        kern,
        grid=(n_pairs + 1,),
        in_specs=[
            pl.BlockSpec((None, 2 * tm, D_MODEL), lambda s: (*att_pair(s), 0)),
            pl.BlockSpec((None, seq, 2 * KV_DIM), lambda s: (att_pair(s)[0], 0, 0), pipeline_mode=pl.Buffered(1)),
            _const_spec(metakv.shape),
            _const_spec(rowsum_cols.shape),
            _const_spec(bias.shape),
            pl.BlockSpec(memory_space=pltpu.SMEM),
            _const_spec(gains.shape),
            _const_spec(wq.shape),
            _const_spec(bq.shape),
            _const_spec(wo.shape),
            _const_spec(bo.shape),
            _layer_spec(wgu.shape, layer),
            _layer_spec(wd.shape, layer),
        ],
        out_specs=pl.BlockSpec((None, 2 * tm, D_MODEL), lambda s: (*epilogue_pair(s), 0)),
        out_shape=jax.ShapeDtypeStruct((batch, seq, D_MODEL), jnp.float32),
        scratch_shapes=[
            pltpu.VMEM((tm, D_MODEL), jnp.bfloat16),
            pltpu.VMEM((tm, D_MODEL), jnp.bfloat16),
            pltpu.VMEM((tm, D_MODEL), jnp.bfloat16),
            pltpu.VMEM((tm, D_MODEL), jnp.float32),
            pltpu.VMEM((2, tm, D_MODEL), jnp.float32),
            pltpu.VMEM((2, tm, D_MODEL), jnp.bfloat16),
            pltpu.VMEM((2, tm, D_MODEL), jnp.float32),
            pltpu.VMEM((tm, FFN_COLS), jnp.float32),
            pltpu.VMEM((tm, FFN_COLS), jnp.float32),
            pltpu.VMEM((tm, D_FF), jnp.bfloat16),
            pltpu.VMEM((n_att_units, ATT_ROWS, 4 * BLOCK), jnp.float32),
            pltpu.VMEM((n_att_units, ATT_ROWS, 2 * BLOCK), jnp.float32),
            pltpu.VMEM((n_att_units, ATT_ROWS, 4 * BLOCK), jnp.bfloat16),
        ],
        compiler_params=pltpu.CompilerParams(
            dimension_semantics=("arbitrary",), vmem_limit_bytes=V7X_VMEM_LIMIT_BYTES),
        name="layer_b",
    )(hs, kv, metakv, rowsum_cols, bias, sinks, gains, wq, bq, wo, bo, wgu, wd)


def _cast_kernel(w_ref, out_ref):
    out_ref[...] = w_ref[...].astype(jnp.bfloat16)


def _cast_weights(w, body, name):
    layers, rows, cols = w.shape
    w2d = w.reshape(layers * rows, cols)
    target = max(16, CAST_BLOCK_BYTES // (4 * cols) // 16 * 16)
    block_rows = next(r for r in range(target, 0, -16) if (layers * rows) % r == 0)
    spec = pl.BlockSpec((block_rows, cols), lambda i: (i, 0))
    out = pl.pallas_call(
        body,
        grid=(layers * rows // block_rows,),
        in_specs=[spec],
        out_specs=spec,
        out_shape=jax.ShapeDtypeStruct(w2d.shape, jnp.bfloat16),
        compiler_params=pltpu.CompilerParams(dimension_semantics=("arbitrary",)),
        name=name,
    )(w2d)
    return out.reshape(layers, rows, cols)


def _gain_table(rows):
    zero = jnp.zeros((D_MODEL,), jnp.float32)
    return jnp.stack([rows[i].astype(jnp.float32) if i in rows else zero for i in range(8)])


def kernel(x, meta_tokens, norm_mix_pre, norm_mix_post, norm_ffn_pre, norm_ffn_post, pool_w, pool_scale, kv_norm, w_k, b_k, w_v, b_v, w_q, b_q, w_o, b_o, sinks, rel_bias, w_gate_up, w_down):
    batch, seq, _ = x.shape
    bf16 = jnp.bfloat16

    gains_a = _gain_table({G_MIX_PRE: norm_mix_pre[0], G_MIX_POST: norm_mix_post[0], G_POOL_SCALE: pool_scale[0],
                           G_FFN_PRE: norm_ffn_pre[0], G_FFN_POST: norm_ffn_post[0], G_KV: kv_norm})
    gains_b = _gain_table({G_MIX_PRE: norm_mix_pre[1], G_MIX_POST: norm_mix_post[1],
                           G_FFN_PRE: norm_ffn_pre[1], G_FFN_POST: norm_ffn_post[1]})
    wkv = jnp.concatenate([w_k, w_v], axis=1).astype(bf16)
    bkv = jnp.concatenate([b_k, b_v])[None, :]
    wgu = _cast_weights(w_gate_up, _cast_kernel, "cast_gate_up")
    wd = _cast_weights(w_down, _cast_kernel, "cast_down")
    layer_a_weights = (gains_a, pool_w[0].astype(bf16), wgu, wd, wkv, bkv)

    x2d = x.reshape(batch * seq, D_MODEL)
    hs2, kv_x = _layer_a_call(x2d, meta_tokens, *layer_a_weights,
                              layer=0, tm=TILE_A, tiles_per_batch=seq // TILE_A, is_meta=False)
    _, kv_meta = _layer_a_call(meta_tokens, meta_tokens, *layer_a_weights,
                               layer=0, tm=N_META, tiles_per_batch=1, is_meta=True)
    metakv = jnp.concatenate([jnp.zeros((PAD_FRONT, 2 * KV_DIM), bf16), kv_meta], axis=0)

    bias = _bias_table_call(rel_bias)
    bias = bias.reshape(2, N_KV_HEADS, ATT_ROWS, 4 * BLOCK)

    out = _layer_b_call(hs2.reshape(batch, seq, D_MODEL), kv_x.reshape(batch, seq, 2 * KV_DIM), metakv, bias,
                        sinks[0], gains_b, w_q[0].astype(bf16), b_q, w_o[0].astype(bf16), b_o,
                        wgu, wd, layer=1, tm=TILE_B)
    return out
```

```python
import functools
import math

import numpy as np

import jax
import jax.numpy as jnp
from jax import lax
from jax.experimental import pallas as pl
from jax.experimental.pallas import tpu as pltpu

D_MODEL = 1024
N_META = 16
POOL_WINDOWS = (2, 4, 8, 16)
POOL_GROUP_DIM = D_MODEL // len(POOL_WINDOWS)
HEAD_DIM = 64
N_Q_HEADS = D_MODEL // HEAD_DIM
N_KV_HEADS = 2
HEADS_PER_KV = N_Q_HEADS // N_KV_HEADS
PAIRS_PER_KV = HEADS_PER_KV // 2
N_HEAD_PAIRS = N_Q_HEADS // 2
KV_DIM = N_KV_HEADS * HEAD_DIM
WINDOW = 128
BLOCK = 128
N_BUCKETS = 32
MAX_DISTANCE = 128
D_FF = 2816
EPS = 1e-6
PAD_FRONT = (-N_META) % BLOCK

LANES = 128
GATE_ROWS = 8
FFN_COLS = 256
SILU_ROW_CHUNK = 32
UNRIDDEN_SEGMENTS = 1
CAST_BLOCK_BYTES = 6 * 1024 * 1024
HALO = 16
ROW_CHUNK = 32
POOL_ROW_CHUNK = 64
ATT_ROWS = PAIRS_PER_KV * BLOCK
SOFTMAX_ROW_CHUNK = 16
TILE_A = 512
PIPELINE_DEPTH = 3
TILE_B = 256
V7X_VMEM_LIMIT_BYTES = 63 * 1024 * 1024

G_MIX_PRE, G_MIX_POST, G_POOL_SCALE, G_FFN_PRE, G_FFN_POST, G_KV = range(6)


def _rms(x, g):
    ms = jnp.sum(x * x, axis=-1, keepdims=True) * (1.0 / D_MODEL)
    return x * lax.rsqrt(ms + EPS) * g


def _gain(gains_ref, i):
    return gains_ref[i:i + 1, :]


def _row_items(rows, chunk, body):
    chunk = min(chunk, rows.size)
    return [functools.partial(body, pl.ds(rows.start + i * chunk, chunk)) for i in range(rows.size // chunk)]


def _run_items(items, gate=None):
    for item in items:
        item(gate)


def _gate_of(value):
    bits = pltpu.bitcast(value[-GATE_ROWS:, -LANES:], jnp.uint32)
    return pltpu.bitcast((bits >> 16) >> 16, jnp.float32)


def _gated(x, gate):
    if gate is None:
        return x
    z = jnp.concatenate([gate] * (x.shape[0] // GATE_ROWS), axis=0)
    return jnp.concatenate([x[:, :LANES] + z, x[:, LANES:]], axis=1)


def _spread(items, n_bins):
    bins = [[] for _ in range(n_bins)]
    for k, item in enumerate(items):
        bins[k * n_bins // max(len(items), 1)].append(item)
    return bins


def _swiglu(h_ref, wgu_ref, wd_ref, gbuf, ubuf, act, out_ref, rows, side_items=()):
    segments = ([("gate_up", c) for c in range(0, D_FF, FFN_COLS)]
                + [("down", c) for c in range(0, D_MODEL, FFN_COLS)])
    side = _spread(list(side_items), len(segments) - UNRIDDEN_SEGMENTS) + [[]] * UNRIDDEN_SEGMENTS
    for (kind, c0), side_group in zip(segments, side):
        if kind == "gate_up":
            g = jnp.dot(h_ref[rows, :], wgu_ref[:, c0:c0 + FFN_COLS], preferred_element_type=jnp.float32)
            gbuf[rows, :] = g
            ubuf[rows, :] = jnp.dot(h_ref[rows, :], wgu_ref[:, D_FF + c0:D_FF + c0 + FFN_COLS],
                                    preferred_element_type=jnp.float32)
            gate_source = g

            def silu_rows(r, gate, c0=c0):
                half_g = 0.5 * gbuf[r, :]
                silu = half_g + half_g * jnp.tanh(half_g)
                act[r, c0:c0 + FFN_COLS] = (silu * ubuf[r, :]).astype(jnp.bfloat16)

            _run_items(_row_items(rows, SILU_ROW_CHUNK, silu_rows))
        else:
            y = jnp.dot(act[rows, :], wd_ref[:, c0:c0 + FFN_COLS], preferred_element_type=jnp.float32)
            out_ref[rows, c0:c0 + FFN_COLS] = y
            gate_source = y
        if side_group:
            _run_items(side_group, _gate_of(gate_source))


def _run_pipelined(step, n_tiles, carried, stages):
    if n_tiles == 1:
        mixer, ffn, epilogue = stages(0)
        _run_items(mixer)
        ffn(())
        _run_items(epilogue)
        return

    last_step = n_tiles + 1

    @pl.when(step == 0)
    def _():
        for ref in carried:
            ref[...] = jnp.zeros(ref.shape, ref.dtype)
        _run_items(stages(0)[0])

    for parity in range(2):
        @pl.when((step % 2 == parity) & (step > 0) & (step < last_step))
        def _(parity=parity):
            mixer, _, epilogue = stages(parity)
            _, ffn, _ = stages(1 - parity)
            ffn(epilogue + mixer)

    @pl.when(step == last_step)
    def _():
        _run_items(stages(last_step % 2)[2])


def _run_pipelined_pairs(step, n_pairs, stages):
    even_mixer, even_ffn, even_epilogue = stages(0)
    odd_mixer, odd_ffn, odd_epilogue = stages(1)

    @pl.when(step == 0)
    def _():
        _run_items(even_mixer)
        even_ffn(odd_mixer)

    @pl.when((step > 0) & (step < n_pairs))
    def _():
        odd_ffn(even_epilogue + even_mixer)
        even_ffn(odd_epilogue + odd_mixer)

    @pl.when(step == n_pairs)
    def _():
        odd_ffn(even_epilogue)
        _run_items(odd_epilogue)


def _layer_a_kernel(x_ref, prev_ref, meta_ref, gains_ref, pool_w_ref, wgu_ref, wd_ref, wkv_ref, bkv_ref,
                    hs_out_ref, kv_out_ref,
                    hext, pbuf, ymix, hs1, hb, yffn, gbuf, ubuf, act, kvin,
                    *, tm, tiles_per_batch, n_tiles, is_meta):
    g_mix_pre = _gain(gains_ref, G_MIX_PRE)
    step = pl.program_id(0)
    tile_rows = pl.ds(0, tm)

    def stages(slot):
        def halo_rows(gate):
            if is_meta:
                hext[0:HALO, :] = jnp.zeros((HALO, D_MODEL), jnp.float32)
            else:
                first_of_batch = (jnp.minimum(step, n_tiles - 1) % tiles_per_batch) == 0
                prev = jnp.where(first_of_batch, meta_ref[...], prev_ref[...])
                hext[0:HALO, :] = _rms(prev, g_mix_pre)

        def norm_rows(rows, gate):
            hext[pl.ds(rows.start + HALO, rows.size), :] = _rms(_gated(x_ref[rows, :], gate), g_mix_pre)

        def pool_rows(rows, gate):
            for gi, w in enumerate(POOL_WINDOWS):
                cols = slice(gi * POOL_GROUP_DIM, (gi + 1) * POOL_GROUP_DIM)
                e0 = _gated(hext[pl.ds(rows.start, rows.size + HALO), cols], gate)
                e = e0
                shift = 1
                while shift < w:
                    e = e + pltpu.roll(e, shift, 0)
                    shift *= 2
                win = e[HALO:, :]
                if is_meta:
                    pos = lax.broadcasted_iota(jnp.int32, win.shape, 0) + 1
                    cnt = jnp.minimum(pos, w).astype(jnp.float32)
                    pooled = win / cnt - e0[HALO:, :]
                else:
                    pooled = win * (1.0 / w) - e0[HALO:, :]
                pbuf[rows, cols] = pooled.astype(jnp.bfloat16)

        def group_matmul(gi, gate):
            cols = slice(gi * POOL_GROUP_DIM, (gi + 1) * POOL_GROUP_DIM)
            ymix[:, cols] = jnp.dot(pbuf[:, cols], pool_w_ref[gi], preferred_element_type=jnp.float32)

        def post_mix_rows(rows, gate):
            y = _gated(ymix[rows, :], gate)
            h1 = x_ref[rows, :] + _rms(y * _gain(gains_ref, G_POOL_SCALE), _gain(gains_ref, G_MIX_POST))
            hs1[slot, rows, :] = h1
            hb[slot, rows, :] = _rms(h1, _gain(gains_ref, G_FFN_PRE)).astype(jnp.bfloat16)

        mixer = ([halo_rows] + _row_items(tile_rows, ROW_CHUNK, norm_rows)
                 + _row_items(tile_rows, POOL_ROW_CHUNK, pool_rows)
                 + [functools.partial(group_matmul, gi) for gi in range(len(POOL_WINDOWS))]
                 + _row_items(tile_rows, ROW_CHUNK, post_mix_rows))

        def ffn(side_items):
            _swiglu(hb.at[slot], wgu_ref, wd_ref, gbuf, ubuf, act, yffn.at[slot], tile_rows, side_items)

        def post_ffn_rows(rows, gate):
            h2 = hs1[slot, rows, :] + _rms(_gated(yffn[slot, rows, :], gate), _gain(gains_ref, G_FFN_POST))
            hs_out_ref[rows, :] = h2
            kvin[rows, :] = _rms(h2, _gain(gains_ref, G_KV)).astype(jnp.bfloat16)

        def kv_projection(gate):
            kv = jnp.dot(kvin[...], wkv_ref[...], preferred_element_type=jnp.float32) + bkv_ref[...]
            kv_out_ref[...] = kv.astype(jnp.bfloat16)

        epilogue = _row_items(tile_rows, ROW_CHUNK, post_ffn_rows) + [kv_projection]
        return mixer, ffn, epilogue

    _run_pipelined(step, n_tiles, (hs1, hb, yffn), stages)


def _const_spec(shape):
    nd = len(shape)
    return pl.BlockSpec(shape, lambda *_: (0,) * nd, pipeline_mode=pl.Buffered(1))


def _layer_spec(stacked_shape, layer):
    _, rows, cols = stacked_shape
    return pl.BlockSpec((None, rows, cols), lambda *_: (layer, 0, 0), pipeline_mode=pl.Buffered(1))


def _layer_a_call(x2d, meta, gains, pool_w, wgu, wd, wkv, bkv, *, layer, tm, tiles_per_batch, is_meta):
    n_rows = x2d.shape[0]
    n_tiles = n_rows // tm
    n_steps = n_tiles if n_tiles == 1 else n_tiles + PIPELINE_DEPTH - 1
    halo_blocks_per_tile = tm // HALO
    kern = functools.partial(_layer_a_kernel, tm=tm, tiles_per_batch=tiles_per_batch, n_tiles=n_tiles,
                             is_meta=is_meta)

    def mixer_tile(s):
        return jnp.minimum(s, n_tiles - 1)

    def epilogue_tile(s):
        return jnp.maximum(s - (n_steps - n_tiles), 0)

    return pl.pallas_call(
        kern,
        grid=(n_steps,),
        in_specs=[
            pl.BlockSpec((tm, D_MODEL), lambda s: (mixer_tile(s), 0)),
            pl.BlockSpec((HALO, D_MODEL), lambda s: (jnp.maximum(mixer_tile(s) * halo_blocks_per_tile - 1, 0), 0)),
            _const_spec((N_META, D_MODEL)),
            _const_spec(gains.shape),
            _const_spec(pool_w.shape),
            _layer_spec(wgu.shape, layer),
            _layer_spec(wd.shape, layer),
            _const_spec(wkv.shape),
            _const_spec(bkv.shape),
        ],
        out_specs=[
            pl.BlockSpec((tm, D_MODEL), lambda s: (epilogue_tile(s), 0)),
            pl.BlockSpec((tm, 2 * KV_DIM), lambda s: (epilogue_tile(s), 0)),
        ],
        out_shape=[
            jax.ShapeDtypeStruct((n_rows, D_MODEL), jnp.float32),
            jax.ShapeDtypeStruct((n_rows, 2 * KV_DIM), jnp.bfloat16),
        ],
        scratch_shapes=[
            pltpu.VMEM((tm + HALO, D_MODEL), jnp.float32),
            pltpu.VMEM((tm, D_MODEL), jnp.bfloat16),
            pltpu.VMEM((tm, D_MODEL), jnp.float32),
            pltpu.VMEM((2, tm, D_MODEL), jnp.float32),
            pltpu.VMEM((2, tm, D_MODEL), jnp.bfloat16),
            pltpu.VMEM((2, tm, D_MODEL), jnp.float32),
            pltpu.VMEM((tm, FFN_COLS), jnp.float32),
            pltpu.VMEM((tm, FFN_COLS), jnp.float32),
            pltpu.VMEM((tm, D_FF), jnp.bfloat16),
            pltpu.VMEM((tm, D_MODEL), jnp.bfloat16),
        ],
        compiler_params=pltpu.CompilerParams(
            dimension_semantics=("arbitrary",), vmem_limit_bytes=V7X_VMEM_LIMIT_BYTES),
        name="layer_a_meta" if is_meta else "layer_a",
    )(x2d, x2d, meta, gains, pool_w, wgu, wd, wkv, bkv)


def _bucket_distance_ranges():
    d = np.arange(WINDOW)
    max_exact = N_BUCKETS // 2
    df = np.maximum(d, 1).astype(np.float32)
    large = max_exact + (np.log(df / np.float32(max_exact)) / np.float32(math.log(MAX_DISTANCE / max_exact))
                         * np.float32(N_BUCKETS - max_exact)).astype(np.int32)
    bucket = np.where(d < max_exact, d, np.minimum(large, N_BUCKETS - 1))
    ranges = []
    for b in range(N_BUCKETS):
        members = d[bucket == b]
        if members.size:
            assert np.array_equal(members, np.arange(members[0], members[-1] + 1))
            ranges.append((int(members[0]), int(members[-1])))
        else:
            ranges.append(None)
    return ranges


def _bias_table_kernel(rel_bias_ref, out_ref):
    m = pl.program_id(0)
    q = lax.broadcasted_iota(jnp.int32, (BLOCK, 2 * BLOCK), 0)
    s = lax.broadcasted_iota(jnp.int32, (BLOCK, 2 * BLOCK), 1)
    d = q + BLOCK - s
    in_window = (d >= 0) & (d < WINDOW)
    valid = (in_window & (s >= PAD_FRONT), in_window)
    for half in range(2):
        h = 2 * m + half
        acc = jnp.zeros((BLOCK, 2 * BLOCK), jnp.float32)
        for b, distances in enumerate(_bucket_distance_ranges()):
            if distances is not None:
                lo, hi = distances
                acc = jnp.where((d >= lo) & (d <= hi), rel_bias_ref[b, h], acc)
        for jsel in range(2):
            out_ref[jsel, 0, :, half * 2 * BLOCK:(half + 1) * 2 * BLOCK] = jnp.where(valid[jsel], acc, -jnp.inf)


def _bias_table_call(rel_bias):
    return pl.pallas_call(
        _bias_table_kernel,
        grid=(N_HEAD_PAIRS,),
        in_specs=[
            pl.BlockSpec(memory_space=pltpu.SMEM),
        ],
        out_specs=pl.BlockSpec((2, 1, BLOCK, 4 * BLOCK), lambda m: (0, m, 0, 0)),
        out_shape=jax.ShapeDtypeStruct((2, N_HEAD_PAIRS, BLOCK, 4 * BLOCK), jnp.float32),
        compiler_params=pltpu.CompilerParams(dimension_semantics=("arbitrary",)),
        name="bias_table",
    )(rel_bias)


def _layer_b_kernel(hs_ref, kv_ref, metakv_ref, rowsum_cols_ref, bias_ref, sinks_ref, gains_ref,
                    wq_ref, bq_ref, wo_ref, bo_ref, wgu_ref, wd_ref,
                    out_ref,
                    hn, qbuf, obuf, yatt, hs3, hb, yffn, gbuf, ubuf, act, lbuf, mbuf, pbuf,
                    *, tm, tiles_per_batch, n_tiles):
    blocks_per_tile = tm // BLOCK
    step = pl.program_id(0)
    pair = jnp.minimum(step, n_tiles // 2 - 1)
    tile_rows = pl.ds(0, tm)

    lane = lax.broadcasted_iota(jnp.int32, (2 * BLOCK, 2 * HEAD_DIM), 1)
    low_half = lane < HEAD_DIM
    zero = jnp.zeros((2 * BLOCK, 2 * HEAD_DIM), jnp.bfloat16)
    low_half_q = lax.broadcasted_iota(jnp.int32, (BLOCK, 2 * HEAD_DIM), 1) < HEAD_DIM

    def per_kv_head_operands(pair_cols):
        swapped = jnp.concatenate([pair_cols[:, HEAD_DIM:], pair_cols[:, :HEAD_DIM]], axis=1)
        g0 = (jnp.where(low_half, pair_cols, zero), jnp.where(low_half, zero, swapped))
        g1 = (jnp.where(low_half, swapped, zero), jnp.where(low_half, zero, pair_cols))
        return g0, g1

    ones_top = rowsum_cols_ref[0:2 * BLOCK, :]
    ones_bottom = rowsum_cols_ref[2 * BLOCK:4 * BLOCK, :]

    def attention_block(slot, jb, gate):
        t = (2 * pair + slot) % tiles_per_batch
        j = t * blocks_per_tile + jb
        own = kv_ref[pl.ds(pl.multiple_of(j * BLOCK, BLOCK), BLOCK), :]
        prev_start = pl.multiple_of(jnp.maximum(j - 1, 0) * BLOCK, BLOCK)
        prev = jnp.where(j == 0, metakv_ref[...], kv_ref[pl.ds(prev_start, BLOCK), :])
        kvb = jnp.concatenate([prev, own], axis=0)
        k_ops = per_kv_head_operands(kvb[:, :KV_DIM])
        v_ops = per_kv_head_operands(kvb[:, KV_DIM:])
        jsel = jnp.minimum(j, 1)
        qrows = slice(jb * BLOCK, (jb + 1) * BLOCK)
        for g in range(N_KV_HEADS):
            u = jb * N_KV_HEADS + g
            pairs = range(g * PAIRS_PER_KV, (g + 1) * PAIRS_PER_KV)
            q4 = jnp.concatenate([qbuf[qrows, m * 2 * HEAD_DIM:(m + 1) * 2 * HEAD_DIM] for m in pairs], axis=0)
            k_op = jnp.concatenate(k_ops[g], axis=0)
            v_op = jnp.concatenate([jnp.concatenate([v_ops[g][0], ones_top], axis=1),
                                    jnp.concatenate([v_ops[g][1], ones_bottom], axis=1)], axis=0)
            logits = _gated(lax.dot_general(q4, k_op, (((1,), (1,)), ((), ())),
                                            preferred_element_type=jnp.float32) + bias_ref[jsel, g], gate)
            lbuf[u] = logits
            for r0 in range(0, ATT_ROWS, SOFTMAX_ROW_CHUNK):
                rows = slice(r0, r0 + SOFTMAX_ROW_CHUNK)
                for half in range(2):
                    sink = sinks_ref[2 * (pairs[0] + r0 // BLOCK) + half]
                    lg = logits[rows, half * 2 * BLOCK:(half + 1) * 2 * BLOCK]
                    mx = jnp.maximum(jnp.max(lg, axis=-1, keepdims=True), sink)
                    mbuf[u, rows, half * BLOCK:(half + 1) * BLOCK] = jnp.broadcast_to(mx, (SOFTMAX_ROW_CHUNK, BLOCK))
            for r0 in range(0, ATT_ROWS, SOFTMAX_ROW_CHUNK):
                rows = slice(r0, r0 + SOFTMAX_ROW_CHUNK)
                for half in range(2):
                    mrep = mbuf[u, rows, half * BLOCK:(half + 1) * BLOCK]
                    lg = lbuf[u, rows, half * 2 * BLOCK:(half + 1) * 2 * BLOCK]
                    p = jnp.exp(lg - jnp.concatenate([mrep, mrep], axis=1))
                    pbuf[u, rows, half * 2 * BLOCK:(half + 1) * 2 * BLOCK] = p.astype(jnp.bfloat16)
            oa = jnp.dot(pbuf[u], v_op, preferred_element_type=jnp.float32)
            for i, m in enumerate(pairs):
                rows = slice(i * BLOCK, (i + 1) * BLOCK)
                sink_gap = jnp.where(low_half_q, sinks_ref[2 * m] - mbuf[u, rows, 0:BLOCK],
                                     sinks_ref[2 * m + 1] - mbuf[u, rows, BLOCK:2 * BLOCK])
                denom = oa[rows, 2 * HEAD_DIM:] + jnp.exp(sink_gap)
                o = oa[rows, :2 * HEAD_DIM] * (1.0 / denom)
                obuf[qrows, m * 2 * HEAD_DIM:(m + 1) * 2 * HEAD_DIM] = o.astype(jnp.bfloat16)

    def stages(slot):
        def io(rows):
            return pl.ds(rows.start + slot * tm, rows.size)

        def norm_rows(rows, gate):
            h = _rms(_gated(hs_ref[io(rows), :], gate), _gain(gains_ref, G_MIX_PRE))
            hn[rows, :] = h.astype(jnp.bfloat16)

        def q_projection(gate):
            q = jnp.dot(hn[...], wq_ref[...], preferred_element_type=jnp.float32) + bq_ref[...]
            qbuf[...] = (q * (HEAD_DIM ** -0.5)).astype(jnp.bfloat16)

        def o_projection(gate):
            yatt[...] = jnp.dot(obuf[...], wo_ref[...], preferred_element_type=jnp.float32) + bo_ref[...]

        def post_mix_rows(rows, gate):
            h3 = hs_ref[io(rows), :] + _rms(_gated(yatt[rows, :], gate), _gain(gains_ref, G_MIX_POST))
            hs3[slot, rows, :] = h3
            hb[slot, rows, :] = _rms(h3, _gain(gains_ref, G_FFN_PRE)).astype(jnp.bfloat16)

        attention = (_row_items(tile_rows, ROW_CHUNK, norm_rows) + [q_projection]
                     + [functools.partial(attention_block, slot, jb) for jb in range(blocks_per_tile)]
                     + [o_projection] + _row_items(tile_rows, ROW_CHUNK, post_mix_rows))

        def ffn(side_items):
            _swiglu(hb.at[slot], wgu_ref, wd_ref, gbuf, ubuf, act, yffn.at[slot], tile_rows, side_items)

        def post_ffn_rows(rows, gate):
            y = _gated(yffn[slot, rows, :], gate)
            out_ref[io(rows), :] = hs3[slot, rows, :] + _rms(y, _gain(gains_ref, G_FFN_POST))

        return attention, ffn, _row_items(tile_rows, ROW_CHUNK, post_ffn_rows)

    _run_pipelined_pairs(step, n_tiles // 2, stages)


def _layer_b_call(hs, kv, metakv, bias, sinks, gains, wq, bq, wo, bo, wgu, wd, *, layer, tm):
    batch, seq, _ = hs.shape
    tiles_per_batch = seq // tm
    n_tiles = batch * tiles_per_batch
    n_att_units = (tm // BLOCK) * N_KV_HEADS
    lane_head = jnp.arange(2 * HEAD_DIM)[None, :] // HEAD_DIM
    row_head = jnp.arange(4 * BLOCK)[:, None] // (2 * BLOCK)
    rowsum_cols = (lane_head == row_head).astype(jnp.bfloat16)
    kern = functools.partial(_layer_b_kernel, tm=tm, tiles_per_batch=tiles_per_batch, n_tiles=n_tiles)

    n_pairs = n_tiles // 2
    pairs_per_batch = tiles_per_batch // 2

    def att_pair(s):
        pair = jnp.minimum(s, n_pairs - 1)
        return pair // pairs_per_batch, pair % pairs_per_batch

    def epilogue_pair(s):
        pair = jnp.maximum(s - 1, 0)
        return pair // pairs_per_batch, pair % pairs_per_batch

    return pl.pallas_call(
        kern,
        grid=(n_pairs + 1,),
        in_specs=[
            pl.BlockSpec((None, 2 * tm, D_MODEL), lambda s: (*att_pair(s), 0)),
            pl.BlockSpec((None, seq, 2 * KV_DIM), lambda s: (att_pair(s)[0], 0, 0), pipeline_mode=pl.Buffered(1)),
            _const_spec(metakv.shape),
            _const_spec(rowsum_cols.shape),
            _const_spec(bias.shape),
            pl.BlockSpec(memory_space=pltpu.SMEM),
            _const_spec(gains.shape),
            _const_spec(wq.shape),
            _const_spec(bq.shape),
            _const_spec(wo.shape),
            _const_spec(bo.shape),
            _layer_spec(wgu.shape, layer),
            _layer_spec(wd.shape, layer),
        ],
        out_specs=pl.BlockSpec((None, 2 * tm, D_MODEL), lambda s: (*epilogue_pair(s), 0)),
        out_shape=jax.ShapeDtypeStruct((batch, seq, D_MODEL), jnp.float32),
        scratch_shapes=[
            pltpu.VMEM((tm, D_MODEL), jnp.bfloat16),
            pltpu.VMEM((tm, D_MODEL), jnp.bfloat16),
            pltpu.VMEM((tm, D_MODEL), jnp.bfloat16),
            pltpu.VMEM((tm, D_MODEL), jnp.float32),
            pltpu.VMEM((2, tm, D_MODEL), jnp.float32),
            pltpu.VMEM((2, tm, D_MODEL), jnp.bfloat16),
            pltpu.VMEM((2, tm, D_MODEL), jnp.float32),
            pltpu.VMEM((tm, FFN_COLS), jnp.float32),
            pltpu.VMEM((tm, FFN_COLS), jnp.float32),
            pltpu.VMEM((tm, D_FF), jnp.bfloat16),
            pltpu.VMEM((n_att_units, ATT_ROWS, 4 * BLOCK), jnp.float32),
            pltpu.VMEM((n_att_units, ATT_ROWS, 2 * BLOCK), jnp.float32),
            pltpu.VMEM((n_att_units, ATT_ROWS, 4 * BLOCK), jnp.bfloat16),
        ],
        compiler_params=pltpu.CompilerParams(
            dimension_semantics=("arbitrary",), vmem_limit_bytes=V7X_VMEM_LIMIT_BYTES),
        name="layer_b",
    )(hs, kv, metakv, rowsum_cols, bias, sinks, gains, wq, bq, wo, bo, wgu, wd)


def _cast_kernel(w_ref, out_ref):
    out_ref[...] = w_ref[...].astype(jnp.bfloat16)


def _cast_weights(w, body, name):
    layers, rows, cols = w.shape
    w2d = w.reshape(layers * rows, cols)
    target = max(16, CAST_BLOCK_BYTES // (4 * cols) // 16 * 16)
    block_rows = next(r for r in range(target, 0, -16) if (layers * rows) % r == 0)
    spec = pl.BlockSpec((block_rows, cols), lambda i: (i, 0))
    out = pl.pallas_call(
        body,
        grid=(layers * rows // block_rows,),
        in_specs=[spec],
        out_specs=spec,
        out_shape=jax.ShapeDtypeStruct(w2d.shape, jnp.bfloat16),
        compiler_params=pltpu.CompilerParams(dimension_semantics=("arbitrary",)),
        name=name,
    )(w2d)
    return out.reshape(layers, rows, cols)


def _gain_table(rows):
    zero = jnp.zeros((D_MODEL,), jnp.float32)
    return jnp.stack([rows[i].astype(jnp.float32) if i in rows else zero for i in range(8)])


def kernel(x, meta_tokens, norm_mix_pre, norm_mix_post, norm_ffn_pre, norm_ffn_post, pool_w, pool_scale, kv_norm, w_k, b_k, w_v, b_v, w_q, b_q, w_o, b_o, sinks, rel_bias, w_gate_up, w_down):
    batch, seq, _ = x.shape
    bf16 = jnp.bfloat16

    gains_a = _gain_table({G_MIX_PRE: norm_mix_pre[0], G_MIX_POST: norm_mix_post[0], G_POOL_SCALE: pool_scale[0],
                           G_FFN_PRE: norm_ffn_pre[0], G_FFN_POST: norm_ffn_post[0], G_KV: kv_norm})
    gains_b = _gain_table({G_MIX_PRE: norm_mix_pre[1], G_MIX_POST: norm_mix_post[1],
                           G_FFN_PRE: norm_ffn_pre[1], G_FFN_POST: norm_ffn_post[1]})
    wkv = jnp.concatenate([w_k, w_v], axis=1).astype(bf16)
    bkv = jnp.concatenate([b_k, b_v])[None, :]
    wgu = _cast_weights(w_gate_up, _cast_kernel, "cast_gate_up")
    wd = _cast_weights(w_down, _cast_kernel, "cast_down")
    layer_a_weights = (gains_a, pool_w[0].astype(bf16), wgu, wd, wkv, bkv)

    x2d = x.reshape(batch * seq, D_MODEL)
    hs2, kv_x = _layer_a_call(x2d, meta_tokens, *layer_a_weights,
                              layer=0, tm=TILE_A, tiles_per_batch=seq // TILE_A, is_meta=False)
    _, kv_meta = _layer_a_call(meta_tokens, meta_tokens, *layer_a_weights,
                               layer=0, tm=N_META, tiles_per_batch=1, is_meta=True)
    metakv = jnp.concatenate([jnp.zeros((PAD_FRONT, 2 * KV_DIM), bf16), kv_meta], axis=0)

    bias = _bias_table_call(rel_bias)
    bias = bias.reshape(2, N_KV_HEADS, ATT_ROWS, 4 * BLOCK)

    out = _layer_b_call(hs2.reshape(batch, seq, D_MODEL), kv_x.reshape(batch, seq, 2 * KV_DIM), metakv, bias,
                        sinks[0], gains_b, w_q[0].astype(bf16), b_q, w_o[0].astype(bf16), b_o,
                        wgu, wd, layer=1, tm=TILE_B)
    return out
```

```python
import functools
import math

import numpy as np

import jax
import jax.numpy as jnp
from jax import lax
from jax.experimental import pallas as pl
from jax.experimental.pallas import tpu as pltpu

D_MODEL = 1024
N_META = 16
POOL_WINDOWS = (2, 4, 8, 16)
POOL_GROUP_DIM = D_MODEL // len(POOL_WINDOWS)
HEAD_DIM = 64
N_Q_HEADS = D_MODEL // HEAD_DIM
N_KV_HEADS = 2
HEADS_PER_KV = N_Q_HEADS // N_KV_HEADS
PAIRS_PER_KV = HEADS_PER_KV // 2
N_HEAD_PAIRS = N_Q_HEADS // 2
KV_DIM = N_KV_HEADS * HEAD_DIM
WINDOW = 128
BLOCK = 128
N_BUCKETS = 32
MAX_DISTANCE = 128
D_FF = 2816
EPS = 1e-6
PAD_FRONT = (-N_META) % BLOCK

LANES = 128
GATE_ROWS = 8
FFN_COLS = 256
SILU_ROW_CHUNK = 32
UNRIDDEN_SEGMENTS = 0
CAST_BLOCK_BYTES = 6 * 1024 * 1024
HALO = 16
ROW_CHUNK = 32
POOL_ROW_CHUNK = 64
ATT_ROWS = PAIRS_PER_KV * BLOCK
SOFTMAX_ROW_CHUNK = 16
TILE_A = 512
PIPELINE_DEPTH = 3
TILE_B = 256
V7X_VMEM_LIMIT_BYTES = 63 * 1024 * 1024

G_MIX_PRE, G_MIX_POST, G_POOL_SCALE, G_FFN_PRE, G_FFN_POST, G_KV = range(6)


def _rms(x, g):
    ms = jnp.sum(x * x, axis=-1, keepdims=True) * (1.0 / D_MODEL)
    return x * lax.rsqrt(ms + EPS) * g


def _gain(gains_ref, i):
    return gains_ref[i:i + 1, :]


def _row_items(rows, chunk, body):
    chunk = min(chunk, rows.size)
    return [functools.partial(body, pl.ds(rows.start + i * chunk, chunk)) for i in range(rows.size // chunk)]


def _run_items(items, gate=None):
    for item in items:
        item(gate)


def _gate_of(value):
    bits = pltpu.bitcast(value[-GATE_ROWS:, -LANES:], jnp.uint32)
    return pltpu.bitcast((bits >> 16) >> 16, jnp.float32)


def _gated(x, gate):
    if gate is None:
        return x
    z = jnp.concatenate([gate] * (x.shape[0] // GATE_ROWS), axis=0)
    return jnp.concatenate([x[:, :LANES] + z, x[:, LANES:]], axis=1)


def _spread(items, n_bins):
    bins = [[] for _ in range(n_bins)]
    for k, item in enumerate(items):
        bins[k * n_bins // max(len(items), 1)].append(item)
    return bins


def _swiglu(h_ref, wgu_ref, wd_ref, gbuf, ubuf, act, out_ref, rows, side_items=()):
    segments = ([("gate_up", c) for c in range(0, D_FF, FFN_COLS)]
                + [("down", c) for c in range(0, D_MODEL, FFN_COLS)])
    side = _spread(list(side_items), len(segments) - UNRIDDEN_SEGMENTS) + [[]] * UNRIDDEN_SEGMENTS
    for (kind, c0), side_group in zip(segments, side):
        if kind == "gate_up":
            g = jnp.dot(h_ref[rows, :], wgu_ref[:, c0:c0 + FFN_COLS], preferred_element_type=jnp.float32)
            gbuf[rows, :] = g
            ubuf[rows, :] = jnp.dot(h_ref[rows, :], wgu_ref[:, D_FF + c0:D_FF + c0 + FFN_COLS],
                                    preferred_element_type=jnp.float32)
            gate_source = g

            def silu_rows(r, gate, c0=c0):
                half_g = 0.5 * gbuf[r, :]
                silu = half_g + half_g * jnp.tanh(half_g)
                act[r, c0:c0 + FFN_COLS] = (silu * ubuf[r, :]).astype(jnp.bfloat16)

            _run_items(_row_items(rows, SILU_ROW_CHUNK, silu_rows))
        else:
            y = jnp.dot(act[rows, :], wd_ref[:, c0:c0 + FFN_COLS], preferred_element_type=jnp.float32)
            out_ref[rows, c0:c0 + FFN_COLS] = y
            gate_source = y
        if side_group:
            _run_items(side_group, _gate_of(gate_source))


def _run_pipelined(step, n_tiles, carried, stages):
    if n_tiles == 1:
        mixer, ffn, epilogue = stages(0)
        _run_items(mixer)
        ffn(())
        _run_items(epilogue)
        return

    last_step = n_tiles + 1

    @pl.when(step == 0)
    def _():
        for ref in carried:
            ref[...] = jnp.zeros(ref.shape, ref.dtype)
        _run_items(stages(0)[0])

    for parity in range(2):
        @pl.when((step % 2 == parity) & (step > 0) & (step < last_step))
        def _(parity=parity):
            mixer, _, epilogue = stages(parity)
            _, ffn, _ = stages(1 - parity)
            ffn(epilogue + mixer)

    @pl.when(step == last_step)
    def _():
        _run_items(stages(last_step % 2)[2])


def _run_pipelined_pairs(step, n_pairs, stages):
    even_mixer, even_ffn, even_epilogue = stages(0)
    odd_mixer, odd_ffn, odd_epilogue = stages(1)

    @pl.when(step == 0)
    def _():
        _run_items(even_mixer)
        even_ffn(odd_mixer)

    @pl.when((step > 0) & (step < n_pairs))
    def _():
        odd_ffn(even_epilogue + even_mixer)
        even_ffn(odd_epilogue + odd_mixer)

    @pl.when(step == n_pairs)
    def _():
        odd_ffn(even_epilogue)
        _run_items(odd_epilogue)


def _layer_a_kernel(x_ref, prev_ref, meta_ref, gains_ref, pool_w_ref, wgu_ref, wd_ref, wkv_ref, bkv_ref,
                    hs_out_ref, kv_out_ref,
                    hext, pbuf, ymix, hs1, hb, yffn, gbuf, ubuf, act, kvin,
                    *, tm, tiles_per_batch, n_tiles, is_meta):
    g_mix_pre = _gain(gains_ref, G_MIX_PRE)
    step = pl.program_id(0)
    tile_rows = pl.ds(0, tm)

    def stages(slot):
        def halo_rows(gate):
            if is_meta:
                hext[0:HALO, :] = jnp.zeros((HALO, D_MODEL), jnp.float32)
            else:
                first_of_batch = (jnp.minimum(step, n_tiles - 1) % tiles_per_batch) == 0
                prev = jnp.where(first_of_batch, meta_ref[...], prev_ref[...])
                hext[0:HALO, :] = _rms(prev, g_mix_pre)

        def norm_rows(rows, gate):
            hext[pl.ds(rows.start + HALO, rows.size), :] = _rms(_gated(x_ref[rows, :], gate), g_mix_pre)

        def pool_rows(rows, gate):
            for gi, w in enumerate(POOL_WINDOWS):
                cols = slice(gi * POOL_GROUP_DIM, (gi + 1) * POOL_GROUP_DIM)
                e0 = _gated(hext[pl.ds(rows.start, rows.size + HALO), cols], gate)
                e = e0
                shift = 1
                while shift < w:
                    e = e + pltpu.roll(e, shift, 0)
                    shift *= 2
                win = e[HALO:, :]
                if is_meta:
                    pos = lax.broadcasted_iota(jnp.int32, win.shape, 0) + 1
                    cnt = jnp.minimum(pos, w).astype(jnp.float32)
                    pooled = win / cnt - e0[HALO:, :]
                else:
                    pooled = win * (1.0 / w) - e0[HALO:, :]
                pbuf[rows, cols] = pooled.astype(jnp.bfloat16)

        def group_matmul(gi, gate):
            cols = slice(gi * POOL_GROUP_DIM, (gi + 1) * POOL_GROUP_DIM)
            ymix[:, cols] = jnp.dot(pbuf[:, cols], pool_w_ref[gi], preferred_element_type=jnp.float32)

        def post_mix_rows(rows, gate):
            y = _gated(ymix[rows, :], gate)
            h1 = x_ref[rows, :] + _rms(y * _gain(gains_ref, G_POOL_SCALE), _gain(gains_ref, G_MIX_POST))
            hs1[slot, rows, :] = h1
            hb[slot, rows, :] = _rms(h1, _gain(gains_ref, G_FFN_PRE)).astype(jnp.bfloat16)

        mixer = ([halo_rows] + _row_items(tile_rows, ROW_CHUNK, norm_rows)
                 + _row_items(tile_rows, POOL_ROW_CHUNK, pool_rows)
                 + [functools.partial(group_matmul, gi) for gi in range(len(POOL_WINDOWS))]
                 + _row_items(tile_rows, ROW_CHUNK, post_mix_rows))

        def ffn(side_items):
            _swiglu(hb.at[slot], wgu_ref, wd_ref, gbuf, ubuf, act, yffn.at[slot], tile_rows, side_items)

        def post_ffn_rows(rows, gate):
            h2 = hs1[slot, rows, :] + _rms(_gated(yffn[slot, rows, :], gate), _gain(gains_ref, G_FFN_POST))
            hs_out_ref[rows, :] = h2
            kvin[rows, :] = _rms(h2, _gain(gains_ref, G_KV)).astype(jnp.bfloat16)

        def kv_projection(gate):
            kv = jnp.dot(kvin[...], wkv_ref[...], preferred_element_type=jnp.float32) + bkv_ref[...]
            kv_out_ref[...] = kv.astype(jnp.bfloat16)

        epilogue = _row_items(tile_rows, ROW_CHUNK, post_ffn_rows) + [kv_projection]
        return mixer, ffn, epilogue

    _run_pipelined(step, n_tiles, (hs1, hb, yffn), stages)


def _const_spec(shape):
    nd = len(shape)
    return pl.BlockSpec(shape, lambda *_: (0,) * nd, pipeline_mode=pl.Buffered(1))


def _layer_spec(stacked_shape, layer):
    _, rows, cols = stacked_shape
    return pl.BlockSpec((None, rows, cols), lambda *_: (layer, 0, 0), pipeline_mode=pl.Buffered(1))


def _layer_a_call(x2d, meta, gains, pool_w, wgu, wd, wkv, bkv, *, layer, tm, tiles_per_batch, is_meta):
    n_rows = x2d.shape[0]
    n_tiles = n_rows // tm
    n_steps = n_tiles if n_tiles == 1 else n_tiles + PIPELINE_DEPTH - 1
    halo_blocks_per_tile = tm // HALO
    kern = functools.partial(_layer_a_kernel, tm=tm, tiles_per_batch=tiles_per_batch, n_tiles=n_tiles,
                             is_meta=is_meta)

    def mixer_tile(s):
        return jnp.minimum(s, n_tiles - 1)

    def epilogue_tile(s):
        return jnp.maximum(s - (n_steps - n_tiles), 0)

    return pl.pallas_call(
        kern,
        grid=(n_steps,),
        in_specs=[
            pl.BlockSpec((tm, D_MODEL), lambda s: (mixer_tile(s), 0)),
            pl.BlockSpec((HALO, D_MODEL), lambda s: (jnp.maximum(mixer_tile(s) * halo_blocks_per_tile - 1, 0), 0)),
            _const_spec((N_META, D_MODEL)),
            _const_spec(gains.shape),
            _const_spec(pool_w.shape),
            _layer_spec(wgu.shape, layer),
            _layer_spec(wd.shape, layer),
            _const_spec(wkv.shape),
            _const_spec(bkv.shape),
        ],
        out_specs=[
            pl.BlockSpec((tm, D_MODEL), lambda s: (epilogue_tile(s), 0)),
            pl.BlockSpec((tm, 2 * KV_DIM), lambda s: (epilogue_tile(s), 0)),
        ],
        out_shape=[
            jax.ShapeDtypeStruct((n_rows, D_MODEL), jnp.float32),
            jax.ShapeDtypeStruct((n_rows, 2 * KV_DIM), jnp.bfloat16),
        ],
        scratch_shapes=[
            pltpu.VMEM((tm + HALO, D_MODEL), jnp.float32),
            pltpu.VMEM((tm, D_MODEL), jnp.bfloat16),
            pltpu.VMEM((tm, D_MODEL), jnp.float32),
            pltpu.VMEM((2, tm, D_MODEL), jnp.float32),
            pltpu.VMEM((2, tm, D_MODEL), jnp.bfloat16),
            pltpu.VMEM((2, tm, D_MODEL), jnp.float32),
            pltpu.VMEM((tm, FFN_COLS), jnp.float32),
            pltpu.VMEM((tm, FFN_COLS), jnp.float32),
            pltpu.VMEM((tm, D_FF), jnp.bfloat16),
            pltpu.VMEM((tm, D_MODEL), jnp.bfloat16),
        ],
        compiler_params=pltpu.CompilerParams(
            dimension_semantics=("arbitrary",), vmem_limit_bytes=V7X_VMEM_LIMIT_BYTES),
        name="layer_a_meta" if is_meta else "layer_a",
    )(x2d, x2d, meta, gains, pool_w, wgu, wd, wkv, bkv)


def _bucket_distance_ranges():
    d = np.arange(WINDOW)
    max_exact = N_BUCKETS // 2
    df = np.maximum(d, 1).astype(np.float32)
    large = max_exact + (np.log(df / np.float32(max_exact)) / np.float32(math.log(MAX_DISTANCE / max_exact))
                         * np.float32(N_BUCKETS - max_exact)).astype(np.int32)
    bucket = np.where(d < max_exact, d, np.minimum(large, N_BUCKETS - 1))
    ranges = []
    for b in range(N_BUCKETS):
        members = d[bucket == b]
        if members.size:
            assert np.array_equal(members, np.arange(members[0], members[-1] + 1))
            ranges.append((int(members[0]), int(members[-1])))
        else:
            ranges.append(None)
    return ranges


def _bias_table_kernel(rel_bias_ref, out_ref):
    m = pl.program_id(0)
    q = lax.broadcasted_iota(jnp.int32, (BLOCK, 2 * BLOCK), 0)
    s = lax.broadcasted_iota(jnp.int32, (BLOCK, 2 * BLOCK), 1)
    d = q + BLOCK - s
    in_window = (d >= 0) & (d < WINDOW)
    valid = (in_window & (s >= PAD_FRONT), in_window)
    for half in range(2):
        h = 2 * m + half
        acc = jnp.zeros((BLOCK, 2 * BLOCK), jnp.float32)
        for b, distances in enumerate(_bucket_distance_ranges()):
            if distances is not None:
                lo, hi = distances
                acc = jnp.where((d >= lo) & (d <= hi), rel_bias_ref[b, h], acc)
        for jsel in range(2):
            out_ref[jsel, 0, :, half * 2 * BLOCK:(half + 1) * 2 * BLOCK] = jnp.where(valid[jsel], acc, -jnp.inf)


def _bias_table_call(rel_bias):
    return pl.pallas_call(
        _bias_table_kernel,
        grid=(N_HEAD_PAIRS,),
        in_specs=[
            pl.BlockSpec(memory_space=pltpu.SMEM),
        ],
        out_specs=pl.BlockSpec((2, 1, BLOCK, 4 * BLOCK), lambda m: (0, m, 0, 0)),
        out_shape=jax.ShapeDtypeStruct((2, N_HEAD_PAIRS, BLOCK, 4 * BLOCK), jnp.float32),
        compiler_params=pltpu.CompilerParams(dimension_semantics=("arbitrary",)),
        name="bias_table",
    )(rel_bias)


def _layer_b_kernel(hs_ref, kv_ref, metakv_ref, rowsum_cols_ref, bias_ref, sinks_ref, gains_ref,
                    wq_ref, bq_ref, wo_ref, bo_ref, wgu_ref, wd_ref,
                    out_ref,
                    hn, qbuf, obuf, yatt, hs3, hb, yffn, gbuf, ubuf, act, lbuf, mbuf, pbuf,
                    *, tm, tiles_per_batch, n_tiles):
    blocks_per_tile = tm // BLOCK
    step = pl.program_id(0)
    pair = jnp.minimum(step, n_tiles // 2 - 1)
    tile_rows = pl.ds(0, tm)

    lane = lax.broadcasted_iota(jnp.int32, (2 * BLOCK, 2 * HEAD_DIM), 1)
    low_half = lane < HEAD_DIM
    zero = jnp.zeros((2 * BLOCK, 2 * HEAD_DIM), jnp.bfloat16)
    low_half_q = lax.broadcasted_iota(jnp.int32, (BLOCK, 2 * HEAD_DIM), 1) < HEAD_DIM

    def per_kv_head_operands(pair_cols):
        swapped = jnp.concatenate([pair_cols[:, HEAD_DIM:], pair_cols[:, :HEAD_DIM]], axis=1)
        g0 = (jnp.where(low_half, pair_cols, zero), jnp.where(low_half, zero, swapped))
        g1 = (jnp.where(low_half, swapped, zero), jnp.where(low_half, zero, pair_cols))
        return g0, g1

    ones_top = rowsum_cols_ref[0:2 * BLOCK, :]
    ones_bottom = rowsum_cols_ref[2 * BLOCK:4 * BLOCK, :]

    def attention_block(slot, jb, gate):
        t = (2 * pair + slot) % tiles_per_batch
        j = t * blocks_per_tile + jb
        own = kv_ref[pl.ds(pl.multiple_of(j * BLOCK, BLOCK), BLOCK), :]
        prev_start = pl.multiple_of(jnp.maximum(j - 1, 0) * BLOCK, BLOCK)
        prev = jnp.where(j == 0, metakv_ref[...], kv_ref[pl.ds(prev_start, BLOCK), :])
        kvb = jnp.concatenate([prev, own], axis=0)
        k_ops = per_kv_head_operands(kvb[:, :KV_DIM])
        v_ops = per_kv_head_operands(kvb[:, KV_DIM:])
        jsel = jnp.minimum(j, 1)
        qrows = slice(jb * BLOCK, (jb + 1) * BLOCK)
        for g in range(N_KV_HEADS):
            u = jb * N_KV_HEADS + g
            pairs = range(g * PAIRS_PER_KV, (g + 1) * PAIRS_PER_KV)
            q4 = jnp.concatenate([qbuf[qrows, m * 2 * HEAD_DIM:(m + 1) * 2 * HEAD_DIM] for m in pairs], axis=0)
            k_op = jnp.concatenate(k_ops[g], axis=0)
            v_op = jnp.concatenate([jnp.concatenate([v_ops[g][0], ones_top], axis=1),
                                    jnp.concatenate([v_ops[g][1], ones_bottom], axis=1)], axis=0)
            logits = _gated(lax.dot_general(q4, k_op, (((1,), (1,)), ((), ())),
                                            preferred_element_type=jnp.float32) + bias_ref[jsel, g], gate)
            lbuf[u] = logits
            for r0 in range(0, ATT_ROWS, SOFTMAX_ROW_CHUNK):
                rows = slice(r0, r0 + SOFTMAX_ROW_CHUNK)
                for half in range(2):
                    sink = sinks_ref[2 * (pairs[0] + r0 // BLOCK) + half]
                    lg = logits[rows, half * 2 * BLOCK:(half + 1) * 2 * BLOCK]
                    mx = jnp.maximum(jnp.max(lg, axis=-1, keepdims=True), sink)
                    mbuf[u, rows, half * BLOCK:(half + 1) * BLOCK] = jnp.broadcast_to(mx, (SOFTMAX_ROW_CHUNK, BLOCK))
            for r0 in range(0, ATT_ROWS, SOFTMAX_ROW_CHUNK):
                rows = slice(r0, r0 + SOFTMAX_ROW_CHUNK)
                for half in range(2):
                    mrep = mbuf[u, rows, half * BLOCK:(half + 1) * BLOCK]
                    lg = lbuf[u, rows, half * 2 * BLOCK:(half + 1) * 2 * BLOCK]
                    p = jnp.exp(lg - jnp.concatenate([mrep, mrep], axis=1))
                    pbuf[u, rows, half * 2 * BLOCK:(half + 1) * 2 * BLOCK] = p.astype(jnp.bfloat16)
            oa = jnp.dot(pbuf[u], v_op, preferred_element_type=jnp.float32)
            for i, m in enumerate(pairs):
                rows = slice(i * BLOCK, (i + 1) * BLOCK)
                sink_gap = jnp.where(low_half_q, sinks_ref[2 * m] - mbuf[u, rows, 0:BLOCK],
                                     sinks_ref[2 * m + 1] - mbuf[u, rows, BLOCK:2 * BLOCK])
                denom = oa[rows, 2 * HEAD_DIM:] + jnp.exp(sink_gap)
                o = oa[rows, :2 * HEAD_DIM] * (1.0 / denom)
                obuf[qrows, m * 2 * HEAD_DIM:(m + 1) * 2 * HEAD_DIM] = o.astype(jnp.bfloat16)

    def stages(slot):
        def io(rows):
            return pl.ds(rows.start + slot * tm, rows.size)

        def norm_rows(rows, gate):
            h = _rms(_gated(hs_ref[io(rows), :], gate), _gain(gains_ref, G_MIX_PRE))
            hn[rows, :] = h.astype(jnp.bfloat16)

        def q_projection(gate):
            q = jnp.dot(hn[...], wq_ref[...], preferred_element_type=jnp.float32) + bq_ref[...]
            qbuf[...] = (q * (HEAD_DIM ** -0.5)).astype(jnp.bfloat16)

        def o_projection(gate):
            yatt[...] = jnp.dot(obuf[...], wo_ref[...], preferred_element_type=jnp.float32) + bo_ref[...]

        def post_mix_rows(rows, gate):
            h3 = hs_ref[io(rows), :] + _rms(_gated(yatt[rows, :], gate), _gain(gains_ref, G_MIX_POST))
            hs3[slot, rows, :] = h3
            hb[slot, rows, :] = _rms(h3, _gain(gains_ref, G_FFN_PRE)).astype(jnp.bfloat16)

        attention = (_row_items(tile_rows, ROW_CHUNK, norm_rows) + [q_projection]
                     + [functools.partial(attention_block, slot, jb) for jb in range(blocks_per_tile)]
                     + [o_projection] + _row_items(tile_rows, ROW_CHUNK, post_mix_rows))

        def ffn(side_items):
            _swiglu(hb.at[slot], wgu_ref, wd_ref, gbuf, ubuf, act, yffn.at[slot], tile_rows, side_items)

        def post_ffn_rows(rows, gate):
            y = _gated(yffn[slot, rows, :], gate)
            out_ref[io(rows), :] = hs3[slot, rows, :] + _rms(y, _gain(gains_ref, G_FFN_POST))

        return attention, ffn, _row_items(tile_rows, ROW_CHUNK, post_ffn_rows)

    _run_pipelined_pairs(step, n_tiles // 2, stages)


def _layer_b_call(hs, kv, metakv, bias, sinks, gains, wq, bq, wo, bo, wgu, wd, *, layer, tm):
    batch, seq, _ = hs.shape
    tiles_per_batch = seq // tm
    n_tiles = batch * tiles_per_batch
    n_att_units = (tm // BLOCK) * N_KV_HEADS
    lane_head = jnp.arange(2 * HEAD_DIM)[None, :] // HEAD_DIM
    row_head = jnp.arange(4 * BLOCK)[:, None] // (2 * BLOCK)
    rowsum_cols = (lane_head == row_head).astype(jnp.bfloat16)
    kern = functools.partial(_layer_b_kernel, tm=tm, tiles_per_batch=tiles_per_batch, n_tiles=n_tiles)

    n_pairs = n_tiles // 2
    pairs_per_batch = tiles_per_batch // 2

    def att_pair(s):
        pair = jnp.minimum(s, n_pairs - 1)
        return pair // pairs_per_batch, pair % pairs_per_batch

    def epilogue_pair(s):
        pair = jnp.maximum(s - 1, 0)
        return pair // pairs_per_batch, pair % pairs_per_batch

    return pl.pallas_call(
        kern,
        grid=(n_pairs + 1,),
        in_specs=[
            pl.BlockSpec((None, 2 * tm, D_MODEL), lambda s: (*att_pair(s), 0)),
            pl.BlockSpec((None, seq, 2 * KV_DIM), lambda s: (att_pair(s)[0], 0, 0), pipeline_mode=pl.Buffered(1)),
            _const_spec(metakv.shape),
            _const_spec(rowsum_cols.shape),
            _const_spec(bias.shape),
            pl.BlockSpec(memory_space=pltpu.SMEM),
            _const_spec(gains.shape),
            _const_spec(wq.shape),
            _const_spec(bq.shape),
            _const_spec(wo.shape),
            _const_spec(bo.shape),
            _layer_spec(wgu.shape, layer),
            _layer_spec(wd.shape, layer),
        ],
        out_specs=pl.BlockSpec((None, 2 * tm, D_MODEL), lambda s: (*epilogue_pair(s), 0)),
        out_shape=jax.ShapeDtypeStruct((batch, seq, D_MODEL), jnp.float32),
        scratch_shapes=[
            pltpu.VMEM((tm, D_MODEL), jnp.bfloat16),
            pltpu.VMEM((tm, D_MODEL), jnp.bfloat16),
            pltpu.VMEM((tm, D_MODEL), jnp.bfloat16),
            pltpu.VMEM((tm, D_MODEL), jnp.float32),
            pltpu.VMEM((2, tm, D_MODEL), jnp.float32),
            pltpu.VMEM((2, tm, D_MODEL), jnp.bfloat16),
            pltpu.VMEM((2, tm, D_MODEL), jnp.float32),
            pltpu.VMEM((tm, FFN_COLS), jnp.float32),
            pltpu.VMEM((tm, FFN_COLS), jnp.float32),
            pltpu.VMEM((tm, D_FF), jnp.bfloat16),
            pltpu.VMEM((n_att_units, ATT_ROWS, 4 * BLOCK), jnp.float32),
            pltpu.VMEM((n_att_units, ATT_ROWS, 2 * BLOCK), jnp.float32),
            pltpu.VMEM((n_att_units, ATT_ROWS, 4 * BLOCK), jnp.bfloat16),
        ],
        compiler_params=pltpu.CompilerParams(
            dimension_semantics=("arbitrary",), vmem_limit_bytes=V7X_VMEM_LIMIT_BYTES),
        name="layer_b",
    )(hs, kv, metakv, rowsum_cols, bias, sinks, gains, wq, bq, wo, bo, wgu, wd)


def _cast_kernel(w_ref, out_ref):
    out_ref[...] = w_ref[...].astype(jnp.bfloat16)


def _cast_weights(w, body, name):
    layers, rows, cols = w.shape
    w2d = w.reshape(layers * rows, cols)
    target = max(16, CAST_BLOCK_BYTES // (4 * cols) // 16 * 16)
    block_rows = next(r for r in range(target, 0, -16) if (layers * rows) % r == 0)
    spec = pl.BlockSpec((block_rows, cols), lambda i: (i, 0))
    out = pl.pallas_call(
        body,
        grid=(layers * rows // block_rows,),
        in_specs=[spec],
        out_specs=spec,
        out_shape=jax.ShapeDtypeStruct(w2d.shape, jnp.bfloat16),
        compiler_params=pltpu.CompilerParams(dimension_semantics=("arbitrary",)),
        name=name,
    )(w2d)
    return out.reshape(layers, rows, cols)


def _gain_table(rows):
    zero = jnp.zeros((D_MODEL,), jnp.float32)
    return jnp.stack([rows[i].astype(jnp.float32) if i in rows else zero for i in range(8)])


def kernel(x, meta_tokens, norm_mix_pre, norm_mix_post, norm_ffn_pre, norm_ffn_post, pool_w, pool_scale, kv_norm, w_k, b_k, w_v, b_v, w_q, b_q, w_o, b_o, sinks, rel_bias, w_gate_up, w_down):
    batch, seq, _ = x.shape
    bf16 = jnp.bfloat16

    gains_a = _gain_table({G_MIX_PRE: norm_mix_pre[0], G_MIX_POST: norm_mix_post[0], G_POOL_SCALE: pool_scale[0],
                           G_FFN_PRE: norm_ffn_pre[0], G_FFN_POST: norm_ffn_post[0], G_KV: kv_norm})
    gains_b = _gain_table({G_MIX_PRE: norm_mix_pre[1], G_MIX_POST: norm_mix_post[1],
                           G_FFN_PRE: norm_ffn_pre[1], G_FFN_POST: norm_ffn_post[1]})
    wkv = jnp.concatenate([w_k, w_v], axis=1).astype(bf16)
    bkv = jnp.concatenate([b_k, b_v])[None, :]
    wgu = _cast_weights(w_gate_up, _cast_kernel, "cast_gate_up")
    wd = _cast_weights(w_down, _cast_kernel, "cast_down")
    layer_a_weights = (gains_a, pool_w[0].astype(bf16), wgu, wd, wkv, bkv)

    x2d = x.reshape(batch * seq, D_MODEL)
    hs2, kv_x = _layer_a_call(x2d, meta_tokens, *layer_a_weights,
                              layer=0, tm=TILE_A, tiles_per_batch=seq // TILE_A, is_meta=False)
    _, kv_meta = _layer_a_call(meta_tokens, meta_tokens, *layer_a_weights,
                               layer=0, tm=N_META, tiles_per_batch=1, is_meta=True)
    metakv = jnp.concatenate([jnp.zeros((PAD_FRONT, 2 * KV_DIM), bf16), kv_meta], axis=0)

    bias = _bias_table_call(rel_bias)
    bias = bias.reshape(2, N_KV_HEADS, ATT_ROWS, 4 * BLOCK)

    out = _layer_b_call(hs2.reshape(batch, seq, D_MODEL), kv_x.reshape(batch, seq, 2 * KV_DIM), metakv, bias,
                        sinks[0], gains_b, w_q[0].astype(bf16), b_q, w_o[0].astype(bf16), b_o,
                        wgu, wd, layer=1, tm=TILE_B)
    return out
```

```python
import functools
import math

import numpy as np

import jax
import jax.numpy as jnp
from jax import lax
from jax.experimental import pallas as pl
from jax.experimental.pallas import tpu as pltpu

D_MODEL = 1024
N_META = 16
POOL_WINDOWS = (2, 4, 8, 16)
POOL_GROUP_DIM = D_MODEL // len(POOL_WINDOWS)
HEAD_DIM = 64
N_Q_HEADS = D_MODEL // HEAD_DIM
N_KV_HEADS = 2
HEADS_PER_KV = N_Q_HEADS // N_KV_HEADS
PAIRS_PER_KV = HEADS_PER_KV // 2
N_HEAD_PAIRS = N_Q_HEADS // 2
KV_DIM = N_KV_HEADS * HEAD_DIM
WINDOW = 128
BLOCK = 128
N_BUCKETS = 32
MAX_DISTANCE = 128
D_FF = 2816
EPS = 1e-6
PAD_FRONT = (-N_META) % BLOCK

LANES = 128
GATE_ROWS = 8
FFN_COLS = 256
SILU_ROW_CHUNK = 32
UNRIDDEN_SEGMENTS = 1
CAST_BLOCK_BYTES = 6 * 1024 * 1024
HALO = 16
ROW_CHUNK = 32
POOL_ROW_CHUNK = 32
ATT_ROWS = PAIRS_PER_KV * BLOCK
SOFTMAX_ROW_CHUNK = 16
TILE_A = 512
PIPELINE_DEPTH = 3
TILE_B = 256
V7X_VMEM_LIMIT_BYTES = 63 * 1024 * 1024

G_MIX_PRE, G_MIX_POST, G_POOL_SCALE, G_FFN_PRE, G_FFN_POST, G_KV = range(6)


def _rms(x, g):
    ms = jnp.sum(x * x, axis=-1, keepdims=True) * (1.0 / D_MODEL)
    return x * lax.rsqrt(ms + EPS) * g


def _gain(gains_ref, i):
    return gains_ref[i:i + 1, :]


def _row_items(rows, chunk, body):
    chunk = min(chunk, rows.size)
    return [functools.partial(body, pl.ds(rows.start + i * chunk, chunk)) for i in range(rows.size // chunk)]


def _run_items(items, gate=None):
    for item in items:
        item(gate)


def _gate_of(value):
    bits = pltpu.bitcast(value[-GATE_ROWS:, -LANES:], jnp.uint32)
    return pltpu.bitcast((bits >> 16) >> 16, jnp.float32)


def _gated(x, gate):
    if gate is None:
        return x
    z = jnp.concatenate([gate] * (x.shape[0] // GATE_ROWS), axis=0)
    return jnp.concatenate([x[:, :LANES] + z, x[:, LANES:]], axis=1)


def _spread(items, n_bins):
    bins = [[] for _ in range(n_bins)]
    for k, item in enumerate(items):
        bins[k * n_bins // max(len(items), 1)].append(item)
    return bins


def _swiglu(h_ref, wgu_ref, wd_ref, gbuf, ubuf, act, out_ref, rows, side_items=()):
    segments = ([("gate_up", c) for c in range(0, D_FF, FFN_COLS)]
                + [("down", c) for c in range(0, D_MODEL, FFN_COLS)])
    side = _spread(list(side_items), len(segments) - UNRIDDEN_SEGMENTS) + [[]] * UNRIDDEN_SEGMENTS
    for (kind, c0), side_group in zip(segments, side):
        if kind == "gate_up":
            g = jnp.dot(h_ref[rows, :], wgu_ref[:, c0:c0 + FFN_COLS], preferred_element_type=jnp.float32)
            gbuf[rows, :] = g
            ubuf[rows, :] = jnp.dot(h_ref[rows, :], wgu_ref[:, D_FF + c0:D_FF + c0 + FFN_COLS],
                                    preferred_element_type=jnp.float32)
            gate_source = g

            def silu_rows(r, gate, c0=c0):
                half_g = 0.5 * gbuf[r, :]
                silu = half_g + half_g * jnp.tanh(half_g)
                act[r, c0:c0 + FFN_COLS] = (silu * ubuf[r, :]).astype(jnp.bfloat16)

            _run_items(_row_items(rows, SILU_ROW_CHUNK, silu_rows))
        else:
            y = jnp.dot(act[rows, :], wd_ref[:, c0:c0 + FFN_COLS], preferred_element_type=jnp.float32)
            out_ref[rows, c0:c0 + FFN_COLS] = y
            gate_source = y
        if side_group:
            _run_items(side_group, _gate_of(gate_source))


def _run_pipelined(step, n_tiles, carried, stages):
    if n_tiles == 1:
        mixer, ffn, epilogue = stages(0)
        _run_items(mixer)
        ffn(())
        _run_items(epilogue)
        return

    last_step = n_tiles + 1

    @pl.when(step == 0)
    def _():
        for ref in carried:
            ref[...] = jnp.zeros(ref.shape, ref.dtype)
        _run_items(stages(0)[0])

    for parity in range(2):
        @pl.when((step % 2 == parity) & (step > 0) & (step < last_step))
        def _(parity=parity):
            mixer, _, epilogue = stages(parity)
            _, ffn, _ = stages(1 - parity)
            ffn(epilogue + mixer)

    @pl.when(step == last_step)
    def _():
        _run_items(stages(last_step % 2)[2])


def _run_pipelined_pairs(step, n_pairs, stages):
    even_mixer, even_ffn, even_epilogue = stages(0)
    odd_mixer, odd_ffn, odd_epilogue = stages(1)

    @pl.when(step == 0)
    def _():
        _run_items(even_mixer)
        even_ffn(odd_mixer)

    @pl.when((step > 0) & (step < n_pairs))
    def _():
        odd_ffn(even_epilogue + even_mixer)
        even_ffn(odd_epilogue + odd_mixer)

    @pl.when(step == n_pairs)
    def _():
        odd_ffn(even_epilogue)
        _run_items(odd_epilogue)


def _layer_a_kernel(x_ref, prev_ref, meta_ref, gains_ref, pool_w_ref, wgu_ref, wd_ref, wkv_ref, bkv_ref,
                    hs_out_ref, kv_out_ref,
                    hext, pbuf, ymix, hs1, hb, yffn, gbuf, ubuf, act, kvin,
                    *, tm, tiles_per_batch, n_tiles, is_meta):
    g_mix_pre = _gain(gains_ref, G_MIX_PRE)
    step = pl.program_id(0)
    tile_rows = pl.ds(0, tm)

    def stages(slot):
        def halo_rows(gate):
            if is_meta:
                hext[0:HALO, :] = jnp.zeros((HALO, D_MODEL), jnp.float32)
            else:
                first_of_batch = (jnp.minimum(step, n_tiles - 1) % tiles_per_batch) == 0
                prev = jnp.where(first_of_batch, meta_ref[...], prev_ref[...])
                hext[0:HALO, :] = _rms(prev, g_mix_pre)

        def norm_rows(rows, gate):
            hext[pl.ds(rows.start + HALO, rows.size), :] = _rms(_gated(x_ref[rows, :], gate), g_mix_pre)

        def pool_rows(rows, gate):
            for gi, w in enumerate(POOL_WINDOWS):
                cols = slice(gi * POOL_GROUP_DIM, (gi + 1) * POOL_GROUP_DIM)
                e0 = _gated(hext[pl.ds(rows.start, rows.size + HALO), cols], gate)
                e = e0
                shift = 1
                while shift < w:
                    e = e + pltpu.roll(e, shift, 0)
                    shift *= 2
                win = e[HALO:, :]
                if is_meta:
                    pos = lax.broadcasted_iota(jnp.int32, win.shape, 0) + 1
                    cnt = jnp.minimum(pos, w).astype(jnp.float32)
                    pooled = win / cnt - e0[HALO:, :]
                else:
                    pooled = win * (1.0 / w) - e0[HALO:, :]
                pbuf[rows, cols] = pooled.astype(jnp.bfloat16)

        def group_matmul(gi, gate):
            cols = slice(gi * POOL_GROUP_DIM, (gi + 1) * POOL_GROUP_DIM)
            ymix[:, cols] = jnp.dot(pbuf[:, cols], pool_w_ref[gi], preferred_element_type=jnp.float32)

        def post_mix_rows(rows, gate):
            y = _gated(ymix[rows, :], gate)
            h1 = x_ref[rows, :] + _rms(y * _gain(gains_ref, G_POOL_SCALE), _gain(gains_ref, G_MIX_POST))
            hs1[slot, rows, :] = h1
            hb[slot, rows, :] = _rms(h1, _gain(gains_ref, G_FFN_PRE)).astype(jnp.bfloat16)

        mixer = ([halo_rows] + _row_items(tile_rows, ROW_CHUNK, norm_rows)
                 + _row_items(tile_rows, POOL_ROW_CHUNK, pool_rows)
                 + [functools.partial(group_matmul, gi) for gi in range(len(POOL_WINDOWS))]
                 + _row_items(tile_rows, ROW_CHUNK, post_mix_rows))

        def ffn(side_items):
            _swiglu(hb.at[slot], wgu_ref, wd_ref, gbuf, ubuf, act, yffn.at[slot], tile_rows, side_items)

        def post_ffn_rows(rows, gate):
            h2 = hs1[slot, rows, :] + _rms(_gated(yffn[slot, rows, :], gate), _gain(gains_ref, G_FFN_POST))
            hs_out_ref[rows, :] = h2
            kvin[rows, :] = _rms(h2, _gain(gains_ref, G_KV)).astype(jnp.bfloat16)

        def kv_projection(gate):
            kv = jnp.dot(kvin[...], wkv_ref[...], preferred_element_type=jnp.float32) + bkv_ref[...]
            kv_out_ref[...] = kv.astype(jnp.bfloat16)

        epilogue = _row_items(tile_rows, ROW_CHUNK, post_ffn_rows) + [kv_projection]
        return mixer, ffn, epilogue

    _run_pipelined(step, n_tiles, (hs1, hb, yffn), stages)


def _const_spec(shape):
    nd = len(shape)
    return pl.BlockSpec(shape, lambda *_: (0,) * nd, pipeline_mode=pl.Buffered(1))


def _layer_spec(stacked_shape, layer):
    _, rows, cols = stacked_shape
    return pl.BlockSpec((None, rows, cols), lambda *_: (layer, 0, 0), pipeline_mode=pl.Buffered(1))


def _layer_a_call(x2d, meta, gains, pool_w, wgu, wd, wkv, bkv, *, layer, tm, tiles_per_batch, is_meta):
    n_rows = x2d.shape[0]
    n_tiles = n_rows // tm
    n_steps = n_tiles if n_tiles == 1 else n_tiles + PIPELINE_DEPTH - 1
    halo_blocks_per_tile = tm // HALO
    kern = functools.partial(_layer_a_kernel, tm=tm, tiles_per_batch=tiles_per_batch, n_tiles=n_tiles,
                             is_meta=is_meta)

    def mixer_tile(s):
        return jnp.minimum(s, n_tiles - 1)

    def epilogue_tile(s):
        return jnp.maximum(s - (n_steps - n_tiles), 0)

    return pl.pallas_call(
        kern,
        grid=(n_steps,),
        in_specs=[
            pl.BlockSpec((tm, D_MODEL), lambda s: (mixer_tile(s), 0)),
            pl.BlockSpec((HALO, D_MODEL), lambda s: (jnp.maximum(mixer_tile(s) * halo_blocks_per_tile - 1, 0), 0)),
            _const_spec((N_META, D_MODEL)),
            _const_spec(gains.shape),
            _const_spec(pool_w.shape),
            _layer_spec(wgu.shape, layer),
            _layer_spec(wd.shape, layer),
            _const_spec(wkv.shape),
            _const_spec(bkv.shape),
        ],
        out_specs=[
            pl.BlockSpec((tm, D_MODEL), lambda s: (epilogue_tile(s), 0)),
            pl.BlockSpec((tm, 2 * KV_DIM), lambda s: (epilogue_tile(s), 0)),
        ],
        out_shape=[
            jax.ShapeDtypeStruct((n_rows, D_MODEL), jnp.float32),
            jax.ShapeDtypeStruct((n_rows, 2 * KV_DIM), jnp.bfloat16),
        ],
        scratch_shapes=[
            pltpu.VMEM((tm + HALO, D_MODEL), jnp.float32),
            pltpu.VMEM((tm, D_MODEL), jnp.bfloat16),
            pltpu.VMEM((tm, D_MODEL), jnp.float32),
            pltpu.VMEM((2, tm, D_MODEL), jnp.float32),
            pltpu.VMEM((2, tm, D_MODEL), jnp.bfloat16),
            pltpu.VMEM((2, tm, D_MODEL), jnp.float32),
            pltpu.VMEM((tm, FFN_COLS), jnp.float32),
            pltpu.VMEM((tm, FFN_COLS), jnp.float32),
            pltpu.VMEM((tm, D_FF), jnp.bfloat16),
            pltpu.VMEM((tm, D_MODEL), jnp.bfloat16),
        ],
        compiler_params=pltpu.CompilerParams(
            dimension_semantics=("arbitrary",), vmem_limit_bytes=V7X_VMEM_LIMIT_BYTES),
        name="layer_a_meta" if is_meta else "layer_a",
    )(x2d, x2d, meta, gains, pool_w, wgu, wd, wkv, bkv)


def _bucket_distance_ranges():
    d = np.arange(WINDOW)
    max_exact = N_BUCKETS // 2
    df = np.maximum(d, 1).astype(np.float32)
    large = max_exact + (np.log(df / np.float32(max_exact)) / np.float32(math.log(MAX_DISTANCE / max_exact))
                         * np.float32(N_BUCKETS - max_exact)).astype(np.int32)
    bucket = np.where(d < max_exact, d, np.minimum(large, N_BUCKETS - 1))
    ranges = []
    for b in range(N_BUCKETS):
        members = d[bucket == b]
        if members.size:
            assert np.array_equal(members, np.arange(members[0], members[-1] + 1))
            ranges.append((int(members[0]), int(members[-1])))
        else:
            ranges.append(None)
    return ranges


def _bias_table_kernel(rel_bias_ref, out_ref):
    m = pl.program_id(0)
    q = lax.broadcasted_iota(jnp.int32, (BLOCK, 2 * BLOCK), 0)
    s = lax.broadcasted_iota(jnp.int32, (BLOCK, 2 * BLOCK), 1)
    d = q + BLOCK - s
    in_window = (d >= 0) & (d < WINDOW)
    valid = (in_window & (s >= PAD_FRONT), in_window)
    for half in range(2):
        h = 2 * m + half
        acc = jnp.zeros((BLOCK, 2 * BLOCK), jnp.float32)
        for b, distances in enumerate(_bucket_distance_ranges()):
            if distances is not None:
                lo, hi = distances
                acc = jnp.where((d >= lo) & (d <= hi), rel_bias_ref[b, h], acc)
        for jsel in range(2):
            out_ref[jsel, 0, :, half * 2 * BLOCK:(half + 1) * 2 * BLOCK] = jnp.where(valid[jsel], acc, -jnp.inf)


def _bias_table_call(rel_bias):
    return pl.pallas_call(
        _bias_table_kernel,
        grid=(N_HEAD_PAIRS,),
        in_specs=[
            pl.BlockSpec(memory_space=pltpu.SMEM),
        ],
        out_specs=pl.BlockSpec((2, 1, BLOCK, 4 * BLOCK), lambda m: (0, m, 0, 0)),
        out_shape=jax.ShapeDtypeStruct((2, N_HEAD_PAIRS, BLOCK, 4 * BLOCK), jnp.float32),
        compiler_params=pltpu.CompilerParams(dimension_semantics=("arbitrary",)),
        name="bias_table",
    )(rel_bias)


def _layer_b_kernel(hs_ref, kv_ref, metakv_ref, rowsum_cols_ref, bias_ref, sinks_ref, gains_ref,
                    wq_ref, bq_ref, wo_ref, bo_ref, wgu_ref, wd_ref,
                    out_ref,
                    hn, qbuf, obuf, yatt, hs3, hb, yffn, gbuf, ubuf, act, lbuf, mbuf, pbuf,
                    *, tm, tiles_per_batch, n_tiles):
    blocks_per_tile = tm // BLOCK
    step = pl.program_id(0)
    pair = jnp.minimum(step, n_tiles // 2 - 1)
    tile_rows = pl.ds(0, tm)

    lane = lax.broadcasted_iota(jnp.int32, (2 * BLOCK, 2 * HEAD_DIM), 1)
    low_half = lane < HEAD_DIM
    zero = jnp.zeros((2 * BLOCK, 2 * HEAD_DIM), jnp.bfloat16)
    low_half_q = lax.broadcasted_iota(jnp.int32, (BLOCK, 2 * HEAD_DIM), 1) < HEAD_DIM

    def per_kv_head_operands(pair_cols):
        swapped = jnp.concatenate([pair_cols[:, HEAD_DIM:], pair_cols[:, :HEAD_DIM]], axis=1)
        g0 = (jnp.where(low_half, pair_cols, zero), jnp.where(low_half, zero, swapped))
        g1 = (jnp.where(low_half, swapped, zero), jnp.where(low_half, zero, pair_cols))
        return g0, g1

    ones_top = rowsum_cols_ref[0:2 * BLOCK, :]
    ones_bottom = rowsum_cols_ref[2 * BLOCK:4 * BLOCK, :]

    def attention_block(slot, jb, gate):
        t = (2 * pair + slot) % tiles_per_batch
        j = t * blocks_per_tile + jb
        own = kv_ref[pl.ds(pl.multiple_of(j * BLOCK, BLOCK), BLOCK), :]
        prev_start = pl.multiple_of(jnp.maximum(j - 1, 0) * BLOCK, BLOCK)
        prev = jnp.where(j == 0, metakv_ref[...], kv_ref[pl.ds(prev_start, BLOCK), :])
        kvb = jnp.concatenate([prev, own], axis=0)
        k_ops = per_kv_head_operands(kvb[:, :KV_DIM])
        v_ops = per_kv_head_operands(kvb[:, KV_DIM:])
        jsel = jnp.minimum(j, 1)
        qrows = slice(jb * BLOCK, (jb + 1) * BLOCK)
        for g in range(N_KV_HEADS):
            u = jb * N_KV_HEADS + g
            pairs = range(g * PAIRS_PER_KV, (g + 1) * PAIRS_PER_KV)
            q4 = jnp.concatenate([qbuf[qrows, m * 2 * HEAD_DIM:(m + 1) * 2 * HEAD_DIM] for m in pairs], axis=0)
            k_op = jnp.concatenate(k_ops[g], axis=0)
            v_op = jnp.concatenate([jnp.concatenate([v_ops[g][0], ones_top], axis=1),
                                    jnp.concatenate([v_ops[g][1], ones_bottom], axis=1)], axis=0)
            logits = _gated(lax.dot_general(q4, k_op, (((1,), (1,)), ((), ())),
                                            preferred_element_type=jnp.float32) + bias_ref[jsel, g], gate)
            lbuf[u] = logits
            for r0 in range(0, ATT_ROWS, SOFTMAX_ROW_CHUNK):
                rows = slice(r0, r0 + SOFTMAX_ROW_CHUNK)
                for half in range(2):
                    sink = sinks_ref[2 * (pairs[0] + r0 // BLOCK) + half]
                    lg = logits[rows, half * 2 * BLOCK:(half + 1) * 2 * BLOCK]
                    mx = jnp.maximum(jnp.max(lg, axis=-1, keepdims=True), sink)
                    mbuf[u, rows, half * BLOCK:(half + 1) * BLOCK] = jnp.broadcast_to(mx, (SOFTMAX_ROW_CHUNK, BLOCK))
            for r0 in range(0, ATT_ROWS, SOFTMAX_ROW_CHUNK):
                rows = slice(r0, r0 + SOFTMAX_ROW_CHUNK)
                for half in range(2):
                    mrep = mbuf[u, rows, half * BLOCK:(half + 1) * BLOCK]
                    lg = lbuf[u, rows, half * 2 * BLOCK:(half + 1) * 2 * BLOCK]
                    p = jnp.exp(lg - jnp.concatenate([mrep, mrep], axis=1))
                    pbuf[u, rows, half * 2 * BLOCK:(half + 1) * 2 * BLOCK] = p.astype(jnp.bfloat16)
            oa = jnp.dot(pbuf[u], v_op, preferred_element_type=jnp.float32)
            for i, m in enumerate(pairs):
                rows = slice(i * BLOCK, (i + 1) * BLOCK)
                sink_gap = jnp.where(low_half_q, sinks_ref[2 * m] - mbuf[u, rows, 0:BLOCK],
                                     sinks_ref[2 * m + 1] - mbuf[u, rows, BLOCK:2 * BLOCK])
                denom = oa[rows, 2 * HEAD_DIM:] + jnp.exp(sink_gap)
                o = oa[rows, :2 * HEAD_DIM] * (1.0 / denom)
                obuf[qrows, m * 2 * HEAD_DIM:(m + 1) * 2 * HEAD_DIM] = o.astype(jnp.bfloat16)

    def stages(slot):
        def io(rows):
            return pl.ds(rows.start + slot * tm, rows.size)

        def norm_rows(rows, gate):
            h = _rms(_gated(hs_ref[io(rows), :], gate), _gain(gains_ref, G_MIX_PRE))
            hn[rows, :] = h.astype(jnp.bfloat16)

        def q_projection(gate):
            q = jnp.dot(hn[...], wq_ref[...], preferred_element_type=jnp.float32) + bq_ref[...]
            qbuf[...] = (q * (HEAD_DIM ** -0.5)).astype(jnp.bfloat16)

        def o_projection(gate):
            yatt[...] = jnp.dot(obuf[...], wo_ref[...], preferred_element_type=jnp.float32) + bo_ref[...]

        def post_mix_rows(rows, gate):
            h3 = hs_ref[io(rows), :] + _rms(_gated(yatt[rows, :], gate), _gain(gains_ref, G_MIX_POST))
            hs3[slot, rows, :] = h3
            hb[slot, rows, :] = _rms(h3, _gain(gains_ref, G_FFN_PRE)).astype(jnp.bfloat16)

        attention = (_row_items(tile_rows, ROW_CHUNK, norm_rows) + [q_projection]
                     + [functools.partial(attention_block, slot, jb) for jb in range(blocks_per_tile)]
                     + [o_projection] + _row_items(tile_rows, ROW_CHUNK, post_mix_rows))

        def ffn(side_items):
            _swiglu(hb.at[slot], wgu_ref, wd_ref, gbuf, ubuf, act, yffn.at[slot], tile_rows, side_items)

        def post_ffn_rows(rows, gate):
            y = _gated(yffn[slot, rows, :], gate)
            out_ref[io(rows), :] = hs3[slot, rows, :] + _rms(y, _gain(gains_ref, G_FFN_POST))

        return attention, ffn, _row_items(tile_rows, ROW_CHUNK, post_ffn_rows)

    _run_pipelined_pairs(step, n_tiles // 2, stages)


def _layer_b_call(hs, kv, metakv, bias, sinks, gains, wq, bq, wo, bo, wgu, wd, *, layer, tm):
    batch, seq, _ = hs.shape
    tiles_per_batch = seq // tm
    n_tiles = batch * tiles_per_batch
    n_att_units = (tm // BLOCK) * N_KV_HEADS
    lane_head = jnp.arange(2 * HEAD_DIM)[None, :] // HEAD_DIM
    row_head = jnp.arange(4 * BLOCK)[:, None] // (2 * BLOCK)
    rowsum_cols = (lane_head == row_head).astype(jnp.bfloat16)
    kern = functools.partial(_layer_b_kernel, tm=tm, tiles_per_batch=tiles_per_batch, n_tiles=n_tiles)

    n_pairs = n_tiles // 2
    pairs_per_batch = tiles_per_batch // 2

    def att_pair(s):
        pair = jnp.minimum(s, n_pairs - 1)
        return pair // pairs_per_batch, pair % pairs_per_batch

    def epilogue_pair(s):
        pair = jnp.maximum(s - 1, 0)
        return pair // pairs_per_batch, pair % pairs_per_batch

    return pl.pallas_call(
        kern,
        grid=(n_pairs + 1,),
        in_specs=[
            pl.BlockSpec((None, 2 * tm, D_MODEL), lambda s: (*att_pair(s), 0)),
            pl.BlockSpec((None, seq, 2 * KV_DIM), lambda s: (att_pair(s)[0], 0, 0), pipeline_mode=pl.Buffered(1)),
            _const_spec(metakv.shape),
            _const_spec(rowsum_cols.shape),
            _const_spec(bias.shape),
            pl.BlockSpec(memory_space=pltpu.SMEM),
            _const_spec(gains.shape),
            _const_spec(wq.shape),
            _const_spec(bq.shape),
            _const_spec(wo.shape),
            _const_spec(bo.shape),
            _layer_spec(wgu.shape, layer),
            _layer_spec(wd.shape, layer),
        ],
        out_specs=pl.BlockSpec((None, 2 * tm, D_MODEL), lambda s: (*epilogue_pair(s), 0)),
        out_shape=jax.ShapeDtypeStruct((batch, seq, D_MODEL), jnp.float32),
        scratch_shapes=[
            pltpu.VMEM((tm, D_MODEL), jnp.bfloat16),
            pltpu.VMEM((tm, D_MODEL), jnp.bfloat16),
            pltpu.VMEM((tm, D_MODEL), jnp.bfloat16),
            pltpu.VMEM((tm, D_MODEL), jnp.float32),
            pltpu.VMEM((2, tm, D_MODEL), jnp.float32),
            pltpu.VMEM((2, tm, D_MODEL), jnp.bfloat16),
            pltpu.VMEM((2, tm, D_MODEL), jnp.float32),
            pltpu.VMEM((tm, FFN_COLS), jnp.float32),
            pltpu.VMEM((tm, FFN_COLS), jnp.float32),
            pltpu.VMEM((tm, D_FF), jnp.bfloat16),
            pltpu.VMEM((n_att_units, ATT_ROWS, 4 * BLOCK), jnp.float32),
            pltpu.VMEM((n_att_units, ATT_ROWS, 2 * BLOCK), jnp.float32),
            pltpu.VMEM((n_att_units, ATT_ROWS, 4 * BLOCK), jnp.bfloat16),
        ],
        compiler_params=pltpu.CompilerParams(
            dimension_semantics=("arbitrary",), vmem_limit_bytes=V7X_VMEM_LIMIT_BYTES),
        name="layer_b",
    )(hs, kv, metakv, rowsum_cols, bias, sinks, gains, wq, bq, wo, bo, wgu, wd)


def _cast_kernel(w_ref, out_ref):
    out_ref[...] = w_ref[...].astype(jnp.bfloat16)


def _cast_weights(w, body, name):
    layers, rows, cols = w.shape
    w2d = w.reshape(layers * rows, cols)
    target = max(16, CAST_BLOCK_BYTES // (4 * cols) // 16 * 16)
    block_rows = next(r for r in range(target, 0, -16) if (layers * rows) % r == 0)
    spec = pl.BlockSpec((block_rows, cols), lambda i: (i, 0))
    out = pl.pallas_call(
        body,
        grid=(layers * rows // block_rows,),
        in_specs=[spec],
        out_specs=spec,
        out_shape=jax.ShapeDtypeStruct(w2d.shape, jnp.bfloat16),
        compiler_params=pltpu.CompilerParams(dimension_semantics=("arbitrary",)),
        name=name,
    )(w2d)
    return out.reshape(layers, rows, cols)


def _gain_table(rows):
    zero = jnp.zeros((D_MODEL,), jnp.float32)
    return jnp.stack([rows[i].astype(jnp.float32) if i in rows else zero for i in range(8)])


def kernel(x, meta_tokens, norm_mix_pre, norm_mix_post, norm_ffn_pre, norm_ffn_post, pool_w, pool_scale, kv_norm, w_k, b_k, w_v, b_v, w_q, b_q, w_o, b_o, sinks, rel_bias, w_gate_up, w_down):
    batch, seq, _ = x.shape
    bf16 = jnp.bfloat16

    gains_a = _gain_table({G_MIX_PRE: norm_mix_pre[0], G_MIX_POST: norm_mix_post[0], G_POOL_SCALE: pool_scale[0],
                           G_FFN_PRE: norm_ffn_pre[0], G_FFN_POST: norm_ffn_post[0], G_KV: kv_norm})
    gains_b = _gain_table({G_MIX_PRE: norm_mix_pre[1], G_MIX_POST: norm_mix_post[1],
                           G_FFN_PRE: norm_ffn_pre[1], G_FFN_POST: norm_ffn_post[1]})
    wkv = jnp.concatenate([w_k, w_v], axis=1).astype(bf16)
    bkv = jnp.concatenate([b_k, b_v])[None, :]
    wgu = _cast_weights(w_gate_up, _cast_kernel, "cast_gate_up")
    wd = _cast_weights(w_down, _cast_kernel, "cast_down")
    layer_a_weights = (gains_a, pool_w[0].astype(bf16), wgu, wd, wkv, bkv)

    x2d = x.reshape(batch * seq, D_MODEL)
    hs2, kv_x = _layer_a_call(x2d, meta_tokens, *layer_a_weights,
                              layer=0, tm=TILE_A, tiles_per_batch=seq // TILE_A, is_meta=False)
    _, kv_meta = _layer_a_call(meta_tokens, meta_tokens, *layer_a_weights,
                               layer=0, tm=N_META, tiles_per_batch=1, is_meta=True)
    metakv = jnp.concatenate([jnp.zeros((PAD_FRONT, 2 * KV_DIM), bf16), kv_meta], axis=0)

    bias = _bias_table_call(rel_bias)
    bias = bias.reshape(2, N_KV_HEADS, ATT_ROWS, 4 * BLOCK)

    out = _layer_b_call(hs2.reshape(batch, seq, D_MODEL), kv_x.reshape(batch, seq, 2 * KV_DIM), metakv, bias,
                        sinks[0], gains_b, w_q[0].astype(bf16), b_q, w_o[0].astype(bf16), b_o,
                        wgu, wd, layer=1, tm=TILE_B)
    return out
```

```python
import functools
import math

import numpy as np

import jax
import jax.numpy as jnp
from jax import lax
from jax.experimental import pallas as pl
from jax.experimental.pallas import tpu as pltpu

D_MODEL = 1024
N_META = 16
POOL_WINDOWS = (2, 4, 8, 16)
POOL_GROUP_DIM = D_MODEL // len(POOL_WINDOWS)
HEAD_DIM = 64
N_Q_HEADS = D_MODEL // HEAD_DIM
N_KV_HEADS = 2
HEADS_PER_KV = N_Q_HEADS // N_KV_HEADS
PAIRS_PER_KV = HEADS_PER_KV // 2
N_HEAD_PAIRS = N_Q_HEADS // 2
KV_DIM = N_KV_HEADS * HEAD_DIM
WINDOW = 128
BLOCK = 128
N_BUCKETS = 32
MAX_DISTANCE = 128
D_FF = 2816
EPS = 1e-6
PAD_FRONT = (-N_META) % BLOCK

LANES = 128
GATE_ROWS = 8
FFN_COLS = 256
SILU_ROW_CHUNK = 32
UNRIDDEN_SEGMENTS = 1
CAST_BLOCK_BYTES = 6 * 1024 * 1024
HALO = 16
ROW_CHUNK = 32
POOL_ROW_CHUNK = 64
ATT_ROWS = PAIRS_PER_KV * BLOCK
SOFTMAX_ROW_CHUNK = 16
TILE_A = 256
TILE_B = 256
V7X_VMEM_LIMIT_BYTES = 63 * 1024 * 1024

G_MIX_PRE, G_MIX_POST, G_POOL_SCALE, G_FFN_PRE, G_FFN_POST, G_KV = range(6)


def _rms(x, g):
    ms = jnp.sum(x * x, axis=-1, keepdims=True) * (1.0 / D_MODEL)
    return x * lax.rsqrt(ms + EPS) * g


def _gain(gains_ref, i):
    return gains_ref[i:i + 1, :]


def _row_items(rows, chunk, body):
    chunk = min(chunk, rows.size)
    return [functools.partial(body, pl.ds(rows.start + i * chunk, chunk)) for i in range(rows.size // chunk)]


def _run_items(items, gate=None):
    for item in items:
        item(gate)


def _gate_of(value):
    bits = pltpu.bitcast(value[-GATE_ROWS:, -LANES:], jnp.uint32)
    return pltpu.bitcast((bits >> 16) >> 16, jnp.float32)


def _gated(x, gate):
    if gate is None:
        return x
    z = jnp.concatenate([gate] * (x.shape[0] // GATE_ROWS), axis=0)
    return jnp.concatenate([x[:, :LANES] + z, x[:, LANES:]], axis=1)


def _spread(items, n_bins):
    bins = [[] for _ in range(n_bins)]
    for k, item in enumerate(items):
        bins[k * n_bins // max(len(items), 1)].append(item)
    return bins


def _swiglu(h_ref, wgu_ref, wd_ref, gbuf, ubuf, act, out_ref, rows, side_items=()):
    segments = ([("gate_up", c) for c in range(0, D_FF, FFN_COLS)]
                + [("down", c) for c in range(0, D_MODEL, FFN_COLS)])
    side = _spread(list(side_items), len(segments) - UNRIDDEN_SEGMENTS) + [[]] * UNRIDDEN_SEGMENTS
    for (kind, c0), side_group in zip(segments, side):
        if kind == "gate_up":
            g = jnp.dot(h_ref[rows, :], wgu_ref[:, c0:c0 + FFN_COLS], preferred_element_type=jnp.float32)
            gbuf[rows, :] = g
            ubuf[rows, :] = jnp.dot(h_ref[rows, :], wgu_ref[:, D_FF + c0:D_FF + c0 + FFN_COLS],
                                    preferred_element_type=jnp.float32)
            gate_source = g

            def silu_rows(r, gate, c0=c0):
                half_g = 0.5 * gbuf[r, :]
                silu = half_g + half_g * jnp.tanh(half_g)
                act[r, c0:c0 + FFN_COLS] = (silu * ubuf[r, :]).astype(jnp.bfloat16)

            _run_items(_row_items(rows, SILU_ROW_CHUNK, silu_rows))
        else:
            y = jnp.dot(act[rows, :], wd_ref[:, c0:c0 + FFN_COLS], preferred_element_type=jnp.float32)
            out_ref[rows, c0:c0 + FFN_COLS] = y
            gate_source = y
        if side_group:
            _run_items(side_group, _gate_of(gate_source))


def _run_single_tile(stages):
    mixer, ffn, epilogue = stages(0)
    _run_items(mixer)
    ffn(())
    _run_items(epilogue)


def _run_pipelined_pairs(step, n_pairs, stages):
    even_mixer, even_ffn, even_epilogue = stages(0)
    odd_mixer, odd_ffn, odd_epilogue = stages(1)

    @pl.when(step == 0)
    def _():
        _run_items(even_mixer)
        even_ffn(odd_mixer)

    @pl.when((step > 0) & (step < n_pairs))
    def _():
        odd_ffn(even_epilogue + even_mixer)
        even_ffn(odd_epilogue + odd_mixer)

    @pl.when(step == n_pairs)
    def _():
        odd_ffn(even_epilogue)
        _run_items(odd_epilogue)


def _layer_a_kernel(x_ref, prev_ref, meta_ref, gains_ref, pool_w_ref, wgu_ref, wd_ref, wkv_ref, bkv_ref,
                    hs_out_ref, kv_out_ref,
                    hext, pbuf, ymix, hs1, hb, yffn, gbuf, ubuf, act, kvin,
                    *, tm, tiles_per_batch, n_tiles, is_meta):
    g_mix_pre = _gain(gains_ref, G_MIX_PRE)
    step = pl.program_id(0)
    tile_rows = pl.ds(0, tm)

    def stages(slot):
        def io(rows):
            return pl.ds(rows.start + slot * tm, rows.size)

        def halo_rows(gate):
            if is_meta:
                hext[0:HALO, :] = jnp.zeros((HALO, D_MODEL), jnp.float32)
            elif slot == 1:
                hext[0:HALO, :] = _rms(x_ref[tm - HALO:tm, :], g_mix_pre)
            else:
                pair = jnp.minimum(step, n_tiles // 2 - 1)
                first_of_batch = ((2 * pair) % tiles_per_batch) == 0
                prev = jnp.where(first_of_batch, meta_ref[...], prev_ref[...])
                hext[0:HALO, :] = _rms(prev, g_mix_pre)

        def norm_rows(rows, gate):
            hext[pl.ds(rows.start + HALO, rows.size), :] = _rms(_gated(x_ref[io(rows), :], gate), g_mix_pre)

        def pool_rows(rows, gate):
            for gi, w in enumerate(POOL_WINDOWS):
                cols = slice(gi * POOL_GROUP_DIM, (gi + 1) * POOL_GROUP_DIM)
                e0 = _gated(hext[pl.ds(rows.start, rows.size + HALO), cols], gate)
                e = e0
                shift = 1
                while shift < w:
                    e = e + pltpu.roll(e, shift, 0)
                    shift *= 2
                win = e[HALO:, :]
                if is_meta:
                    pos = lax.broadcasted_iota(jnp.int32, win.shape, 0) + 1
                    cnt = jnp.minimum(pos, w).astype(jnp.float32)
                    pooled = win / cnt - e0[HALO:, :]
                else:
                    pooled = win * (1.0 / w) - e0[HALO:, :]
                pbuf[rows, cols] = pooled.astype(jnp.bfloat16)

        def group_matmul(gi, gate):
            cols = slice(gi * POOL_GROUP_DIM, (gi + 1) * POOL_GROUP_DIM)
            ymix[:, cols] = jnp.dot(pbuf[:, cols], pool_w_ref[gi], preferred_element_type=jnp.float32)

        def post_mix_rows(rows, gate):
            y = _gated(ymix[rows, :], gate)
            h1 = x_ref[io(rows), :] + _rms(y * _gain(gains_ref, G_POOL_SCALE), _gain(gains_ref, G_MIX_POST))
            hs1[slot, rows, :] = h1
            hb[slot, rows, :] = _rms(h1, _gain(gains_ref, G_FFN_PRE)).astype(jnp.bfloat16)

        mixer = ([halo_rows] + _row_items(tile_rows, ROW_CHUNK, norm_rows)
                 + _row_items(tile_rows, POOL_ROW_CHUNK, pool_rows)
                 + [functools.partial(group_matmul, gi) for gi in range(len(POOL_WINDOWS))]
                 + _row_items(tile_rows, ROW_CHUNK, post_mix_rows))

        def ffn(side_items):
            _swiglu(hb.at[slot], wgu_ref, wd_ref, gbuf, ubuf, act, yffn.at[slot], tile_rows, side_items)

        def post_ffn_rows(rows, gate):
            h2 = hs1[slot, rows, :] + _rms(_gated(yffn[slot, rows, :], gate), _gain(gains_ref, G_FFN_POST))
            hs_out_ref[io(rows), :] = h2
            kvin[rows, :] = _rms(h2, _gain(gains_ref, G_KV)).astype(jnp.bfloat16)

        def kv_projection(gate):
            kv = jnp.dot(kvin[...], wkv_ref[...], preferred_element_type=jnp.float32) + bkv_ref[...]
            kv_out_ref[io(tile_rows), :] = kv.astype(jnp.bfloat16)

        epilogue = _row_items(tile_rows, ROW_CHUNK, post_ffn_rows) + [kv_projection]
        return mixer, ffn, epilogue

    if n_tiles == 1:
        _run_single_tile(stages)
    else:
        _run_pipelined_pairs(step, n_tiles // 2, stages)


def _const_spec(shape):
    nd = len(shape)
    return pl.BlockSpec(shape, lambda *_: (0,) * nd, pipeline_mode=pl.Buffered(1))


def _layer_spec(stacked_shape, layer):
    _, rows, cols = stacked_shape
    return pl.BlockSpec((None, rows, cols), lambda *_: (layer, 0, 0), pipeline_mode=pl.Buffered(1))


def _layer_a_call(x2d, meta, gains, pool_w, wgu, wd, wkv, bkv, *, layer, tm, tiles_per_batch, is_meta):
    n_rows = x2d.shape[0]
    n_tiles = n_rows // tm
    tiles_per_step = 1 if n_tiles == 1 else 2
    block_rows = tiles_per_step * tm
    n_blocks = n_tiles // tiles_per_step
    n_steps = n_blocks if n_tiles == 1 else n_blocks + 1
    halo_blocks_per_block = block_rows // HALO
    kern = functools.partial(_layer_a_kernel, tm=tm, tiles_per_batch=tiles_per_batch, n_tiles=n_tiles,
                             is_meta=is_meta)

    def mixer_block(s):
        return jnp.minimum(s, n_blocks - 1)

    def epilogue_block(s):
        return jnp.maximum(s - (n_steps - n_blocks), 0)

    return pl.pallas_call(
        kern,
        grid=(n_steps,),
        in_specs=[
            pl.BlockSpec((block_rows, D_MODEL), lambda s: (mixer_block(s), 0)),
            pl.BlockSpec((HALO, D_MODEL), lambda s: (jnp.maximum(mixer_block(s) * halo_blocks_per_block - 1, 0), 0)),
            _const_spec((N_META, D_MODEL)),
            _const_spec(gains.shape),
            _const_spec(pool_w.shape),
            _layer_spec(wgu.shape, layer),
            _layer_spec(wd.shape, layer),
            _const_spec(wkv.shape),
            _const_spec(bkv.shape),
        ],
        out_specs=[
            pl.BlockSpec((block_rows, D_MODEL), lambda s: (epilogue_block(s), 0)),
            pl.BlockSpec((block_rows, 2 * KV_DIM), lambda s: (epilogue_block(s), 0)),
        ],
        out_shape=[
            jax.ShapeDtypeStruct((n_rows, D_MODEL), jnp.float32),
            jax.ShapeDtypeStruct((n_rows, 2 * KV_DIM), jnp.bfloat16),
        ],
        scratch_shapes=[
            pltpu.VMEM((tm + HALO, D_MODEL), jnp.float32),
            pltpu.VMEM((tm, D_MODEL), jnp.bfloat16),
            pltpu.VMEM((tm, D_MODEL), jnp.float32),
            pltpu.VMEM((2, tm, D_MODEL), jnp.float32),
            pltpu.VMEM((2, tm, D_MODEL), jnp.bfloat16),
            pltpu.VMEM((2, tm, D_MODEL), jnp.float32),
            pltpu.VMEM((tm, FFN_COLS), jnp.float32),
            pltpu.VMEM((tm, FFN_COLS), jnp.float32),
            pltpu.VMEM((tm, D_FF), jnp.bfloat16),
            pltpu.VMEM((tm, D_MODEL), jnp.bfloat16),
        ],
        compiler_params=pltpu.CompilerParams(
            dimension_semantics=("arbitrary",), vmem_limit_bytes=V7X_VMEM_LIMIT_BYTES),
        name="layer_a_meta" if is_meta else "layer_a",
    )(x2d, x2d, meta, gains, pool_w, wgu, wd, wkv, bkv)


def _bucket_distance_ranges():
    d = np.arange(WINDOW)
    max_exact = N_BUCKETS // 2
    df = np.maximum(d, 1).astype(np.float32)
    large = max_exact + (np.log(df / np.float32(max_exact)) / np.float32(math.log(MAX_DISTANCE / max_exact))
                         * np.float32(N_BUCKETS - max_exact)).astype(np.int32)
    bucket = np.where(d < max_exact, d, np.minimum(large, N_BUCKETS - 1))
    ranges = []
    for b in range(N_BUCKETS):
        members = d[bucket == b]
        if members.size:
            assert np.array_equal(members, np.arange(members[0], members[-1] + 1))
            ranges.append((int(members[0]), int(members[-1])))
        else:
            ranges.append(None)
    return ranges


def _bias_table_kernel(rel_bias_ref, out_ref):
    m = pl.program_id(0)
    q = lax.broadcasted_iota(jnp.int32, (BLOCK, 2 * BLOCK), 0)
    s = lax.broadcasted_iota(jnp.int32, (BLOCK, 2 * BLOCK), 1)
    d = q + BLOCK - s
    in_window = (d >= 0) & (d < WINDOW)
    valid = (in_window & (s >= PAD_FRONT), in_window)
    for half in range(2):
        h = 2 * m + half
        acc = jnp.zeros((BLOCK, 2 * BLOCK), jnp.float32)
        for b, distances in enumerate(_bucket_distance_ranges()):
            if distances is not None:
                lo, hi = distances
                acc = jnp.where((d >= lo) & (d <= hi), rel_bias_ref[b, h], acc)
        for jsel in range(2):
            out_ref[jsel, 0, :, half * 2 * BLOCK:(half + 1) * 2 * BLOCK] = jnp.where(valid[jsel], acc, -jnp.inf)


def _bias_table_call(rel_bias):
    return pl.pallas_call(
        _bias_table_kernel,
        grid=(N_HEAD_PAIRS,),
        in_specs=[
            pl.BlockSpec(memory_space=pltpu.SMEM),
        ],
        out_specs=pl.BlockSpec((2, 1, BLOCK, 4 * BLOCK), lambda m: (0, m, 0, 0)),
        out_shape=jax.ShapeDtypeStruct((2, N_HEAD_PAIRS, BLOCK, 4 * BLOCK), jnp.float32),
        compiler_params=pltpu.CompilerParams(dimension_semantics=("arbitrary",)),
        name="bias_table",
    )(rel_bias)


def _layer_b_kernel(hs_ref, kv_ref, metakv_ref, rowsum_cols_ref, bias_ref, sinks_ref, gains_ref,
                    wq_ref, bq_ref, wo_ref, bo_ref, wgu_ref, wd_ref,
                    out_ref,
                    hn, qbuf, obuf, yatt, hs3, hb, yffn, gbuf, ubuf, act, lbuf, mbuf, pbuf,
                    *, tm, tiles_per_batch, n_tiles):
    blocks_per_tile = tm // BLOCK
    step = pl.program_id(0)
    pair = jnp.minimum(step, n_tiles // 2 - 1)
    tile_rows = pl.ds(0, tm)

    lane = lax.broadcasted_iota(jnp.int32, (2 * BLOCK, 2 * HEAD_DIM), 1)
    low_half = lane < HEAD_DIM
    zero = jnp.zeros((2 * BLOCK, 2 * HEAD_DIM), jnp.bfloat16)
    low_half_q = lax.broadcasted_iota(jnp.int32, (BLOCK, 2 * HEAD_DIM), 1) < HEAD_DIM

    def per_kv_head_operands(pair_cols):
        swapped = jnp.concatenate([pair_cols[:, HEAD_DIM:], pair_cols[:, :HEAD_DIM]], axis=1)
        g0 = (jnp.where(low_half, pair_cols, zero), jnp.where(low_half, zero, swapped))
        g1 = (jnp.where(low_half, swapped, zero), jnp.where(low_half, zero, pair_cols))
        return g0, g1

    ones_top = rowsum_cols_ref[0:2 * BLOCK, :]
    ones_bottom = rowsum_cols_ref[2 * BLOCK:4 * BLOCK, :]

    def attention_block(slot, jb, gate):
        t = (2 * pair + slot) % tiles_per_batch
        j = t * blocks_per_tile + jb
        own = kv_ref[pl.ds(pl.multiple_of(j * BLOCK, BLOCK), BLOCK), :]
        prev_start = pl.multiple_of(jnp.maximum(j - 1, 0) * BLOCK, BLOCK)
        prev = jnp.where(j == 0, metakv_ref[...], kv_ref[pl.ds(prev_start, BLOCK), :])
        kvb = jnp.concatenate([prev, own], axis=0)
        k_ops = per_kv_head_operands(kvb[:, :KV_DIM])
        v_ops = per_kv_head_operands(kvb[:, KV_DIM:])
        jsel = jnp.minimum(j, 1)
        qrows = slice(jb * BLOCK, (jb + 1) * BLOCK)
        for g in range(N_KV_HEADS):
            u = jb * N_KV_HEADS + g
            pairs = range(g * PAIRS_PER_KV, (g + 1) * PAIRS_PER_KV)
            q4 = jnp.concatenate([qbuf[qrows, m * 2 * HEAD_DIM:(m + 1) * 2 * HEAD_DIM] for m in pairs], axis=0)
            k_op = jnp.concatenate(k_ops[g], axis=0)
            v_op = jnp.concatenate([jnp.concatenate([v_ops[g][0], ones_top], axis=1),
                                    jnp.concatenate([v_ops[g][1], ones_bottom], axis=1)], axis=0)
            logits = _gated(lax.dot_general(q4, k_op, (((1,), (1,)), ((), ())),
                                            preferred_element_type=jnp.float32) + bias_ref[jsel, g], gate)
            lbuf[u] = logits
            for r0 in range(0, ATT_ROWS, SOFTMAX_ROW_CHUNK):
                rows = slice(r0, r0 + SOFTMAX_ROW_CHUNK)
                for half in range(2):
                    sink = sinks_ref[2 * (pairs[0] + r0 // BLOCK) + half]
                    lg = logits[rows, half * 2 * BLOCK:(half + 1) * 2 * BLOCK]
                    mx = jnp.maximum(jnp.max(lg, axis=-1, keepdims=True), sink)
                    mbuf[u, rows, half * BLOCK:(half + 1) * BLOCK] = jnp.broadcast_to(mx, (SOFTMAX_ROW_CHUNK, BLOCK))
            for r0 in range(0, ATT_ROWS, SOFTMAX_ROW_CHUNK):
                rows = slice(r0, r0 + SOFTMAX_ROW_CHUNK)
                for half in range(2):
                    mrep = mbuf[u, rows, half * BLOCK:(half + 1) * BLOCK]
                    lg = lbuf[u, rows, half * 2 * BLOCK:(half + 1) * 2 * BLOCK]
                    p = jnp.exp(lg - jnp.concatenate([mrep, mrep], axis=1))
                    pbuf[u, rows, half * 2 * BLOCK:(half + 1) * 2 * BLOCK] = p.astype(jnp.bfloat16)
            oa = jnp.dot(pbuf[u], v_op, preferred_element_type=jnp.float32)
            for i, m in enumerate(pairs):
                rows = slice(i * BLOCK, (i + 1) * BLOCK)
                sink_gap = jnp.where(low_half_q, sinks_ref[2 * m] - mbuf[u, rows, 0:BLOCK],
                                     sinks_ref[2 * m + 1] - mbuf[u, rows, BLOCK:2 * BLOCK])
                denom = oa[rows, 2 * HEAD_DIM:] + jnp.exp(sink_gap)
                o = oa[rows, :2 * HEAD_DIM] * (1.0 / denom)
                obuf[qrows, m * 2 * HEAD_DIM:(m + 1) * 2 * HEAD_DIM] = o.astype(jnp.bfloat16)

    def stages(slot):
        def io(rows):
            return pl.ds(rows.start + slot * tm, rows.size)

        def norm_rows(rows, gate):
            h = _rms(_gated(hs_ref[io(rows), :], gate), _gain(gains_ref, G_MIX_PRE))
            hn[rows, :] = h.astype(jnp.bfloat16)

        def q_projection(gate):
            q = jnp.dot(hn[...], wq_ref[...], preferred_element_type=jnp.float32) + bq_ref[...]
            qbuf[...] = (q * (HEAD_DIM ** -0.5)).astype(jnp.bfloat16)

        def o_projection(gate):
            yatt[...] = jnp.dot(obuf[...], wo_ref[...], preferred_element_type=jnp.float32) + bo_ref[...]

        def post_mix_rows(rows, gate):
            h3 = hs_ref[io(rows), :] + _rms(_gated(yatt[rows, :], gate), _gain(gains_ref, G_MIX_POST))
            hs3[slot, rows, :] = h3
            hb[slot, rows, :] = _rms(h3, _gain(gains_ref, G_FFN_PRE)).astype(jnp.bfloat16)

        attention = (_row_items(tile_rows, ROW_CHUNK, norm_rows) + [q_projection]
                     + [functools.partial(attention_block, slot, jb) for jb in range(blocks_per_tile)]
                     + [o_projection] + _row_items(tile_rows, ROW_CHUNK, post_mix_rows))

        def ffn(side_items):
            _swiglu(hb.at[slot], wgu_ref, wd_ref, gbuf, ubuf, act, yffn.at[slot], tile_rows, side_items)

        def post_ffn_rows(rows, gate):
            y = _gated(yffn[slot, rows, :], gate)
            out_ref[io(rows), :] = hs3[slot, rows, :] + _rms(y, _gain(gains_ref, G_FFN_POST))

        return attention, ffn, _row_items(tile_rows, ROW_CHUNK, post_ffn_rows)

    _run_pipelined_pairs(step, n_tiles // 2, stages)


def _layer_b_call(hs, kv, metakv, bias, sinks, gains, wq, bq, wo, bo, wgu, wd, *, layer, tm):
    batch, seq, _ = hs.shape
    tiles_per_batch = seq // tm
    n_tiles = batch * tiles_per_batch
    n_att_units = (tm // BLOCK) * N_KV_HEADS
    lane_head = jnp.arange(2 * HEAD_DIM)[None, :] // HEAD_DIM
    row_head = jnp.arange(4 * BLOCK)[:, None] // (2 * BLOCK)
    rowsum_cols = (lane_head == row_head).astype(jnp.bfloat16)
    kern = functools.partial(_layer_b_kernel, tm=tm, tiles_per_batch=tiles_per_batch, n_tiles=n_tiles)

    n_pairs = n_tiles // 2
    pairs_per_batch = tiles_per_batch // 2

    def att_pair(s):
        pair = jnp.minimum(s, n_pairs - 1)
        return pair // pairs_per_batch, pair % pairs_per_batch

    def epilogue_pair(s):
        pair = jnp.maximum(s - 1, 0)
        return pair // pairs_per_batch, pair % pairs_per_batch

    return pl.pallas_call(
        kern,
        grid=(n_pairs + 1,),
        in_specs=[
            pl.BlockSpec((None, 2 * tm, D_MODEL), lambda s: (*att_pair(s), 0)),
            pl.BlockSpec((None, seq, 2 * KV_DIM), lambda s: (att_pair(s)[0], 0, 0), pipeline_mode=pl.Buffered(1)),
            _const_spec(metakv.shape),
            _const_spec(rowsum_cols.shape),
            _const_spec(bias.shape),
            pl.BlockSpec(memory_space=pltpu.SMEM),
            _const_spec(gains.shape),
            _const_spec(wq.shape),
            _const_spec(bq.shape),
            _const_spec(wo.shape),
            _const_spec(bo.shape),
            _layer_spec(wgu.shape, layer),
            _layer_spec(wd.shape, layer),
        ],
        out_specs=pl.BlockSpec((None, 2 * tm, D_MODEL), lambda s: (*epilogue_pair(s), 0)),
        out_shape=jax.ShapeDtypeStruct((batch, seq, D_MODEL), jnp.float32),
        scratch_shapes=[
            pltpu.VMEM((tm, D_MODEL), jnp.bfloat16),
            pltpu.VMEM((tm, D_MODEL), jnp.bfloat16),
            pltpu.VMEM((tm, D_MODEL), jnp.bfloat16),
            pltpu.VMEM((tm, D_MODEL), jnp.float32),
            pltpu.VMEM((2, tm, D_MODEL), jnp.float32),
            pltpu.VMEM((2, tm, D_MODEL), jnp.bfloat16),
            pltpu.VMEM((2, tm, D_MODEL), jnp.float32),
            pltpu.VMEM((tm, FFN_COLS), jnp.float32),
            pltpu.VMEM((tm, FFN_COLS), jnp.float32),
            pltpu.VMEM((tm, D_FF), jnp.bfloat16),
            pltpu.VMEM((n_att_units, ATT_ROWS, 4 * BLOCK), jnp.float32),
            pltpu.VMEM((n_att_units, ATT_ROWS, 2 * BLOCK), jnp.float32),
            pltpu.VMEM((n_att_units, ATT_ROWS, 4 * BLOCK), jnp.bfloat16),
        ],
        compiler_params=pltpu.CompilerParams(
            dimension_semantics=("arbitrary",), vmem_limit_bytes=V7X_VMEM_LIMIT_BYTES),
        name="layer_b",
    )(hs, kv, metakv, rowsum_cols, bias, sinks, gains, wq, bq, wo, bo, wgu, wd)


def _cast_kernel(w_ref, out_ref):
    out_ref[...] = w_ref[...].astype(jnp.bfloat16)


def _cast_weights(w, body, name):
    layers, rows, cols = w.shape
    w2d = w.reshape(layers * rows, cols)
    target = max(16, CAST_BLOCK_BYTES // (4 * cols) // 16 * 16)
    block_rows = next(r for r in range(target, 0, -16) if (layers * rows) % r == 0)
    spec = pl.BlockSpec((block_rows, cols), lambda i: (i, 0))
    out = pl.pallas_call(
        body,
        grid=(layers * rows // block_rows,),
        in_specs=[spec],
        out_specs=spec,
        out_shape=jax.ShapeDtypeStruct(w2d.shape, jnp.bfloat16),
        compiler_params=pltpu.CompilerParams(dimension_semantics=("arbitrary",)),
        name=name,
    )(w2d)
    return out.reshape(layers, rows, cols)


def _gain_table(rows):
    zero = jnp.zeros((D_MODEL,), jnp.float32)
    return jnp.stack([rows[i].astype(jnp.float32) if i in rows else zero for i in range(8)])


def kernel(x, meta_tokens, norm_mix_pre, norm_mix_post, norm_ffn_pre, norm_ffn_post, pool_w, pool_scale, kv_norm, w_k, b_k, w_v, b_v, w_q, b_q, w_o, b_o, sinks, rel_bias, w_gate_up, w_down):
    batch, seq, _ = x.shape
    bf16 = jnp.bfloat16

    gains_a = _gain_table({G_MIX_PRE: norm_mix_pre[0], G_MIX_POST: norm_mix_post[0], G_POOL_SCALE: pool_scale[0],
                           G_FFN_PRE: norm_ffn_pre[0], G_FFN_POST: norm_ffn_post[0], G_KV: kv_norm})
    gains_b = _gain_table({G_MIX_PRE: norm_mix_pre[1], G_MIX_POST: norm_mix_post[1],
                           G_FFN_PRE: norm_ffn_pre[1], G_FFN_POST: norm_ffn_post[1]})
    wkv = jnp.concatenate([w_k, w_v], axis=1).astype(bf16)
    bkv = jnp.concatenate([b_k, b_v])[None, :]
    wgu = _cast_weights(w_gate_up, _cast_kernel, "cast_gate_up")
    wd = _cast_weights(w_down, _cast_kernel, "cast_down")
    layer_a_weights = (gains_a, pool_w[0].astype(bf16), wgu, wd, wkv, bkv)

    x2d = x.reshape(batch * seq, D_MODEL)
    hs2, kv_x = _layer_a_call(x2d, meta_tokens, *layer_a_weights,
                              layer=0, tm=TILE_A, tiles_per_batch=seq // TILE_A, is_meta=False)
    _, kv_meta = _layer_a_call(meta_tokens, meta_tokens, *layer_a_weights,
                               layer=0, tm=N_META, tiles_per_batch=1, is_meta=True)
    metakv = jnp.concatenate([jnp.zeros((PAD_FRONT, 2 * KV_DIM), bf16), kv_meta], axis=0)

    bias = _bias_table_call(rel_bias)
    bias = bias.reshape(2, N_KV_HEADS, ATT_ROWS, 4 * BLOCK)

    out = _layer_b_call(hs2.reshape(batch, seq, D_MODEL), kv_x.reshape(batch, seq, 2 * KV_DIM), metakv, bias,
                        sinks[0], gains_b, w_q[0].astype(bf16), b_q, w_o[0].astype(bf16), b_o,
                        wgu, wd, layer=1, tm=TILE_B)
    return out
```

```python
import functools
import math

import numpy as np

import jax
import jax.numpy as jnp
from jax import lax
from jax.experimental import pallas as pl
from jax.experimental.pallas import tpu as pltpu

D_MODEL = 1024
N_META = 16
POOL_WINDOWS = (2, 4, 8, 16)
POOL_GROUP_DIM = D_MODEL // len(POOL_WINDOWS)
HEAD_DIM = 64
N_Q_HEADS = D_MODEL // HEAD_DIM
N_KV_HEADS = 2
HEADS_PER_KV = N_Q_HEADS // N_KV_HEADS
PAIRS_PER_KV = HEADS_PER_KV // 2
N_HEAD_PAIRS = N_Q_HEADS // 2
KV_DIM = N_KV_HEADS * HEAD_DIM
WINDOW = 128
BLOCK = 128
N_BUCKETS = 32
MAX_DISTANCE = 128
D_FF = 2816
EPS = 1e-6
PAD_FRONT = (-N_META) % BLOCK

LANES = 128
GATE_ROWS = 8
FFN_COLS = 256
SILU_ROW_CHUNK = 32
UNRIDDEN_SEGMENTS = 1
CAST_BLOCK_BYTES = 6 * 1024 * 1024
HALO = 16
ROW_CHUNK = 32
POOL_ROW_CHUNK = 64
ATT_ROWS = PAIRS_PER_KV * BLOCK
SOFTMAX_ROW_CHUNK = 16
TILE_A = 256
TILE_B = 256
V7X_VMEM_LIMIT_BYTES = 63 * 1024 * 1024

G_MIX_PRE, G_MIX_POST, G_POOL_SCALE, G_FFN_PRE, G_FFN_POST, G_KV = range(6)


def _rms(x, g):
    ms = jnp.sum(x * x, axis=-1, keepdims=True) * (1.0 / D_MODEL)
    return x * lax.rsqrt(ms + EPS) * g


def _gain(gains_ref, i):
    return gains_ref[i:i + 1, :]


def _row_items(rows, chunk, body):
    chunk = min(chunk, rows.size)
    return [functools.partial(body, pl.ds(rows.start + i * chunk, chunk)) for i in range(rows.size // chunk)]


def _run_items(items, gate=None):
    for item in items:
        item(gate)


def _gate_of(value):
    bits = pltpu.bitcast(value[-GATE_ROWS:, -LANES:], jnp.uint32)
    return pltpu.bitcast((bits >> 16) >> 16, jnp.float32)


def _gated(x, gate):
    if gate is None:
        return x
    z = jnp.concatenate([gate] * (x.shape[0] // GATE_ROWS), axis=0)
    return jnp.concatenate([x[:, :LANES] + z, x[:, LANES:]], axis=1)


def _spread(items, n_bins):
    bins = [[] for _ in range(n_bins)]
    for k, item in enumerate(items):
        bins[k * n_bins // max(len(items), 1)].append(item)
    return bins


def _swiglu(h_ref, wgu_ref, wd_ref, gbuf, ubuf, act, out_ref, rows, side_items=()):
    segments = ([("gate_up", c) for c in range(0, D_FF, FFN_COLS)]
                + [("down", c) for c in range(0, D_MODEL, FFN_COLS)])
    side = _spread(list(side_items), len(segments) - UNRIDDEN_SEGMENTS) + [[]] * UNRIDDEN_SEGMENTS
    for (kind, c0), side_group in zip(segments, side):
        if kind == "gate_up":
            g = jnp.dot(h_ref[rows, :], wgu_ref[:, c0:c0 + FFN_COLS], preferred_element_type=jnp.float32)
            gbuf[rows, :] = g
            ubuf[rows, :] = jnp.dot(h_ref[rows, :], wgu_ref[:, D_FF + c0:D_FF + c0 + FFN_COLS],
                                    preferred_element_type=jnp.float32)
            gate_source = g

            def silu_rows(r, gate, c0=c0):
                half_g = 0.5 * gbuf[r, :]
                silu = half_g + half_g * jnp.tanh(half_g)
                act[r, c0:c0 + FFN_COLS] = (silu * ubuf[r, :]).astype(jnp.bfloat16)

            _run_items(_row_items(rows, SILU_ROW_CHUNK, silu_rows))
        else:
            y = jnp.dot(act[rows, :], wd_ref[:, c0:c0 + FFN_COLS], preferred_element_type=jnp.float32)
            out_ref[rows, c0:c0 + FFN_COLS] = y
            gate_source = y
        if side_group:
            _run_items(side_group, _gate_of(gate_source))


def _run_single_tile(stages):
    mixer, ffn, epilogue = stages(0)
    _run_items(mixer)
    ffn(())
    _run_items(epilogue)


def _run_pipelined_pairs(step, n_pairs, stages):
    even_mixer, even_ffn, even_epilogue = stages(0)
    odd_mixer, odd_ffn, odd_epilogue = stages(1)

    @pl.when(step == 0)
    def _():
        _run_items(even_mixer)
        even_ffn(odd_mixer)

    @pl.when((step > 0) & (step < n_pairs))
    def _():
        odd_ffn(even_epilogue + even_mixer)
        even_ffn(odd_epilogue + odd_mixer)

    @pl.when(step == n_pairs)
    def _():
        odd_ffn(even_epilogue)
        _run_items(odd_epilogue)


def _layer_a_kernel(x_ref, prev_ref, meta_ref, gains_ref, pool_w_ref, wgu_ref, wd_ref, wkv_ref, bkv_ref,
                    hs_out_ref, kv_out_ref,
                    hext, pbuf, ymix, hs1, hb, yffn, gbuf, ubuf, act, kvin,
                    *, tm, tiles_per_batch, n_tiles, is_meta):
    g_mix_pre = _gain(gains_ref, G_MIX_PRE)
    step = pl.program_id(0)
    tile_rows = pl.ds(0, tm)

    def stages(slot):
        def io(rows):
            return pl.ds(rows.start + slot * tm, rows.size)

        def halo_rows(gate):
            if is_meta:
                hext[0:HALO, :] = jnp.zeros((HALO, D_MODEL), jnp.float32)
            elif slot == 1:
                hext[0:HALO, :] = _rms(x_ref[tm - HALO:tm, :], g_mix_pre)
            else:
                pair = jnp.minimum(step, n_tiles // 2 - 1)
                first_of_batch = ((2 * pair) % tiles_per_batch) == 0
                prev = jnp.where(first_of_batch, meta_ref[...], prev_ref[...])
                hext[0:HALO, :] = _rms(prev, g_mix_pre)

        def norm_rows(rows, gate):
            hext[pl.ds(rows.start + HALO, rows.size), :] = _rms(_gated(x_ref[io(rows), :], gate), g_mix_pre)

        def pool_rows(rows, gate):
            for gi, w in enumerate(POOL_WINDOWS):
                cols = slice(gi * POOL_GROUP_DIM, (gi + 1) * POOL_GROUP_DIM)
                e0 = _gated(hext[pl.ds(rows.start, rows.size + HALO), cols], gate)
                e = e0
                shift = 1
                while shift < w:
                    e = e + pltpu.roll(e, shift, 0)
                    shift *= 2
                win = e[HALO:, :]
                if is_meta:
                    pos = lax.broadcasted_iota(jnp.int32, win.shape, 0) + 1
                    cnt = jnp.minimum(pos, w).astype(jnp.float32)
                    pooled = win / cnt - e0[HALO:, :]
                else:
                    pooled = win * (1.0 / w) - e0[HALO:, :]
                pbuf[rows, cols] = pooled.astype(jnp.bfloat16)

        def group_matmul(gi, gate):
            cols = slice(gi * POOL_GROUP_DIM, (gi + 1) * POOL_GROUP_DIM)
            ymix[:, cols] = jnp.dot(pbuf[:, cols], pool_w_ref[gi], preferred_element_type=jnp.float32)

        def post_mix_rows(rows, gate):
            y = _gated(ymix[rows, :], gate)
            h1 = x_ref[io(rows), :] + _rms(y * _gain(gains_ref, G_POOL_SCALE), _gain(gains_ref, G_MIX_POST))
            hs1[slot, rows, :] = h1
            hb[slot, rows, :] = _rms(h1, _gain(gains_ref, G_FFN_PRE)).astype(jnp.bfloat16)

        mixer = ([halo_rows] + _row_items(tile_rows, ROW_CHUNK, norm_rows)
                 + _row_items(tile_rows, POOL_ROW_CHUNK, pool_rows)
                 + [functools.partial(group_matmul, gi) for gi in range(len(POOL_WINDOWS))]
                 + _row_items(tile_rows, ROW_CHUNK, post_mix_rows))

        def ffn(side_items):
            _swiglu(hb.at[slot], wgu_ref, wd_ref, gbuf, ubuf, act, yffn.at[slot], tile_rows, side_items)

        def post_ffn_rows(rows, gate):
            h2 = hs1[slot, rows, :] + _rms(_gated(yffn[slot, rows, :], gate), _gain(gains_ref, G_FFN_POST))
            hs_out_ref[io(rows), :] = h2
            kvin[rows, :] = _rms(h2, _gain(gains_ref, G_KV)).astype(jnp.bfloat16)

        def kv_projection(gate):
            kv = jnp.dot(kvin[...], wkv_ref[...], preferred_element_type=jnp.float32) + bkv_ref[...]
            kv_out_ref[io(tile_rows), :] = kv.astype(jnp.bfloat16)

        epilogue = _row_items(tile_rows, ROW_CHUNK, post_ffn_rows) + [kv_projection]
        return mixer, ffn, epilogue

    if n_tiles == 1:
        _run_single_tile(stages)
    else:
        _run_pipelined_pairs(step, n_tiles // 2, stages)


def _const_spec(shape):
    nd = len(shape)
    return pl.BlockSpec(shape, lambda *_: (0,) * nd, pipeline_mode=pl.Buffered(1))


def _layer_spec(stacked_shape, layer):
    _, rows, cols = stacked_shape
    return pl.BlockSpec((None, rows, cols), lambda *_: (layer, 0, 0), pipeline_mode=pl.Buffered(1))


def _layer_a_call(x2d, meta, gains, pool_w, wgu, wd, wkv, bkv, *, layer, tm, tiles_per_batch, is_meta):
    n_rows = x2d.shape[0]
    n_tiles = n_rows // tm
    tiles_per_step = 1 if n_tiles == 1 else 2
    block_rows = tiles_per_step * tm
    n_blocks = n_tiles // tiles_per_step
    n_steps = n_blocks if n_tiles == 1 else n_blocks + 1
    halo_blocks_per_block = block_rows // HALO
    kern = functools.partial(_layer_a_kernel, tm=tm, tiles_per_batch=tiles_per_batch, n_tiles=n_tiles,
                             is_meta=is_meta)

    def mixer_block(s):
        return jnp.minimum(s, n_blocks - 1)

    def epilogue_block(s):
        return jnp.maximum(s - (n_steps - n_blocks), 0)

    return pl.pallas_call(
        kern,
        grid=(n_steps,),
        in_specs=[
            pl.BlockSpec((block_rows, D_MODEL), lambda s: (mixer_block(s), 0)),
            pl.BlockSpec((HALO, D_MODEL), lambda s: (jnp.maximum(mixer_block(s) * halo_blocks_per_block - 1, 0), 0)),
            _const_spec((N_META, D_MODEL)),
            _const_spec(gains.shape),
            _const_spec(pool_w.shape),
            _layer_spec(wgu.shape, layer),
            _layer_spec(wd.shape, layer),
            _const_spec(wkv.shape),
            _const_spec(bkv.shape),
        ],
        out_specs=[
            pl.BlockSpec((block_rows, D_MODEL), lambda s: (epilogue_block(s), 0)),
            pl.BlockSpec((block_rows, 2 * KV_DIM), lambda s: (epilogue_block(s), 0)),
        ],
        out_shape=[
            jax.ShapeDtypeStruct((n_rows, D_MODEL), jnp.float32),
            jax.ShapeDtypeStruct((n_rows, 2 * KV_DIM), jnp.bfloat16),
        ],
        scratch_shapes=[
            pltpu.VMEM((tm + HALO, D_MODEL), jnp.float32),
            pltpu.VMEM((tm, D_MODEL), jnp.bfloat16),
            pltpu.VMEM((tm, D_MODEL), jnp.float32),
            pltpu.VMEM((2, tm, D_MODEL), jnp.float32),
            pltpu.VMEM((2, tm, D_MODEL), jnp.bfloat16),
            pltpu.VMEM((2, tm, D_MODEL), jnp.float32),
            pltpu.VMEM((tm, FFN_COLS), jnp.float32),
            pltpu.VMEM((tm, FFN_COLS), jnp.float32),
            pltpu.VMEM((tm, D_FF), jnp.bfloat16),
            pltpu.VMEM((tm, D_MODEL), jnp.bfloat16),
        ],
        compiler_params=pltpu.CompilerParams(
            dimension_semantics=("arbitrary",), vmem_limit_bytes=V7X_VMEM_LIMIT_BYTES),
        name="layer_a_meta" if is_meta else "layer_a",
    )(x2d, x2d, meta, gains, pool_w, wgu, wd, wkv, bkv)


def _bucket_distance_ranges():
    d = np.arange(WINDOW)
    max_exact = N_BUCKETS // 2
    df = np.maximum(d, 1).astype(np.float32)
    large = max_exact + (np.log(df / np.float32(max_exact)) / np.float32(math.log(MAX_DISTANCE / max_exact))
                         * np.float32(N_BUCKETS - max_exact)).astype(np.int32)
    bucket = np.where(d < max_exact, d, np.minimum(large, N_BUCKETS - 1))
    ranges = []
    for b in range(N_BUCKETS):
        members = d[bucket == b]
        if members.size:
            assert np.array_equal(members, np.arange(members[0], members[-1] + 1))
            ranges.append((int(members[0]), int(members[-1])))
        else:
            ranges.append(None)
    return ranges


def _bias_table_kernel(rel_bias_ref, out_ref):
    m = pl.program_id(0)
    q = lax.broadcasted_iota(jnp.int32, (BLOCK, 2 * BLOCK), 0)
    s = lax.broadcasted_iota(jnp.int32, (BLOCK, 2 * BLOCK), 1)
    d = q + BLOCK - s
    in_window = (d >= 0) & (d < WINDOW)
    valid = (in_window & (s >= PAD_FRONT), in_window)
    for half in range(2):
        h = 2 * m + half
        acc = jnp.zeros((BLOCK, 2 * BLOCK), jnp.float32)
        for b, distances in enumerate(_bucket_distance_ranges()):
            if distances is not None:
                lo, hi = distances
                acc = jnp.where((d >= lo) & (d <= hi), rel_bias_ref[b, h], acc)
        for jsel in range(2):
            out_ref[jsel, 0, :, half * 2 * BLOCK:(half + 1) * 2 * BLOCK] = jnp.where(valid[jsel], acc, -jnp.inf)


def _bias_table_call(rel_bias):
    return pl.pallas_call(
        _bias_table_kernel,
        grid=(N_HEAD_PAIRS,),
        in_specs=[
            pl.BlockSpec(memory_space=pltpu.SMEM),
        ],
        out_specs=pl.BlockSpec((2, 1, BLOCK, 4 * BLOCK), lambda m: (0, m, 0, 0)),
        out_shape=jax.ShapeDtypeStruct((2, N_HEAD_PAIRS, BLOCK, 4 * BLOCK), jnp.float32),
        compiler_params=pltpu.CompilerParams(dimension_semantics=("arbitrary",)),
        name="bias_table",
    )(rel_bias)


def _layer_b_kernel(hs_ref, kv_ref, metakv_ref, rowsum_cols_ref, bias_ref, sinks_ref, gains_ref,
                    wq_ref, bq_ref, wo_ref, bo_ref, wgu_ref, wd_ref,
                    out_ref,
                    hn, qbuf, obuf, yatt, hs3, hb, yffn, gbuf, ubuf, act, lbuf, mbuf, pbuf,
                    *, tm, tiles_per_batch, n_tiles):
    blocks_per_tile = tm // BLOCK
    step = pl.program_id(0)
    pair = jnp.minimum(step, n_tiles // 2 - 1)
    tile_rows = pl.ds(0, tm)

    lane = lax.broadcasted_iota(jnp.int32, (2 * BLOCK, 2 * HEAD_DIM), 1)
    low_half = lane < HEAD_DIM
    zero = jnp.zeros((2 * BLOCK, 2 * HEAD_DIM), jnp.bfloat16)
    low_half_q = lax.broadcasted_iota(jnp.int32, (BLOCK, 2 * HEAD_DIM), 1) < HEAD_DIM

    def per_kv_head_operands(pair_cols):
        swapped = jnp.concatenate([pair_cols[:, HEAD_DIM:], pair_cols[:, :HEAD_DIM]], axis=1)
        g0 = (jnp.where(low_half, pair_cols, zero), jnp.where(low_half, zero, swapped))
        g1 = (jnp.where(low_half, swapped, zero), jnp.where(low_half, zero, pair_cols))
        return g0, g1

    ones_top = rowsum_cols_ref[0:2 * BLOCK, :]
    ones_bottom = rowsum_cols_ref[2 * BLOCK:4 * BLOCK, :]

    def attention_block(slot, jb, gate):
        t = (2 * pair + slot) % tiles_per_batch
        j = t * blocks_per_tile + jb
        own = kv_ref[pl.ds(pl.multiple_of(j * BLOCK, BLOCK), BLOCK), :]
        prev_start = pl.multiple_of(jnp.maximum(j - 1, 0) * BLOCK, BLOCK)
        prev = jnp.where(j == 0, metakv_ref[...], kv_ref[pl.ds(prev_start, BLOCK), :])
        kvb = jnp.concatenate([prev, own], axis=0)
        k_ops = per_kv_head_operands(kvb[:, :KV_DIM])
        v_ops = per_kv_head_operands(kvb[:, KV_DIM:])
        jsel = jnp.minimum(j, 1)
        qrows = slice(jb * BLOCK, (jb + 1) * BLOCK)
        for g in range(N_KV_HEADS):
            u = jb * N_KV_HEADS + g
            pairs = range(g * PAIRS_PER_KV, (g + 1) * PAIRS_PER_KV)
            q4 = jnp.concatenate([qbuf[qrows, m * 2 * HEAD_DIM:(m + 1) * 2 * HEAD_DIM] for m in pairs], axis=0)
            k_op = jnp.concatenate(k_ops[g], axis=0)
            v_op = jnp.concatenate([jnp.concatenate([v_ops[g][0], ones_top], axis=1),
                                    jnp.concatenate([v_ops[g][1], ones_bottom], axis=1)], axis=0)
            logits = _gated(lax.dot_general(q4, k_op, (((1,), (1,)), ((), ())),
                                            preferred_element_type=jnp.float32) + bias_ref[jsel, g], gate)
            lbuf[u] = logits
            for r0 in range(0, ATT_ROWS, SOFTMAX_ROW_CHUNK):
                rows = slice(r0, r0 + SOFTMAX_ROW_CHUNK)
                for half in range(2):
                    sink = sinks_ref[2 * (pairs[0] + r0 // BLOCK) + half]
                    lg = logits[rows, half * 2 * BLOCK:(half + 1) * 2 * BLOCK]
                    mx = jnp.maximum(jnp.max(lg, axis=-1, keepdims=True), sink)
                    mbuf[u, rows, half * BLOCK:(half + 1) * BLOCK] = jnp.broadcast_to(mx, (SOFTMAX_ROW_CHUNK, BLOCK))
            for r0 in range(0, ATT_ROWS, SOFTMAX_ROW_CHUNK):
                rows = slice(r0, r0 + SOFTMAX_ROW_CHUNK)
                for half in range(2):
                    mrep = mbuf[u, rows, half * BLOCK:(half + 1) * BLOCK]
                    lg = lbuf[u, rows, half * 2 * BLOCK:(half + 1) * 2 * BLOCK]
                    p = jnp.exp(lg - jnp.concatenate([mrep, mrep], axis=1))
                    pbuf[u, rows, half * 2 * BLOCK:(half + 1) * 2 * BLOCK] = p.astype(jnp.bfloat16)
            oa = jnp.dot(pbuf[u], v_op, preferred_element_type=jnp.float32)
            for i, m in enumerate(pairs):
                rows = slice(i * BLOCK, (i + 1) * BLOCK)
                sink_gap = jnp.where(low_half_q, sinks_ref[2 * m] - mbuf[u, rows, 0:BLOCK],
                                     sinks_ref[2 * m + 1] - mbuf[u, rows, BLOCK:2 * BLOCK])
                denom = oa[rows, 2 * HEAD_DIM:] + jnp.exp(sink_gap)
                o = oa[rows, :2 * HEAD_DIM] * (1.0 / denom)
                obuf[qrows, m * 2 * HEAD_DIM:(m + 1) * 2 * HEAD_DIM] = o.astype(jnp.bfloat16)

    def stages(slot):
        def io(rows):
            return pl.ds(rows.start + slot * tm, rows.size)

        def norm_rows(rows, gate):
            h = _rms(_gated(hs_ref[io(rows), :], gate), _gain(gains_ref, G_MIX_PRE))
            hn[rows, :] = h.astype(jnp.bfloat16)

        def q_projection(gate):
            q = jnp.dot(hn[...], wq_ref[...], preferred_element_type=jnp.float32) + bq_ref[...]
            qbuf[...] = (q * (HEAD_DIM ** -0.5)).astype(jnp.bfloat16)

        def o_projection(gate):
            yatt[...] = jnp.dot(obuf[...], wo_ref[...], preferred_element_type=jnp.float32) + bo_ref[...]

        def post_mix_rows(rows, gate):
            h3 = hs_ref[io(rows), :] + _rms(_gated(yatt[rows, :], gate), _gain(gains_ref, G_MIX_POST))
            hs3[slot, rows, :] = h3
            hb[slot, rows, :] = _rms(h3, _gain(gains_ref, G_FFN_PRE)).astype(jnp.bfloat16)

        attention = (_row_items(tile_rows, ROW_CHUNK, norm_rows) + [q_projection]
                     + [functools.partial(attention_block, slot, jb) for jb in range(blocks_per_tile)]
                     + [o_projection] + _row_items(tile_rows, ROW_CHUNK, post_mix_rows))

        def ffn(side_items):
            _swiglu(hb.at[slot], wgu_ref, wd_ref, gbuf, ubuf, act, yffn.at[slot], tile_rows, side_items)

        def post_ffn_rows(rows, gate):
            y = _gated(yffn[slot, rows, :], gate)
            out_ref[io(rows), :] = hs3[slot, rows, :] + _rms(y, _gain(gains_ref, G_FFN_POST))

        return attention, ffn, _row_items(tile_rows, ROW_CHUNK, post_ffn_rows)

    _run_pipelined_pairs(step, n_tiles // 2, stages)


def _layer_b_call(hs, kv, metakv, bias, sinks, gains, wq, bq, wo, bo, wgu, wd, *, layer, tm):
    batch, seq, _ = hs.shape
    tiles_per_batch = seq // tm
    n_tiles = batch * tiles_per_batch
    n_att_units = (tm // BLOCK) * N_KV_HEADS
    lane_head = jnp.arange(2 * HEAD_DIM)[None, :] // HEAD_DIM
    row_head = jnp.arange(4 * BLOCK)[:, None] // (2 * BLOCK)
    rowsum_cols = (lane_head == row_head).astype(jnp.bfloat16)
    kern = functools.partial(_layer_b_kernel, tm=tm, tiles_per_batch=tiles_per_batch, n_tiles=n_tiles)

    n_pairs = n_tiles // 2
    pairs_per_batch = tiles_per_batch // 2

    def att_pair(s):
        pair = jnp.minimum(s, n_pairs - 1)
        return pair // pairs_per_batch, pair % pairs_per_batch

    def epilogue_pair(s):
        pair = jnp.maximum(s - 1, 0)
        return pair // pairs_per_batch, pair % pairs_per_batch

    return pl.pallas_call(
        kern,
        grid=(n_pairs + 1,),
        in_specs=[
            pl.BlockSpec((None, 2 * tm, D_MODEL), lambda s: (*att_pair(s), 0)),
            pl.BlockSpec((None, seq, 2 * KV_DIM), lambda s: (att_pair(s)[0], 0, 0), pipeline_mode=pl.Buffered(1)),
            _const_spec(metakv.shape),
            _const_spec(rowsum_cols.shape),
            _const_spec(bias.shape),
            pl.BlockSpec(memory_space=pltpu.SMEM),
            _const_spec(gains.shape),
            _const_spec(wq.shape),
            _const_spec(bq.shape),
            _const_spec(wo.shape),
            _const_spec(bo.shape),
            _layer_spec(wgu.shape, layer),
            _layer_spec(wd.shape, layer),
        ],
        out_specs=pl.BlockSpec((None, 2 * tm, D_MODEL), lambda s: (*epilogue_pair(s), 0)),
        out_shape=jax.ShapeDtypeStruct((batch, seq, D_MODEL), jnp.float32),
        scratch_shapes=[
            pltpu.VMEM((tm, D_MODEL), jnp.bfloat16),
            pltpu.VMEM((tm, D_MODEL), jnp.bfloat16),
            pltpu.VMEM((tm, D_MODEL), jnp.bfloat16),
            pltpu.VMEM((tm, D_MODEL), jnp.float32),
            pltpu.VMEM((2, tm, D_MODEL), jnp.float32),
            pltpu.VMEM((2, tm, D_MODEL), jnp.bfloat16),
            pltpu.VMEM((2, tm, D_MODEL), jnp.float32),
            pltpu.VMEM((tm, FFN_COLS), jnp.float32),
            pltpu.VMEM((tm, FFN_COLS), jnp.float32),
            pltpu.VMEM((tm, D_FF), jnp.bfloat16),
            pltpu.VMEM((n_att_units, ATT_ROWS, 4 * BLOCK), jnp.float32),
            pltpu.VMEM((n_att_units, ATT_ROWS, 2 * BLOCK), jnp.float32),
            pltpu.VMEM((n_att_units, ATT_ROWS, 4 * BLOCK), jnp.bfloat16),
        ],
        compiler_params=pltpu.CompilerParams(
            dimension_semantics=("arbitrary",), vmem_limit_bytes=V7X_VMEM_LIMIT_BYTES),
        name="layer_b",
    )(hs, kv, metakv, rowsum_cols, bias, sinks, gains, wq, bq, wo, bo, wgu, wd)


def _cast_kernel(w_ref, out_ref):
    out_ref[...] = w_ref[...].astype(jnp.bfloat16)


def _cast_weights(w, body, name):
    layers, rows, cols = w.shape
    w2d = w.reshape(layers * rows, cols)
    target = max(16, CAST_BLOCK_BYTES // (4 * cols) // 16 * 16)
    block_rows = next(r for r in range(target, 0, -16) if (layers * rows) % r == 0)
    spec = pl.BlockSpec((block_rows, cols), lambda i: (i, 0))
    out = pl.pallas_call(
        body,
        grid=(layers * rows // block_rows,),
        in_specs=[spec],
        out_specs=spec,
        out_shape=jax.ShapeDtypeStruct(w2d.shape, jnp.bfloat16),
        compiler_params=pltpu.CompilerParams(dimension_semantics=("arbitrary",)),
        name=name,
    )(w2d)
    return out.reshape(layers, rows, cols)


def _gain_table(rows):
    zero = jnp.zeros((D_MODEL,), jnp.float32)
    return jnp.stack([rows[i].astype(jnp.float32) if i in rows else zero for i in range(8)])


def kernel(x, meta_tokens, norm_mix_pre, norm_mix_post, norm_ffn_pre, norm_ffn_post, pool_w, pool_scale, kv_norm, w_k, b_k, w_v, b_v, w_q, b_q, w_o, b_o, sinks, rel_bias, w_gate_up, w_down):
    batch, seq, _ = x.shape
    bf16 = jnp.bfloat16

    gains_a = _gain_table({G_MIX_PRE: norm_mix_pre[0], G_MIX_POST: norm_mix_post[0], G_POOL_SCALE: pool_scale[0],
                           G_FFN_PRE: norm_ffn_pre[0], G_FFN_POST: norm_ffn_post[0], G_KV: kv_norm})
    gains_b = _gain_table({G_MIX_PRE: norm_mix_pre[1], G_MIX_POST: norm_mix_post[1],
                           G_FFN_PRE: norm_ffn_pre[1], G_FFN_POST: norm_ffn_post[1]})
    wkv = jnp.concatenate([w_k, w_v], axis=1).astype(bf16)
    bkv = jnp.concatenate([b_k, b_v])[None, :]
    wgu = _cast_weights(w_gate_up, _cast_kernel, "cast_gate_up")
    wd = _cast_weights(w_down, _cast_kernel, "cast_down")
    layer_a_weights = (gains_a, pool_w[0].astype(bf16), wgu, wd, wkv, bkv)

    x2d = x.reshape(batch * seq, D_MODEL)
    hs2, kv_x = _layer_a_call(x2d, meta_tokens, *layer_a_weights,
                              layer=0, tm=TILE_A, tiles_per_batch=seq // TILE_A, is_meta=False)
    _, kv_meta = _layer_a_call(meta_tokens, meta_tokens, *layer_a_weights,
                               layer=0, tm=N_META, tiles_per_batch=1, is_meta=True)
    metakv = jnp.concatenate([jnp.zeros((PAD_FRONT, 2 * KV_DIM), bf16), kv_meta], axis=0)

    bias = _bias_table_call(rel_bias)
    bias = bias.reshape(2, N_KV_HEADS, ATT_ROWS, 4 * BLOCK)

    out = _layer_b_call(hs2.reshape(batch, seq, D_MODEL), kv_x.reshape(batch, seq, 2 * KV_DIM), metakv, bias,
                        sinks[0], gains_b, _cast_weights(w_q, _cast_kernel, "cast_q")[0], b_q,
                        _cast_weights(w_o, _cast_kernel, "cast_o")[0], b_o,
                        wgu, wd, layer=1, tm=TILE_B)
    return out
```

```python
import functools
import math

import numpy as np

import jax
import jax.numpy as jnp
from jax import lax
from jax.experimental import pallas as pl
from jax.experimental.pallas import tpu as pltpu

D_MODEL = 1024
N_META = 16
POOL_WINDOWS = (2, 4, 8, 16)
POOL_GROUP_DIM = D_MODEL // len(POOL_WINDOWS)
HEAD_DIM = 64
N_Q_HEADS = D_MODEL // HEAD_DIM
N_KV_HEADS = 2
HEADS_PER_KV = N_Q_HEADS // N_KV_HEADS
PAIRS_PER_KV = HEADS_PER_KV // 2
N_HEAD_PAIRS = N_Q_HEADS // 2
KV_DIM = N_KV_HEADS * HEAD_DIM
WINDOW = 128
BLOCK = 128
N_BUCKETS = 32
MAX_DISTANCE = 128
D_FF = 2816
EPS = 1e-6
PAD_FRONT = (-N_META) % BLOCK

LANES = 128
GATE_ROWS = 8
FFN_COLS = 256
SILU_ROW_CHUNK = 32
UNRIDDEN_SEGMENTS = 1
CAST_BLOCK_BYTES = 6 * 1024 * 1024
HALO = 16
ROW_CHUNK = 32
POOL_ROW_CHUNK = 64
ATT_ROWS = PAIRS_PER_KV * BLOCK
SOFTMAX_ROW_CHUNK = 16
TILE_A = 256
TILE_B = 256
V7X_VMEM_LIMIT_BYTES = 63 * 1024 * 1024

G_MIX_PRE, G_MIX_POST, G_POOL_SCALE, G_FFN_PRE, G_FFN_POST, G_KV = range(6)


def _rms(x, g):
    ms = jnp.sum(x * x, axis=-1, keepdims=True) * (1.0 / D_MODEL)
    return x * lax.rsqrt(ms + EPS) * g


def _gain(gains_ref, i):
    return gains_ref[i:i + 1, :]


def _row_items(rows, chunk, body):
    chunk = min(chunk, rows.size)
    return [functools.partial(body, pl.ds(rows.start + i * chunk, chunk)) for i in range(rows.size // chunk)]


def _run_items(items, gate=None):
    for item in items:
        item(gate)


def _gate_of(value):
    bits = pltpu.bitcast(value[-GATE_ROWS:, -LANES:], jnp.uint32)
    return pltpu.bitcast((bits >> 16) >> 16, jnp.float32)


def _gated(x, gate):
    if gate is None:
        return x
    z = jnp.concatenate([gate] * (x.shape[0] // GATE_ROWS), axis=0)
    return jnp.concatenate([x[:, :LANES] + z, x[:, LANES:]], axis=1)


def _spread(items, n_bins):
    bins = [[] for _ in range(n_bins)]
    for k, item in enumerate(items):
        bins[k * n_bins // max(len(items), 1)].append(item)
    return bins


def _swiglu(h_ref, wgu_ref, wd_ref, gbuf, ubuf, act, out_ref, rows, side_items=()):
    segments = ([("gate_up", c) for c in range(0, D_FF, FFN_COLS)]
                + [("down", c) for c in range(0, D_MODEL, FFN_COLS)])
    side = _spread(list(side_items), len(segments) - UNRIDDEN_SEGMENTS) + [[]] * UNRIDDEN_SEGMENTS
    for (kind, c0), side_group in zip(segments, side):
        if kind == "gate_up":
            g = jnp.dot(h_ref[rows, :], wgu_ref[:, c0:c0 + FFN_COLS], preferred_element_type=jnp.float32)
            gbuf[rows, :] = g
            ubuf[rows, :] = jnp.dot(h_ref[rows, :], wgu_ref[:, D_FF + c0:D_FF + c0 + FFN_COLS],
                                    preferred_element_type=jnp.float32)
            gate_source = g

            def silu_rows(r, gate, c0=c0):
                half_g = 0.5 * gbuf[r, :]
                silu = half_g + half_g * jnp.tanh(half_g)
                act[r, c0:c0 + FFN_COLS] = (silu * ubuf[r, :]).astype(jnp.bfloat16)

            _run_items(_row_items(rows, SILU_ROW_CHUNK, silu_rows))
        else:
            y = jnp.dot(act[rows, :], wd_ref[:, c0:c0 + FFN_COLS], preferred_element_type=jnp.float32)
            out_ref[rows, c0:c0 + FFN_COLS] = y
            gate_source = y
        if side_group:
            _run_items(side_group, _gate_of(gate_source))


def _run_single_tile(stages):
    mixer, ffn, epilogue = stages(0)
    _run_items(mixer)
    ffn(())
    _run_items(epilogue)


def _run_pipelined_pairs(step, n_pairs, stages):
    even_mixer, even_ffn, even_epilogue = stages(0)
    odd_mixer, odd_ffn, odd_epilogue = stages(1)

    @pl.when(step == 0)
    def _():
        _run_items(even_mixer)
        even_ffn(odd_mixer)

    @pl.when((step > 0) & (step < n_pairs))
    def _():
        odd_ffn(even_epilogue + even_mixer)
        even_ffn(odd_epilogue + odd_mixer)

    @pl.when(step == n_pairs)
    def _():
        odd_ffn(even_epilogue)
        _run_items(odd_epilogue)


def _layer_a_kernel(x_ref, prev_ref, meta_ref, gains_ref, pool_w_ref, wgu_ref, wd_ref, wkv_ref, bkv_ref,
                    hs_out_ref, kv_out_ref,
                    hext, pbuf, ymix, hs1, hb, yffn, gbuf, ubuf, act, kvin,
                    *, tm, tiles_per_batch, n_tiles, is_meta):
    g_mix_pre = _gain(gains_ref, G_MIX_PRE)
    step = pl.program_id(0)
    tile_rows = pl.ds(0, tm)

    def stages(slot):
        def io(rows):
            return pl.ds(rows.start + slot * tm, rows.size)

        def halo_rows(gate):
            if is_meta:
                hext[0:HALO, :] = jnp.zeros((HALO, D_MODEL), jnp.float32)
            elif slot == 1:
                hext[0:HALO, :] = _rms(x_ref[tm - HALO:tm, :], g_mix_pre)
            else:
                pair = jnp.minimum(step, n_tiles // 2 - 1)
                first_of_batch = ((2 * pair) % tiles_per_batch) == 0
                prev = jnp.where(first_of_batch, meta_ref[...], prev_ref[...])
                hext[0:HALO, :] = _rms(prev, g_mix_pre)

        def norm_rows(rows, gate):
            hext[pl.ds(rows.start + HALO, rows.size), :] = _rms(_gated(x_ref[io(rows), :], gate), g_mix_pre)

        def pool_rows(rows, gate):
            for gi, w in enumerate(POOL_WINDOWS):
                cols = slice(gi * POOL_GROUP_DIM, (gi + 1) * POOL_GROUP_DIM)
                e0 = _gated(hext[pl.ds(rows.start, rows.size + HALO), cols], gate)
                e = e0
                shift = 1
                while shift < w:
                    e = e + pltpu.roll(e, shift, 0)
                    shift *= 2
                win = e[HALO:, :]
                if is_meta:
                    pos = lax.broadcasted_iota(jnp.int32, win.shape, 0) + 1
                    cnt = jnp.minimum(pos, w).astype(jnp.float32)
                    pooled = win / cnt - e0[HALO:, :]
                else:
                    pooled = win * (1.0 / w) - e0[HALO:, :]
                pbuf[rows, cols] = pooled.astype(jnp.bfloat16)

        def group_matmul(gi, gate):
            cols = slice(gi * POOL_GROUP_DIM, (gi + 1) * POOL_GROUP_DIM)
            ymix[:, cols] = jnp.dot(pbuf[:, cols], pool_w_ref[gi], preferred_element_type=jnp.float32)

        def post_mix_rows(rows, gate):
            y = _gated(ymix[rows, :], gate)
            h1 = x_ref[io(rows), :] + _rms(y * _gain(gains_ref, G_POOL_SCALE), _gain(gains_ref, G_MIX_POST))
            hs1[slot, rows, :] = h1
            hb[slot, rows, :] = _rms(h1, _gain(gains_ref, G_FFN_PRE)).astype(jnp.bfloat16)

        mixer = ([halo_rows] + _row_items(tile_rows, ROW_CHUNK, norm_rows)
                 + _row_items(tile_rows, POOL_ROW_CHUNK, pool_rows)
                 + [functools.partial(group_matmul, gi) for gi in range(len(POOL_WINDOWS))]
                 + _row_items(tile_rows, ROW_CHUNK, post_mix_rows))

        def ffn(side_items):
            _swiglu(hb.at[slot], wgu_ref, wd_ref, gbuf, ubuf, act, yffn.at[slot], tile_rows, side_items)

        def post_ffn_rows(rows, gate):
            h2 = hs1[slot, rows, :] + _rms(_gated(yffn[slot, rows, :], gate), _gain(gains_ref, G_FFN_POST))
            hs_out_ref[io(rows), :] = h2
            kvin[rows, :] = _rms(h2, _gain(gains_ref, G_KV)).astype(jnp.bfloat16)

        def kv_projection(gate):
            kv = jnp.dot(kvin[...], wkv_ref[...], preferred_element_type=jnp.float32) + bkv_ref[...]
            kv_out_ref[io(tile_rows), :] = kv.astype(jnp.bfloat16)

        epilogue = _row_items(tile_rows, ROW_CHUNK, post_ffn_rows) + [kv_projection]
        return mixer, ffn, epilogue

    if n_tiles == 1:
        _run_single_tile(stages)
    else:
        _run_pipelined_pairs(step, n_tiles // 2, stages)


def _const_spec(shape):
    nd = len(shape)
    return pl.BlockSpec(shape, lambda *_: (0,) * nd, pipeline_mode=pl.Buffered(1))


def _layer_spec(stacked_shape, layer):
    _, rows, cols = stacked_shape
    return pl.BlockSpec((None, rows, cols), lambda *_: (layer, 0, 0), pipeline_mode=pl.Buffered(1))


def _layer_a_call(x2d, meta, gains, pool_w, wgu, wd, wkv, bkv, *, layer, tm, tiles_per_batch, is_meta):
    n_rows = x2d.shape[0]
    n_tiles = n_rows // tm
    tiles_per_step = 1 if n_tiles == 1 else 2
    block_rows = tiles_per_step * tm
    n_blocks = n_tiles // tiles_per_step
    n_steps = n_blocks if n_tiles == 1 else n_blocks + 1
    halo_blocks_per_block = block_rows // HALO
    kern = functools.partial(_layer_a_kernel, tm=tm, tiles_per_batch=tiles_per_batch, n_tiles=n_tiles,
                             is_meta=is_meta)

    def mixer_block(s):
        return jnp.minimum(s, n_blocks - 1)

    def epilogue_block(s):
        return jnp.maximum(s - (n_steps - n_blocks), 0)

    return pl.pallas_call(
        kern,
        grid=(n_steps,),
        in_specs=[
            pl.BlockSpec((block_rows, D_MODEL), lambda s: (mixer_block(s), 0)),
            pl.BlockSpec((HALO, D_MODEL), lambda s: (jnp.maximum(mixer_block(s) * halo_blocks_per_block - 1, 0), 0)),
            _const_spec((N_META, D_MODEL)),
            _const_spec(gains.shape),
            _const_spec(pool_w.shape),
            _layer_spec(wgu.shape, layer),
            _layer_spec(wd.shape, layer),
            _const_spec(wkv.shape),
            _const_spec(bkv.shape),
        ],
        out_specs=[
            pl.BlockSpec((block_rows, D_MODEL), lambda s: (epilogue_block(s), 0)),
            pl.BlockSpec((block_rows, 2 * KV_DIM), lambda s: (epilogue_block(s), 0)),
        ],
        out_shape=[
            jax.ShapeDtypeStruct((n_rows, D_MODEL), jnp.float32),
            jax.ShapeDtypeStruct((n_rows, 2 * KV_DIM), jnp.bfloat16),
        ],
        scratch_shapes=[
            pltpu.VMEM((tm + HALO, D_MODEL), jnp.float32),
            pltpu.VMEM((tm, D_MODEL), jnp.bfloat16),
            pltpu.VMEM((tm, D_MODEL), jnp.float32),
            pltpu.VMEM((2, tm, D_MODEL), jnp.float32),
            pltpu.VMEM((2, tm, D_MODEL), jnp.bfloat16),
            pltpu.VMEM((2, tm, D_MODEL), jnp.float32),
            pltpu.VMEM((tm, FFN_COLS), jnp.float32),
            pltpu.VMEM((tm, FFN_COLS), jnp.float32),
            pltpu.VMEM((tm, D_FF), jnp.bfloat16),
            pltpu.VMEM((tm, D_MODEL), jnp.bfloat16),
        ],
        compiler_params=pltpu.CompilerParams(
            dimension_semantics=("arbitrary",), vmem_limit_bytes=V7X_VMEM_LIMIT_BYTES),
        name="layer_a_meta" if is_meta else "layer_a",
    )(x2d, x2d, meta, gains, pool_w, wgu, wd, wkv, bkv)


def _bucket_distance_ranges():
    d = np.arange(WINDOW)
    max_exact = N_BUCKETS // 2
    df = np.maximum(d, 1).astype(np.float32)
    large = max_exact + (np.log(df / np.float32(max_exact)) / np.float32(math.log(MAX_DISTANCE / max_exact))
                         * np.float32(N_BUCKETS - max_exact)).astype(np.int32)
    bucket = np.where(d < max_exact, d, np.minimum(large, N_BUCKETS - 1))
    ranges = []
    for b in range(N_BUCKETS):
        members = d[bucket == b]
        if members.size:
            assert np.array_equal(members, np.arange(members[0], members[-1] + 1))
            ranges.append((int(members[0]), int(members[-1])))
        else:
            ranges.append(None)
    return ranges


def _bias_table_kernel(rel_bias_ref, out_ref):
    m = pl.program_id(0)
    q = lax.broadcasted_iota(jnp.int32, (BLOCK, 2 * BLOCK), 0)
    s = lax.broadcasted_iota(jnp.int32, (BLOCK, 2 * BLOCK), 1)
    d = q + BLOCK - s
    in_window = (d >= 0) & (d < WINDOW)
    valid = (in_window & (s >= PAD_FRONT), in_window)
    for half in range(2):
        h = 2 * m + half
        acc = jnp.zeros((BLOCK, 2 * BLOCK), jnp.float32)
        for b, distances in enumerate(_bucket_distance_ranges()):
            if distances is not None:
                lo, hi = distances
                acc = jnp.where((d >= lo) & (d <= hi), rel_bias_ref[b, h], acc)
        for jsel in range(2):
            out_ref[jsel, 0, :, half * 2 * BLOCK:(half + 1) * 2 * BLOCK] = jnp.where(valid[jsel], acc, -jnp.inf)


def _bias_table_call(rel_bias):
    return pl.pallas_call(
        _bias_table_kernel,
        grid=(N_HEAD_PAIRS,),
        in_specs=[
            pl.BlockSpec(memory_space=pltpu.SMEM),
        ],
        out_specs=pl.BlockSpec((2, 1, BLOCK, 4 * BLOCK), lambda m: (0, m, 0, 0)),
        out_shape=jax.ShapeDtypeStruct((2, N_HEAD_PAIRS, BLOCK, 4 * BLOCK), jnp.float32),
        compiler_params=pltpu.CompilerParams(dimension_semantics=("arbitrary",)),
        name="bias_table",
    )(rel_bias)


def _layer_b_kernel(hs_ref, kv_ref, metakv_ref, rowsum_cols_ref, bias_ref, sinks_ref, gains_ref,
                    wq_ref, bq_ref, wo_ref, bo_ref, wgu_ref, wd_ref,
                    out_ref,
                    hn, qbuf, obuf, yatt, hs3, hb, yffn, gbuf, ubuf, act, lbuf, mbuf, pbuf,
                    *, tm, tiles_per_batch, n_tiles):
    blocks_per_tile = tm // BLOCK
    step = pl.program_id(0)
    pair = jnp.minimum(step, n_tiles // 2 - 1)
    tile_rows = pl.ds(0, tm)

    lane = lax.broadcasted_iota(jnp.int32, (2 * BLOCK, 2 * HEAD_DIM), 1)
    low_half = lane < HEAD_DIM
    zero = jnp.zeros((2 * BLOCK, 2 * HEAD_DIM), jnp.bfloat16)
    low_half_q = lax.broadcasted_iota(jnp.int32, (BLOCK, 2 * HEAD_DIM), 1) < HEAD_DIM

    def per_kv_head_operands(pair_cols):
        swapped = jnp.concatenate([pair_cols[:, HEAD_DIM:], pair_cols[:, :HEAD_DIM]], axis=1)
        g0 = (jnp.where(low_half, pair_cols, zero), jnp.where(low_half, zero, swapped))
        g1 = (jnp.where(low_half, swapped, zero), jnp.where(low_half, zero, pair_cols))
        return g0, g1

    ones_top = rowsum_cols_ref[0:2 * BLOCK, :]
    ones_bottom = rowsum_cols_ref[2 * BLOCK:4 * BLOCK, :]

    def attention_block(slot, jb, gate):
        t = (2 * pair + slot) % tiles_per_batch
        j = t * blocks_per_tile + jb
        own = kv_ref[pl.ds(pl.multiple_of(j * BLOCK, BLOCK), BLOCK), :]
        prev_start = pl.multiple_of(jnp.maximum(j - 1, 0) * BLOCK, BLOCK)
        prev = jnp.where(j == 0, metakv_ref[...], kv_ref[pl.ds(prev_start, BLOCK), :])
        kvb = jnp.concatenate([prev, own], axis=0)
        k_ops = per_kv_head_operands(kvb[:, :KV_DIM])
        v_ops = per_kv_head_operands(kvb[:, KV_DIM:])
        jsel = jnp.minimum(j, 1)
        qrows = slice(jb * BLOCK, (jb + 1) * BLOCK)
        for g in range(N_KV_HEADS):
            u = jb * N_KV_HEADS + g
            pairs = range(g * PAIRS_PER_KV, (g + 1) * PAIRS_PER_KV)
            q4 = jnp.concatenate([qbuf[qrows, m * 2 * HEAD_DIM:(m + 1) * 2 * HEAD_DIM] for m in pairs], axis=0)
            k_op = jnp.concatenate(k_ops[g], axis=0)
            v_op = jnp.concatenate([jnp.concatenate([v_ops[g][0], ones_top], axis=1),
                                    jnp.concatenate([v_ops[g][1], ones_bottom], axis=1)], axis=0)
            logits = _gated(lax.dot_general(q4, k_op, (((1,), (1,)), ((), ())),
                                            preferred_element_type=jnp.float32) + bias_ref[jsel, g], gate)
            lbuf[u] = logits
            for r0 in range(0, ATT_ROWS, SOFTMAX_ROW_CHUNK):
                rows = slice(r0, r0 + SOFTMAX_ROW_CHUNK)
                for half in range(2):
                    sink = sinks_ref[2 * (pairs[0] + r0 // BLOCK) + half]
                    lg = logits[rows, half * 2 * BLOCK:(half + 1) * 2 * BLOCK]
                    mx = jnp.maximum(jnp.max(lg, axis=-1, keepdims=True), sink)
                    mbuf[u, rows, half * BLOCK:(half + 1) * BLOCK] = jnp.broadcast_to(mx, (SOFTMAX_ROW_CHUNK, BLOCK))
            for r0 in range(0, ATT_ROWS, SOFTMAX_ROW_CHUNK):
                rows = slice(r0, r0 + SOFTMAX_ROW_CHUNK)
                for half in range(2):
                    mrep = mbuf[u, rows, half * BLOCK:(half + 1) * BLOCK]
                    lg = lbuf[u, rows, half * 2 * BLOCK:(half + 1) * 2 * BLOCK]
                    p = jnp.exp(lg - jnp.concatenate([mrep, mrep], axis=1))
                    pbuf[u, rows, half * 2 * BLOCK:(half + 1) * 2 * BLOCK] = p.astype(jnp.bfloat16)
            oa = jnp.dot(pbuf[u], v_op, preferred_element_type=jnp.float32)
            for i, m in enumerate(pairs):
                rows = slice(i * BLOCK, (i + 1) * BLOCK)
                sink_gap = jnp.where(low_half_q, sinks_ref[2 * m] - mbuf[u, rows, 0:BLOCK],
                                     sinks_ref[2 * m + 1] - mbuf[u, rows, BLOCK:2 * BLOCK])
                denom = oa[rows, 2 * HEAD_DIM:] + jnp.exp(sink_gap)
                o = oa[rows, :2 * HEAD_DIM] * (1.0 / denom)
                obuf[qrows, m * 2 * HEAD_DIM:(m + 1) * 2 * HEAD_DIM] = o.astype(jnp.bfloat16)

    def stages(slot):
        def io(rows):
            return pl.ds(rows.start + slot * tm, rows.size)

        def norm_rows(rows, gate):
            h = _rms(_gated(hs_ref[io(rows), :], gate), _gain(gains_ref, G_MIX_PRE))
            hn[rows, :] = h.astype(jnp.bfloat16)

        def q_projection(gate):
            q = jnp.dot(hn[...], wq_ref[...], preferred_element_type=jnp.float32) + bq_ref[...]
            qbuf[...] = (q * (HEAD_DIM ** -0.5)).astype(jnp.bfloat16)

        def o_projection(gate):
            yatt[...] = jnp.dot(obuf[...], wo_ref[...], preferred_element_type=jnp.float32) + bo_ref[...]

        def post_mix_rows(rows, gate):
            h3 = hs_ref[io(rows), :] + _rms(_gated(yatt[rows, :], gate), _gain(gains_ref, G_MIX_POST))
            hs3[slot, rows, :] = h3
            hb[slot, rows, :] = _rms(h3, _gain(gains_ref, G_FFN_PRE)).astype(jnp.bfloat16)

        attention = (_row_items(tile_rows, ROW_CHUNK, norm_rows) + [q_projection]
                     + [functools.partial(attention_block, slot, jb) for jb in range(blocks_per_tile)]
                     + [o_projection] + _row_items(tile_rows, ROW_CHUNK, post_mix_rows))

        def ffn(side_items):
            _swiglu(hb.at[slot], wgu_ref, wd_ref, gbuf, ubuf, act, yffn.at[slot], tile_rows, side_items)

        def post_ffn_rows(rows, gate):
            y = _gated(yffn[slot, rows, :], gate)
            out_ref[io(rows), :] = hs3[slot, rows, :] + _rms(y, _gain(gains_ref, G_FFN_POST))

        return attention, ffn, _row_items(tile_rows, ROW_CHUNK, post_ffn_rows)

    _run_pipelined_pairs(step, n_tiles // 2, stages)


def _layer_b_call(hs, kv, metakv, bias, sinks, gains, wq, bq, wo, bo, wgu, wd, *, layer, tm):
    batch, seq, _ = hs.shape
    tiles_per_batch = seq // tm
    n_tiles = batch * tiles_per_batch
    n_att_units = (tm // BLOCK) * N_KV_HEADS
    lane_head = jnp.arange(2 * HEAD_DIM)[None, :] // HEAD_DIM
    row_head = jnp.arange(4 * BLOCK)[:, None] // (2 * BLOCK)
    rowsum_cols = (lane_head == row_head).astype(jnp.bfloat16)
    kern = functools.partial(_layer_b_kernel, tm=tm, tiles_per_batch=tiles_per_batch, n_tiles=n_tiles)

    n_pairs = n_tiles // 2
    pairs_per_batch = tiles_per_batch // 2

    def att_pair(s):
        pair = jnp.minimum(s, n_pairs - 1)
        return pair // pairs_per_batch, pair % pairs_per_batch

    def epilogue_pair(s):
        pair = jnp.maximum(s - 1, 0)
        return pair // pairs_per_batch, pair % pairs_per_batch

    return pl.pallas_call(
        kern,
        grid=(n_pairs + 1,),
        in_specs=[
            pl.BlockSpec((None, 2 * tm, D_MODEL), lambda s: (*att_pair(s), 0)),
            pl.BlockSpec((None, seq, 2 * KV_DIM), lambda s: (att_pair(s)[0], 0, 0)),
            _const_spec(metakv.shape),
            _const_spec(rowsum_cols.shape),
            _const_spec(bias.shape),
            pl.BlockSpec(memory_space=pltpu.SMEM),
            _const_spec(gains.shape),
            _const_spec(wq.shape),
            _const_spec(bq.shape),
            _const_spec(wo.shape),
            _const_spec(bo.shape),
            _layer_spec(wgu.shape, layer),
            _layer_spec(wd.shape, layer),
        ],
        out_specs=pl.BlockSpec((None, 2 * tm, D_MODEL), lambda s: (*epilogue_pair(s), 0)),
        out_shape=jax.ShapeDtypeStruct((batch, seq, D_MODEL), jnp.float32),
        scratch_shapes=[
            pltpu.VMEM((tm, D_MODEL), jnp.bfloat16),
            pltpu.VMEM((tm, D_MODEL), jnp.bfloat16),
            pltpu.VMEM((tm, D_MODEL), jnp.bfloat16),
            pltpu.VMEM((tm, D_MODEL), jnp.float32),
            pltpu.VMEM((2, tm, D_MODEL), jnp.float32),
            pltpu.VMEM((2, tm, D_MODEL), jnp.bfloat16),
            pltpu.VMEM((2, tm, D_MODEL), jnp.float32),
            pltpu.VMEM((tm, FFN_COLS), jnp.float32),
            pltpu.VMEM((tm, FFN_COLS), jnp.float32),
            pltpu.VMEM((tm, D_FF), jnp.bfloat16),
            pltpu.VMEM((n_att_units, ATT_ROWS, 4 * BLOCK), jnp.float32),
            pltpu.VMEM((n_att_units, ATT_ROWS, 2 * BLOCK), jnp.float32),
            pltpu.VMEM((n_att_units, ATT_ROWS, 4 * BLOCK), jnp.bfloat16),
        ],
        compiler_params=pltpu.CompilerParams(
            dimension_semantics=("arbitrary",), vmem_limit_bytes=V7X_VMEM_LIMIT_BYTES),
        name="layer_b",
    )(hs, kv, metakv, rowsum_cols, bias, sinks, gains, wq, bq, wo, bo, wgu, wd)


def _cast_kernel(w_ref, out_ref):
    out_ref[...] = w_ref[...].astype(jnp.bfloat16)


def _cast_weights(w, body, name):
    layers, rows, cols = w.shape
    w2d = w.reshape(layers * rows, cols)
    target = max(16, CAST_BLOCK_BYTES // (4 * cols) // 16 * 16)
    block_rows = next(r for r in range(target, 0, -16) if (layers * rows) % r == 0)
    spec = pl.BlockSpec((block_rows, cols), lambda i: (i, 0))
    out = pl.pallas_call(
        body,
        grid=(layers * rows // block_rows,),
        in_specs=[spec],
        out_specs=spec,
        out_shape=jax.ShapeDtypeStruct(w2d.shape, jnp.bfloat16),
        compiler_params=pltpu.CompilerParams(dimension_semantics=("arbitrary",)),
        name=name,
    )(w2d)
    return out.reshape(layers, rows, cols)


def _gain_table(rows):
    zero = jnp.zeros((D_MODEL,), jnp.float32)
    return jnp.stack([rows[i].astype(jnp.float32) if i in rows else zero for i in range(8)])


def kernel(x, meta_tokens, norm_mix_pre, norm_mix_post, norm_ffn_pre, norm_ffn_post, pool_w, pool_scale, kv_norm, w_k, b_k, w_v, b_v, w_q, b_q, w_o, b_o, sinks, rel_bias, w_gate_up, w_down):
    batch, seq, _ = x.shape
    bf16 = jnp.bfloat16

    gains_a = _gain_table({G_MIX_PRE: norm_mix_pre[0], G_MIX_POST: norm_mix_post[0], G_POOL_SCALE: pool_scale[0],
                           G_FFN_PRE: norm_ffn_pre[0], G_FFN_POST: norm_ffn_post[0], G_KV: kv_norm})
    gains_b = _gain_table({G_MIX_PRE: norm_mix_pre[1], G_MIX_POST: norm_mix_post[1],
                           G_FFN_PRE: norm_ffn_pre[1], G_FFN_POST: norm_ffn_post[1]})
    wkv = jnp.concatenate([w_k, w_v], axis=1).astype(bf16)
    bkv = jnp.concatenate([b_k, b_v])[None, :]
    wgu = _cast_weights(w_gate_up, _cast_kernel, "cast_gate_up")
    wd = _cast_weights(w_down, _cast_kernel, "cast_down")
    layer_a_weights = (gains_a, pool_w[0].astype(bf16), wgu, wd, wkv, bkv)

    x2d = x.reshape(batch * seq, D_MODEL)
    hs2, kv_x = _layer_a_call(x2d, meta_tokens, *layer_a_weights,
                              layer=0, tm=TILE_A, tiles_per_batch=seq // TILE_A, is_meta=False)
    _, kv_meta = _layer_a_call(meta_tokens, meta_tokens, *layer_a_weights,
                               layer=0, tm=N_META, tiles_per_batch=1, is_meta=True)
    metakv = jnp.concatenate([jnp.zeros((PAD_FRONT, 2 * KV_DIM), bf16), kv_meta], axis=0)

    bias = _bias_table_call(rel_bias)
    bias = bias.reshape(2, N_KV_HEADS, ATT_ROWS, 4 * BLOCK)

    out = _layer_b_call(hs2.reshape(batch, seq, D_MODEL), kv_x.reshape(batch, seq, 2 * KV_DIM), metakv, bias,
                        sinks[0], gains_b, w_q[0].astype(bf16), b_q, w_o[0].astype(bf16), b_o,
                        wgu, wd, layer=1, tm=TILE_B)
    return out
```
